```python
import math
import jax, jax.numpy as jnp
from jax import lax
import numpy as np

D_MODEL = 1024
BATCH = 4
SEQ = 4096
DEPTH = 2

CHUNK = 64
RET_HEADS = 4
RET_DK = 64
RET_DV = 128
RET_QK_WIDTH = RET_HEADS * RET_DK
RET_WIDTH = RET_HEADS * RET_DV
ROPE_BASE = 10000.0
CONV_CH = D_MODEL // 2
CONV_WIDTH = 31
L0_IN = 2 * RET_QK_WIDTH + 2 * RET_WIDTH + 2 * CONV_CH
L0_OUT_IN = RET_WIDTH + CONV_CH
SB_HEADS = 8
SB_HEAD_DIM = D_MODEL // SB_HEADS
SB_BLOCK = 128
N_EXPERTS = 32
TOP_K = 4
D_FF = D_MODEL
SWIGLU_LIMIT = 7.0
SWIGLU_ALPHA = 1.702
EXPERT_BLOCK = 128
LN_EPS = 1e-5
DEEPNORM_ALPHA = (2 * DEPTH) ** 0.25
DEEPNORM_BETA = (8 * DEPTH) ** -0.25

kernel_name = 'hybrid_retention_conformer_stickbreaking_moe_deepnorm'


def layer_norm(x, g, b):
    xf = x.astype(jnp.float32)
    mu = jnp.mean(xf, axis=-1, keepdims=True)
    var = jnp.mean(jnp.square(xf - mu), axis=-1, keepdims=True)
    y = (xf - mu) * lax.rsqrt(var + LN_EPS)
    return (y * g.astype(jnp.float32) + b.astype(jnp.float32)).astype(x.dtype)


def rotary(x, positions):
    half = x.shape[-1] // 2
    inv = ROPE_BASE ** (-jnp.arange(half, dtype=jnp.float32) / half)
    ang = positions.astype(jnp.float32)[:, None] * inv[None, :]
    cos = jnp.cos(ang)[None, :, None, :].astype(x.dtype)
    sin = jnp.sin(ang)[None, :, None, :].astype(x.dtype)
    x1, x2 = x[..., :half], x[..., half:]
    return jnp.concatenate([x1 * cos - x2 * sin, x2 * cos + x1 * sin], axis=-1)


def retention_chunkwise(q, k, v):
    B, S, H, dk = q.shape
    dv = v.shape[-1]
    nc = S // CHUNK
    to_chunks = lambda t: t.reshape(B, nc, CHUNK, H, t.shape[-1]).transpose(0, 3, 1, 2, 4)
    qc, kc, vc = to_chunks(q), to_chunks(k), to_chunks(v)
    log_g = jnp.log(1.0 - 2.0 ** (-5.0 - jnp.arange(H, dtype=jnp.float32)))
    n = jnp.arange(CHUNK, dtype=jnp.float32)
    dt = q.dtype
    intra_decay = jnp.exp(jnp.abs(n[:, None] - n[None, :])[None] * log_g[:, None, None]).astype(dt)
    xi = jnp.exp((n[None, :] + 1.0) * log_g[:, None]).astype(dt)
    zeta = jnp.exp((CHUNK - 1.0 - n[None, :]) * log_g[:, None]).astype(dt)
    g_chunk = jnp.exp(CHUNK * log_g)
    scores = jnp.einsum('bhcnd,bhcmd->bhcnm', qc, kc) * intra_decay[None, :, None]
    intra = jnp.einsum('bhcnm,bhcme->bhcne', scores, vc)
    kv = jnp.einsum('bhcmd,bhcme->cbhde', kc * zeta[None, :, None, :, None], vc)
    g_state = g_chunk[None, :, None, None].astype(kv.dtype)

    def step(state, kv_c):
        return g_state * state + kv_c, state

    _, r_prev = lax.scan(step, jnp.zeros((B, H, dk, dv), kv.dtype), kv)
    inter = jnp.einsum('bhcnd,cbhde->bhcne', qc * xi[None, :, None, :, None], r_prev)
    out = intra + inter
    return out.transpose(0, 2, 3, 1, 4).reshape(B, S, H, dv)


def depthwise_causal_conv(u, w, b):
    y = lax.conv_general_dilated(u, w[:, None, :], window_strides=(1,),
                                 padding=[(CONV_WIDTH - 1, 0)],
                                 dimension_numbers=('NWC', 'WIO', 'NWC'),
                                 feature_group_count=u.shape[-1])
    return y + b


def retention_conv_mixer(x, w_in, ret_gn_g, conv_w, conv_b, conv_ln_g, conv_ln_b, w_out):
    B, S, _ = x.shape
    h = x @ w_in
    q, k, v, gate, glu_a, glu_b = jnp.split(
        h, [RET_QK_WIDTH, 2 * RET_QK_WIDTH, 2 * RET_QK_WIDTH + RET_WIDTH,
            2 * RET_QK_WIDTH + 2 * RET_WIDTH, 2 * RET_QK_WIDTH + 2 * RET_WIDTH + CONV_CH], axis=-1)
    pos = jnp.arange(S)
    q = rotary(q.reshape(B, S, RET_HEADS, RET_DK), pos)
    k = rotary(k.reshape(B, S, RET_HEADS, RET_DK), pos) * (RET_DK ** -0.5)
    v = v.reshape(B, S, RET_HEADS, RET_DV)
    r = retention_chunkwise(q, k, v).astype(jnp.float32)
    mu = jnp.mean(r, axis=-1, keepdims=True)
    var = jnp.mean(jnp.square(r - mu), axis=-1, keepdims=True)
    r = ((r - mu) * lax.rsqrt(var + LN_EPS)).reshape(B, S, RET_WIDTH) * ret_gn_g.astype(jnp.float32)
    ret_out = jax.nn.silu(gate) * r.astype(x.dtype)
    u = glu_a * jax.nn.sigmoid(glu_b)
    c = depthwise_causal_conv(u, conv_w, conv_b)
    c = jax.nn.silu(layer_norm(c, conv_ln_g, conv_ln_b))
    return jnp.concatenate([ret_out, c], axis=-1) @ w_out


def stick_breaking_attention(q, k, v):
    S = q.shape[2]
    scale = q.shape[-1] ** -0.5
    outs = []
    for i in range(S // SB_BLOCK):
        start = i * SB_BLOCK
        end = start + SB_BLOCK
        qb, kb, vb = q[:, :, start:end], k[:, :, :end], v[:, :, :end]
        z = jnp.einsum('bhtd,bhsd->bhts', qb, kb, preferred_element_type=jnp.float32) * scale
        t_idx = start + jnp.arange(SB_BLOCK)
        s_idx = jnp.arange(end)
        mask = s_idx[None, :] < t_idx[:, None]
        log_beta = jax.nn.log_sigmoid(z)
        log_one_minus = jnp.where(mask, jax.nn.log_sigmoid(-z), 0.0)
        between = lax.cumsum(log_one_minus, axis=3, reverse=True) - log_one_minus
        a = jnp.where(mask, jnp.exp(log_beta + between), 0.0)
        outs.append(jnp.einsum('bhts,bhsd->bhtd', a.astype(vb.dtype), vb))
    return jnp.concatenate(outs, axis=2)


def stick_breaking_mixer(x, w_in, w_out):
    B, S, _ = x.shape
    q, k, v = jnp.split(x @ w_in, 3, axis=-1)
    heads = lambda t: t.reshape(B, S, SB_HEADS, SB_HEAD_DIM).transpose(0, 2, 1, 3)
    o = stick_breaking_attention(heads(q), heads(k), heads(v))
    return o.transpose(0, 2, 1, 3).reshape(B, S, D_MODEL) @ w_out


def clamped_swiglu(h):
    h_glu = jnp.minimum(h[..., ::2], SWIGLU_LIMIT)
    h_lin = jnp.clip(h[..., 1::2], -SWIGLU_LIMIT, SWIGLU_LIMIT)
    return h_glu * jax.nn.sigmoid(SWIGLU_ALPHA * h_glu) * (h_lin + 1.0)


def moe(x, router_w, router_b, w_up, b_up, w_down, b_down):
    B, S, D = x.shape
    N = B * S
    NK = N * TOP_K
    xf = x.reshape(N, D)
    logits = jnp.dot(xf, router_w, preferred_element_type=jnp.float32) + router_b.astype(jnp.float32)
    top_vals, top_idx = lax.top_k(logits, TOP_K)
    gates = jax.nn.softmax(top_vals, axis=-1)
    flat_e = top_idx.reshape(NK)
    flat_tok = jnp.repeat(jnp.arange(N, dtype=jnp.int32), TOP_K)
    flat_g = gates.reshape(NK)
    order = jnp.argsort(flat_e)
    sorted_e = flat_e[order]
    counts = jnp.bincount(flat_e, length=N_EXPERTS)
    padded = (counts + EXPERT_BLOCK - 1) // EXPERT_BLOCK * EXPERT_BLOCK
    starts = jnp.cumsum(counts) - counts
    padded_ends = jnp.cumsum(padded)
    padded_starts = padded_ends - padded
    dest = padded_starts[sorted_e] + jnp.arange(NK, dtype=jnp.int32) - starts[sorted_e]
    n_blocks = -(-(NK + N_EXPERTS * (EXPERT_BLOCK - 1)) // EXPERT_BLOCK)
    P = n_blocks * EXPERT_BLOCK
    buf_tok = jnp.zeros((P,), jnp.int32).at[dest].set(flat_tok[order])
    buf_gate = jnp.zeros((P,), jnp.float32).at[dest].set(flat_g[order])
    block_e = jnp.minimum(
        jnp.searchsorted(padded_ends, jnp.arange(n_blocks, dtype=jnp.int32) * EXPERT_BLOCK, side='right'),
        N_EXPERTS - 1)
    xin = xf[buf_tok].reshape(n_blocks, EXPERT_BLOCK, D)

    def expert_block(args):
        xb, e = args
        h = xb @ w_up[e] + b_up[e]
        return clamped_swiglu(h) @ w_down[e] + b_down[e]

    y = lax.map(expert_block, (xin, block_e)).reshape(P, D)
    out = jnp.zeros((N, D), y.dtype).at[buf_tok].add(y * buf_gate[:, None].astype(y.dtype))
    return out.reshape(B, S, D)


def _normal(k, shape, scale):
    return scale * jax.random.normal(k, shape, jnp.float32)


def setup_inputs(seed: int = 0) -> dict:
    key = jax.random.key(seed)
    ks = iter(jax.random.split(key, 64))
    D = D_MODEL
    gain = lambda n: 1.0 + _normal(next(ks), (n,), 0.02)
    bias = lambda n: _normal(next(ks), (n,), 0.02)
    p = {}
    p['x'] = _normal(next(ks), (BATCH, SEQ, D), 1.0)
    p['l0_w_in'] = _normal(next(ks), (D, L0_IN), D ** -0.5)
    p['l0_ret_gn_g'] = gain(RET_WIDTH)
    p['l0_conv_w'] = _normal(next(ks), (CONV_WIDTH, CONV_CH), CONV_WIDTH ** -0.5)
    p['l0_conv_b'] = bias(CONV_CH)
    p['l0_conv_ln_g'] = gain(CONV_CH)
    p['l0_conv_ln_b'] = bias(CONV_CH)
    p['l0_w_out'] = _normal(next(ks), (L0_OUT_IN, D), DEEPNORM_BETA * L0_OUT_IN ** -0.5)
    p['l0_ln1_g'] = gain(D)
    p['l0_ln1_b'] = bias(D)
    p['l0_router_w'] = _normal(next(ks), (D, N_EXPERTS), D ** -0.5)
    p['l0_router_b'] = _normal(next(ks), (N_EXPERTS,), 0.01)
    p['l0_w_up'] = _normal(next(ks), (N_EXPERTS, D, 2 * D_FF), D ** -0.5)
    p['l0_b_up'] = _normal(next(ks), (N_EXPERTS, 2 * D_FF), 0.02)
    p['l0_w_down'] = _normal(next(ks), (N_EXPERTS, D_FF, D), DEEPNORM_BETA * D_FF ** -0.5)
    p['l0_b_down'] = _normal(next(ks), (N_EXPERTS, D), 0.02)
    p['l0_ln2_g'] = gain(D)
    p['l0_ln2_b'] = bias(D)
    p['l1_w_in'] = _normal(next(ks), (D, 3 * D), D ** -0.5)
    p['l1_w_out'] = _normal(next(ks), (D, D), DEEPNORM_BETA * D ** -0.5)
    p['l1_ln1_g'] = gain(D)
    p['l1_ln1_b'] = bias(D)
    p['l1_router_w'] = _normal(next(ks), (D, N_EXPERTS), D ** -0.5)
    p['l1_router_b'] = _normal(next(ks), (N_EXPERTS,), 0.01)
    p['l1_w_up'] = _normal(next(ks), (N_EXPERTS, D, 2 * D_FF), D ** -0.5)
    p['l1_b_up'] = _normal(next(ks), (N_EXPERTS, 2 * D_FF), 0.02)
    p['l1_w_down'] = _normal(next(ks), (N_EXPERTS, D_FF, D), DEEPNORM_BETA * D_FF ** -0.5)
    p['l1_b_down'] = _normal(next(ks), (N_EXPERTS, D), 0.02)
    p['l1_ln2_g'] = gain(D)
    p['l1_ln2_b'] = bias(D)
    return p


def reference(x,
              l0_w_in, l0_ret_gn_g, l0_conv_w, l0_conv_b, l0_conv_ln_g, l0_conv_ln_b, l0_w_out,
              l0_ln1_g, l0_ln1_b,
              l0_router_w, l0_router_b, l0_w_up, l0_b_up, l0_w_down, l0_b_down,
              l0_ln2_g, l0_ln2_b,
              l1_w_in, l1_w_out, l1_ln1_g, l1_ln1_b,
              l1_router_w, l1_router_b, l1_w_up, l1_b_up, l1_w_down, l1_b_down,
              l1_ln2_g, l1_ln2_b):
    mixers = [(l0_w_in, l0_ret_gn_g, l0_conv_w, l0_conv_b, l0_conv_ln_g, l0_conv_ln_b, l0_w_out),
              (l1_w_in, l1_w_out)]
    norms1 = [(l0_ln1_g, l0_ln1_b), (l1_ln1_g, l1_ln1_b)]
    moes = [(l0_router_w, l0_router_b, l0_w_up, l0_b_up, l0_w_down, l0_b_down),
            (l1_router_w, l1_router_b, l1_w_up, l1_b_up, l1_w_down, l1_b_down)]
    norms2 = [(l0_ln2_g, l0_ln2_b), (l1_ln2_g, l1_ln2_b)]
    for layer in range(DEPTH):
        if layer % 2 == 0:
            m = retention_conv_mixer(x, *mixers[layer])
        else:
            m = stick_breaking_mixer(x, *mixers[layer])
        x = layer_norm(DEEPNORM_ALPHA * x + m, *norms1[layer])
        x = layer_norm(DEEPNORM_ALPHA * x + moe(x, *moes[layer]), *norms2[layer])
    return x
```

```python
import functools

import jax
import jax.numpy as jnp
from jax import lax
from jax.experimental import pallas as pl
from jax.experimental.pallas import tpu as pltpu

D_MODEL = 1024
CHUNK = 64
RET_HEADS = 4
RET_DK = 64
RET_DV = 128
RET_QK_WIDTH = RET_HEADS * RET_DK
RET_WIDTH = RET_HEADS * RET_DV
ROPE_BASE = 10000.0
CONV_CH = D_MODEL // 2
CONV_WIDTH = 31
L0_IN = 2 * RET_QK_WIDTH + 2 * RET_WIDTH + 2 * CONV_CH
SB_HEADS = 8
SB_HEAD_DIM = D_MODEL // SB_HEADS
N_EXPERTS = 32
TOP_K = 4
D_FF = D_MODEL
SWIGLU_LIMIT = 7.0
SWIGLU_ALPHA = 1.702
LN_EPS = 1e-5
DEPTH = 2
DEEPNORM_ALPHA = (2 * DEPTH) ** 0.25

VMEM_LIMIT_BYTES = 48 * 1024 * 1024
PROJ_ROWS = 512
MIX_ROWS = 256
CONV_HALO = 32
ROUTER_ROWS = 512
MOE_ROWS = 256
COMBINE_ROWS = 256
SB_BLOCK = 128
SB_CUTOFF = 104.0

_F32 = jnp.float32
_BF16 = jnp.bfloat16


def _params(*sem):
    return pltpu.CompilerParams(dimension_semantics=sem, vmem_limit_bytes=VMEM_LIMIT_BYTES)


def _const_spec(shape):
    nd = len(shape)
    return pl.BlockSpec(shape, lambda *_: (0,) * nd)


def _layer_norm_rows(y, g, b):
    mu = jnp.mean(y, axis=-1, keepdims=True)
    yc = y - mu
    var = jnp.mean(yc * yc, axis=-1, keepdims=True)
    return yc * lax.rsqrt(var + LN_EPS) * g + b


def _silu(x):
    return x * jax.nn.sigmoid(x)


def _proj_kernel(x_ref, w_ref, o_ref):
    o_ref[...] = jnp.dot(x_ref[...].astype(_BF16), w_ref[...],
                         preferred_element_type=_F32).astype(o_ref.dtype)


def _proj(x, w_bf16):
    n, d = x.shape
    width = w_bf16.shape[1]
    return pl.pallas_call(
        _proj_kernel,
        grid=(n // PROJ_ROWS,),
        in_specs=[pl.BlockSpec((PROJ_ROWS, d), lambda i: (i, 0)), _const_spec((d, width))],
        out_specs=pl.BlockSpec((PROJ_ROWS, width), lambda i: (i, 0)),
        out_shape=jax.ShapeDtypeStruct((n, width), _BF16),
        compiler_params=_params("arbitrary"),
        name="proj",
    )(x, w_bf16)


def _proj_res_ln_kernel(a_ref, w_ref, res_ref, g_ref, b_ref, o_ref, obf_ref):
    m = jnp.dot(a_ref[...], w_ref[...], preferred_element_type=_F32)
    y = _layer_norm_rows(DEEPNORM_ALPHA * res_ref[...] + m, g_ref[...], b_ref[...])
    o_ref[...] = y
    obf_ref[...] = y.astype(_BF16)


def _proj_res_ln(a_bf16, w_bf16, res, g, b):
    n, d = res.shape
    k = a_bf16.shape[1]
    row = lambda i: (i, 0)
    return pl.pallas_call(
        _proj_res_ln_kernel,
        grid=(n // PROJ_ROWS,),
        in_specs=[pl.BlockSpec((PROJ_ROWS, k), row), _const_spec((k, d)),
                  pl.BlockSpec((PROJ_ROWS, d), row), _const_spec((1, d)), _const_spec((1, d))],
        out_specs=[pl.BlockSpec((PROJ_ROWS, d), row), pl.BlockSpec((PROJ_ROWS, d), row)],
        out_shape=[jax.ShapeDtypeStruct((n, d), _F32), jax.ShapeDtypeStruct((n, d), _BF16)],
        compiler_params=_params("arbitrary"),
        name="proj_res_ln",
    )(a_bf16, w_bf16, res, g.reshape(1, d), b.reshape(1, d))


def _l0_mixer_kernel(h_ref, cos_ref, sin_ref, xi_ref, zeta_ref, decay_ref, gstate_ref, gn_g_ref,
                     conv_w_ref, conv_b_ref, cln_g_ref, cln_b_ref, o_ref, state_ref, u_ref):
    rows = MIX_ROWS
    step = pl.program_id(1)

    @pl.when(step == 0)
    def _():
        state_ref[...] = jnp.zeros_like(state_ref)
        u_ref[0:CONV_HALO, :] = jnp.zeros((CONV_HALO, CONV_CH), _F32)

    lane = lax.broadcasted_iota(jnp.int32, (rows, RET_QK_WIDTH), 1)
    first_half = (lane % RET_DK) < (RET_DK // 2)
    cos = cos_ref[...]
    sin = sin_ref[...]

    def rotary(t):
        partner = jnp.where(first_half,
                            pltpu.roll(t, RET_QK_WIDTH - RET_DK // 2, axis=1),
                            pltpu.roll(t, RET_DK // 2, axis=1))
        return t * cos + partner * sin

    q = rotary(h_ref[:, 0:RET_QK_WIDTH].astype(_F32))
    k = rotary(h_ref[:, RET_QK_WIDTH:2 * RET_QK_WIDTH].astype(_F32)) * (RET_DK ** -0.5)
    q_in = (q * xi_ref[...]).astype(_BF16)
    k_out = (k * zeta_ref[...]).astype(_BF16)
    q = q.astype(_BF16)
    k = k.astype(_BF16)
    v_off = 2 * RET_QK_WIDTH
    g_off = v_off + RET_WIDTH
    for hd in range(RET_HEADS):
        qk = slice(hd * RET_DK, (hd + 1) * RET_DK)
        vs = slice(hd * RET_DV, (hd + 1) * RET_DV)
        v = h_ref[:, v_off + hd * RET_DV:v_off + (hd + 1) * RET_DV]
        s = lax.dot_general(q[:, qk], k[:, qk], (((1,), (1,)), ((), ())),
                            preferred_element_type=_F32) * decay_ref[hd]
        r = jnp.dot(s.astype(_BF16), v, preferred_element_type=_F32)
        r += jnp.dot(q_in[:, qk], state_ref[hd].astype(_BF16), preferred_element_type=_F32)
        kv = lax.dot_general(k_out[:, qk], v, (((0,), (0,)), ((), ())),
                             preferred_element_type=_F32)
        state_ref[hd] = gstate_ref[hd] * state_ref[hd] + kv
        mu = jnp.mean(r, axis=-1, keepdims=True)
        rc = r - mu
        var = jnp.mean(rc * rc, axis=-1, keepdims=True)
        rn = rc * lax.rsqrt(var + LN_EPS) * gn_g_ref[:, vs]
        gate = h_ref[:, g_off + hd * RET_DV:g_off + (hd + 1) * RET_DV].astype(_F32)
        o_ref[:, vs] = (_silu(gate) * rn).astype(o_ref.dtype)

    a_off = g_off + RET_WIDTH
    glu_a = h_ref[:, a_off:a_off + CONV_CH].astype(_F32)
    glu_b = h_ref[:, a_off + CONV_CH:a_off + 2 * CONV_CH].astype(_F32)
    u_ref[CONV_HALO:CONV_HALO + rows, :] = glu_a * jax.nn.sigmoid(glu_b)
    acc = jnp.broadcast_to(conv_b_ref[...], (rows, CONV_CH))
    first = CONV_HALO - (CONV_WIDTH - 1)
    for j in range(CONV_WIDTH):
        acc = acc + conv_w_ref[j:j + 1, :] * u_ref[first + j:first + j + rows, :]
    u_ref[0:CONV_HALO, :] = u_ref[rows:rows + CONV_HALO, :]
    c = _silu(_layer_norm_rows(acc, cln_g_ref[...], cln_b_ref[...]))
    o_ref[:, RET_WIDTH:RET_WIDTH + CONV_CH] = c.astype(o_ref.dtype)


def _l0_mixer_tables(seq):
    half = RET_DK // 2
    inv = ROPE_BASE ** (-jnp.arange(half, dtype=_F32) / half)
    ang = jnp.arange(seq).astype(_F32)[:, None] * inv[None, :]
    cos = jnp.tile(jnp.cos(ang), (1, 2 * RET_HEADS))
    sin = jnp.tile(jnp.concatenate([-jnp.sin(ang), jnp.sin(ang)], axis=1), (1, RET_HEADS))
    log_g = jnp.log(1.0 - 2.0 ** (-5.0 - jnp.arange(RET_HEADS, dtype=_F32)))
    n = jnp.arange(MIX_ROWS, dtype=_F32)
    diff = n[:, None] - n[None, :]
    chunk = jnp.arange(MIX_ROWS) // CHUNK
    expo = jnp.where(chunk[:, None] == chunk[None, :], jnp.abs(diff), diff)
    decay = jnp.where((chunk[:, None] >= chunk[None, :])[None],
                      jnp.exp(expo[None] * log_g[:, None, None]), 0.0)
    xi = jnp.repeat(jnp.exp((n[:, None] + 1.0) * log_g[None, :]), RET_DK, axis=1)
    zeta = jnp.repeat(jnp.exp((MIX_ROWS - 1.0 - n[:, None]) * log_g[None, :]), RET_DK, axis=1)
    g_state = jnp.exp(MIX_ROWS * log_g)
    return cos, sin, xi, zeta, decay, g_state


def _l0_mixer(h, batch, seq, gn_g, conv_w, conv_b, cln_g, cln_b):
    n = h.shape[0]
    steps = seq // MIX_ROWS
    cos, sin, xi, zeta, decay, g_state = _l0_mixer_tables(seq)
    row = lambda b, i: (b * steps + i, 0)
    pos = lambda b, i: (i, 0)
    return pl.pallas_call(
        _l0_mixer_kernel,
        grid=(batch, steps),
        in_specs=[pl.BlockSpec((MIX_ROWS, L0_IN), row),
                  pl.BlockSpec((MIX_ROWS, RET_QK_WIDTH), pos),
                  pl.BlockSpec((MIX_ROWS, RET_QK_WIDTH), pos),
                  _const_spec((MIX_ROWS, RET_QK_WIDTH)), _const_spec((MIX_ROWS, RET_QK_WIDTH)),
                  _const_spec((RET_HEADS, MIX_ROWS, MIX_ROWS)),
                  pl.BlockSpec(memory_space=pltpu.SMEM),
                  _const_spec((1, RET_WIDTH)), _const_spec((CONV_WIDTH, CONV_CH)),
                  _const_spec((1, CONV_CH)), _const_spec((1, CONV_CH)), _const_spec((1, CONV_CH))],
        out_specs=pl.BlockSpec((MIX_ROWS, D_MODEL), row),
        out_shape=jax.ShapeDtypeStruct((n, D_MODEL), _BF16),
        scratch_shapes=[pltpu.VMEM((RET_HEADS, RET_DK, RET_DV), _F32),
                        pltpu.VMEM((MIX_ROWS + CONV_HALO, CONV_CH), _F32)],
        compiler_params=_params("arbitrary", "arbitrary"),
        name="l0_mixer",
    )(h, cos, sin, xi, zeta, decay, g_state, gn_g.reshape(1, -1), conv_w,
      conv_b.reshape(1, -1), cln_g.reshape(1, -1), cln_b.reshape(1, -1))


def _sb_attention_kernel(q_ref, k_ref, v_ref, tri_ref, o_ref):
    blk = SB_BLOCK
    qi = pl.program_id(2)
    q = q_ref[...]
    tri = tri_ref[...]
    scale = SB_HEAD_DIM ** -0.5

    def tile(j, carry, masked):
        start = pl.multiple_of(j * blk, blk)
        kb = k_ref[pl.ds(start, blk), :]
        vb = v_ref[pl.ds(start, blk), :]
        z = lax.dot_general(q, kb, (((1,), (1,)), ((), ())), preferred_element_type=_F32) * scale
        softplus = jnp.log(1.0 + jnp.exp(-jnp.abs(z)))
        log_beta = jnp.minimum(z, 0.0) - softplus
        log_rest = log_beta - z
        if masked:
            t_idx = lax.broadcasted_iota(jnp.int32, (blk, blk), 0)
            s_idx = lax.broadcasted_iota(jnp.int32, (blk, blk), 1)
            keep = s_idx < t_idx
            log_rest = jnp.where(keep, log_rest, 0.0)
        hi = log_rest.astype(_BF16)
        lo = (log_rest - hi.astype(_F32)).astype(_BF16)
        later = jnp.dot(jnp.concatenate([hi, lo], axis=1), tri, preferred_element_type=_F32)
        a = jnp.exp(log_beta + later + carry)
        if masked:
            a = jnp.where(keep, a, 0.0)
        out = jnp.dot(a.astype(_BF16), vb, preferred_element_type=_F32)
        return out, carry + jnp.sum(log_rest, axis=1, keepdims=True)

    acc, carry = tile(qi, jnp.zeros((blk, 1), _F32), True)

    def cond(state):
        j, _, carry = state
        return jnp.logical_and(j >= 0, jnp.max(carry) > -SB_CUTOFF)

    def body(state):
        j, acc, carry = state
        out, carry = tile(j, carry, False)
        return j - 1, acc + out, carry

    _, acc, _ = lax.while_loop(cond, body, (qi - 1, acc, carry))
    o_ref[...] = acc.astype(o_ref.dtype)


def _sb_attention(qkv, batch, seq):
    n = qkv.shape[0]
    nq = seq // SB_BLOCK
    j = jnp.arange(2 * SB_BLOCK)[:, None] % SB_BLOCK
    s = jnp.arange(SB_BLOCK)[None, :]
    tri = (j > s).astype(_BF16)
    return pl.pallas_call(
        _sb_attention_kernel,
        grid=(batch, SB_HEADS, nq),
        in_specs=[pl.BlockSpec((SB_BLOCK, SB_HEAD_DIM), lambda b, h, i: (b * nq + i, h)),
                  pl.BlockSpec((seq, SB_HEAD_DIM), lambda b, h, i: (b, SB_HEADS + h)),
                  pl.BlockSpec((seq, SB_HEAD_DIM), lambda b, h, i: (b, 2 * SB_HEADS + h)),
                  _const_spec((2 * SB_BLOCK, SB_BLOCK))],
        out_specs=pl.BlockSpec((SB_BLOCK, SB_HEAD_DIM), lambda b, h, i: (b * nq + i, h)),
        out_shape=jax.ShapeDtypeStruct((n, D_MODEL), _BF16),
        compiler_params=_params("arbitrary", "arbitrary", "arbitrary"),
        name="sb_attention",
    )(qkv, qkv, qkv, tri)


def _router_kernel(x_ref, w_ref, b_ref, idx_ref, gate_ref):
    x = x_ref[...]
    w = w_ref[...]
    x_hi = x.astype(_BF16)
    x_lo = (x - x_hi.astype(_F32)).astype(_BF16)
    w_hi = w.astype(_BF16)
    w_lo = (w - w_hi.astype(_F32)).astype(_BF16)
    nt = (((1,), (1,)), ((), ()))
    logits = (lax.dot_general(w_hi, x_hi, nt, preferred_element_type=_F32)
              + lax.dot_general(w_lo, x_hi, nt, preferred_element_type=_F32)
              + lax.dot_general(w_hi, x_lo, nt, preferred_element_type=_F32)) + b_ref[...]
    expert = lax.broadcasted_iota(jnp.int32, logits.shape, 0)
    vals, idxs = [], []
    for _ in range(TOP_K):
        m = jnp.max(logits, axis=0, keepdims=True)
        sel = jnp.min(jnp.where(logits == m, expert, N_EXPERTS), axis=0, keepdims=True)
        vals.append(m)
        idxs.append(sel)
        logits = jnp.where(expert == sel, -jnp.inf, logits)
    vals = jnp.concatenate(vals, axis=0)
    e = jnp.exp(vals - vals[0:1])
    gate_ref[...] = e / jnp.sum(e, axis=0, keepdims=True)
    idx_ref[...] = jnp.concatenate(idxs, axis=0)


def _router(x, router_w, router_b):
    n, d = x.shape
    return pl.pallas_call(
        _router_kernel,
        grid=(n // ROUTER_ROWS,),
        in_specs=[pl.BlockSpec((ROUTER_ROWS, d), lambda i: (i, 0)),
                  _const_spec((N_EXPERTS, d)), _const_spec((N_EXPERTS, 1))],
        out_specs=[pl.BlockSpec((TOP_K, ROUTER_ROWS), lambda i: (0, i)),
                   pl.BlockSpec((TOP_K, ROUTER_ROWS), lambda i: (0, i))],
        out_shape=[jax.ShapeDtypeStruct((TOP_K, n), jnp.int32),
                   jax.ShapeDtypeStruct((TOP_K, n), _F32)],
        compiler_params=_params("arbitrary"),
        name="router",
    )(x, router_w.T, router_b.reshape(N_EXPERTS, 1))


def _moe_ffn_kernel(block_e_ref, n_used_ref, x_ref, gate_ref, wg_ref, wl_ref, bg_ref, bl_ref,
                    wd_ref, bd_ref, o_ref):
    i = pl.program_id(0)

    @pl.when(i < n_used_ref[0])
    def _():
        x = x_ref[...]
        h_glu = jnp.dot(x, wg_ref[0], preferred_element_type=_F32) + bg_ref[0]
        h_lin = jnp.dot(x, wl_ref[0], preferred_element_type=_F32) + bl_ref[0]
        h_glu = jnp.minimum(h_glu, SWIGLU_LIMIT)
        h_lin = jnp.clip(h_lin, -SWIGLU_LIMIT, SWIGLU_LIMIT)
        act = h_glu * jax.nn.sigmoid(SWIGLU_ALPHA * h_glu) * (h_lin + 1.0)
        y = jnp.dot(act.astype(_BF16), wd_ref[0], preferred_element_type=_F32) + bd_ref[0]
        o_ref[...] = y * gate_ref[...]

    @pl.when(i >= n_used_ref[0])
    def _():
        o_ref[...] = jnp.zeros_like(o_ref)


def _moe_ffn(xs, slot_gate, block_e, n_used, wg, wl, bg, bl, wd, bd):
    p, d = xs.shape
    n_blocks = p // MOE_ROWS
    ff = wg.shape[2]
    row = lambda i, be, nu: (i, 0)
    exp3 = lambda i, be, nu: (be[i], 0, 0)
    grid_spec = pltpu.PrefetchScalarGridSpec(
        num_scalar_prefetch=2,
        grid=(n_blocks,),
        in_specs=[pl.BlockSpec((MOE_ROWS, d), row), pl.BlockSpec((MOE_ROWS, 1), row),
                  pl.BlockSpec((1, d, ff), exp3), pl.BlockSpec((1, d, ff), exp3),
                  pl.BlockSpec((1, 1, ff), exp3), pl.BlockSpec((1, 1, ff), exp3),
                  pl.BlockSpec((1, ff, d), exp3), pl.BlockSpec((1, 1, d), exp3)],
        out_specs=pl.BlockSpec((MOE_ROWS, d), row),
    )
    return pl.pallas_call(
        _moe_ffn_kernel,
        grid_spec=grid_spec,
        out_shape=jax.ShapeDtypeStruct((p, d), _F32),
        compiler_params=_params("arbitrary"),
        name="moe_ffn",
    )(block_e, n_used, xs, slot_gate, wg, wl, bg, bl, wd, bd)


def _combine_ln_kernel(y_ref, res_ref, g_ref, b_ref, o_ref):
    moe = (y_ref[0] + y_ref[1]) + (y_ref[2] + y_ref[3])
    o_ref[...] = _layer_norm_rows(DEEPNORM_ALPHA * res_ref[...] + moe, g_ref[...], b_ref[...])


def _combine_ln(y4, res, g, b):
    n, d = res.shape
    row = lambda i: (i, 0)
    return pl.pallas_call(
        _combine_ln_kernel,
        grid=(n // COMBINE_ROWS,),
        in_specs=[pl.BlockSpec((TOP_K, COMBINE_ROWS, d), lambda i: (0, i, 0)),
                  pl.BlockSpec((COMBINE_ROWS, d), row), _const_spec((1, d)), _const_spec((1, d))],
        out_specs=pl.BlockSpec((COMBINE_ROWS, d), row),
        out_shape=jax.ShapeDtypeStruct((n, d), _F32),
        compiler_params=_params("arbitrary"),
        name="combine_ln",
    )(y4, res, g.reshape(1, d), b.reshape(1, d))


def _moe_layer(x, x_bf16, router_w, router_b, w_up, b_up, w_down, b_down, ln_g, ln_b):
    n, d = x.shape
    nk = n * TOP_K
    idx_t, gate_t = _router(x, router_w, router_b)
    flat_e = idx_t.reshape(nk)
    flat_g = gate_t.reshape(nk)
    order = jnp.argsort(flat_e).astype(jnp.int32)
    sorted_e = flat_e[order]
    counts = jnp.bincount(flat_e, length=N_EXPERTS).astype(jnp.int32)
    padded = (counts + MOE_ROWS - 1) // MOE_ROWS * MOE_ROWS
    starts = jnp.cumsum(counts) - counts
    padded_ends = jnp.cumsum(padded)
    padded_starts = padded_ends - padded
    n_blocks = -(-(nk + N_EXPERTS * (MOE_ROWS - 1)) // MOE_ROWS)
    p = n_blocks * MOE_ROWS
    dest = padded_starts[sorted_e] + jnp.arange(nk, dtype=jnp.int32) - starts[sorted_e]
    slot_of = jnp.zeros((nk,), jnp.int32).at[order].set(dest, unique_indices=True)
    block_e = jnp.minimum(
        jnp.searchsorted(padded_ends, jnp.arange(n_blocks, dtype=jnp.int32) * MOE_ROWS, side='right'),
        N_EXPERTS - 1).astype(jnp.int32)
    n_used = (padded_ends[-1] // MOE_ROWS).astype(jnp.int32).reshape(1)
    slot = jnp.arange(p, dtype=jnp.int32)
    slot_e = jnp.repeat(block_e, MOE_ROWS)
    within = slot - padded_starts[slot_e]
    valid = within < counts[slot_e]
    src = jnp.where(valid, order[jnp.clip(starts[slot_e] + within, 0, nk - 1)], 0)
    slot_tok = src % n
    slot_gate = jnp.where(valid, flat_g[src], 0.0).reshape(p, 1)

    xs = x_bf16[slot_tok]
    wg = w_up[:, :, 0::2].astype(_BF16)
    wl = w_up[:, :, 1::2].astype(_BF16)
    bg = b_up[:, 0::2].reshape(N_EXPERTS, 1, D_FF)
    bl = b_up[:, 1::2].reshape(N_EXPERTS, 1, D_FF)
    y = _moe_ffn(xs, slot_gate, block_e, n_used, wg, wl, bg, bl,
                 w_down.astype(_BF16), b_down.reshape(N_EXPERTS, 1, d))
    y4 = y[slot_of.reshape(TOP_K, n)]
    return _combine_ln(y4, x, ln_g, ln_b)


def kernel(x, l0_w_in, l0_ret_gn_g, l0_conv_w, l0_conv_b, l0_conv_ln_g, l0_conv_ln_b, l0_w_out, l0_ln1_g, l0_ln1_b, l0_router_w, l0_router_b, l0_w_up, l0_b_up, l0_w_down, l0_b_down, l0_ln2_g, l0_ln2_b, l1_w_in, l1_w_out, l1_ln1_g, l1_ln1_b, l1_router_w, l1_router_b, l1_w_up, l1_b_up, l1_w_down, l1_b_down, l1_ln2_g, l1_ln2_b):
    batch, seq, d = x.shape
    n = batch * seq
    x0 = x.reshape(n, d)

    h = _proj(x0, l0_w_in.astype(_BF16))
    mix = _l0_mixer(h, batch, seq, l0_ret_gn_g, l0_conv_w, l0_conv_b, l0_conv_ln_g, l0_conv_ln_b)
    x1, x1_bf16 = _proj_res_ln(mix, l0_w_out.astype(_BF16), x0, l0_ln1_g, l0_ln1_b)
    x2 = _moe_layer(x1, x1_bf16, l0_router_w, l0_router_b, l0_w_up, l0_b_up, l0_w_down, l0_b_down,
                    l0_ln2_g, l0_ln2_b)

    qkv = _proj(x2, l1_w_in.astype(_BF16))
    att = _sb_attention(qkv, batch, seq)
    x3, x3_bf16 = _proj_res_ln(att, l1_w_out.astype(_BF16), x2, l1_ln1_g, l1_ln1_b)
    x4 = _moe_layer(x3, x3_bf16, l1_router_w, l1_router_b, l1_w_up, l1_b_up, l1_w_down, l1_b_down,
                    l1_ln2_g, l1_ln2_b)
    return x4.reshape(batch, seq, d)
```

```python
import functools

import jax
import jax.numpy as jnp
from jax import lax
from jax.experimental import pallas as pl
from jax.experimental.pallas import tpu as pltpu

D_MODEL = 1024
CHUNK = 64
RET_HEADS = 4
RET_DK = 64
RET_DV = 128
RET_QK_WIDTH = RET_HEADS * RET_DK
RET_WIDTH = RET_HEADS * RET_DV
ROPE_BASE = 10000.0
CONV_CH = D_MODEL // 2
CONV_WIDTH = 31
L0_IN = 2 * RET_QK_WIDTH + 2 * RET_WIDTH + 2 * CONV_CH
SB_HEADS = 8
SB_HEAD_DIM = D_MODEL // SB_HEADS
N_EXPERTS = 32
TOP_K = 4
D_FF = D_MODEL
SWIGLU_LIMIT = 7.0
SWIGLU_ALPHA = 1.702
LN_EPS = 1e-5
DEPTH = 2
DEEPNORM_ALPHA = (2 * DEPTH) ** 0.25

VMEM_LIMIT_BYTES = 48 * 1024 * 1024
PROJ_ROWS = 512
MIX_ROWS = 256
CONV_HALO = 32
ROUTER_ROWS = 512
MOE_ROWS = 256
PREP_ROWS = 512
PREP_LANES = 128
COMBINE_ROWS = 256
SB_BLOCK = 128
SB_CUTOFF = 104.0

_F32 = jnp.float32
_BF16 = jnp.bfloat16


def _params(*sem):
    return pltpu.CompilerParams(dimension_semantics=sem, vmem_limit_bytes=VMEM_LIMIT_BYTES)


def _const_spec(shape):
    nd = len(shape)
    return pl.BlockSpec(shape, lambda *_: (0,) * nd)


def _layer_norm_rows(y, g, b):
    mu = jnp.mean(y, axis=-1, keepdims=True)
    yc = y - mu
    var = jnp.mean(yc * yc, axis=-1, keepdims=True)
    return yc * lax.rsqrt(var + LN_EPS) * g + b


def _silu(x):
    return x * jax.nn.sigmoid(x)


def _proj_kernel(x_ref, w_ref, o_ref):
    o_ref[...] = jnp.dot(x_ref[...].astype(_BF16), w_ref[...],
                         preferred_element_type=_F32).astype(o_ref.dtype)


def _proj(x, w_bf16):
    n, d = x.shape
    width = w_bf16.shape[1]
    return pl.pallas_call(
        _proj_kernel,
        grid=(n // PROJ_ROWS,),
        in_specs=[pl.BlockSpec((PROJ_ROWS, d), lambda i: (i, 0)), _const_spec((d, width))],
        out_specs=pl.BlockSpec((PROJ_ROWS, width), lambda i: (i, 0)),
        out_shape=jax.ShapeDtypeStruct((n, width), _BF16),
        compiler_params=_params("arbitrary"),
        name="proj",
    )(x, w_bf16)


def _proj_res_ln_kernel(a_ref, w_ref, res_ref, g_ref, b_ref, o_ref, obf_ref):
    m = jnp.dot(a_ref[...], w_ref[...], preferred_element_type=_F32)
    y = _layer_norm_rows(DEEPNORM_ALPHA * res_ref[...] + m, g_ref[...], b_ref[...])
    o_ref[...] = y
    obf_ref[...] = y.astype(_BF16)


def _proj_res_ln(a_bf16, w_bf16, res, g, b):
    n, d = res.shape
    k = a_bf16.shape[1]
    row = lambda i: (i, 0)
    return pl.pallas_call(
        _proj_res_ln_kernel,
        grid=(n // PROJ_ROWS,),
        in_specs=[pl.BlockSpec((PROJ_ROWS, k), row), _const_spec((k, d)),
                  pl.BlockSpec((PROJ_ROWS, d), row), _const_spec((1, d)), _const_spec((1, d))],
        out_specs=[pl.BlockSpec((PROJ_ROWS, d), row), pl.BlockSpec((PROJ_ROWS, d), row)],
        out_shape=[jax.ShapeDtypeStruct((n, d), _F32), jax.ShapeDtypeStruct((n, d), _BF16)],
        compiler_params=_params("arbitrary"),
        name="proj_res_ln",
    )(a_bf16, w_bf16, res, g.reshape(1, d), b.reshape(1, d))


def _l0_mixer_kernel(h_ref, cos_ref, sin_ref, xi_ref, zeta_ref, decay_ref, gstate_ref, gn_g_ref,
                     conv_w_ref, conv_b_ref, cln_g_ref, cln_b_ref, o_ref, state_ref, u_ref):
    rows = MIX_ROWS
    step = pl.program_id(1)

    @pl.when(step == 0)
    def _():
        state_ref[...] = jnp.zeros_like(state_ref)
        u_ref[0:CONV_HALO, :] = jnp.zeros((CONV_HALO, CONV_CH), _F32)

    lane = lax.broadcasted_iota(jnp.int32, (rows, RET_QK_WIDTH), 1)
    first_half = (lane % RET_DK) < (RET_DK // 2)
    cos = cos_ref[...]
    sin = sin_ref[...]

    def rotary(t):
        partner = jnp.where(first_half,
                            pltpu.roll(t, RET_QK_WIDTH - RET_DK // 2, axis=1),
                            pltpu.roll(t, RET_DK // 2, axis=1))
        return t * cos + partner * sin

    q = rotary(h_ref[:, 0:RET_QK_WIDTH].astype(_F32))
    k = rotary(h_ref[:, RET_QK_WIDTH:2 * RET_QK_WIDTH].astype(_F32)) * (RET_DK ** -0.5)
    q_in = (q * xi_ref[...]).astype(_BF16)
    k_out = (k * zeta_ref[...]).astype(_BF16)
    q = q.astype(_BF16)
    k = k.astype(_BF16)
    v_off = 2 * RET_QK_WIDTH
    g_off = v_off + RET_WIDTH
    for hd in range(RET_HEADS):
        qk = slice(hd * RET_DK, (hd + 1) * RET_DK)
        vs = slice(hd * RET_DV, (hd + 1) * RET_DV)
        v = h_ref[:, v_off + hd * RET_DV:v_off + (hd + 1) * RET_DV]
        s = lax.dot_general(q[:, qk], k[:, qk], (((1,), (1,)), ((), ())),
                            preferred_element_type=_F32) * decay_ref[hd]
        r = jnp.dot(s.astype(_BF16), v, preferred_element_type=_F32)
        r += jnp.dot(q_in[:, qk], state_ref[hd].astype(_BF16), preferred_element_type=_F32)
        kv = lax.dot_general(k_out[:, qk], v, (((0,), (0,)), ((), ())),
                             preferred_element_type=_F32)
        state_ref[hd] = gstate_ref[hd] * state_ref[hd] + kv
        mu = jnp.mean(r, axis=-1, keepdims=True)
        rc = r - mu
        var = jnp.mean(rc * rc, axis=-1, keepdims=True)
        rn = rc * lax.rsqrt(var + LN_EPS) * gn_g_ref[:, vs]
        gate = h_ref[:, g_off + hd * RET_DV:g_off + (hd + 1) * RET_DV].astype(_F32)
        o_ref[:, vs] = (_silu(gate) * rn).astype(o_ref.dtype)

    a_off = g_off + RET_WIDTH
    glu_a = h_ref[:, a_off:a_off + CONV_CH].astype(_F32)
    glu_b = h_ref[:, a_off + CONV_CH:a_off + 2 * CONV_CH].astype(_F32)
    u_ref[CONV_HALO:CONV_HALO + rows, :] = glu_a * jax.nn.sigmoid(glu_b)
    acc = jnp.broadcast_to(conv_b_ref[...], (rows, CONV_CH))
    first = CONV_HALO - (CONV_WIDTH - 1)
    for j in range(CONV_WIDTH):
        acc = acc + conv_w_ref[j:j + 1, :] * u_ref[first + j:first + j + rows, :]
    u_ref[0:CONV_HALO, :] = u_ref[rows:rows + CONV_HALO, :]
    c = _silu(_layer_norm_rows(acc, cln_g_ref[...], cln_b_ref[...]))
    o_ref[:, RET_WIDTH:RET_WIDTH + CONV_CH] = c.astype(o_ref.dtype)


def _l0_mixer_tables(seq):
    half = RET_DK // 2
    inv = ROPE_BASE ** (-jnp.arange(half, dtype=_F32) / half)
    ang = jnp.arange(seq).astype(_F32)[:, None] * inv[None, :]
    cos = jnp.tile(jnp.cos(ang), (1, 2 * RET_HEADS))
    sin = jnp.tile(jnp.concatenate([-jnp.sin(ang), jnp.sin(ang)], axis=1), (1, RET_HEADS))
    log_g = jnp.log(1.0 - 2.0 ** (-5.0 - jnp.arange(RET_HEADS, dtype=_F32)))
    n = jnp.arange(MIX_ROWS, dtype=_F32)
    diff = n[:, None] - n[None, :]
    chunk = jnp.arange(MIX_ROWS) // CHUNK
    expo = jnp.where(chunk[:, None] == chunk[None, :], jnp.abs(diff), diff)
    decay = jnp.where((chunk[:, None] >= chunk[None, :])[None],
                      jnp.exp(expo[None] * log_g[:, None, None]), 0.0)
    xi = jnp.repeat(jnp.exp((n[:, None] + 1.0) * log_g[None, :]), RET_DK, axis=1)
    zeta = jnp.repeat(jnp.exp((MIX_ROWS - 1.0 - n[:, None]) * log_g[None, :]), RET_DK, axis=1)
    g_state = jnp.exp(MIX_ROWS * log_g)
    return cos, sin, xi, zeta, decay, g_state


def _l0_mixer(h, batch, seq, gn_g, conv_w, conv_b, cln_g, cln_b):
    n = h.shape[0]
    steps = seq // MIX_ROWS
    cos, sin, xi, zeta, decay, g_state = _l0_mixer_tables(seq)
    row = lambda b, i: (b * steps + i, 0)
    pos = lambda b, i: (i, 0)
    return pl.pallas_call(
        _l0_mixer_kernel,
        grid=(batch, steps),
        in_specs=[pl.BlockSpec((MIX_ROWS, L0_IN), row),
                  pl.BlockSpec((MIX_ROWS, RET_QK_WIDTH), pos),
                  pl.BlockSpec((MIX_ROWS, RET_QK_WIDTH), pos),
                  _const_spec((MIX_ROWS, RET_QK_WIDTH)), _const_spec((MIX_ROWS, RET_QK_WIDTH)),
                  _const_spec((RET_HEADS, MIX_ROWS, MIX_ROWS)),
                  pl.BlockSpec(memory_space=pltpu.SMEM),
                  _const_spec((1, RET_WIDTH)), _const_spec((CONV_WIDTH, CONV_CH)),
                  _const_spec((1, CONV_CH)), _const_spec((1, CONV_CH)), _const_spec((1, CONV_CH))],
        out_specs=pl.BlockSpec((MIX_ROWS, D_MODEL), row),
        out_shape=jax.ShapeDtypeStruct((n, D_MODEL), _BF16),
        scratch_shapes=[pltpu.VMEM((RET_HEADS, RET_DK, RET_DV), _F32),
                        pltpu.VMEM((MIX_ROWS + CONV_HALO, CONV_CH), _F32)],
        compiler_params=_params("arbitrary", "arbitrary"),
        name="l0_mixer",
    )(h, cos, sin, xi, zeta, decay, g_state, gn_g.reshape(1, -1), conv_w,
      conv_b.reshape(1, -1), cln_g.reshape(1, -1), cln_b.reshape(1, -1))


def _sb_attention_kernel(q_ref, k_ref, v_ref, tri_ref, o_ref, acc_ref, carry_ref):
    blk = SB_BLOCK
    qi = pl.program_id(1)
    scale = SB_HEAD_DIM ** -0.5
    t_idx = lax.broadcasted_iota(jnp.int32, (blk, blk), 0)
    s_idx = lax.broadcasted_iota(jnp.int32, (blk, blk), 1)
    keep = s_idx < t_idx

    def tile(hd, j, masked):
        hs = slice(hd * SB_HEAD_DIM, (hd + 1) * SB_HEAD_DIM)
        start = pl.multiple_of(j * blk, blk)
        kb = k_ref[pl.ds(start, blk), hs]
        vb = v_ref[pl.ds(start, blk), hs]
        z = lax.dot_general(q_ref[:, hs], kb, (((1,), (1,)), ((), ())),
                            preferred_element_type=_F32) * scale
        softplus = jnp.log(1.0 + jnp.exp(-jnp.abs(z)))
        log_beta = jnp.minimum(z, 0.0) - softplus
        log_rest = log_beta - z
        if masked:
            log_rest = jnp.where(keep, log_rest, 0.0)
        hi = log_rest.astype(_BF16)
        lo = (log_rest - hi.astype(_F32)).astype(_BF16)
        sums = jnp.dot(jnp.concatenate([hi, lo], axis=1), tri_ref[...], preferred_element_type=_F32)
        later, total = sums[:, :blk], sums[:, blk:]
        if masked:
            a = jnp.where(keep, jnp.exp(log_beta + later), 0.0)
            acc_ref[:, hs] = jnp.dot(a.astype(_BF16), vb, preferred_element_type=_F32)
            carry_ref[:, hs] = total
        else:
            a = jnp.exp(log_beta + later + carry_ref[:, hs])
            acc_ref[:, hs] += jnp.dot(a.astype(_BF16), vb, preferred_element_type=_F32)
            carry_ref[:, hs] += total

    def stick_left():
        return jnp.max(carry_ref[...]) > -SB_CUTOFF

    for hd in range(SB_HEADS):
        tile(hd, qi, True)

    def cond(state):
        j, alive = state
        return jnp.logical_and(j >= 0, alive)

    def body(state):
        j, _ = state
        for hd in range(SB_HEADS):
            tile(hd, j, False)
        return j - 1, stick_left()

    lax.while_loop(cond, body, (qi - 1, stick_left()))
    o_ref[...] = acc_ref[...].astype(o_ref.dtype)


def _sb_attention(qkv, batch, seq):
    n = qkv.shape[0]
    nq = seq // SB_BLOCK
    j = jnp.arange(2 * SB_BLOCK)[:, None] % SB_BLOCK
    s = jnp.arange(SB_BLOCK)[None, :]
    tri = jnp.concatenate([(j > s).astype(_BF16), jnp.ones((2 * SB_BLOCK, SB_BLOCK), _BF16)], axis=1)
    whole_seq = lambda col: pl.BlockSpec((seq, D_MODEL), lambda b, i: (b, col),
                                         pipeline_mode=pl.Buffered(1))
    return pl.pallas_call(
        _sb_attention_kernel,
        grid=(batch, nq),
        in_specs=[pl.BlockSpec((SB_BLOCK, D_MODEL), lambda b, i: (b * nq + i, 0)),
                  whole_seq(1), whole_seq(2), _const_spec((2 * SB_BLOCK, 2 * SB_BLOCK))],
        out_specs=pl.BlockSpec((SB_BLOCK, D_MODEL), lambda b, i: (b * nq + i, 0)),
        out_shape=jax.ShapeDtypeStruct((n, D_MODEL), _BF16),
        scratch_shapes=[pltpu.VMEM((SB_BLOCK, D_MODEL), _F32), pltpu.VMEM((SB_BLOCK, D_MODEL), _F32)],
        compiler_params=_params("arbitrary", "arbitrary"),
        name="sb_attention",
    )(qkv, qkv, qkv, tri)


def _router_kernel(x_ref, w_ref, b_ref, tri_ref, idx_ref, gate_ref, rank_ref, count_ref, base_ref):
    @pl.when(pl.program_id(0) == 0)
    def _():
        base_ref[...] = jnp.zeros_like(base_ref)

    x = x_ref[...]
    w = w_ref[...]
    x_hi = x.astype(_BF16)
    x_lo = (x - x_hi.astype(_F32)).astype(_BF16)
    w_hi = w.astype(_BF16)
    w_lo = (w - w_hi.astype(_F32)).astype(_BF16)
    nt = (((1,), (1,)), ((), ()))
    logits = (lax.dot_general(w_hi, x_hi, nt, preferred_element_type=_F32)
              + lax.dot_general(w_lo, x_hi, nt, preferred_element_type=_F32)
              + lax.dot_general(w_hi, x_lo, nt, preferred_element_type=_F32)) + b_ref[...]
    expert = lax.broadcasted_iota(jnp.int32, logits.shape, 0)
    vals, idxs = [], []
    for _ in range(TOP_K):
        m = jnp.max(logits, axis=0, keepdims=True)
        sel = jnp.min(jnp.where(logits == m, expert, N_EXPERTS), axis=0, keepdims=True)
        vals.append(m)
        idxs.append(sel)
        logits = jnp.where(expert == sel, -jnp.inf, logits)
    vals = jnp.concatenate(vals, axis=0)
    e = jnp.exp(vals - vals[0:1])
    gate_ref[...] = e / jnp.sum(e, axis=0, keepdims=True)
    idx_ref[...] = jnp.concatenate(idxs, axis=0)

    member = jnp.zeros(logits.shape, _F32)
    for sel in idxs:
        member += (expert == sel).astype(_F32)
    before = jnp.dot(member.astype(_BF16), tri_ref[...], preferred_element_type=_F32) + base_ref[...]
    ranks = [jnp.sum(jnp.where(expert == sel, before, 0.0), axis=0, keepdims=True) for sel in idxs]
    rank_ref[...] = jnp.concatenate(ranks, axis=0).astype(jnp.int32)
    base_ref[...] += jnp.sum(member, axis=1, keepdims=True)
    count_ref[...] = base_ref[...]


def _router(x, router_w, router_b):
    n, d = x.shape
    t = jnp.arange(ROUTER_ROWS)
    tri = (t[:, None] < t[None, :]).astype(_BF16)
    tok = pl.BlockSpec((TOP_K, ROUTER_ROWS), lambda i: (0, i))
    return pl.pallas_call(
        _router_kernel,
        grid=(n // ROUTER_ROWS,),
        in_specs=[pl.BlockSpec((ROUTER_ROWS, d), lambda i: (i, 0)),
                  _const_spec((N_EXPERTS, d)), _const_spec((N_EXPERTS, 1)),
                  _const_spec((ROUTER_ROWS, ROUTER_ROWS))],
        out_specs=[tok, tok, tok, _const_spec((N_EXPERTS, 1))],
        out_shape=[jax.ShapeDtypeStruct((TOP_K, n), jnp.int32),
                   jax.ShapeDtypeStruct((TOP_K, n), _F32),
                   jax.ShapeDtypeStruct((TOP_K, n), jnp.int32),
                   jax.ShapeDtypeStruct((N_EXPERTS, 1), _F32)],
        scratch_shapes=[pltpu.VMEM((N_EXPERTS, 1), _F32)],
        compiler_params=_params("arbitrary"),
        name="router",
    )(x, router_w.T, router_b.reshape(N_EXPERTS, 1), tri)


def _prep_up_kernel(w_ref, sel_ref, o_ref):
    group = 2 * PREP_LANES
    for c in range(w_ref.shape[2] // group):
        t = jnp.dot(w_ref[0, :, c * group:(c + 1) * group].astype(_BF16), sel_ref[...],
                    preferred_element_type=_F32)
        o_ref[0, :, c * PREP_LANES:(c + 1) * PREP_LANES] = t[:, :PREP_LANES].astype(_BF16)
        o_ref[0, :, D_FF + c * PREP_LANES:D_FF + (c + 1) * PREP_LANES] = t[:, PREP_LANES:].astype(_BF16)


def _prep_up(w_up):
    e, d, ff2 = w_up.shape
    i = jnp.arange(2 * PREP_LANES)[:, None]
    j = jnp.arange(2 * PREP_LANES)[None, :]
    sel = (i == jnp.where(j < PREP_LANES, 2 * j, 2 * (j - PREP_LANES) + 1)).astype(_BF16)
    blk = lambda x, r: (x, r, 0)
    return pl.pallas_call(
        _prep_up_kernel,
        grid=(e, d // PREP_ROWS),
        in_specs=[pl.BlockSpec((1, PREP_ROWS, ff2), blk), _const_spec((2 * PREP_LANES, 2 * PREP_LANES))],
        out_specs=pl.BlockSpec((1, PREP_ROWS, ff2), blk),
        out_shape=jax.ShapeDtypeStruct((e, d, ff2), _BF16),
        compiler_params=_params("arbitrary", "arbitrary"),
        name="prep_up",
    )(w_up, sel)


def _cast_kernel(w_ref, o_ref):
    o_ref[...] = w_ref[...].astype(o_ref.dtype)


def _prep_down(w_down):
    e, ff, d = w_down.shape
    blk = lambda x: (x, 0, 0)
    return pl.pallas_call(
        _cast_kernel,
        grid=(e,),
        in_specs=[pl.BlockSpec((1, ff, d), blk)],
        out_specs=pl.BlockSpec((1, ff, d), blk),
        out_shape=jax.ShapeDtypeStruct((e, ff, d), _BF16),
        compiler_params=_params("arbitrary"),
        name="prep_down",
    )(w_down)


def _moe_ffn_kernel(block_e_ref, n_used_ref, x_ref, wu_ref, bu_ref, wd_ref, bd_ref, o_ref):
    i = pl.program_id(0)

    @pl.when(i < n_used_ref[0])
    def _():
        h = jnp.dot(x_ref[...], wu_ref[0], preferred_element_type=_F32) + bu_ref[0]
        h_glu = jnp.minimum(h[:, :D_FF], SWIGLU_LIMIT)
        h_lin = jnp.clip(h[:, D_FF:], -SWIGLU_LIMIT, SWIGLU_LIMIT)
        act = h_glu * jax.nn.sigmoid(SWIGLU_ALPHA * h_glu) * (h_lin + 1.0)
        y = jnp.dot(act.astype(_BF16), wd_ref[0], preferred_element_type=_F32) + bd_ref[0]
        o_ref[...] = y.astype(o_ref.dtype)

    @pl.when(i >= n_used_ref[0])
    def _():
        o_ref[...] = jnp.zeros_like(o_ref)


def _moe_ffn(xs, block_e, n_used, wu, bu, wd, bd):
    p, d = xs.shape
    n_blocks = p // MOE_ROWS
    ff2 = wu.shape[2]
    row = lambda i, be, nu: (i, 0)
    exp3 = lambda i, be, nu: (be[i], 0, 0)
    grid_spec = pltpu.PrefetchScalarGridSpec(
        num_scalar_prefetch=2,
        grid=(n_blocks,),
        in_specs=[pl.BlockSpec((MOE_ROWS, d), row),
                  pl.BlockSpec((1, d, ff2), exp3), pl.BlockSpec((1, 1, ff2), exp3),
                  pl.BlockSpec((1, ff2 // 2, d), exp3), pl.BlockSpec((1, 1, d), exp3)],
        out_specs=pl.BlockSpec((MOE_ROWS, d), row),
    )
    return pl.pallas_call(
        _moe_ffn_kernel,
        grid_spec=grid_spec,
        out_shape=jax.ShapeDtypeStruct((p, d), _BF16),
        compiler_params=_params("arbitrary"),
        name="moe_ffn",
    )(block_e, n_used, xs, wu, bu, wd, bd)


def _combine_ln_kernel(y_ref, gate_ref, res_ref, g_ref, b_ref, o_ref, obf_ref):
    gate = gate_ref[...]
    moe = y_ref[0].astype(_F32) * gate[:, 0:1]
    for k in range(1, TOP_K):
        moe += y_ref[k].astype(_F32) * gate[:, k:k + 1]
    y = _layer_norm_rows(DEEPNORM_ALPHA * res_ref[...] + moe, g_ref[...], b_ref[...])
    o_ref[...] = y
    obf_ref[...] = y.astype(_BF16)


def _combine_ln(y4, gates, res, g, b):
    n, d = res.shape
    row = lambda i: (i, 0)
    return pl.pallas_call(
        _combine_ln_kernel,
        grid=(n // COMBINE_ROWS,),
        in_specs=[pl.BlockSpec((TOP_K, COMBINE_ROWS, d), lambda i: (0, i, 0)),
                  pl.BlockSpec((COMBINE_ROWS, TOP_K), row),
                  pl.BlockSpec((COMBINE_ROWS, d), row), _const_spec((1, d)), _const_spec((1, d))],
        out_specs=[pl.BlockSpec((COMBINE_ROWS, d), row), pl.BlockSpec((COMBINE_ROWS, d), row)],
        out_shape=[jax.ShapeDtypeStruct((n, d), _F32), jax.ShapeDtypeStruct((n, d), _BF16)],
        compiler_params=_params("arbitrary"),
        name="combine_ln",
    )(y4, gates, res, g.reshape(1, d), b.reshape(1, d))


def _moe_layer(x, x_bf16, router_w, router_b, w_up, b_up, w_down, b_down, ln_g, ln_b):
    n, d = x.shape
    nk = n * TOP_K
    idx_t, gate_t, rank_t, count = _router(x, router_w, router_b)
    counts = count[:, 0].astype(jnp.int32)
    padded = (counts + MOE_ROWS - 1) // MOE_ROWS * MOE_ROWS
    padded_ends = jnp.cumsum(padded)
    padded_starts = padded_ends - padded
    n_blocks = -(-(nk + N_EXPERTS * (MOE_ROWS - 1)) // MOE_ROWS)
    p = n_blocks * MOE_ROWS
    experts = jnp.arange(N_EXPERTS, dtype=jnp.int32)
    start_of = jnp.sum(jnp.where(idx_t[None] == experts[:, None, None],
                                 padded_starts[:, None, None], 0), axis=0)
    slot_of = (start_of + rank_t).reshape(nk)
    block_start = jnp.arange(n_blocks, dtype=jnp.int32) * MOE_ROWS
    block_e = jnp.minimum(jnp.sum(padded_ends[None, :] <= block_start[:, None], axis=1),
                          N_EXPERTS - 1).astype(jnp.int32)
    n_used = (padded_ends[-1] // MOE_ROWS).astype(jnp.int32).reshape(1)
    tok = jnp.tile(jnp.arange(n, dtype=jnp.int32), TOP_K)
    slot_tok = jnp.zeros((p,), jnp.int32).at[slot_of].set(tok)

    xs = x_bf16[slot_tok]
    bu = jnp.concatenate([b_up[:, 0::2], b_up[:, 1::2]], axis=1).reshape(N_EXPERTS, 1, 2 * D_FF)
    y = _moe_ffn(xs, block_e, n_used, _prep_up(w_up), bu, _prep_down(w_down),
                 b_down.reshape(N_EXPERTS, 1, d))
    y4 = y[slot_of].reshape(TOP_K, n, d)
    return _combine_ln(y4, gate_t.T, x, ln_g, ln_b)


def kernel(x, l0_w_in, l0_ret_gn_g, l0_conv_w, l0_conv_b, l0_conv_ln_g, l0_conv_ln_b, l0_w_out, l0_ln1_g, l0_ln1_b, l0_router_w, l0_router_b, l0_w_up, l0_b_up, l0_w_down, l0_b_down, l0_ln2_g, l0_ln2_b, l1_w_in, l1_w_out, l1_ln1_g, l1_ln1_b, l1_router_w, l1_router_b, l1_w_up, l1_b_up, l1_w_down, l1_b_down, l1_ln2_g, l1_ln2_b):
    batch, seq, d = x.shape
    n = batch * seq
    x0 = x.reshape(n, d)

    h = _proj(x0, l0_w_in.astype(_BF16))
    mix = _l0_mixer(h, batch, seq, l0_ret_gn_g, l0_conv_w, l0_conv_b, l0_conv_ln_g, l0_conv_ln_b)
    x1, x1_bf16 = _proj_res_ln(mix, l0_w_out.astype(_BF16), x0, l0_ln1_g, l0_ln1_b)
    x2, x2_bf16 = _moe_layer(x1, x1_bf16, l0_router_w, l0_router_b, l0_w_up, l0_b_up, l0_w_down,
                             l0_b_down, l0_ln2_g, l0_ln2_b)

    qkv = _proj(x2_bf16, l1_w_in.astype(_BF16))
    att = _sb_attention(qkv, batch, seq)
    x3, x3_bf16 = _proj_res_ln(att, l1_w_out.astype(_BF16), x2, l1_ln1_g, l1_ln1_b)
    x4, _ = _moe_layer(x3, x3_bf16, l1_router_w, l1_router_b, l1_w_up, l1_b_up, l1_w_down,
                       l1_b_down, l1_ln2_g, l1_ln2_b)
    return x4.reshape(batch, seq, d)
```

```python
import functools

import jax
import jax.numpy as jnp
from jax import lax
from jax.experimental import pallas as pl
from jax.experimental.pallas import tpu as pltpu
from jax.experimental.pallas import tpu_sc as plsc

D_MODEL = 1024
CHUNK = 64
RET_HEADS = 4
RET_DK = 64
RET_DV = 128
RET_QK_WIDTH = RET_HEADS * RET_DK
RET_WIDTH = RET_HEADS * RET_DV
ROPE_BASE = 10000.0
CONV_CH = D_MODEL // 2
CONV_WIDTH = 31
L0_IN = 2 * RET_QK_WIDTH + 2 * RET_WIDTH + 2 * CONV_CH
SB_HEADS = 8
SB_HEAD_DIM = D_MODEL // SB_HEADS
N_EXPERTS = 32
TOP_K = 4
D_FF = D_MODEL
SWIGLU_LIMIT = 7.0
SWIGLU_ALPHA = 1.702
LN_EPS = 1e-5
DEPTH = 2
DEEPNORM_ALPHA = (2 * DEPTH) ** 0.25

VMEM_LIMIT_BYTES = 48 * 1024 * 1024
PROJ_ROWS = 512
MIX_ROWS = 256
CONV_HALO = 32
ROUTER_ROWS = 512
MOE_ROWS = 256
PREP_ROWS = 512
PREP_LANES = 128
COMBINE_ROWS = 256
DISPATCH_ROWS = 128
DISPATCH_PARTS = 2
SC_CORES = 2
SC_SUBCORES = 16
SB_BLOCK = 128
SB_CUTOFF = 104.0

_F32 = jnp.float32
_BF16 = jnp.bfloat16


def _params(*sem):
    return pltpu.CompilerParams(dimension_semantics=sem, vmem_limit_bytes=VMEM_LIMIT_BYTES)


def _const_spec(shape):
    nd = len(shape)
    return pl.BlockSpec(shape, lambda *_: (0,) * nd)


def _layer_norm_rows(y, g, b):
    mu = jnp.mean(y, axis=-1, keepdims=True)
    yc = y - mu
    var = jnp.mean(yc * yc, axis=-1, keepdims=True)
    return yc * lax.rsqrt(var + LN_EPS) * g + b


def _silu(x):
    return x * jax.nn.sigmoid(x)


def _pack_bf16_pairs(y):
    half = y.shape[1] // 2
    lo = lax.bitcast_convert_type(y[:, :half].astype(_BF16).astype(_F32), jnp.uint32)
    hi = lax.bitcast_convert_type(y[:, half:].astype(_BF16).astype(_F32), jnp.uint32)
    return (lo >> 16) | (hi & jnp.uint32(0xFFFF0000))


def _unpack_bf16_pairs(w):
    lo = lax.bitcast_convert_type(w << 16, _F32)
    hi = lax.bitcast_convert_type(w & jnp.uint32(0xFFFF0000), _F32)
    return jnp.concatenate([lo, hi], axis=1).astype(_BF16)


def _proj_kernel(x_ref, w_ref, o_ref):
    o_ref[...] = jnp.dot(x_ref[...].astype(_BF16), w_ref[...],
                         preferred_element_type=_F32).astype(o_ref.dtype)


def _proj(x, w_bf16):
    n, d = x.shape
    width = w_bf16.shape[1]
    return pl.pallas_call(
        _proj_kernel,
        grid=(n // PROJ_ROWS,),
        in_specs=[pl.BlockSpec((PROJ_ROWS, d), lambda i: (i, 0)), _const_spec((d, width))],
        out_specs=pl.BlockSpec((PROJ_ROWS, width), lambda i: (i, 0)),
        out_shape=jax.ShapeDtypeStruct((n, width), _BF16),
        compiler_params=_params("arbitrary"),
        name="proj",
    )(x, w_bf16)


def _proj_res_ln_kernel(a_ref, w_ref, res_ref, g_ref, b_ref, o_ref, words_ref):
    m = jnp.dot(a_ref[...], w_ref[...], preferred_element_type=_F32)
    y = _layer_norm_rows(DEEPNORM_ALPHA * res_ref[...] + m, g_ref[...], b_ref[...])
    o_ref[...] = y
    words_ref[...] = _pack_bf16_pairs(y)


def _proj_res_ln(a_bf16, w_bf16, res, g, b):
    n, d = res.shape
    k = a_bf16.shape[1]
    row = lambda i: (i, 0)
    return pl.pallas_call(
        _proj_res_ln_kernel,
        grid=(n // PROJ_ROWS,),
        in_specs=[pl.BlockSpec((PROJ_ROWS, k), row), _const_spec((k, d)),
                  pl.BlockSpec((PROJ_ROWS, d), row), _const_spec((1, d)), _const_spec((1, d))],
        out_specs=[pl.BlockSpec((PROJ_ROWS, d), row), pl.BlockSpec((PROJ_ROWS, d // 2), row)],
        out_shape=[jax.ShapeDtypeStruct((n, d), _F32), jax.ShapeDtypeStruct((n, d // 2), jnp.uint32)],
        compiler_params=_params("arbitrary"),
        name="proj_res_ln",
    )(a_bf16, w_bf16, res, g.reshape(1, d), b.reshape(1, d))


def _l0_mixer_kernel(h_ref, cos_ref, sin_ref, xi_ref, zeta_ref, decay_ref, gstate_ref, gn_g_ref,
                     conv_w_ref, conv_b_ref, cln_g_ref, cln_b_ref, o_ref, state_ref, u_ref):
    rows = MIX_ROWS
    step = pl.program_id(1)

    @pl.when(step == 0)
    def _():
        state_ref[...] = jnp.zeros_like(state_ref)
        u_ref[0:CONV_HALO, :] = jnp.zeros((CONV_HALO, CONV_CH), _F32)

    lane = lax.broadcasted_iota(jnp.int32, (rows, RET_QK_WIDTH), 1)
    first_half = (lane % RET_DK) < (RET_DK // 2)
    cos = cos_ref[...]
    sin = sin_ref[...]

    def rotary(t):
        partner = jnp.where(first_half,
                            pltpu.roll(t, RET_QK_WIDTH - RET_DK // 2, axis=1),
                            pltpu.roll(t, RET_DK // 2, axis=1))
        return t * cos + partner * sin

    q = rotary(h_ref[:, 0:RET_QK_WIDTH].astype(_F32))
    k = rotary(h_ref[:, RET_QK_WIDTH:2 * RET_QK_WIDTH].astype(_F32)) * (RET_DK ** -0.5)
    q_in = (q * xi_ref[...]).astype(_BF16)
    k_out = (k * zeta_ref[...]).astype(_BF16)
    q = q.astype(_BF16)
    k = k.astype(_BF16)
    v_off = 2 * RET_QK_WIDTH
    g_off = v_off + RET_WIDTH
    for hd in range(RET_HEADS):
        qk = slice(hd * RET_DK, (hd + 1) * RET_DK)
        vs = slice(hd * RET_DV, (hd + 1) * RET_DV)
        v = h_ref[:, v_off + hd * RET_DV:v_off + (hd + 1) * RET_DV]
        s = lax.dot_general(q[:, qk], k[:, qk], (((1,), (1,)), ((), ())),
                            preferred_element_type=_F32) * decay_ref[hd]
        r = jnp.dot(s.astype(_BF16), v, preferred_element_type=_F32)
        r += jnp.dot(q_in[:, qk], state_ref[hd].astype(_BF16), preferred_element_type=_F32)
        kv = lax.dot_general(k_out[:, qk], v, (((0,), (0,)), ((), ())),
                             preferred_element_type=_F32)
        state_ref[hd] = gstate_ref[hd] * state_ref[hd] + kv
        mu = jnp.mean(r, axis=-1, keepdims=True)
        rc = r - mu
        var = jnp.mean(rc * rc, axis=-1, keepdims=True)
        rn = rc * lax.rsqrt(var + LN_EPS) * gn_g_ref[:, vs]
        gate = h_ref[:, g_off + hd * RET_DV:g_off + (hd + 1) * RET_DV].astype(_F32)
        o_ref[:, vs] = (_silu(gate) * rn).astype(o_ref.dtype)

    a_off = g_off + RET_WIDTH
    glu_a = h_ref[:, a_off:a_off + CONV_CH].astype(_F32)
    glu_b = h_ref[:, a_off + CONV_CH:a_off + 2 * CONV_CH].astype(_F32)
    u_ref[CONV_HALO:CONV_HALO + rows, :] = glu_a * jax.nn.sigmoid(glu_b)
    acc = jnp.broadcast_to(conv_b_ref[...], (rows, CONV_CH))
    first = CONV_HALO - (CONV_WIDTH - 1)
    for j in range(CONV_WIDTH):
        acc = acc + conv_w_ref[j:j + 1, :] * u_ref[first + j:first + j + rows, :]
    u_ref[0:CONV_HALO, :] = u_ref[rows:rows + CONV_HALO, :]
    c = _silu(_layer_norm_rows(acc, cln_g_ref[...], cln_b_ref[...]))
    o_ref[:, RET_WIDTH:RET_WIDTH + CONV_CH] = c.astype(o_ref.dtype)


def _l0_mixer_tables(seq):
    half = RET_DK // 2
    inv = ROPE_BASE ** (-jnp.arange(half, dtype=_F32) / half)
    ang = jnp.arange(seq).astype(_F32)[:, None] * inv[None, :]
    cos = jnp.tile(jnp.cos(ang), (1, 2 * RET_HEADS))
    sin = jnp.tile(jnp.concatenate([-jnp.sin(ang), jnp.sin(ang)], axis=1), (1, RET_HEADS))
    log_g = jnp.log(1.0 - 2.0 ** (-5.0 - jnp.arange(RET_HEADS, dtype=_F32)))
    n = jnp.arange(MIX_ROWS, dtype=_F32)
    diff = n[:, None] - n[None, :]
    chunk = jnp.arange(MIX_ROWS) // CHUNK
    expo = jnp.where(chunk[:, None] == chunk[None, :], jnp.abs(diff), diff)
    decay = jnp.where((chunk[:, None] >= chunk[None, :])[None],
                      jnp.exp(expo[None] * log_g[:, None, None]), 0.0)
    xi = jnp.repeat(jnp.exp((n[:, None] + 1.0) * log_g[None, :]), RET_DK, axis=1)
    zeta = jnp.repeat(jnp.exp((MIX_ROWS - 1.0 - n[:, None]) * log_g[None, :]), RET_DK, axis=1)
    g_state = jnp.exp(MIX_ROWS * log_g)
    return cos, sin, xi, zeta, decay, g_state


def _l0_mixer(h, batch, seq, gn_g, conv_w, conv_b, cln_g, cln_b):
    n = h.shape[0]
    steps = seq // MIX_ROWS
    cos, sin, xi, zeta, decay, g_state = _l0_mixer_tables(seq)
    row = lambda b, i: (b * steps + i, 0)
    pos = lambda b, i: (i, 0)
    return pl.pallas_call(
        _l0_mixer_kernel,
        grid=(batch, steps),
        in_specs=[pl.BlockSpec((MIX_ROWS, L0_IN), row),
                  pl.BlockSpec((MIX_ROWS, RET_QK_WIDTH), pos),
                  pl.BlockSpec((MIX_ROWS, RET_QK_WIDTH), pos),
                  _const_spec((MIX_ROWS, RET_QK_WIDTH)), _const_spec((MIX_ROWS, RET_QK_WIDTH)),
                  _const_spec((RET_HEADS, MIX_ROWS, MIX_ROWS)),
                  pl.BlockSpec(memory_space=pltpu.SMEM),
                  _const_spec((1, RET_WIDTH)), _const_spec((CONV_WIDTH, CONV_CH)),
                  _const_spec((1, CONV_CH)), _const_spec((1, CONV_CH)), _const_spec((1, CONV_CH))],
        out_specs=pl.BlockSpec((MIX_ROWS, D_MODEL), row),
        out_shape=jax.ShapeDtypeStruct((n, D_MODEL), _BF16),
        scratch_shapes=[pltpu.VMEM((RET_HEADS, RET_DK, RET_DV), _F32),
                        pltpu.VMEM((MIX_ROWS + CONV_HALO, CONV_CH), _F32)],
        compiler_params=_params("arbitrary", "arbitrary"),
        name="l0_mixer",
    )(h, cos, sin, xi, zeta, decay, g_state, gn_g.reshape(1, -1), conv_w,
      conv_b.reshape(1, -1), cln_g.reshape(1, -1), cln_b.reshape(1, -1))


def _sb_attention_kernel(q_ref, k_ref, v_ref, tri_ref, o_ref, acc_ref, carry_ref):
    blk = SB_BLOCK
    qi = pl.program_id(1)
    scale = SB_HEAD_DIM ** -0.5
    t_idx = lax.broadcasted_iota(jnp.int32, (blk, blk), 0)
    s_idx = lax.broadcasted_iota(jnp.int32, (blk, blk), 1)
    keep = s_idx < t_idx

    def tile(hd, j, masked):
        hs = slice(hd * SB_HEAD_DIM, (hd + 1) * SB_HEAD_DIM)
        start = pl.multiple_of(j * blk, blk)
        kb = k_ref[pl.ds(start, blk), hs]
        vb = v_ref[pl.ds(start, blk), hs]
        z = lax.dot_general(q_ref[:, hs], kb, (((1,), (1,)), ((), ())),
                            preferred_element_type=_F32) * scale
        softplus = jnp.log(1.0 + jnp.exp(-jnp.abs(z)))
        log_beta = jnp.minimum(z, 0.0) - softplus
        log_rest = log_beta - z
        if masked:
            log_rest = jnp.where(keep, log_rest, 0.0)
        hi = log_rest.astype(_BF16)
        lo = (log_rest - hi.astype(_F32)).astype(_BF16)
        sums = jnp.dot(jnp.concatenate([hi, lo], axis=1), tri_ref[...], preferred_element_type=_F32)
        later, total = sums[:, :blk], sums[:, blk:]
        if masked:
            a = jnp.where(keep, jnp.exp(log_beta + later), 0.0)
            acc_ref[:, hs] = jnp.dot(a.astype(_BF16), vb, preferred_element_type=_F32)
            carry_ref[:, hs] = total
        else:
            a = jnp.exp(log_beta + later + carry_ref[:, hs])
            acc_ref[:, hs] += jnp.dot(a.astype(_BF16), vb, preferred_element_type=_F32)
            carry_ref[:, hs] += total

    def stick_left():
        return jnp.max(carry_ref[...]) > -SB_CUTOFF

    for hd in range(SB_HEADS):
        tile(hd, qi, True)

    def cond(state):
        j, alive = state
        return jnp.logical_and(j >= 0, alive)

    def body(state):
        j, _ = state
        for hd in range(SB_HEADS):
            tile(hd, j, False)
        return j - 1, stick_left()

    lax.while_loop(cond, body, (qi - 1, stick_left()))
    o_ref[...] = acc_ref[...].astype(o_ref.dtype)


def _sb_attention(qkv, batch, seq):
    n = qkv.shape[0]
    nq = seq // SB_BLOCK
    j = jnp.arange(2 * SB_BLOCK)[:, None] % SB_BLOCK
    s = jnp.arange(SB_BLOCK)[None, :]
    tri = jnp.concatenate([(j > s).astype(_BF16), jnp.ones((2 * SB_BLOCK, SB_BLOCK), _BF16)], axis=1)
    whole_seq = lambda col: pl.BlockSpec((seq, D_MODEL), lambda b, i: (b, col),
                                         pipeline_mode=pl.Buffered(1))
    return pl.pallas_call(
        _sb_attention_kernel,
        grid=(batch, nq),
        in_specs=[pl.BlockSpec((SB_BLOCK, D_MODEL), lambda b, i: (b * nq + i, 0)),
                  whole_seq(1), whole_seq(2), _const_spec((2 * SB_BLOCK, 2 * SB_BLOCK))],
        out_specs=pl.BlockSpec((SB_BLOCK, D_MODEL), lambda b, i: (b * nq + i, 0)),
        out_shape=jax.ShapeDtypeStruct((n, D_MODEL), _BF16),
        scratch_shapes=[pltpu.VMEM((SB_BLOCK, D_MODEL), _F32), pltpu.VMEM((SB_BLOCK, D_MODEL), _F32)],
        compiler_params=_params("arbitrary", "arbitrary"),
        name="sb_attention",
    )(qkv, qkv, qkv, tri)


def _router_kernel(x_ref, w_ref, b_ref, tri_ref, idx_ref, gate_ref, rank_ref, count_ref, base_ref):
    @pl.when(pl.program_id(0) == 0)
    def _():
        base_ref[...] = jnp.zeros_like(base_ref)

    x = x_ref[...]
    w = w_ref[...]
    x_hi = x.astype(_BF16)
    x_lo = (x - x_hi.astype(_F32)).astype(_BF16)
    w_hi = w.astype(_BF16)
    w_lo = (w - w_hi.astype(_F32)).astype(_BF16)
    nt = (((1,), (1,)), ((), ()))
    logits = (lax.dot_general(w_hi, x_hi, nt, preferred_element_type=_F32)
              + lax.dot_general(w_lo, x_hi, nt, preferred_element_type=_F32)
              + lax.dot_general(w_hi, x_lo, nt, preferred_element_type=_F32)) + b_ref[...]
    expert = lax.broadcasted_iota(jnp.int32, logits.shape, 0)
    vals, idxs = [], []
    for _ in range(TOP_K):
        m = jnp.max(logits, axis=0, keepdims=True)
        sel = jnp.min(jnp.where(logits == m, expert, N_EXPERTS), axis=0, keepdims=True)
        vals.append(m)
        idxs.append(sel)
        logits = jnp.where(expert == sel, -jnp.inf, logits)
    vals = jnp.concatenate(vals, axis=0)
    e = jnp.exp(vals - vals[0:1])
    gate_ref[...] = e / jnp.sum(e, axis=0, keepdims=True)
    idx_ref[...] = jnp.concatenate(idxs, axis=0)

    member = jnp.zeros(logits.shape, _F32)
    for sel in idxs:
        member += (expert == sel).astype(_F32)
    before = jnp.dot(member.astype(_BF16), tri_ref[...], preferred_element_type=_F32) + base_ref[...]
    ranks = [jnp.sum(jnp.where(expert == sel, before, 0.0), axis=0, keepdims=True) for sel in idxs]
    rank_ref[...] = jnp.concatenate(ranks, axis=0).astype(jnp.int32)
    base_ref[...] += jnp.sum(member, axis=1, keepdims=True)
    count_ref[...] = base_ref[...]


def _router(x, router_w, router_b):
    n, d = x.shape
    t = jnp.arange(ROUTER_ROWS)
    tri = (t[:, None] < t[None, :]).astype(_BF16)
    tok = pl.BlockSpec((TOP_K, ROUTER_ROWS), lambda i: (0, i))
    return pl.pallas_call(
        _router_kernel,
        grid=(n // ROUTER_ROWS,),
        in_specs=[pl.BlockSpec((ROUTER_ROWS, d), lambda i: (i, 0)),
                  _const_spec((N_EXPERTS, d)), _const_spec((N_EXPERTS, 1)),
                  _const_spec((ROUTER_ROWS, ROUTER_ROWS))],
        out_specs=[tok, tok, tok, _const_spec((N_EXPERTS, 1))],
        out_shape=[jax.ShapeDtypeStruct((TOP_K, n), jnp.int32),
                   jax.ShapeDtypeStruct((TOP_K, n), _F32),
                   jax.ShapeDtypeStruct((TOP_K, n), jnp.int32),
                   jax.ShapeDtypeStruct((N_EXPERTS, 1), _F32)],
        scratch_shapes=[pltpu.VMEM((N_EXPERTS, 1), _F32)],
        compiler_params=_params("arbitrary"),
        name="router",
    )(x, router_w.T, router_b.reshape(N_EXPERTS, 1), tri)


def _prep_up_kernel(w_ref, sel_ref, o_ref):
    group = 2 * PREP_LANES
    for c in range(w_ref.shape[2] // group):
        t = jnp.dot(w_ref[0, :, c * group:(c + 1) * group].astype(_BF16), sel_ref[...],
                    preferred_element_type=_F32)
        o_ref[0, :, c * PREP_LANES:(c + 1) * PREP_LANES] = t[:, :PREP_LANES].astype(_BF16)
        o_ref[0, :, D_FF + c * PREP_LANES:D_FF + (c + 1) * PREP_LANES] = t[:, PREP_LANES:].astype(_BF16)


def _prep_up(w_up):
    e, d, ff2 = w_up.shape
    i = jnp.arange(2 * PREP_LANES)[:, None]
    j = jnp.arange(2 * PREP_LANES)[None, :]
    sel = (i == jnp.where(j < PREP_LANES, 2 * j, 2 * (j - PREP_LANES) + 1)).astype(_BF16)
    blk = lambda x, r: (x, r, 0)
    return pl.pallas_call(
        _prep_up_kernel,
        grid=(e, d // PREP_ROWS),
        in_specs=[pl.BlockSpec((1, PREP_ROWS, ff2), blk), _const_spec((2 * PREP_LANES, 2 * PREP_LANES))],
        out_specs=pl.BlockSpec((1, PREP_ROWS, ff2), blk),
        out_shape=jax.ShapeDtypeStruct((e, d, ff2), _BF16),
        compiler_params=_params("arbitrary", "arbitrary"),
        name="prep_up",
    )(w_up, sel)


def _cast_kernel(w_ref, o_ref):
    o_ref[...] = w_ref[...].astype(o_ref.dtype)


def _prep_down(w_down):
    e, ff, d = w_down.shape
    blk = lambda x: (x, 0, 0)
    return pl.pallas_call(
        _cast_kernel,
        grid=(e,),
        in_specs=[pl.BlockSpec((1, ff, d), blk)],
        out_specs=pl.BlockSpec((1, ff, d), blk),
        out_shape=jax.ShapeDtypeStruct((e, ff, d), _BF16),
        compiler_params=_params("arbitrary"),
        name="prep_down",
    )(w_down)


def _dispatch(x_words, slot_of, p):
    n, words = x_words.shape
    nk = slot_of.shape[0]
    tiles = n // DISPATCH_ROWS
    part = words // DISPATCH_PARTS
    mesh = plsc.VectorSubcoreMesh(core_axis_name="core", subcore_axis_name="subcore",
                                  num_cores=SC_CORES, num_subcores=SC_SUBCORES)
    out = jax.ShapeDtypeStruct((p, part), x_words.dtype)

    @functools.partial(
        pl.kernel, out_type=[out] * DISPATCH_PARTS, mesh=mesh, scratch_types=[],
        compiler_params=pltpu.CompilerParams(use_tc_tiling_on_sc=True), name="moe_dispatch")
    def dispatch(x_hbm, slot_hbm, *o_hbm):
        for c in range(DISPATCH_PARTS):
            def body(x_vmem, slot_vmem, o_ref=o_hbm[c]):
                pltpu.sync_copy(x_vmem, o_ref.at[slot_vmem.at[0]])

            pltpu.emit_pipeline(
                body,
                grid=(nk // DISPATCH_ROWS,),
                in_specs=[pl.BlockSpec((DISPATCH_ROWS, part), lambda i, c=c: (i % tiles, c)),
                          pl.BlockSpec((1, DISPATCH_ROWS), lambda i: (0, i))],
                out_specs=[],
                core_axis_name=("core", "subcore"),
                dimension_semantics=(pltpu.PARALLEL,),
            )(x_hbm, slot_hbm)

    return dispatch(x_words, slot_of.reshape(1, nk))


def _moe_ffn_kernel(block_e_ref, n_used_ref, xa_ref, xb_ref, wu_ref, bu_ref, wd_ref, bd_ref, o_ref):
    i = pl.program_id(0)

    @pl.when(i < n_used_ref[0])
    def _():
        x = _unpack_bf16_pairs(jnp.concatenate([xa_ref[...], xb_ref[...]], axis=1))
        h = jnp.dot(x, wu_ref[0], preferred_element_type=_F32) + bu_ref[0]
        h_glu = jnp.minimum(h[:, :D_FF], SWIGLU_LIMIT)
        h_lin = jnp.clip(h[:, D_FF:], -SWIGLU_LIMIT, SWIGLU_LIMIT)
        act = h_glu * jax.nn.sigmoid(SWIGLU_ALPHA * h_glu) * (h_lin + 1.0)
        y = jnp.dot(act.astype(_BF16), wd_ref[0], preferred_element_type=_F32) + bd_ref[0]
        o_ref[...] = y.astype(o_ref.dtype)

    @pl.when(i >= n_used_ref[0])
    def _():
        o_ref[...] = jnp.zeros_like(o_ref)


def _moe_ffn(xs, block_e, n_used, wu, bu, wd, bd):
    xa, xb = xs
    p, part = xa.shape
    d = wd.shape[2]
    n_blocks = p // MOE_ROWS
    ff2 = wu.shape[2]
    row = lambda i, be, nu: (i, 0)
    exp3 = lambda i, be, nu: (be[i], 0, 0)
    grid_spec = pltpu.PrefetchScalarGridSpec(
        num_scalar_prefetch=2,
        grid=(n_blocks,),
        in_specs=[pl.BlockSpec((MOE_ROWS, part), row), pl.BlockSpec((MOE_ROWS, part), row),
                  pl.BlockSpec((1, d, ff2), exp3), pl.BlockSpec((1, 1, ff2), exp3),
                  pl.BlockSpec((1, ff2 // 2, d), exp3), pl.BlockSpec((1, 1, d), exp3)],
        out_specs=pl.BlockSpec((MOE_ROWS, d), row),
    )
    return pl.pallas_call(
        _moe_ffn_kernel,
        grid_spec=grid_spec,
        out_shape=jax.ShapeDtypeStruct((p, d), _BF16),
        compiler_params=_params("arbitrary"),
        name="moe_ffn",
    )(block_e, n_used, xa, xb, wu, bu, wd, bd)


def _combine_ln_kernel(y_ref, gate_ref, res_ref, g_ref, b_ref, o_ref, obf_ref):
    gate = gate_ref[...]
    moe = y_ref[0].astype(_F32) * gate[:, 0:1]
    for k in range(1, TOP_K):
        moe += y_ref[k].astype(_F32) * gate[:, k:k + 1]
    y = _layer_norm_rows(DEEPNORM_ALPHA * res_ref[...] + moe, g_ref[...], b_ref[...])
    o_ref[...] = y
    obf_ref[...] = y.astype(_BF16)


def _combine_ln(y4, gates, res, g, b):
    n, d = res.shape
    row = lambda i: (i, 0)
    return pl.pallas_call(
        _combine_ln_kernel,
        grid=(n // COMBINE_ROWS,),
        in_specs=[pl.BlockSpec((TOP_K, COMBINE_ROWS, d), lambda i: (0, i, 0)),
                  pl.BlockSpec((COMBINE_ROWS, TOP_K), row),
                  pl.BlockSpec((COMBINE_ROWS, d), row), _const_spec((1, d)), _const_spec((1, d))],
        out_specs=[pl.BlockSpec((COMBINE_ROWS, d), row), pl.BlockSpec((COMBINE_ROWS, d), row)],
        out_shape=[jax.ShapeDtypeStruct((n, d), _F32), jax.ShapeDtypeStruct((n, d), _BF16)],
        compiler_params=_params("arbitrary"),
        name="combine_ln",
    )(y4, gates, res, g.reshape(1, d), b.reshape(1, d))


def _moe_layer(x, x_words, router_w, router_b, w_up, b_up, w_down, b_down, ln_g, ln_b):
    n, d = x.shape
    nk = n * TOP_K
    idx_t, gate_t, rank_t, count = _router(x, router_w, router_b)
    counts = count[:, 0].astype(jnp.int32)
    padded = (counts + MOE_ROWS - 1) // MOE_ROWS * MOE_ROWS
    padded_ends = jnp.cumsum(padded)
    padded_starts = padded_ends - padded
    n_blocks = -(-(nk + N_EXPERTS * (MOE_ROWS - 1)) // MOE_ROWS)
    p = n_blocks * MOE_ROWS
    experts = jnp.arange(N_EXPERTS, dtype=jnp.int32)
    start_of = jnp.sum(jnp.where(idx_t[None] == experts[:, None, None],
                                 padded_starts[:, None, None], 0), axis=0)
    slot_of = (start_of + rank_t).reshape(nk)
    block_start = jnp.arange(n_blocks, dtype=jnp.int32) * MOE_ROWS
    block_e = jnp.minimum(jnp.sum(padded_ends[None, :] <= block_start[:, None], axis=1),
                          N_EXPERTS - 1).astype(jnp.int32)
    n_used = (padded_ends[-1] // MOE_ROWS).astype(jnp.int32).reshape(1)
    xs = _dispatch(x_words, slot_of, p)
    bu = jnp.concatenate([b_up[:, 0::2], b_up[:, 1::2]], axis=1).reshape(N_EXPERTS, 1, 2 * D_FF)
    y = _moe_ffn(xs, block_e, n_used, _prep_up(w_up), bu, _prep_down(w_down),
                 b_down.reshape(N_EXPERTS, 1, d))
    y4 = y[slot_of].reshape(TOP_K, n, d)
    return _combine_ln(y4, gate_t.T, x, ln_g, ln_b)


def kernel(x, l0_w_in, l0_ret_gn_g, l0_conv_w, l0_conv_b, l0_conv_ln_g, l0_conv_ln_b, l0_w_out, l0_ln1_g, l0_ln1_b, l0_router_w, l0_router_b, l0_w_up, l0_b_up, l0_w_down, l0_b_down, l0_ln2_g, l0_ln2_b, l1_w_in, l1_w_out, l1_ln1_g, l1_ln1_b, l1_router_w, l1_router_b, l1_w_up, l1_b_up, l1_w_down, l1_b_down, l1_ln2_g, l1_ln2_b):
    batch, seq, d = x.shape
    n = batch * seq
    x0 = x.reshape(n, d)

    h = _proj(x0, l0_w_in.astype(_BF16))
    mix = _l0_mixer(h, batch, seq, l0_ret_gn_g, l0_conv_w, l0_conv_b, l0_conv_ln_g, l0_conv_ln_b)
    x1, x1_words = _proj_res_ln(mix, l0_w_out.astype(_BF16), x0, l0_ln1_g, l0_ln1_b)
    x2, x2_bf16 = _moe_layer(x1, x1_words, l0_router_w, l0_router_b, l0_w_up, l0_b_up, l0_w_down,
                             l0_b_down, l0_ln2_g, l0_ln2_b)

    qkv = _proj(x2_bf16, l1_w_in.astype(_BF16))
    att = _sb_attention(qkv, batch, seq)
    x3, x3_words = _proj_res_ln(att, l1_w_out.astype(_BF16), x2, l1_ln1_g, l1_ln1_b)
    x4, _ = _moe_layer(x3, x3_words, l1_router_w, l1_router_b, l1_w_up, l1_b_up, l1_w_down,
                       l1_b_down, l1_ln2_g, l1_ln2_b)
    return x4.reshape(batch, seq, d)
```

```python
import functools

import jax
import jax.numpy as jnp
from jax import lax
from jax.experimental import pallas as pl
from jax.experimental.pallas import tpu as pltpu
from jax.experimental.pallas import tpu_sc as plsc

D_MODEL = 1024
CHUNK = 64
RET_HEADS = 4
RET_DK = 64
RET_DV = 128
RET_QK_WIDTH = RET_HEADS * RET_DK
RET_WIDTH = RET_HEADS * RET_DV
ROPE_BASE = 10000.0
CONV_CH = D_MODEL // 2
CONV_WIDTH = 31
L0_IN = 2 * RET_QK_WIDTH + 2 * RET_WIDTH + 2 * CONV_CH
SB_HEADS = 8
SB_HEAD_DIM = D_MODEL // SB_HEADS
N_EXPERTS = 32
TOP_K = 4
D_FF = D_MODEL
SWIGLU_LIMIT = 7.0
SWIGLU_ALPHA = 1.702
LN_EPS = 1e-5
DEPTH = 2
DEEPNORM_ALPHA = (2 * DEPTH) ** 0.25

VMEM_LIMIT_BYTES = 48 * 1024 * 1024
F32_SUBLANES = 8
PROJ_ROWS = 512
MIX_ROWS = 256
CONV_HALO = 32
ROUTER_ROWS = 512
MOE_ROWS = 256
PREP_ROWS = 512
PREP_LANES = 128
COMBINE_ROWS = 256
DISPATCH_ROWS = 128
DISPATCH_PARTS = 2
SC_CORES = 2
SC_SUBCORES = 16
SB_BLOCK = 128
SB_CUTOFF = 104.0

_F32 = jnp.float32
_BF16 = jnp.bfloat16


def _params(*sem):
    return pltpu.CompilerParams(dimension_semantics=sem, vmem_limit_bytes=VMEM_LIMIT_BYTES)


def _const_spec(shape):
    nd = len(shape)
    return pl.BlockSpec(shape, lambda *_: (0,) * nd)


def _layer_norm_rows(y, g, b):
    mu = jnp.mean(y, axis=-1, keepdims=True)
    yc = y - mu
    var = jnp.mean(yc * yc, axis=-1, keepdims=True)
    return yc * lax.rsqrt(var + LN_EPS) * g + b


def _silu(x):
    return x * jax.nn.sigmoid(x)


def _pack_bf16_pairs(y):
    half = y.shape[1] // 2
    lo = lax.bitcast_convert_type(y[:, :half].astype(_BF16).astype(_F32), jnp.uint32)
    hi = lax.bitcast_convert_type(y[:, half:].astype(_BF16).astype(_F32), jnp.uint32)
    return (lo >> 16) | (hi & jnp.uint32(0xFFFF0000))


def _unpack_bf16_pairs(w):
    lo = lax.bitcast_convert_type(w << 16, _F32)
    hi = lax.bitcast_convert_type(w & jnp.uint32(0xFFFF0000), _F32)
    return jnp.concatenate([lo, hi], axis=1).astype(_BF16)


def _proj_kernel(x_ref, w_ref, o_ref):
    o_ref[...] = jnp.dot(x_ref[...].astype(_BF16), w_ref[...],
                         preferred_element_type=_F32).astype(o_ref.dtype)


def _proj(x, w_bf16):
    n, d = x.shape
    width = w_bf16.shape[1]
    return pl.pallas_call(
        _proj_kernel,
        grid=(n // PROJ_ROWS,),
        in_specs=[pl.BlockSpec((PROJ_ROWS, d), lambda i: (i, 0)), _const_spec((d, width))],
        out_specs=pl.BlockSpec((PROJ_ROWS, width), lambda i: (i, 0)),
        out_shape=jax.ShapeDtypeStruct((n, width), _BF16),
        compiler_params=_params("arbitrary"),
        name="proj",
    )(x, w_bf16)


def _proj_res_ln_kernel(a_ref, w_ref, res_ref, g_ref, b_ref, o_ref, words_ref):
    m = jnp.dot(a_ref[...], w_ref[...], preferred_element_type=_F32)
    y = _layer_norm_rows(DEEPNORM_ALPHA * res_ref[...] + m, g_ref[...], b_ref[...])
    o_ref[...] = y
    words_ref[...] = _pack_bf16_pairs(y)


def _proj_res_ln(a_bf16, w_bf16, res, g, b):
    n, d = res.shape
    k = a_bf16.shape[1]
    row = lambda i: (i, 0)
    return pl.pallas_call(
        _proj_res_ln_kernel,
        grid=(n // PROJ_ROWS,),
        in_specs=[pl.BlockSpec((PROJ_ROWS, k), row), _const_spec((k, d)),
                  pl.BlockSpec((PROJ_ROWS, d), row), _const_spec((1, d)), _const_spec((1, d))],
        out_specs=[pl.BlockSpec((PROJ_ROWS, d), row), pl.BlockSpec((PROJ_ROWS, d // 2), row)],
        out_shape=[jax.ShapeDtypeStruct((n, d), _F32), jax.ShapeDtypeStruct((n, d // 2), jnp.uint32)],
        compiler_params=_params("arbitrary"),
        name="proj_res_ln",
    )(a_bf16, w_bf16, res, g.reshape(1, d), b.reshape(1, d))


def _l0_mixer_kernel(h_ref, cos_ref, sin_ref, xi_ref, zeta_ref, decay_ref, gstate_ref, gn_g_ref,
                     conv_w_ref, conv_b_ref, cln_g_ref, cln_b_ref, o_ref, state_ref, u_ref, shift_ref):
    rows = MIX_ROWS
    step = pl.program_id(1)

    @pl.when(step == 0)
    def _():
        state_ref[...] = jnp.zeros_like(state_ref)
        u_ref[0:CONV_HALO, :] = jnp.zeros((CONV_HALO, CONV_CH), _F32)

    lane = lax.broadcasted_iota(jnp.int32, (rows, RET_QK_WIDTH), 1)
    first_half = (lane % RET_DK) < (RET_DK // 2)
    cos = cos_ref[...]
    sin = sin_ref[...]

    def rotary(t):
        partner = jnp.where(first_half,
                            pltpu.roll(t, RET_QK_WIDTH - RET_DK // 2, axis=1),
                            pltpu.roll(t, RET_DK // 2, axis=1))
        return t * cos + partner * sin

    q = rotary(h_ref[:, 0:RET_QK_WIDTH].astype(_F32))
    k = rotary(h_ref[:, RET_QK_WIDTH:2 * RET_QK_WIDTH].astype(_F32)) * (RET_DK ** -0.5)
    q_in = (q * xi_ref[...]).astype(_BF16)
    k_out = (k * zeta_ref[...]).astype(_BF16)
    q = q.astype(_BF16)
    k = k.astype(_BF16)
    v_off = 2 * RET_QK_WIDTH
    g_off = v_off + RET_WIDTH
    for hd in range(RET_HEADS):
        qk = slice(hd * RET_DK, (hd + 1) * RET_DK)
        vs = slice(hd * RET_DV, (hd + 1) * RET_DV)
        v = h_ref[:, v_off + hd * RET_DV:v_off + (hd + 1) * RET_DV]
        s = lax.dot_general(q[:, qk], k[:, qk], (((1,), (1,)), ((), ())),
                            preferred_element_type=_F32) * decay_ref[hd]
        r = jnp.dot(s.astype(_BF16), v, preferred_element_type=_F32)
        r += jnp.dot(q_in[:, qk], state_ref[hd].astype(_BF16), preferred_element_type=_F32)
        kv = lax.dot_general(k_out[:, qk], v, (((0,), (0,)), ((), ())),
                             preferred_element_type=_F32)
        state_ref[hd] = gstate_ref[hd] * state_ref[hd] + kv
        mu = jnp.mean(r, axis=-1, keepdims=True)
        rc = r - mu
        var = jnp.mean(rc * rc, axis=-1, keepdims=True)
        rn = rc * lax.rsqrt(var + LN_EPS) * gn_g_ref[:, vs]
        gate = h_ref[:, g_off + hd * RET_DV:g_off + (hd + 1) * RET_DV].astype(_F32)
        o_ref[:, vs] = (_silu(gate) * rn).astype(o_ref.dtype)

    a_off = g_off + RET_WIDTH
    glu_a = h_ref[:, a_off:a_off + CONV_CH].astype(_F32)
    glu_b = h_ref[:, a_off + CONV_CH:a_off + 2 * CONV_CH].astype(_F32)
    u_ref[CONV_HALO:CONV_HALO + rows, :] = glu_a * jax.nn.sigmoid(glu_b)
    acc = jnp.broadcast_to(conv_b_ref[...], (rows, CONV_CH))
    first = CONV_HALO - (CONV_WIDTH - 1)
    for phase in range(F32_SUBLANES):
        offsets = [first + j for j in range(CONV_WIDTH) if (first + j) % F32_SUBLANES == phase]
        span = max(offsets) - phase + rows
        if phase == 0:
            src = u_ref
        else:
            shift_ref[0:span, :] = u_ref[phase:phase + span, :]
            src = shift_ref
        for off in offsets:
            j = off - first
            acc = acc + conv_w_ref[j:j + 1, :] * src[off - phase:off - phase + rows, :]
    u_ref[0:CONV_HALO, :] = u_ref[rows:rows + CONV_HALO, :]
    c = _silu(_layer_norm_rows(acc, cln_g_ref[...], cln_b_ref[...]))
    o_ref[:, RET_WIDTH:RET_WIDTH + CONV_CH] = c.astype(o_ref.dtype)


def _l0_mixer_tables(seq):
    half = RET_DK // 2
    inv = ROPE_BASE ** (-jnp.arange(half, dtype=_F32) / half)
    ang = jnp.arange(seq).astype(_F32)[:, None] * inv[None, :]
    cos = jnp.tile(jnp.cos(ang), (1, 2 * RET_HEADS))
    sin = jnp.tile(jnp.concatenate([-jnp.sin(ang), jnp.sin(ang)], axis=1), (1, RET_HEADS))
    log_g = jnp.log(1.0 - 2.0 ** (-5.0 - jnp.arange(RET_HEADS, dtype=_F32)))
    n = jnp.arange(MIX_ROWS, dtype=_F32)
    diff = n[:, None] - n[None, :]
    chunk = jnp.arange(MIX_ROWS) // CHUNK
    expo = jnp.where(chunk[:, None] == chunk[None, :], jnp.abs(diff), diff)
    decay = jnp.where((chunk[:, None] >= chunk[None, :])[None],
                      jnp.exp(expo[None] * log_g[:, None, None]), 0.0)
    xi = jnp.repeat(jnp.exp((n[:, None] + 1.0) * log_g[None, :]), RET_DK, axis=1)
    zeta = jnp.repeat(jnp.exp((MIX_ROWS - 1.0 - n[:, None]) * log_g[None, :]), RET_DK, axis=1)
    g_state = jnp.exp(MIX_ROWS * log_g)
    return cos, sin, xi, zeta, decay, g_state


def _l0_mixer(h, batch, seq, gn_g, conv_w, conv_b, cln_g, cln_b):
    n = h.shape[0]
    steps = seq // MIX_ROWS
    cos, sin, xi, zeta, decay, g_state = _l0_mixer_tables(seq)
    row = lambda b, i: (b * steps + i, 0)
    pos = lambda b, i: (i, 0)
    return pl.pallas_call(
        _l0_mixer_kernel,
        grid=(batch, steps),
        in_specs=[pl.BlockSpec((MIX_ROWS, L0_IN), row),
                  pl.BlockSpec((MIX_ROWS, RET_QK_WIDTH), pos),
                  pl.BlockSpec((MIX_ROWS, RET_QK_WIDTH), pos),
                  _const_spec((MIX_ROWS, RET_QK_WIDTH)), _const_spec((MIX_ROWS, RET_QK_WIDTH)),
                  _const_spec((RET_HEADS, MIX_ROWS, MIX_ROWS)),
                  pl.BlockSpec(memory_space=pltpu.SMEM),
                  _const_spec((1, RET_WIDTH)), _const_spec((CONV_WIDTH, CONV_CH)),
                  _const_spec((1, CONV_CH)), _const_spec((1, CONV_CH)), _const_spec((1, CONV_CH))],
        out_specs=pl.BlockSpec((MIX_ROWS, D_MODEL), row),
        out_shape=jax.ShapeDtypeStruct((n, D_MODEL), _BF16),
        scratch_shapes=[pltpu.VMEM((RET_HEADS, RET_DK, RET_DV), _F32),
                        pltpu.VMEM((MIX_ROWS + CONV_HALO, CONV_CH), _F32),
                        pltpu.VMEM((MIX_ROWS + CONV_HALO, CONV_CH), _F32)],
        compiler_params=_params("arbitrary", "arbitrary"),
        name="l0_mixer",
    )(h, cos, sin, xi, zeta, decay, g_state, gn_g.reshape(1, -1), conv_w,
      conv_b.reshape(1, -1), cln_g.reshape(1, -1), cln_b.reshape(1, -1))


def _sb_attention_kernel(q_ref, k_ref, v_ref, tri_ref, o_ref, acc_ref, carry_ref):
    blk = SB_BLOCK
    qi = pl.program_id(1)
    scale = SB_HEAD_DIM ** -0.5
    t_idx = lax.broadcasted_iota(jnp.int32, (blk, blk), 0)
    s_idx = lax.broadcasted_iota(jnp.int32, (blk, blk), 1)
    keep = s_idx < t_idx

    heads = [slice(hd * SB_HEAD_DIM, (hd + 1) * SB_HEAD_DIM) for hd in range(SB_HEADS)]

    def sweep(j, carry):
        masked = carry is None
        start = pl.multiple_of(j * blk, blk)
        zs = [lax.dot_general(q_ref[:, hs], k_ref[pl.ds(start, blk), hs], (((1,), (1,)), ((), ())),
                              preferred_element_type=_F32) * scale for hs in heads]
        log_betas, addends = [], []
        for z in zs:
            softplus = jnp.log(1.0 + jnp.exp(-jnp.abs(z)))
            log_beta = jnp.minimum(z, 0.0) - softplus
            log_rest = log_beta - z
            if masked:
                log_rest = jnp.where(keep, log_rest, 0.0)
            hi = log_rest.astype(_BF16)
            lo = (log_rest - hi.astype(_F32)).astype(_BF16)
            log_betas.append(log_beta)
            addends.append(jnp.concatenate([hi, lo], axis=1))
        sums = [jnp.dot(t, tri_ref[...], preferred_element_type=_F32) for t in addends]
        weights = []
        for hd, (log_beta, s) in enumerate(zip(log_betas, sums)):
            if masked:
                a = jnp.where(keep, jnp.exp(log_beta + s[:, :blk]), 0.0)
            else:
                a = jnp.exp(log_beta + s[:, :blk] + carry[:, heads[hd]])
            weights.append(a.astype(_BF16))
        outs = [jnp.dot(a, v_ref[pl.ds(start, blk), hs], preferred_element_type=_F32)
                for a, hs in zip(weights, heads)]
        return jnp.concatenate(outs, axis=1), jnp.concatenate([s[:, blk:] for s in sums], axis=1)

    out, carry = sweep(qi, None)
    acc_ref[...] = out
    carry_ref[...] = carry

    def cond(state):
        j, alive = state
        return jnp.logical_and(j >= 0, alive)

    def body(state):
        j, _ = state
        carry = carry_ref[...]
        out, total = sweep(j, carry)
        carry = carry + total
        acc_ref[...] += out
        carry_ref[...] = carry
        return j - 1, jnp.max(carry) > -SB_CUTOFF

    lax.while_loop(cond, body, (qi - 1, jnp.max(carry) > -SB_CUTOFF))
    o_ref[...] = acc_ref[...].astype(o_ref.dtype)


def _sb_attention(qkv, batch, seq):
    n = qkv.shape[0]
    nq = seq // SB_BLOCK
    j = jnp.arange(2 * SB_BLOCK)[:, None] % SB_BLOCK
    s = jnp.arange(SB_BLOCK)[None, :]
    tri = jnp.concatenate([(j > s).astype(_BF16), jnp.ones((2 * SB_BLOCK, SB_BLOCK), _BF16)], axis=1)
    whole_seq = lambda col: pl.BlockSpec((seq, D_MODEL), lambda b, i: (b, col),
                                         pipeline_mode=pl.Buffered(1))
    return pl.pallas_call(
        _sb_attention_kernel,
        grid=(batch, nq),
        in_specs=[pl.BlockSpec((SB_BLOCK, D_MODEL), lambda b, i: (b * nq + i, 0)),
                  whole_seq(1), whole_seq(2), _const_spec((2 * SB_BLOCK, 2 * SB_BLOCK))],
        out_specs=pl.BlockSpec((SB_BLOCK, D_MODEL), lambda b, i: (b * nq + i, 0)),
        out_shape=jax.ShapeDtypeStruct((n, D_MODEL), _BF16),
        scratch_shapes=[pltpu.VMEM((SB_BLOCK, D_MODEL), _F32), pltpu.VMEM((SB_BLOCK, D_MODEL), _F32)],
        compiler_params=_params("arbitrary", "arbitrary"),
        name="sb_attention",
    )(qkv, qkv, qkv, tri)


def _router_kernel(x_ref, w_ref, b_ref, tri_ref, idx_ref, gate_ref, rank_ref, count_ref, base_ref):
    @pl.when(pl.program_id(0) == 0)
    def _():
        base_ref[...] = jnp.zeros_like(base_ref)

    x = x_ref[...]
    w = w_ref[...]
    x_hi = x.astype(_BF16)
    x_lo = (x - x_hi.astype(_F32)).astype(_BF16)
    w_hi = w.astype(_BF16)
    w_lo = (w - w_hi.astype(_F32)).astype(_BF16)
    nt = (((1,), (1,)), ((), ()))
    logits = (lax.dot_general(w_hi, x_hi, nt, preferred_element_type=_F32)
              + lax.dot_general(w_lo, x_hi, nt, preferred_element_type=_F32)
              + lax.dot_general(w_hi, x_lo, nt, preferred_element_type=_F32)) + b_ref[...]
    expert = lax.broadcasted_iota(jnp.int32, logits.shape, 0)
    vals, idxs = [], []
    for _ in range(TOP_K):
        m = jnp.max(logits, axis=0, keepdims=True)
        sel = jnp.min(jnp.where(logits == m, expert, N_EXPERTS), axis=0, keepdims=True)
        vals.append(m)
        idxs.append(sel)
        logits = jnp.where(expert == sel, -jnp.inf, logits)
    vals = jnp.concatenate(vals, axis=0)
    e = jnp.exp(vals - vals[0:1])
    gate_ref[...] = e / jnp.sum(e, axis=0, keepdims=True)
    idx_ref[...] = jnp.concatenate(idxs, axis=0)

    member = jnp.zeros(logits.shape, _F32)
    for sel in idxs:
        member += (expert == sel).astype(_F32)
    before = jnp.dot(member.astype(_BF16), tri_ref[...], preferred_element_type=_F32) + base_ref[...]
    ranks = [jnp.sum(jnp.where(expert == sel, before, 0.0), axis=0, keepdims=True) for sel in idxs]
    rank_ref[...] = jnp.concatenate(ranks, axis=0).astype(jnp.int32)
    base_ref[...] += jnp.sum(member, axis=1, keepdims=True)
    count_ref[...] = base_ref[...]


def _router(x, router_w, router_b):
    n, d = x.shape
    t = jnp.arange(ROUTER_ROWS)
    tri = (t[:, None] < t[None, :]).astype(_BF16)
    tok = pl.BlockSpec((TOP_K, ROUTER_ROWS), lambda i: (0, i))
    return pl.pallas_call(
        _router_kernel,
        grid=(n // ROUTER_ROWS,),
        in_specs=[pl.BlockSpec((ROUTER_ROWS, d), lambda i: (i, 0)),
                  _const_spec((N_EXPERTS, d)), _const_spec((N_EXPERTS, 1)),
                  _const_spec((ROUTER_ROWS, ROUTER_ROWS))],
        out_specs=[tok, tok, tok, _const_spec((N_EXPERTS, 1))],
        out_shape=[jax.ShapeDtypeStruct((TOP_K, n), jnp.int32),
                   jax.ShapeDtypeStruct((TOP_K, n), _F32),
                   jax.ShapeDtypeStruct((TOP_K, n), jnp.int32),
                   jax.ShapeDtypeStruct((N_EXPERTS, 1), _F32)],
        scratch_shapes=[pltpu.VMEM((N_EXPERTS, 1), _F32)],
        compiler_params=_params("arbitrary"),
        name="router",
    )(x, router_w.T, router_b.reshape(N_EXPERTS, 1), tri)


def _prep_up_kernel(w_ref, sel_ref, o_ref):
    group = 2 * PREP_LANES
    for c in range(w_ref.shape[2] // group):
        t = jnp.dot(w_ref[0, :, c * group:(c + 1) * group].astype(_BF16), sel_ref[...],
                    preferred_element_type=_F32)
        o_ref[0, :, c * PREP_LANES:(c + 1) * PREP_LANES] = t[:, :PREP_LANES].astype(_BF16)
        o_ref[0, :, D_FF + c * PREP_LANES:D_FF + (c + 1) * PREP_LANES] = t[:, PREP_LANES:].astype(_BF16)


def _prep_up(w_up):
    e, d, ff2 = w_up.shape
    i = jnp.arange(2 * PREP_LANES)[:, None]
    j = jnp.arange(2 * PREP_LANES)[None, :]
    sel = (i == jnp.where(j < PREP_LANES, 2 * j, 2 * (j - PREP_LANES) + 1)).astype(_BF16)
    blk = lambda x, r: (x, r, 0)
    return pl.pallas_call(
        _prep_up_kernel,
        grid=(e, d // PREP_ROWS),
        in_specs=[pl.BlockSpec((1, PREP_ROWS, ff2), blk), _const_spec((2 * PREP_LANES, 2 * PREP_LANES))],
        out_specs=pl.BlockSpec((1, PREP_ROWS, ff2), blk),
        out_shape=jax.ShapeDtypeStruct((e, d, ff2), _BF16),
        compiler_params=_params("arbitrary", "arbitrary"),
        name="prep_up",
    )(w_up, sel)


def _cast_kernel(w_ref, o_ref):
    o_ref[...] = w_ref[...].astype(o_ref.dtype)


def _prep_down(w_down):
    e, ff, d = w_down.shape
    blk = lambda x: (x, 0, 0)
    return pl.pallas_call(
        _cast_kernel,
        grid=(e,),
        in_specs=[pl.BlockSpec((1, ff, d), blk)],
        out_specs=pl.BlockSpec((1, ff, d), blk),
        out_shape=jax.ShapeDtypeStruct((e, ff, d), _BF16),
        compiler_params=_params("arbitrary"),
        name="prep_down",
    )(w_down)


def _dispatch(x_words, slot_of, p):
    n, words = x_words.shape
    nk = slot_of.shape[0]
    tiles = n // DISPATCH_ROWS
    part = words // DISPATCH_PARTS
    mesh = plsc.VectorSubcoreMesh(core_axis_name="core", subcore_axis_name="subcore",
                                  num_cores=SC_CORES, num_subcores=SC_SUBCORES)
    out = jax.ShapeDtypeStruct((p, part), x_words.dtype)

    @functools.partial(
        pl.kernel, out_type=[out] * DISPATCH_PARTS, mesh=mesh, scratch_types=[],
        compiler_params=pltpu.CompilerParams(use_tc_tiling_on_sc=True), name="moe_dispatch")
    def dispatch(x_hbm, slot_hbm, *o_hbm):
        for c in range(DISPATCH_PARTS):
            def body(x_vmem, slot_vmem, o_ref=o_hbm[c]):
                pltpu.sync_copy(x_vmem, o_ref.at[slot_vmem.at[0]])

            pltpu.emit_pipeline(
                body,
                grid=(nk // DISPATCH_ROWS,),
                in_specs=[pl.BlockSpec((DISPATCH_ROWS, part), lambda i, c=c: (i % tiles, c)),
                          pl.BlockSpec((1, DISPATCH_ROWS), lambda i: (0, i))],
                out_specs=[],
                core_axis_name=("core", "subcore"),
                dimension_semantics=(pltpu.PARALLEL,),
            )(x_hbm, slot_hbm)

    return dispatch(x_words, slot_of.reshape(1, nk))


def _moe_ffn_kernel(block_e_ref, n_used_ref, xa_ref, xb_ref, wu_ref, bu_ref, wd_ref, bd_ref, o_ref):
    i = pl.program_id(0)

    @pl.when(i < n_used_ref[0])
    def _():
        x = _unpack_bf16_pairs(jnp.concatenate([xa_ref[...], xb_ref[...]], axis=1))
        h = jnp.dot(x, wu_ref[0], preferred_element_type=_F32) + bu_ref[0]
        h_glu = jnp.minimum(h[:, :D_FF], SWIGLU_LIMIT)
        h_lin = jnp.clip(h[:, D_FF:], -SWIGLU_LIMIT, SWIGLU_LIMIT)
        act = h_glu * jax.nn.sigmoid(SWIGLU_ALPHA * h_glu) * (h_lin + 1.0)
        y = jnp.dot(act.astype(_BF16), wd_ref[0], preferred_element_type=_F32) + bd_ref[0]
        o_ref[...] = y.astype(o_ref.dtype)

    @pl.when(i >= n_used_ref[0])
    def _():
        o_ref[...] = jnp.zeros_like(o_ref)


def _moe_ffn(xs, block_e, n_used, wu, bu, wd, bd):
    xa, xb = xs
    p, part = xa.shape
    d = wd.shape[2]
    n_blocks = p // MOE_ROWS
    ff2 = wu.shape[2]
    row = lambda i, be, nu: (i, 0)
    exp3 = lambda i, be, nu: (be[i], 0, 0)
    grid_spec = pltpu.PrefetchScalarGridSpec(
        num_scalar_prefetch=2,
        grid=(n_blocks,),
        in_specs=[pl.BlockSpec((MOE_ROWS, part), row), pl.BlockSpec((MOE_ROWS, part), row),
                  pl.BlockSpec((1, d, ff2), exp3), pl.BlockSpec((1, 1, ff2), exp3),
                  pl.BlockSpec((1, ff2 // 2, d), exp3), pl.BlockSpec((1, 1, d), exp3)],
        out_specs=pl.BlockSpec((MOE_ROWS, d), row),
    )
    return pl.pallas_call(
        _moe_ffn_kernel,
        grid_spec=grid_spec,
        out_shape=jax.ShapeDtypeStruct((p, d), _BF16),
        compiler_params=_params("arbitrary"),
        name="moe_ffn",
    )(block_e, n_used, xa, xb, wu, bu, wd, bd)


def _combine_ln_kernel(y_ref, gate_ref, res_ref, g_ref, b_ref, o_ref, obf_ref):
    gate = gate_ref[...]
    moe = y_ref[0].astype(_F32) * gate[:, 0:1]
    for k in range(1, TOP_K):
        moe += y_ref[k].astype(_F32) * gate[:, k:k + 1]
    y = _layer_norm_rows(DEEPNORM_ALPHA * res_ref[...] + moe, g_ref[...], b_ref[...])
    o_ref[...] = y
    obf_ref[...] = y.astype(_BF16)


def _combine_ln(y4, gates, res, g, b):
    n, d = res.shape
    row = lambda i: (i, 0)
    return pl.pallas_call(
        _combine_ln_kernel,
        grid=(n // COMBINE_ROWS,),
        in_specs=[pl.BlockSpec((TOP_K, COMBINE_ROWS, d), lambda i: (0, i, 0)),
                  pl.BlockSpec((COMBINE_ROWS, TOP_K), row),
                  pl.BlockSpec((COMBINE_ROWS, d), row), _const_spec((1, d)), _const_spec((1, d))],
        out_specs=[pl.BlockSpec((COMBINE_ROWS, d), row), pl.BlockSpec((COMBINE_ROWS, d), row)],
        out_shape=[jax.ShapeDtypeStruct((n, d), _F32), jax.ShapeDtypeStruct((n, d), _BF16)],
        compiler_params=_params("arbitrary"),
        name="combine_ln",
    )(y4, gates, res, g.reshape(1, d), b.reshape(1, d))


def _moe_layer(x, x_words, router_w, router_b, w_up, b_up, w_down, b_down, ln_g, ln_b):
    n, d = x.shape
    nk = n * TOP_K
    idx_t, gate_t, rank_t, count = _router(x, router_w, router_b)
    counts = count[:, 0].astype(jnp.int32)
    padded = (counts + MOE_ROWS - 1) // MOE_ROWS * MOE_ROWS
    padded_ends = jnp.cumsum(padded)
    padded_starts = padded_ends - padded
    n_blocks = -(-(nk + N_EXPERTS * (MOE_ROWS - 1)) // MOE_ROWS)
    p = n_blocks * MOE_ROWS
    experts = jnp.arange(N_EXPERTS, dtype=jnp.int32)
    start_of = jnp.sum(jnp.where(idx_t[None] == experts[:, None, None],
                                 padded_starts[:, None, None], 0), axis=0)
    slot_of = (start_of + rank_t).reshape(nk)
    block_start = jnp.arange(n_blocks, dtype=jnp.int32) * MOE_ROWS
    block_e = jnp.minimum(jnp.sum(padded_ends[None, :] <= block_start[:, None], axis=1),
                          N_EXPERTS - 1).astype(jnp.int32)
    n_used = (padded_ends[-1] // MOE_ROWS).astype(jnp.int32).reshape(1)
    xs = _dispatch(x_words, slot_of, p)
    bu = jnp.concatenate([b_up[:, 0::2], b_up[:, 1::2]], axis=1).reshape(N_EXPERTS, 1, 2 * D_FF)
    y = _moe_ffn(xs, block_e, n_used, _prep_up(w_up), bu, _prep_down(w_down),
                 b_down.reshape(N_EXPERTS, 1, d))
    y4 = y[slot_of].reshape(TOP_K, n, d)
    return _combine_ln(y4, gate_t.T, x, ln_g, ln_b)


def kernel(x, l0_w_in, l0_ret_gn_g, l0_conv_w, l0_conv_b, l0_conv_ln_g, l0_conv_ln_b, l0_w_out, l0_ln1_g, l0_ln1_b, l0_router_w, l0_router_b, l0_w_up, l0_b_up, l0_w_down, l0_b_down, l0_ln2_g, l0_ln2_b, l1_w_in, l1_w_out, l1_ln1_g, l1_ln1_b, l1_router_w, l1_router_b, l1_w_up, l1_b_up, l1_w_down, l1_b_down, l1_ln2_g, l1_ln2_b):
    batch, seq, d = x.shape
    n = batch * seq
    x0 = x.reshape(n, d)

    h = _proj(x0, l0_w_in.astype(_BF16))
    mix = _l0_mixer(h, batch, seq, l0_ret_gn_g, l0_conv_w, l0_conv_b, l0_conv_ln_g, l0_conv_ln_b)
    x1, x1_words = _proj_res_ln(mix, l0_w_out.astype(_BF16), x0, l0_ln1_g, l0_ln1_b)
    x2, x2_bf16 = _moe_layer(x1, x1_words, l0_router_w, l0_router_b, l0_w_up, l0_b_up, l0_w_down,
                             l0_b_down, l0_ln2_g, l0_ln2_b)

    qkv = _proj(x2_bf16, l1_w_in.astype(_BF16))
    att = _sb_attention(qkv, batch, seq)
    x3, x3_words = _proj_res_ln(att, l1_w_out.astype(_BF16), x2, l1_ln1_g, l1_ln1_b)
    x4, _ = _moe_layer(x3, x3_words, l1_router_w, l1_router_b, l1_w_up, l1_b_up, l1_w_down,
                       l1_b_down, l1_ln2_g, l1_ln2_b)
    return x4.reshape(batch, seq, d)
```

```python
import functools

import jax
import jax.numpy as jnp
from jax import lax
from jax.experimental import pallas as pl
from jax.experimental.pallas import tpu as pltpu
from jax.experimental.pallas import tpu_sc as plsc

D_MODEL = 1024
CHUNK = 64
RET_HEADS = 4
RET_DK = 64
RET_DV = 128
RET_QK_WIDTH = RET_HEADS * RET_DK
RET_WIDTH = RET_HEADS * RET_DV
ROPE_BASE = 10000.0
CONV_CH = D_MODEL // 2
CONV_WIDTH = 31
L0_IN = 2 * RET_QK_WIDTH + 2 * RET_WIDTH + 2 * CONV_CH
SB_HEADS = 8
SB_HEAD_DIM = D_MODEL // SB_HEADS
N_EXPERTS = 32
TOP_K = 4
D_FF = D_MODEL
SWIGLU_LIMIT = 7.0
SWIGLU_ALPHA = 1.702
LN_EPS = 1e-5
DEPTH = 2
DEEPNORM_ALPHA = (2 * DEPTH) ** 0.25

VMEM_LIMIT_BYTES = 48 * 1024 * 1024
F32_SUBLANES = 8
PROJ_ROWS = 512
MIX_ROWS = 256
CONV_HALO = 32
ROUTER_ROWS = 512
MOE_ROWS = 256
MOE_STEP_ROWS = 1024
PREP_ROWS = 512
PREP_LANES = 128
COMBINE_ROWS = 256
DISPATCH_ROWS = 128
DISPATCH_PARTS = 2
SC_CORES = 2
SC_SUBCORES = 16
SB_BLOCK = 128
SB_CUTOFF = 104.0

_F32 = jnp.float32
_BF16 = jnp.bfloat16


def _params(*sem):
    return pltpu.CompilerParams(dimension_semantics=sem, vmem_limit_bytes=VMEM_LIMIT_BYTES)


def _const_spec(shape):
    nd = len(shape)
    return pl.BlockSpec(shape, lambda *_: (0,) * nd)


def _layer_norm_rows(y, g, b):
    mu = jnp.mean(y, axis=-1, keepdims=True)
    yc = y - mu
    var = jnp.mean(yc * yc, axis=-1, keepdims=True)
    return yc * lax.rsqrt(var + LN_EPS) * g + b


def _silu(x):
    return x * jax.nn.sigmoid(x)


def _pack_bf16_pairs(y):
    half = y.shape[1] // 2
    lo = lax.bitcast_convert_type(y[:, :half].astype(_BF16).astype(_F32), jnp.uint32)
    hi = lax.bitcast_convert_type(y[:, half:].astype(_BF16).astype(_F32), jnp.uint32)
    return (lo >> 16) | (hi & jnp.uint32(0xFFFF0000))


def _unpack_bf16_pairs(w):
    lo = lax.bitcast_convert_type(w << 16, _F32)
    hi = lax.bitcast_convert_type(w & jnp.uint32(0xFFFF0000), _F32)
    return jnp.concatenate([lo, hi], axis=1).astype(_BF16)


def _proj_kernel(x_ref, w_ref, o_ref):
    o_ref[...] = jnp.dot(x_ref[...].astype(_BF16), w_ref[...],
                         preferred_element_type=_F32).astype(o_ref.dtype)


def _proj(x, w_bf16):
    n, d = x.shape
    width = w_bf16.shape[1]
    return pl.pallas_call(
        _proj_kernel,
        grid=(n // PROJ_ROWS,),
        in_specs=[pl.BlockSpec((PROJ_ROWS, d), lambda i: (i, 0)), _const_spec((d, width))],
        out_specs=pl.BlockSpec((PROJ_ROWS, width), lambda i: (i, 0)),
        out_shape=jax.ShapeDtypeStruct((n, width), _BF16),
        compiler_params=_params("arbitrary"),
        name="proj",
    )(x, w_bf16)


def _proj_res_ln_kernel(a_ref, w_ref, res_ref, g_ref, b_ref, o_ref, words_ref):
    m = jnp.dot(a_ref[...], w_ref[...], preferred_element_type=_F32)
    y = _layer_norm_rows(DEEPNORM_ALPHA * res_ref[...] + m, g_ref[...], b_ref[...])
    o_ref[...] = y
    words_ref[...] = _pack_bf16_pairs(y)


def _proj_res_ln(a_bf16, w_bf16, res, g, b):
    n, d = res.shape
    k = a_bf16.shape[1]
    row = lambda i: (i, 0)
    return pl.pallas_call(
        _proj_res_ln_kernel,
        grid=(n // PROJ_ROWS,),
        in_specs=[pl.BlockSpec((PROJ_ROWS, k), row), _const_spec((k, d)),
                  pl.BlockSpec((PROJ_ROWS, d), row), _const_spec((1, d)), _const_spec((1, d))],
        out_specs=[pl.BlockSpec((PROJ_ROWS, d), row), pl.BlockSpec((PROJ_ROWS, d // 2), row)],
        out_shape=[jax.ShapeDtypeStruct((n, d), _F32), jax.ShapeDtypeStruct((n, d // 2), jnp.uint32)],
        compiler_params=_params("arbitrary"),
        name="proj_res_ln",
    )(a_bf16, w_bf16, res, g.reshape(1, d), b.reshape(1, d))


def _l0_mixer_kernel(h_ref, cos_ref, sin_ref, xi_ref, zeta_ref, decay_ref, gstate_ref, gn_g_ref,
                     conv_w_ref, conv_b_ref, cln_g_ref, cln_b_ref, o_ref, state_ref, u_ref, shift_ref):
    rows = MIX_ROWS
    step = pl.program_id(1)

    @pl.when(step == 0)
    def _():
        state_ref[...] = jnp.zeros_like(state_ref)
        u_ref[0:CONV_HALO, :] = jnp.zeros((CONV_HALO, CONV_CH), _F32)

    lane = lax.broadcasted_iota(jnp.int32, (rows, RET_QK_WIDTH), 1)
    first_half = (lane % RET_DK) < (RET_DK // 2)
    cos = cos_ref[...]
    sin = sin_ref[...]

    def rotary(t):
        partner = jnp.where(first_half,
                            pltpu.roll(t, RET_QK_WIDTH - RET_DK // 2, axis=1),
                            pltpu.roll(t, RET_DK // 2, axis=1))
        return t * cos + partner * sin

    q = rotary(h_ref[:, 0:RET_QK_WIDTH].astype(_F32))
    k = rotary(h_ref[:, RET_QK_WIDTH:2 * RET_QK_WIDTH].astype(_F32)) * (RET_DK ** -0.5)
    q_in = (q * xi_ref[...]).astype(_BF16)
    k_out = (k * zeta_ref[...]).astype(_BF16)
    q = q.astype(_BF16)
    k = k.astype(_BF16)
    v_off = 2 * RET_QK_WIDTH
    g_off = v_off + RET_WIDTH
    for hd in range(RET_HEADS):
        qk = slice(hd * RET_DK, (hd + 1) * RET_DK)
        vs = slice(hd * RET_DV, (hd + 1) * RET_DV)
        v = h_ref[:, v_off + hd * RET_DV:v_off + (hd + 1) * RET_DV]
        s = lax.dot_general(q[:, qk], k[:, qk], (((1,), (1,)), ((), ())),
                            preferred_element_type=_F32) * decay_ref[hd]
        r = jnp.dot(s.astype(_BF16), v, preferred_element_type=_F32)
        r += jnp.dot(q_in[:, qk], state_ref[hd].astype(_BF16), preferred_element_type=_F32)
        kv = lax.dot_general(k_out[:, qk], v, (((0,), (0,)), ((), ())),
                             preferred_element_type=_F32)
        state_ref[hd] = gstate_ref[hd] * state_ref[hd] + kv
        mu = jnp.mean(r, axis=-1, keepdims=True)
        rc = r - mu
        var = jnp.mean(rc * rc, axis=-1, keepdims=True)
        rn = rc * lax.rsqrt(var + LN_EPS) * gn_g_ref[:, vs]
        gate = h_ref[:, g_off + hd * RET_DV:g_off + (hd + 1) * RET_DV].astype(_F32)
        o_ref[:, vs] = (_silu(gate) * rn).astype(o_ref.dtype)

    a_off = g_off + RET_WIDTH
    glu_a = h_ref[:, a_off:a_off + CONV_CH].astype(_F32)
    glu_b = h_ref[:, a_off + CONV_CH:a_off + 2 * CONV_CH].astype(_F32)
    u_ref[CONV_HALO:CONV_HALO + rows, :] = glu_a * jax.nn.sigmoid(glu_b)
    acc = jnp.broadcast_to(conv_b_ref[...], (rows, CONV_CH))
    first = CONV_HALO - (CONV_WIDTH - 1)
    for phase in range(F32_SUBLANES):
        offsets = [first + j for j in range(CONV_WIDTH) if (first + j) % F32_SUBLANES == phase]
        span = max(offsets) - phase + rows
        if phase == 0:
            src = u_ref
        else:
            shift_ref[0:span, :] = u_ref[phase:phase + span, :]
            src = shift_ref
        for off in offsets:
            j = off - first
            acc = acc + conv_w_ref[j:j + 1, :] * src[off - phase:off - phase + rows, :]
    u_ref[0:CONV_HALO, :] = u_ref[rows:rows + CONV_HALO, :]
    c = _silu(_layer_norm_rows(acc, cln_g_ref[...], cln_b_ref[...]))
    o_ref[:, RET_WIDTH:RET_WIDTH + CONV_CH] = c.astype(o_ref.dtype)


def _l0_mixer_tables(seq):
    half = RET_DK // 2
    inv = ROPE_BASE ** (-jnp.arange(half, dtype=_F32) / half)
    ang = jnp.arange(seq).astype(_F32)[:, None] * inv[None, :]
    cos = jnp.tile(jnp.cos(ang), (1, 2 * RET_HEADS))
    sin = jnp.tile(jnp.concatenate([-jnp.sin(ang), jnp.sin(ang)], axis=1), (1, RET_HEADS))
    log_g = jnp.log(1.0 - 2.0 ** (-5.0 - jnp.arange(RET_HEADS, dtype=_F32)))
    n = jnp.arange(MIX_ROWS, dtype=_F32)
    diff = n[:, None] - n[None, :]
    chunk = jnp.arange(MIX_ROWS) // CHUNK
    expo = jnp.where(chunk[:, None] == chunk[None, :], jnp.abs(diff), diff)
    decay = jnp.where((chunk[:, None] >= chunk[None, :])[None],
                      jnp.exp(expo[None] * log_g[:, None, None]), 0.0)
    xi = jnp.repeat(jnp.exp((n[:, None] + 1.0) * log_g[None, :]), RET_DK, axis=1)
    zeta = jnp.repeat(jnp.exp((MIX_ROWS - 1.0 - n[:, None]) * log_g[None, :]), RET_DK, axis=1)
    g_state = jnp.exp(MIX_ROWS * log_g)
    return cos, sin, xi, zeta, decay, g_state


def _l0_mixer(h, batch, seq, gn_g, conv_w, conv_b, cln_g, cln_b):
    n = h.shape[0]
    steps = seq // MIX_ROWS
    cos, sin, xi, zeta, decay, g_state = _l0_mixer_tables(seq)
    row = lambda b, i: (b * steps + i, 0)
    pos = lambda b, i: (i, 0)
    return pl.pallas_call(
        _l0_mixer_kernel,
        grid=(batch, steps),
        in_specs=[pl.BlockSpec((MIX_ROWS, L0_IN), row),
                  pl.BlockSpec((MIX_ROWS, RET_QK_WIDTH), pos),
                  pl.BlockSpec((MIX_ROWS, RET_QK_WIDTH), pos),
                  _const_spec((MIX_ROWS, RET_QK_WIDTH)), _const_spec((MIX_ROWS, RET_QK_WIDTH)),
                  _const_spec((RET_HEADS, MIX_ROWS, MIX_ROWS)),
                  pl.BlockSpec(memory_space=pltpu.SMEM),
                  _const_spec((1, RET_WIDTH)), _const_spec((CONV_WIDTH, CONV_CH)),
                  _const_spec((1, CONV_CH)), _const_spec((1, CONV_CH)), _const_spec((1, CONV_CH))],
        out_specs=pl.BlockSpec((MIX_ROWS, D_MODEL), row),
        out_shape=jax.ShapeDtypeStruct((n, D_MODEL), _BF16),
        scratch_shapes=[pltpu.VMEM((RET_HEADS, RET_DK, RET_DV), _F32),
                        pltpu.VMEM((MIX_ROWS + CONV_HALO, CONV_CH), _F32),
                        pltpu.VMEM((MIX_ROWS + CONV_HALO, CONV_CH), _F32)],
        compiler_params=_params("arbitrary", "arbitrary"),
        name="l0_mixer",
    )(h, cos, sin, xi, zeta, decay, g_state, gn_g.reshape(1, -1), conv_w,
      conv_b.reshape(1, -1), cln_g.reshape(1, -1), cln_b.reshape(1, -1))


def _sb_attention_kernel(q_ref, k_ref, v_ref, tri_ref, o_ref, acc_ref, carry_ref):
    blk = SB_BLOCK
    qi = pl.program_id(1)
    scale = SB_HEAD_DIM ** -0.5
    t_idx = lax.broadcasted_iota(jnp.int32, (blk, blk), 0)
    s_idx = lax.broadcasted_iota(jnp.int32, (blk, blk), 1)
    keep = s_idx < t_idx

    heads = [slice(hd * SB_HEAD_DIM, (hd + 1) * SB_HEAD_DIM) for hd in range(SB_HEADS)]

    def sweep(j, carry):
        masked = carry is None
        start = pl.multiple_of(j * blk, blk)
        zs = [lax.dot_general(q_ref[:, hs], k_ref[pl.ds(start, blk), hs], (((1,), (1,)), ((), ())),
                              preferred_element_type=_F32) * scale for hs in heads]
        log_betas, addends = [], []
        for z in zs:
            softplus = jnp.log(1.0 + jnp.exp(-jnp.abs(z)))
            log_beta = jnp.minimum(z, 0.0) - softplus
            log_rest = log_beta - z
            if masked:
                log_rest = jnp.where(keep, log_rest, 0.0)
            hi = log_rest.astype(_BF16)
            lo = (log_rest - hi.astype(_F32)).astype(_BF16)
            log_betas.append(log_beta)
            addends.append(jnp.concatenate([hi, lo], axis=1))
        sums = [jnp.dot(t, tri_ref[...], preferred_element_type=_F32) for t in addends]
        weights = []
        for hd, (log_beta, s) in enumerate(zip(log_betas, sums)):
            if masked:
                a = jnp.where(keep, jnp.exp(log_beta + s[:, :blk]), 0.0)
            else:
                a = jnp.exp(log_beta + s[:, :blk] + carry[:, heads[hd]])
            weights.append(a.astype(_BF16))
        outs = [jnp.dot(a, v_ref[pl.ds(start, blk), hs], preferred_element_type=_F32)
                for a, hs in zip(weights, heads)]
        return jnp.concatenate(outs, axis=1), jnp.concatenate([s[:, blk:] for s in sums], axis=1)

    out, carry = sweep(qi, None)
    acc_ref[...] = out
    carry_ref[...] = carry

    def cond(state):
        j, alive = state
        return jnp.logical_and(j >= 0, alive)

    def body(state):
        j, _ = state
        carry = carry_ref[...]
        out, total = sweep(j, carry)
        carry = carry + total
        acc_ref[...] += out
        carry_ref[...] = carry
        return j - 1, jnp.max(carry) > -SB_CUTOFF

    lax.while_loop(cond, body, (qi - 1, jnp.max(carry) > -SB_CUTOFF))
    o_ref[...] = acc_ref[...].astype(o_ref.dtype)


def _sb_attention(qkv, batch, seq):
    n = qkv.shape[0]
    nq = seq // SB_BLOCK
    j = jnp.arange(2 * SB_BLOCK)[:, None] % SB_BLOCK
    s = jnp.arange(SB_BLOCK)[None, :]
    tri = jnp.concatenate([(j > s).astype(_BF16), jnp.ones((2 * SB_BLOCK, SB_BLOCK), _BF16)], axis=1)
    whole_seq = lambda col: pl.BlockSpec((seq, D_MODEL), lambda b, i: (b, col),
                                         pipeline_mode=pl.Buffered(1))
    return pl.pallas_call(
        _sb_attention_kernel,
        grid=(batch, nq),
        in_specs=[pl.BlockSpec((SB_BLOCK, D_MODEL), lambda b, i: (b * nq + i, 0)),
                  whole_seq(1), whole_seq(2), _const_spec((2 * SB_BLOCK, 2 * SB_BLOCK))],
        out_specs=pl.BlockSpec((SB_BLOCK, D_MODEL), lambda b, i: (b * nq + i, 0)),
        out_shape=jax.ShapeDtypeStruct((n, D_MODEL), _BF16),
        scratch_shapes=[pltpu.VMEM((SB_BLOCK, D_MODEL), _F32), pltpu.VMEM((SB_BLOCK, D_MODEL), _F32)],
        compiler_params=_params("arbitrary", "arbitrary"),
        name="sb_attention",
    )(qkv, qkv, qkv, tri)


def _router_kernel(x_ref, w_ref, b_ref, tri_ref, idx_ref, gate_ref, rank_ref, count_ref, base_ref):
    @pl.when(pl.program_id(0) == 0)
    def _():
        base_ref[...] = jnp.zeros_like(base_ref)

    x = x_ref[...]
    w = w_ref[...]
    x_hi = x.astype(_BF16)
    x_lo = (x - x_hi.astype(_F32)).astype(_BF16)
    w_hi = w.astype(_BF16)
    w_lo = (w - w_hi.astype(_F32)).astype(_BF16)
    nt = (((1,), (1,)), ((), ()))
    logits = (lax.dot_general(w_hi, x_hi, nt, preferred_element_type=_F32)
              + lax.dot_general(w_lo, x_hi, nt, preferred_element_type=_F32)
              + lax.dot_general(w_hi, x_lo, nt, preferred_element_type=_F32)) + b_ref[...]
    expert = lax.broadcasted_iota(jnp.int32, logits.shape, 0)
    vals, idxs = [], []
    for _ in range(TOP_K):
        m = jnp.max(logits, axis=0, keepdims=True)
        sel = jnp.min(jnp.where(logits == m, expert, N_EXPERTS), axis=0, keepdims=True)
        vals.append(m)
        idxs.append(sel)
        logits = jnp.where(expert == sel, -jnp.inf, logits)
    vals = jnp.concatenate(vals, axis=0)
    e = jnp.exp(vals - vals[0:1])
    gate_ref[...] = e / jnp.sum(e, axis=0, keepdims=True)
    idx_ref[...] = jnp.concatenate(idxs, axis=0)

    member = jnp.zeros(logits.shape, _F32)
    for sel in idxs:
        member += (expert == sel).astype(_F32)
    before = jnp.dot(member.astype(_BF16), tri_ref[...], preferred_element_type=_F32) + base_ref[...]
    ranks = [jnp.sum(jnp.where(expert == sel, before, 0.0), axis=0, keepdims=True) for sel in idxs]
    rank_ref[...] = jnp.concatenate(ranks, axis=0).astype(jnp.int32)
    base_ref[...] += jnp.sum(member, axis=1, keepdims=True)
    count_ref[...] = base_ref[...]


def _router(x, router_w, router_b):
    n, d = x.shape
    t = jnp.arange(ROUTER_ROWS)
    tri = (t[:, None] < t[None, :]).astype(_BF16)
    tok = pl.BlockSpec((TOP_K, ROUTER_ROWS), lambda i: (0, i))
    return pl.pallas_call(
        _router_kernel,
        grid=(n // ROUTER_ROWS,),
        in_specs=[pl.BlockSpec((ROUTER_ROWS, d), lambda i: (i, 0)),
                  _const_spec((N_EXPERTS, d)), _const_spec((N_EXPERTS, 1)),
                  _const_spec((ROUTER_ROWS, ROUTER_ROWS))],
        out_specs=[tok, tok, tok, _const_spec((N_EXPERTS, 1))],
        out_shape=[jax.ShapeDtypeStruct((TOP_K, n), jnp.int32),
                   jax.ShapeDtypeStruct((TOP_K, n), _F32),
                   jax.ShapeDtypeStruct((TOP_K, n), jnp.int32),
                   jax.ShapeDtypeStruct((N_EXPERTS, 1), _F32)],
        scratch_shapes=[pltpu.VMEM((N_EXPERTS, 1), _F32)],
        compiler_params=_params("arbitrary"),
        name="router",
    )(x, router_w.T, router_b.reshape(N_EXPERTS, 1), tri)


def _prep_up_kernel(w_ref, sel_ref, o_ref):
    group = 2 * PREP_LANES
    for c in range(w_ref.shape[2] // group):
        t = jnp.dot(w_ref[0, :, c * group:(c + 1) * group].astype(_BF16), sel_ref[...],
                    preferred_element_type=_F32)
        o_ref[0, :, c * PREP_LANES:(c + 1) * PREP_LANES] = t[:, :PREP_LANES].astype(_BF16)
        o_ref[0, :, D_FF + c * PREP_LANES:D_FF + (c + 1) * PREP_LANES] = t[:, PREP_LANES:].astype(_BF16)


def _prep_up(w_up):
    e, d, ff2 = w_up.shape
    i = jnp.arange(2 * PREP_LANES)[:, None]
    j = jnp.arange(2 * PREP_LANES)[None, :]
    sel = (i == jnp.where(j < PREP_LANES, 2 * j, 2 * (j - PREP_LANES) + 1)).astype(_BF16)
    blk = lambda x, r: (x, r, 0)
    return pl.pallas_call(
        _prep_up_kernel,
        grid=(e, d // PREP_ROWS),
        in_specs=[pl.BlockSpec((1, PREP_ROWS, ff2), blk), _const_spec((2 * PREP_LANES, 2 * PREP_LANES))],
        out_specs=pl.BlockSpec((1, PREP_ROWS, ff2), blk),
        out_shape=jax.ShapeDtypeStruct((e, d, ff2), _BF16),
        compiler_params=_params("arbitrary", "arbitrary"),
        name="prep_up",
    )(w_up, sel)


def _cast_kernel(w_ref, o_ref):
    o_ref[...] = w_ref[...].astype(o_ref.dtype)


def _prep_down(w_down):
    e, ff, d = w_down.shape
    blk = lambda x: (x, 0, 0)
    return pl.pallas_call(
        _cast_kernel,
        grid=(e,),
        in_specs=[pl.BlockSpec((1, ff, d), blk)],
        out_specs=pl.BlockSpec((1, ff, d), blk),
        out_shape=jax.ShapeDtypeStruct((e, ff, d), _BF16),
        compiler_params=_params("arbitrary"),
        name="prep_down",
    )(w_down)


def _dispatch(x_words, slot_of, p):
    n, words = x_words.shape
    nk = slot_of.shape[0]
    tiles = n // DISPATCH_ROWS
    part = words // DISPATCH_PARTS
    mesh = plsc.VectorSubcoreMesh(core_axis_name="core", subcore_axis_name="subcore",
                                  num_cores=SC_CORES, num_subcores=SC_SUBCORES)
    out = jax.ShapeDtypeStruct((p, part), x_words.dtype)

    @functools.partial(
        pl.kernel, out_type=[out] * DISPATCH_PARTS, mesh=mesh, scratch_types=[],
        compiler_params=pltpu.CompilerParams(use_tc_tiling_on_sc=True), name="moe_dispatch")
    def dispatch(x_hbm, slot_hbm, *o_hbm):
        for c in range(DISPATCH_PARTS):
            def body(x_vmem, slot_vmem, o_ref=o_hbm[c]):
                pltpu.sync_copy(x_vmem, o_ref.at[slot_vmem.at[0]])

            pltpu.emit_pipeline(
                body,
                grid=(nk // DISPATCH_ROWS,),
                in_specs=[pl.BlockSpec((DISPATCH_ROWS, part), lambda i, c=c: (i % tiles, c)),
                          pl.BlockSpec((1, DISPATCH_ROWS), lambda i: (0, i))],
                out_specs=[],
                core_axis_name=("core", "subcore"),
                dimension_semantics=(pltpu.PARALLEL,),
            )(x_hbm, slot_hbm)

    return dispatch(x_words, slot_of.reshape(1, nk))


def _combine_gather(y_parts, slot_of):
    nk = slot_of.shape[0]
    part = y_parts[0].shape[1]
    mesh = plsc.VectorSubcoreMesh(core_axis_name="core", subcore_axis_name="subcore",
                                  num_cores=SC_CORES, num_subcores=SC_SUBCORES)
    out = jax.ShapeDtypeStruct((nk, part), y_parts[0].dtype)

    @functools.partial(
        pl.kernel, out_type=[out] * DISPATCH_PARTS, mesh=mesh, scratch_types=[],
        compiler_params=pltpu.CompilerParams(use_tc_tiling_on_sc=True), name="moe_combine_gather")
    def gather(*refs):
        y_hbm, slot_hbm, o_hbm = refs[:DISPATCH_PARTS], refs[DISPATCH_PARTS], refs[DISPATCH_PARTS + 1:]
        for c in range(DISPATCH_PARTS):
            def body(slot_vmem, o_vmem, y_ref=y_hbm[c]):
                pltpu.sync_copy(y_ref.at[slot_vmem.at[0]], o_vmem)

            pltpu.emit_pipeline(
                body,
                grid=(nk // DISPATCH_ROWS,),
                in_specs=[pl.BlockSpec((1, DISPATCH_ROWS), lambda i: (0, i))],
                out_specs=[pl.BlockSpec((DISPATCH_ROWS, part), lambda i: (i, 0))],
                core_axis_name=("core", "subcore"),
                dimension_semantics=(pltpu.PARALLEL,),
            )(slot_hbm, o_hbm[c])

    return gather(*y_parts, slot_of.reshape(1, nk))


def _moe_ffn_kernel(block_e_ref, block_rows_ref, xa_ref, xb_ref, wu_ref, bu_ref, wd_ref, bd_ref,
                    oa_ref, ob_ref):
    rows_used = block_rows_ref[pl.program_id(0)]
    part = oa_ref.shape[1]
    for sub in range(MOE_STEP_ROWS // MOE_ROWS):
        rs = slice(sub * MOE_ROWS, (sub + 1) * MOE_ROWS)

        @pl.when(sub * MOE_ROWS < rows_used)
        def _():
            x = _unpack_bf16_pairs(jnp.concatenate([xa_ref[rs, :], xb_ref[rs, :]], axis=1))
            h = jnp.dot(x, wu_ref[0], preferred_element_type=_F32) + bu_ref[0]
            h_glu = jnp.minimum(h[:, :D_FF], SWIGLU_LIMIT)
            h_lin = jnp.clip(h[:, D_FF:], -SWIGLU_LIMIT, SWIGLU_LIMIT)
            act = h_glu * jax.nn.sigmoid(SWIGLU_ALPHA * h_glu) * (h_lin + 1.0)
            y = jnp.dot(act.astype(_BF16), wd_ref[0], preferred_element_type=_F32) + bd_ref[0]
            words = _pack_bf16_pairs(y)
            oa_ref[rs, :] = words[:, :part]
            ob_ref[rs, :] = words[:, part:]

        @pl.when(sub * MOE_ROWS >= rows_used)
        def _():
            oa_ref[rs, :] = jnp.zeros((MOE_ROWS, part), oa_ref.dtype)
            ob_ref[rs, :] = jnp.zeros((MOE_ROWS, part), ob_ref.dtype)


def _moe_ffn(xs, block_e, block_rows, wu, bu, wd, bd):
    xa, xb = xs
    p, part = xa.shape
    d = wd.shape[2]
    ff2 = wu.shape[2]
    row = lambda i, be, br: (i, 0)
    exp3 = lambda i, be, br: (be[i], 0, 0)
    rows_spec = pl.BlockSpec((MOE_STEP_ROWS, part), row)
    grid_spec = pltpu.PrefetchScalarGridSpec(
        num_scalar_prefetch=2,
        grid=(p // MOE_STEP_ROWS,),
        in_specs=[rows_spec, rows_spec,
                  pl.BlockSpec((1, d, ff2), exp3), pl.BlockSpec((1, 1, ff2), exp3),
                  pl.BlockSpec((1, ff2 // 2, d), exp3), pl.BlockSpec((1, 1, d), exp3)],
        out_specs=[rows_spec, rows_spec],
    )
    out = jax.ShapeDtypeStruct((p, part), xa.dtype)
    return pl.pallas_call(
        _moe_ffn_kernel,
        grid_spec=grid_spec,
        out_shape=[out, out],
        compiler_params=_params("arbitrary"),
        name="moe_ffn",
    )(block_e, block_rows, xa, xb, wu, bu, wd, bd)


def _combine_ln_kernel(ya_ref, yb_ref, gate_ref, res_ref, g_ref, b_ref, o_ref, obf_ref):
    gate = gate_ref[...]
    moe = None
    for k in range(TOP_K):
        words = jnp.concatenate([ya_ref[k], yb_ref[k]], axis=1)
        term = _unpack_bf16_pairs(words).astype(_F32) * gate[:, k:k + 1]
        moe = term if moe is None else moe + term
    y = _layer_norm_rows(DEEPNORM_ALPHA * res_ref[...] + moe, g_ref[...], b_ref[...])
    o_ref[...] = y
    obf_ref[...] = y.astype(_BF16)


def _combine_ln(y4_parts, gates, res, g, b):
    n, d = res.shape
    ya, yb = (t.reshape(TOP_K, n, t.shape[1]) for t in y4_parts)
    part = ya.shape[2]
    row = lambda i: (i, 0)
    y_spec = pl.BlockSpec((TOP_K, COMBINE_ROWS, part), lambda i: (0, i, 0))
    return pl.pallas_call(
        _combine_ln_kernel,
        grid=(n // COMBINE_ROWS,),
        in_specs=[y_spec, y_spec, pl.BlockSpec((COMBINE_ROWS, TOP_K), row),
                  pl.BlockSpec((COMBINE_ROWS, d), row), _const_spec((1, d)), _const_spec((1, d))],
        out_specs=[pl.BlockSpec((COMBINE_ROWS, d), row), pl.BlockSpec((COMBINE_ROWS, d), row)],
        out_shape=[jax.ShapeDtypeStruct((n, d), _F32), jax.ShapeDtypeStruct((n, d), _BF16)],
        compiler_params=_params("arbitrary"),
        name="combine_ln",
    )(ya, yb, gates, res, g.reshape(1, d), b.reshape(1, d))


def _moe_layer(x, x_words, router_w, router_b, w_up, b_up, w_down, b_down, ln_g, ln_b):
    n, d = x.shape
    nk = n * TOP_K
    idx_t, gate_t, rank_t, count = _router(x, router_w, router_b)
    counts = count[:, 0].astype(jnp.int32)
    padded = (counts + MOE_STEP_ROWS - 1) // MOE_STEP_ROWS * MOE_STEP_ROWS
    padded_ends = jnp.cumsum(padded)
    padded_starts = padded_ends - padded
    n_blocks = -(-(nk + N_EXPERTS * (MOE_STEP_ROWS - 1)) // MOE_STEP_ROWS)
    p = n_blocks * MOE_STEP_ROWS
    experts = jnp.arange(N_EXPERTS, dtype=jnp.int32)
    start_of = jnp.sum(jnp.where(idx_t[None] == experts[:, None, None],
                                 padded_starts[:, None, None], 0), axis=0)
    slot_of = (start_of + rank_t).reshape(nk)
    block_start = jnp.arange(n_blocks, dtype=jnp.int32) * MOE_STEP_ROWS
    block_e = jnp.minimum(jnp.sum(padded_ends[None, :] <= block_start[:, None], axis=1),
                          N_EXPERTS - 1).astype(jnp.int32)
    block_rows = jnp.where(block_start < padded_ends[-1],
                           jnp.clip(counts[block_e] - (block_start - padded_starts[block_e]),
                                    0, MOE_STEP_ROWS), 0).astype(jnp.int32)
    xs = _dispatch(x_words, slot_of, p)
    bu = jnp.concatenate([b_up[:, 0::2], b_up[:, 1::2]], axis=1).reshape(N_EXPERTS, 1, 2 * D_FF)
    ys = _moe_ffn(xs, block_e, block_rows, _prep_up(w_up), bu, _prep_down(w_down),
                  b_down.reshape(N_EXPERTS, 1, d))
    return _combine_ln(_combine_gather(ys, slot_of), gate_t.T, x, ln_g, ln_b)


def kernel(x, l0_w_in, l0_ret_gn_g, l0_conv_w, l0_conv_b, l0_conv_ln_g, l0_conv_ln_b, l0_w_out, l0_ln1_g, l0_ln1_b, l0_router_w, l0_router_b, l0_w_up, l0_b_up, l0_w_down, l0_b_down, l0_ln2_g, l0_ln2_b, l1_w_in, l1_w_out, l1_ln1_g, l1_ln1_b, l1_router_w, l1_router_b, l1_w_up, l1_b_up, l1_w_down, l1_b_down, l1_ln2_g, l1_ln2_b):
    batch, seq, d = x.shape
    n = batch * seq
    x0 = x.reshape(n, d)

    h = _proj(x0, l0_w_in.astype(_BF16))
    mix = _l0_mixer(h, batch, seq, l0_ret_gn_g, l0_conv_w, l0_conv_b, l0_conv_ln_g, l0_conv_ln_b)
    x1, x1_words = _proj_res_ln(mix, l0_w_out.astype(_BF16), x0, l0_ln1_g, l0_ln1_b)
    x2, x2_bf16 = _moe_layer(x1, x1_words, l0_router_w, l0_router_b, l0_w_up, l0_b_up, l0_w_down,
                             l0_b_down, l0_ln2_g, l0_ln2_b)

    qkv = _proj(x2_bf16, l1_w_in.astype(_BF16))
    att = _sb_attention(qkv, batch, seq)
    x3, x3_words = _proj_res_ln(att, l1_w_out.astype(_BF16), x2, l1_ln1_g, l1_ln1_b)
    x4, _ = _moe_layer(x3, x3_words, l1_router_w, l1_router_b, l1_w_up, l1_b_up, l1_w_down,
                       l1_b_down, l1_ln2_g, l1_ln2_b)
    return x4.reshape(batch, seq, d)
```

```python
import functools

import jax
import jax.numpy as jnp
from jax import lax
from jax.experimental import pallas as pl
from jax.experimental.pallas import tpu as pltpu
from jax.experimental.pallas import tpu_sc as plsc

D_MODEL = 1024
CHUNK = 64
RET_HEADS = 4
RET_DK = 64
RET_DV = 128
RET_QK_WIDTH = RET_HEADS * RET_DK
RET_WIDTH = RET_HEADS * RET_DV
ROPE_BASE = 10000.0
CONV_CH = D_MODEL // 2
CONV_WIDTH = 31
L0_IN = 2 * RET_QK_WIDTH + 2 * RET_WIDTH + 2 * CONV_CH
SB_HEADS = 8
SB_HEAD_DIM = D_MODEL // SB_HEADS
N_EXPERTS = 32
TOP_K = 4
D_FF = D_MODEL
SWIGLU_LIMIT = 7.0
SWIGLU_ALPHA = 1.702
LN_EPS = 1e-5
DEPTH = 2
DEEPNORM_ALPHA = (2 * DEPTH) ** 0.25

VMEM_LIMIT_BYTES = 48 * 1024 * 1024
F32_SUBLANES = 8
PROJ_ROWS = 512
MIX_ROWS = 256
CONV_HALO = 32
ROUTER_ROWS = 512
MOE_ROWS = 512
MOE_STEP_ROWS = 1024
PREP_ROWS = 512
PREP_LANES = 128
COMBINE_ROWS = 256
DISPATCH_ROWS = 128
DISPATCH_PARTS = 2
SC_CORES = 2
SC_SUBCORES = 16
SB_BLOCK = 128
SB_CUTOFF = 104.0

_F32 = jnp.float32
_BF16 = jnp.bfloat16


def _params(*sem):
    return pltpu.CompilerParams(dimension_semantics=sem, vmem_limit_bytes=VMEM_LIMIT_BYTES)


def _const_spec(shape):
    nd = len(shape)
    return pl.BlockSpec(shape, lambda *_: (0,) * nd)


def _layer_norm_rows(y, g, b):
    mu = jnp.mean(y, axis=-1, keepdims=True)
    yc = y - mu
    var = jnp.mean(yc * yc, axis=-1, keepdims=True)
    return yc * lax.rsqrt(var + LN_EPS) * g + b


def _silu(x):
    return x * jax.nn.sigmoid(x)


def _pack_bf16_pairs(y):
    half = y.shape[1] // 2
    lo = lax.bitcast_convert_type(y[:, :half].astype(_BF16).astype(_F32), jnp.uint32)
    hi = lax.bitcast_convert_type(y[:, half:].astype(_BF16).astype(_F32), jnp.uint32)
    return (lo >> 16) | (hi & jnp.uint32(0xFFFF0000))


def _unpack_bf16_pairs(w):
    lo = lax.bitcast_convert_type(w << 16, _F32)
    hi = lax.bitcast_convert_type(w & jnp.uint32(0xFFFF0000), _F32)
    return jnp.concatenate([lo, hi], axis=1).astype(_BF16)


def _proj_kernel(x_ref, w_ref, o_ref):
    o_ref[...] = jnp.dot(x_ref[...].astype(_BF16), w_ref[...],
                         preferred_element_type=_F32).astype(o_ref.dtype)


def _proj(x, w_bf16):
    n, d = x.shape
    width = w_bf16.shape[1]
    return pl.pallas_call(
        _proj_kernel,
        grid=(n // PROJ_ROWS,),
        in_specs=[pl.BlockSpec((PROJ_ROWS, d), lambda i: (i, 0)), _const_spec((d, width))],
        out_specs=pl.BlockSpec((PROJ_ROWS, width), lambda i: (i, 0)),
        out_shape=jax.ShapeDtypeStruct((n, width), _BF16),
        compiler_params=_params("arbitrary"),
        name="proj",
    )(x, w_bf16)


def _proj_res_ln_kernel(a_ref, w_ref, res_ref, g_ref, b_ref, o_ref, words_ref):
    m = jnp.dot(a_ref[...], w_ref[...], preferred_element_type=_F32)
    y = _layer_norm_rows(DEEPNORM_ALPHA * res_ref[...] + m, g_ref[...], b_ref[...])
    o_ref[...] = y
    words_ref[...] = _pack_bf16_pairs(y)


def _proj_res_ln(a_bf16, w_bf16, res, g, b):
    n, d = res.shape
    k = a_bf16.shape[1]
    row = lambda i: (i, 0)
    return pl.pallas_call(
        _proj_res_ln_kernel,
        grid=(n // PROJ_ROWS,),
        in_specs=[pl.BlockSpec((PROJ_ROWS, k), row), _const_spec((k, d)),
                  pl.BlockSpec((PROJ_ROWS, d), row), _const_spec((1, d)), _const_spec((1, d))],
        out_specs=[pl.BlockSpec((PROJ_ROWS, d), row), pl.BlockSpec((PROJ_ROWS, d // 2), row)],
        out_shape=[jax.ShapeDtypeStruct((n, d), _F32), jax.ShapeDtypeStruct((n, d // 2), jnp.uint32)],
        compiler_params=_params("arbitrary"),
        name="proj_res_ln",
    )(a_bf16, w_bf16, res, g.reshape(1, d), b.reshape(1, d))


def _l0_mixer_kernel(h_ref, cos_ref, sin_ref, xi_ref, zeta_ref, decay_ref, gstate_ref, gn_g_ref,
                     conv_w_ref, conv_b_ref, cln_g_ref, cln_b_ref, o_ref, state_ref, u_ref, shift_ref):
    rows = MIX_ROWS
    step = pl.program_id(1)

    @pl.when(step == 0)
    def _():
        state_ref[...] = jnp.zeros_like(state_ref)
        u_ref[0:CONV_HALO, :] = jnp.zeros((CONV_HALO, CONV_CH), _F32)

    lane = lax.broadcasted_iota(jnp.int32, (rows, RET_QK_WIDTH), 1)
    first_half = (lane % RET_DK) < (RET_DK // 2)
    cos = cos_ref[...]
    sin = sin_ref[...]

    def rotary(t):
        partner = jnp.where(first_half,
                            pltpu.roll(t, RET_QK_WIDTH - RET_DK // 2, axis=1),
                            pltpu.roll(t, RET_DK // 2, axis=1))
        return t * cos + partner * sin

    q = rotary(h_ref[:, 0:RET_QK_WIDTH].astype(_F32))
    k = rotary(h_ref[:, RET_QK_WIDTH:2 * RET_QK_WIDTH].astype(_F32)) * (RET_DK ** -0.5)
    q_in = (q * xi_ref[...]).astype(_BF16)
    k_out = (k * zeta_ref[...]).astype(_BF16)
    q = q.astype(_BF16)
    k = k.astype(_BF16)
    v_off = 2 * RET_QK_WIDTH
    g_off = v_off + RET_WIDTH
    for hd in range(RET_HEADS):
        qk = slice(hd * RET_DK, (hd + 1) * RET_DK)
        vs = slice(hd * RET_DV, (hd + 1) * RET_DV)
        v = h_ref[:, v_off + hd * RET_DV:v_off + (hd + 1) * RET_DV]
        s = lax.dot_general(q[:, qk], k[:, qk], (((1,), (1,)), ((), ())),
                            preferred_element_type=_F32) * decay_ref[hd]
        r = jnp.dot(s.astype(_BF16), v, preferred_element_type=_F32)
        r += jnp.dot(q_in[:, qk], state_ref[hd].astype(_BF16), preferred_element_type=_F32)
        kv = lax.dot_general(k_out[:, qk], v, (((0,), (0,)), ((), ())),
                             preferred_element_type=_F32)
        state_ref[hd] = gstate_ref[hd] * state_ref[hd] + kv
        mu = jnp.mean(r, axis=-1, keepdims=True)
        rc = r - mu
        var = jnp.mean(rc * rc, axis=-1, keepdims=True)
        rn = rc * lax.rsqrt(var + LN_EPS) * gn_g_ref[:, vs]
        gate = h_ref[:, g_off + hd * RET_DV:g_off + (hd + 1) * RET_DV].astype(_F32)
        o_ref[:, vs] = (_silu(gate) * rn).astype(o_ref.dtype)

    a_off = g_off + RET_WIDTH
    glu_a = h_ref[:, a_off:a_off + CONV_CH].astype(_F32)
    glu_b = h_ref[:, a_off + CONV_CH:a_off + 2 * CONV_CH].astype(_F32)
    u_ref[CONV_HALO:CONV_HALO + rows, :] = glu_a * jax.nn.sigmoid(glu_b)
    acc = jnp.broadcast_to(conv_b_ref[...], (rows, CONV_CH))
    first = CONV_HALO - (CONV_WIDTH - 1)
    for phase in range(F32_SUBLANES):
        offsets = [first + j for j in range(CONV_WIDTH) if (first + j) % F32_SUBLANES == phase]
        span = max(offsets) - phase + rows
        if phase == 0:
            src = u_ref
        else:
            shift_ref[0:span, :] = u_ref[phase:phase + span, :]
            src = shift_ref
        for off in offsets:
            j = off - first
            acc = acc + conv_w_ref[j:j + 1, :] * src[off - phase:off - phase + rows, :]
    u_ref[0:CONV_HALO, :] = u_ref[rows:rows + CONV_HALO, :]
    c = _silu(_layer_norm_rows(acc, cln_g_ref[...], cln_b_ref[...]))
    o_ref[:, RET_WIDTH:RET_WIDTH + CONV_CH] = c.astype(o_ref.dtype)


def _l0_mixer_tables(seq):
    half = RET_DK // 2
    inv = ROPE_BASE ** (-jnp.arange(half, dtype=_F32) / half)
    ang = jnp.arange(seq).astype(_F32)[:, None] * inv[None, :]
    cos = jnp.tile(jnp.cos(ang), (1, 2 * RET_HEADS))
    sin = jnp.tile(jnp.concatenate([-jnp.sin(ang), jnp.sin(ang)], axis=1), (1, RET_HEADS))
    log_g = jnp.log(1.0 - 2.0 ** (-5.0 - jnp.arange(RET_HEADS, dtype=_F32)))
    n = jnp.arange(MIX_ROWS, dtype=_F32)
    diff = n[:, None] - n[None, :]
    chunk = jnp.arange(MIX_ROWS) // CHUNK
    expo = jnp.where(chunk[:, None] == chunk[None, :], jnp.abs(diff), diff)
    decay = jnp.where((chunk[:, None] >= chunk[None, :])[None],
                      jnp.exp(expo[None] * log_g[:, None, None]), 0.0)
    xi = jnp.repeat(jnp.exp((n[:, None] + 1.0) * log_g[None, :]), RET_DK, axis=1)
    zeta = jnp.repeat(jnp.exp((MIX_ROWS - 1.0 - n[:, None]) * log_g[None, :]), RET_DK, axis=1)
    g_state = jnp.exp(MIX_ROWS * log_g)
    return cos, sin, xi, zeta, decay, g_state


def _l0_mixer(h, batch, seq, gn_g, conv_w, conv_b, cln_g, cln_b):
    n = h.shape[0]
    steps = seq // MIX_ROWS
    cos, sin, xi, zeta, decay, g_state = _l0_mixer_tables(seq)
    row = lambda b, i: (b * steps + i, 0)
    pos = lambda b, i: (i, 0)
    return pl.pallas_call(
        _l0_mixer_kernel,
        grid=(batch, steps),
        in_specs=[pl.BlockSpec((MIX_ROWS, L0_IN), row),
                  pl.BlockSpec((MIX_ROWS, RET_QK_WIDTH), pos),
                  pl.BlockSpec((MIX_ROWS, RET_QK_WIDTH), pos),
                  _const_spec((MIX_ROWS, RET_QK_WIDTH)), _const_spec((MIX_ROWS, RET_QK_WIDTH)),
                  _const_spec((RET_HEADS, MIX_ROWS, MIX_ROWS)),
                  pl.BlockSpec(memory_space=pltpu.SMEM),
                  _const_spec((1, RET_WIDTH)), _const_spec((CONV_WIDTH, CONV_CH)),
                  _const_spec((1, CONV_CH)), _const_spec((1, CONV_CH)), _const_spec((1, CONV_CH))],
        out_specs=pl.BlockSpec((MIX_ROWS, D_MODEL), row),
        out_shape=jax.ShapeDtypeStruct((n, D_MODEL), _BF16),
        scratch_shapes=[pltpu.VMEM((RET_HEADS, RET_DK, RET_DV), _F32),
                        pltpu.VMEM((MIX_ROWS + CONV_HALO, CONV_CH), _F32),
                        pltpu.VMEM((MIX_ROWS + CONV_HALO, CONV_CH), _F32)],
        compiler_params=_params("arbitrary", "arbitrary"),
        name="l0_mixer",
    )(h, cos, sin, xi, zeta, decay, g_state, gn_g.reshape(1, -1), conv_w,
      conv_b.reshape(1, -1), cln_g.reshape(1, -1), cln_b.reshape(1, -1))


def _sb_attention_kernel(q_ref, k_ref, v_ref, tri_ref, o_ref, acc_ref, carry_ref):
    blk = SB_BLOCK
    qi = pl.program_id(1)
    scale = SB_HEAD_DIM ** -0.5
    t_idx = lax.broadcasted_iota(jnp.int32, (blk, blk), 0)
    s_idx = lax.broadcasted_iota(jnp.int32, (blk, blk), 1)
    keep = s_idx < t_idx

    heads = [slice(hd * SB_HEAD_DIM, (hd + 1) * SB_HEAD_DIM) for hd in range(SB_HEADS)]

    def sweep(j, carry):
        masked = carry is None
        start = pl.multiple_of(j * blk, blk)
        zs = [lax.dot_general(q_ref[:, hs], k_ref[pl.ds(start, blk), hs], (((1,), (1,)), ((), ())),
                              preferred_element_type=_F32) * scale for hs in heads]
        log_betas, addends = [], []
        for z in zs:
            softplus = jnp.log(1.0 + jnp.exp(-jnp.abs(z)))
            log_beta = jnp.minimum(z, 0.0) - softplus
            log_rest = log_beta - z
            if masked:
                log_rest = jnp.where(keep, log_rest, 0.0)
            hi = log_rest.astype(_BF16)
            lo = (log_rest - hi.astype(_F32)).astype(_BF16)
            log_betas.append(log_beta)
            addends.append(jnp.concatenate([hi, lo], axis=1))
        sums = [jnp.dot(t, tri_ref[...], preferred_element_type=_F32) for t in addends]
        weights = []
        for hd, (log_beta, s) in enumerate(zip(log_betas, sums)):
            if masked:
                a = jnp.where(keep, jnp.exp(log_beta + s[:, :blk]), 0.0)
            else:
                a = jnp.exp(log_beta + s[:, :blk] + carry[:, heads[hd]])
            weights.append(a.astype(_BF16))
        outs = [jnp.dot(a, v_ref[pl.ds(start, blk), hs], preferred_element_type=_F32)
                for a, hs in zip(weights, heads)]
        return jnp.concatenate(outs, axis=1), jnp.concatenate([s[:, blk:] for s in sums], axis=1)

    out, carry = sweep(qi, None)
    acc_ref[...] = out
    carry_ref[...] = carry

    def cond(state):
        j, alive = state
        return jnp.logical_and(j >= 0, alive)

    def body(state):
        j, _ = state
        carry = carry_ref[...]
        out, total = sweep(j, carry)
        carry = carry + total
        acc_ref[...] += out
        carry_ref[...] = carry
        return j - 1, jnp.max(carry) > -SB_CUTOFF

    lax.while_loop(cond, body, (qi - 1, jnp.max(carry) > -SB_CUTOFF))
    o_ref[...] = acc_ref[...].astype(o_ref.dtype)


def _sb_attention(qkv, batch, seq):
    n = qkv.shape[0]
    nq = seq // SB_BLOCK
    j = jnp.arange(2 * SB_BLOCK)[:, None] % SB_BLOCK
    s = jnp.arange(SB_BLOCK)[None, :]
    tri = jnp.concatenate([(j > s).astype(_BF16), jnp.ones((2 * SB_BLOCK, SB_BLOCK), _BF16)], axis=1)
    whole_seq = lambda col: pl.BlockSpec((seq, D_MODEL), lambda b, i: (b, col),
                                         pipeline_mode=pl.Buffered(1))
    return pl.pallas_call(
        _sb_attention_kernel,
        grid=(batch, nq),
        in_specs=[pl.BlockSpec((SB_BLOCK, D_MODEL), lambda b, i: (b * nq + i, 0)),
                  whole_seq(1), whole_seq(2), _const_spec((2 * SB_BLOCK, 2 * SB_BLOCK))],
        out_specs=pl.BlockSpec((SB_BLOCK, D_MODEL), lambda b, i: (b * nq + i, 0)),
        out_shape=jax.ShapeDtypeStruct((n, D_MODEL), _BF16),
        scratch_shapes=[pltpu.VMEM((SB_BLOCK, D_MODEL), _F32), pltpu.VMEM((SB_BLOCK, D_MODEL), _F32)],
        compiler_params=_params("arbitrary", "arbitrary"),
        name="sb_attention",
    )(qkv, qkv, qkv, tri)


def _router_kernel(x_ref, w_ref, b_ref, tri_ref, idx_ref, gate_ref, rank_ref, count_ref, base_ref):
    @pl.when(pl.program_id(0) == 0)
    def _():
        base_ref[...] = jnp.zeros_like(base_ref)

    x = x_ref[...]
    w = w_ref[...]
    x_hi = x.astype(_BF16)
    x_lo = (x - x_hi.astype(_F32)).astype(_BF16)
    w_hi = w.astype(_BF16)
    w_lo = (w - w_hi.astype(_F32)).astype(_BF16)
    nt = (((1,), (1,)), ((), ()))
    logits = (lax.dot_general(w_hi, x_hi, nt, preferred_element_type=_F32)
              + lax.dot_general(w_lo, x_hi, nt, preferred_element_type=_F32)
              + lax.dot_general(w_hi, x_lo, nt, preferred_element_type=_F32)) + b_ref[...]
    expert = lax.broadcasted_iota(jnp.int32, logits.shape, 0)
    vals, idxs = [], []
    for _ in range(TOP_K):
        m = jnp.max(logits, axis=0, keepdims=True)
        sel = jnp.min(jnp.where(logits == m, expert, N_EXPERTS), axis=0, keepdims=True)
        vals.append(m)
        idxs.append(sel)
        logits = jnp.where(expert == sel, -jnp.inf, logits)
    vals = jnp.concatenate(vals, axis=0)
    e = jnp.exp(vals - vals[0:1])
    gate_ref[...] = e / jnp.sum(e, axis=0, keepdims=True)
    idx_ref[...] = jnp.concatenate(idxs, axis=0)

    member = jnp.zeros(logits.shape, _F32)
    for sel in idxs:
        member += (expert == sel).astype(_F32)
    before = jnp.dot(member.astype(_BF16), tri_ref[...], preferred_element_type=_F32) + base_ref[...]
    ranks = [jnp.sum(jnp.where(expert == sel, before, 0.0), axis=0, keepdims=True) for sel in idxs]
    rank_ref[...] = jnp.concatenate(ranks, axis=0).astype(jnp.int32)
    base_ref[...] += jnp.sum(member, axis=1, keepdims=True)
    count_ref[...] = base_ref[...]


def _router(x, router_w, router_b):
    n, d = x.shape
    t = jnp.arange(ROUTER_ROWS)
    tri = (t[:, None] < t[None, :]).astype(_BF16)
    tok = pl.BlockSpec((TOP_K, ROUTER_ROWS), lambda i: (0, i))
    return pl.pallas_call(
        _router_kernel,
        grid=(n // ROUTER_ROWS,),
        in_specs=[pl.BlockSpec((ROUTER_ROWS, d), lambda i: (i, 0)),
                  _const_spec((N_EXPERTS, d)), _const_spec((N_EXPERTS, 1)),
                  _const_spec((ROUTER_ROWS, ROUTER_ROWS))],
        out_specs=[tok, tok, tok, _const_spec((N_EXPERTS, 1))],
        out_shape=[jax.ShapeDtypeStruct((TOP_K, n), jnp.int32),
                   jax.ShapeDtypeStruct((TOP_K, n), _F32),
                   jax.ShapeDtypeStruct((TOP_K, n), jnp.int32),
                   jax.ShapeDtypeStruct((N_EXPERTS, 1), _F32)],
        scratch_shapes=[pltpu.VMEM((N_EXPERTS, 1), _F32)],
        compiler_params=_params("arbitrary"),
        name="router",
    )(x, router_w.T, router_b.reshape(N_EXPERTS, 1), tri)


def _prep_up_kernel(w_ref, sel_ref, o_ref):
    group = 2 * PREP_LANES
    for c in range(w_ref.shape[2] // group):
        t = jnp.dot(w_ref[0, :, c * group:(c + 1) * group].astype(_BF16), sel_ref[...],
                    preferred_element_type=_F32)
        o_ref[0, :, c * PREP_LANES:(c + 1) * PREP_LANES] = t[:, :PREP_LANES].astype(_BF16)
        o_ref[0, :, D_FF + c * PREP_LANES:D_FF + (c + 1) * PREP_LANES] = t[:, PREP_LANES:].astype(_BF16)


def _prep_up(w_up):
    e, d, ff2 = w_up.shape
    i = jnp.arange(2 * PREP_LANES)[:, None]
    j = jnp.arange(2 * PREP_LANES)[None, :]
    sel = (i == jnp.where(j < PREP_LANES, 2 * j, 2 * (j - PREP_LANES) + 1)).astype(_BF16)
    blk = lambda x, r: (x, r, 0)
    return pl.pallas_call(
        _prep_up_kernel,
        grid=(e, d // PREP_ROWS),
        in_specs=[pl.BlockSpec((1, PREP_ROWS, ff2), blk), _const_spec((2 * PREP_LANES, 2 * PREP_LANES))],
        out_specs=pl.BlockSpec((1, PREP_ROWS, ff2), blk),
        out_shape=jax.ShapeDtypeStruct((e, d, ff2), _BF16),
        compiler_params=_params("arbitrary", "arbitrary"),
        name="prep_up",
    )(w_up, sel)


def _cast_kernel(w_ref, o_ref):
    o_ref[...] = w_ref[...].astype(o_ref.dtype)


def _prep_down(w_down):
    e, ff, d = w_down.shape
    blk = lambda x: (x, 0, 0)
    return pl.pallas_call(
        _cast_kernel,
        grid=(e,),
        in_specs=[pl.BlockSpec((1, ff, d), blk)],
        out_specs=pl.BlockSpec((1, ff, d), blk),
        out_shape=jax.ShapeDtypeStruct((e, ff, d), _BF16),
        compiler_params=_params("arbitrary"),
        name="prep_down",
    )(w_down)


def _dispatch(x_words, slot_of, p):
    n, words = x_words.shape
    nk = slot_of.shape[0]
    tiles = n // DISPATCH_ROWS
    part = words // DISPATCH_PARTS
    mesh = plsc.VectorSubcoreMesh(core_axis_name="core", subcore_axis_name="subcore",
                                  num_cores=SC_CORES, num_subcores=SC_SUBCORES)
    out = jax.ShapeDtypeStruct((p, part), x_words.dtype)

    @functools.partial(
        pl.kernel, out_type=[out] * DISPATCH_PARTS, mesh=mesh, scratch_types=[],
        compiler_params=pltpu.CompilerParams(use_tc_tiling_on_sc=True), name="moe_dispatch")
    def dispatch(x_hbm, slot_hbm, *o_hbm):
        for c in range(DISPATCH_PARTS):
            def body(x_vmem, slot_vmem, o_ref=o_hbm[c]):
                pltpu.sync_copy(x_vmem, o_ref.at[slot_vmem.at[0]])

            pltpu.emit_pipeline(
                body,
                grid=(nk // DISPATCH_ROWS,),
                in_specs=[pl.BlockSpec((DISPATCH_ROWS, part), lambda i, c=c: (i % tiles, c)),
                          pl.BlockSpec((1, DISPATCH_ROWS), lambda i: (0, i))],
                out_specs=[],
                core_axis_name=("core", "subcore"),
                dimension_semantics=(pltpu.PARALLEL,),
            )(x_hbm, slot_hbm)

    return dispatch(x_words, slot_of.reshape(1, nk))


def _combine_gather(y_parts, slot_of):
    nk = slot_of.shape[0]
    part = y_parts[0].shape[1]
    mesh = plsc.VectorSubcoreMesh(core_axis_name="core", subcore_axis_name="subcore",
                                  num_cores=SC_CORES, num_subcores=SC_SUBCORES)
    out = jax.ShapeDtypeStruct((nk, part), y_parts[0].dtype)

    @functools.partial(
        pl.kernel, out_type=[out] * DISPATCH_PARTS, mesh=mesh, scratch_types=[],
        compiler_params=pltpu.CompilerParams(use_tc_tiling_on_sc=True), name="moe_combine_gather")
    def gather(*refs):
        y_hbm, slot_hbm, o_hbm = refs[:DISPATCH_PARTS], refs[DISPATCH_PARTS], refs[DISPATCH_PARTS + 1:]
        for c in range(DISPATCH_PARTS):
            def body(slot_vmem, o_vmem, y_ref=y_hbm[c]):
                pltpu.sync_copy(y_ref.at[slot_vmem.at[0]], o_vmem)

            pltpu.emit_pipeline(
                body,
                grid=(nk // DISPATCH_ROWS,),
                in_specs=[pl.BlockSpec((1, DISPATCH_ROWS), lambda i: (0, i))],
                out_specs=[pl.BlockSpec((DISPATCH_ROWS, part), lambda i: (i, 0))],
                core_axis_name=("core", "subcore"),
                dimension_semantics=(pltpu.PARALLEL,),
            )(slot_hbm, o_hbm[c])

    return gather(*y_parts, slot_of.reshape(1, nk))


def _moe_ffn_kernel(block_e_ref, block_rows_ref, xa_ref, xb_ref, wu_ref, bu_ref, wd_ref, bd_ref,
                    oa_ref, ob_ref, *, sub_rows):
    rows_used = block_rows_ref[pl.program_id(0)]
    part = oa_ref.shape[1]
    for sub in range(MOE_STEP_ROWS // sub_rows):
        rs = slice(sub * sub_rows, (sub + 1) * sub_rows)

        @pl.when(sub * sub_rows < rows_used)
        def _():
            x = _unpack_bf16_pairs(jnp.concatenate([xa_ref[rs, :], xb_ref[rs, :]], axis=1))
            h = jnp.dot(x, wu_ref[0], preferred_element_type=_F32) + bu_ref[0]
            h_glu = jnp.minimum(h[:, :D_FF], SWIGLU_LIMIT)
            h_lin = jnp.clip(h[:, D_FF:], -SWIGLU_LIMIT, SWIGLU_LIMIT)
            act = h_glu * jax.nn.sigmoid(SWIGLU_ALPHA * h_glu) * (h_lin + 1.0)
            y = jnp.dot(act.astype(_BF16), wd_ref[0], preferred_element_type=_F32) + bd_ref[0]
            words = _pack_bf16_pairs(y)
            oa_ref[rs, :] = words[:, :part]
            ob_ref[rs, :] = words[:, part:]

        @pl.when(sub * sub_rows >= rows_used)
        def _():
            oa_ref[rs, :] = jnp.zeros((sub_rows, part), oa_ref.dtype)
            ob_ref[rs, :] = jnp.zeros((sub_rows, part), ob_ref.dtype)


def _moe_ffn(xs, block_e, block_rows, wu, bu, wd, bd, sub_rows):
    xa, xb = xs
    p, part = xa.shape
    d = wd.shape[2]
    ff2 = wu.shape[2]
    row = lambda i, be, br: (i, 0)
    exp3 = lambda i, be, br: (be[i], 0, 0)
    rows_spec = pl.BlockSpec((MOE_STEP_ROWS, part), row)
    grid_spec = pltpu.PrefetchScalarGridSpec(
        num_scalar_prefetch=2,
        grid=(p // MOE_STEP_ROWS,),
        in_specs=[rows_spec, rows_spec,
                  pl.BlockSpec((1, d, ff2), exp3), pl.BlockSpec((1, 1, ff2), exp3),
                  pl.BlockSpec((1, ff2 // 2, d), exp3), pl.BlockSpec((1, 1, d), exp3)],
        out_specs=[rows_spec, rows_spec],
    )
    out = jax.ShapeDtypeStruct((p, part), xa.dtype)
    return pl.pallas_call(
        functools.partial(_moe_ffn_kernel, sub_rows=sub_rows),
        grid_spec=grid_spec,
        out_shape=[out, out],
        compiler_params=_params("arbitrary"),
        name="moe_ffn",
    )(block_e, block_rows, xa, xb, wu, bu, wd, bd)


def _combine_ln_kernel(ya_ref, yb_ref, gate_ref, res_ref, g_ref, b_ref, o_ref, obf_ref):
    gate = gate_ref[...]
    moe = None
    for k in range(TOP_K):
        words = jnp.concatenate([ya_ref[k], yb_ref[k]], axis=1)
        term = _unpack_bf16_pairs(words).astype(_F32) * gate[:, k:k + 1]
        moe = term if moe is None else moe + term
    y = _layer_norm_rows(DEEPNORM_ALPHA * res_ref[...] + moe, g_ref[...], b_ref[...])
    o_ref[...] = y
    obf_ref[...] = y.astype(_BF16)


def _combine_ln(y4_parts, gates, res, g, b):
    n, d = res.shape
    ya, yb = (t.reshape(TOP_K, n, t.shape[1]) for t in y4_parts)
    part = ya.shape[2]
    row = lambda i: (i, 0)
    y_spec = pl.BlockSpec((TOP_K, COMBINE_ROWS, part), lambda i: (0, i, 0))
    return pl.pallas_call(
        _combine_ln_kernel,
        grid=(n // COMBINE_ROWS,),
        in_specs=[y_spec, y_spec, pl.BlockSpec((COMBINE_ROWS, TOP_K), row),
                  pl.BlockSpec((COMBINE_ROWS, d), row), _const_spec((1, d)), _const_spec((1, d))],
        out_specs=[pl.BlockSpec((COMBINE_ROWS, d), row), pl.BlockSpec((COMBINE_ROWS, d), row)],
        out_shape=[jax.ShapeDtypeStruct((n, d), _F32), jax.ShapeDtypeStruct((n, d), _BF16)],
        compiler_params=_params("arbitrary"),
        name="combine_ln",
    )(ya, yb, gates, res, g.reshape(1, d), b.reshape(1, d))


def _moe_layer(x, x_words, router_w, router_b, w_up, b_up, w_down, b_down, ln_g, ln_b,
               sub_rows):
    n, d = x.shape
    nk = n * TOP_K
    idx_t, gate_t, rank_t, count = _router(x, router_w, router_b)
    counts = count[:, 0].astype(jnp.int32)
    padded = (counts + MOE_STEP_ROWS - 1) // MOE_STEP_ROWS * MOE_STEP_ROWS
    padded_ends = jnp.cumsum(padded)
    padded_starts = padded_ends - padded
    n_blocks = -(-(nk + N_EXPERTS * (MOE_STEP_ROWS - 1)) // MOE_STEP_ROWS)
    p = n_blocks * MOE_STEP_ROWS
    experts = jnp.arange(N_EXPERTS, dtype=jnp.int32)
    start_of = jnp.sum(jnp.where(idx_t[None] == experts[:, None, None],
                                 padded_starts[:, None, None], 0), axis=0)
    slot_of = (start_of + rank_t).reshape(nk)
    block_start = jnp.arange(n_blocks, dtype=jnp.int32) * MOE_STEP_ROWS
    block_e = jnp.minimum(jnp.sum(padded_ends[None, :] <= block_start[:, None], axis=1),
                          N_EXPERTS - 1).astype(jnp.int32)
    block_rows = jnp.where(block_start < padded_ends[-1],
                           jnp.clip(counts[block_e] - (block_start - padded_starts[block_e]),
                                    0, MOE_STEP_ROWS), 0).astype(jnp.int32)
    xs = _dispatch(x_words, slot_of, p)
    bu = jnp.concatenate([b_up[:, 0::2], b_up[:, 1::2]], axis=1).reshape(N_EXPERTS, 1, 2 * D_FF)
    ys = _moe_ffn(xs, block_e, block_rows, _prep_up(w_up), bu, _prep_down(w_down),
                  b_down.reshape(N_EXPERTS, 1, d), sub_rows)
    return _combine_ln(_combine_gather(ys, slot_of), gate_t.T, x, ln_g, ln_b)


def kernel(x, l0_w_in, l0_ret_gn_g, l0_conv_w, l0_conv_b, l0_conv_ln_g, l0_conv_ln_b, l0_w_out, l0_ln1_g, l0_ln1_b, l0_router_w, l0_router_b, l0_w_up, l0_b_up, l0_w_down, l0_b_down, l0_ln2_g, l0_ln2_b, l1_w_in, l1_w_out, l1_ln1_g, l1_ln1_b, l1_router_w, l1_router_b, l1_w_up, l1_b_up, l1_w_down, l1_b_down, l1_ln2_g, l1_ln2_b):
    batch, seq, d = x.shape
    n = batch * seq
    x0 = x.reshape(n, d)

    h = _proj(x0, l0_w_in.astype(_BF16))
    mix = _l0_mixer(h, batch, seq, l0_ret_gn_g, l0_conv_w, l0_conv_b, l0_conv_ln_g, l0_conv_ln_b)
    x1, x1_words = _proj_res_ln(mix, l0_w_out.astype(_BF16), x0, l0_ln1_g, l0_ln1_b)
    x2, x2_bf16 = _moe_layer(x1, x1_words, l0_router_w, l0_router_b, l0_w_up, l0_b_up, l0_w_down,
                             l0_b_down, l0_ln2_g, l0_ln2_b, 512)

    qkv = _proj(x2_bf16, l1_w_in.astype(_BF16))
    att = _sb_attention(qkv, batch, seq)
    x3, x3_words = _proj_res_ln(att, l1_w_out.astype(_BF16), x2, l1_ln1_g, l1_ln1_b)
    x4, _ = _moe_layer(x3, x3_words, l1_router_w, l1_router_b, l1_w_up, l1_b_up, l1_w_down,
                       l1_b_down, l1_ln2_g, l1_ln2_b, 1024)
    return x4.reshape(batch, seq, d)
```

```python
import functools

import jax
import jax.numpy as jnp
from jax import lax
from jax.experimental import pallas as pl
from jax.experimental.pallas import tpu as pltpu
from jax.experimental.pallas import tpu_sc as plsc

D_MODEL = 1024
CHUNK = 64
RET_HEADS = 4
RET_DK = 64
RET_DV = 128
RET_QK_WIDTH = RET_HEADS * RET_DK
RET_WIDTH = RET_HEADS * RET_DV
ROPE_BASE = 10000.0
CONV_CH = D_MODEL // 2
CONV_WIDTH = 31
L0_IN = 2 * RET_QK_WIDTH + 2 * RET_WIDTH + 2 * CONV_CH
SB_HEADS = 8
SB_HEAD_DIM = D_MODEL // SB_HEADS
N_EXPERTS = 32
TOP_K = 4
D_FF = D_MODEL
SWIGLU_LIMIT = 7.0
SWIGLU_ALPHA = 1.702
LN_EPS = 1e-5
DEPTH = 2
DEEPNORM_ALPHA = (2 * DEPTH) ** 0.25

VMEM_LIMIT_BYTES = 48 * 1024 * 1024
F32_SUBLANES = 8
PROJ_ROWS = 512
MIX_ROWS = 256
CONV_HALO = 32
ROUTER_ROWS = 512
MOE_ROWS = 512
MOE_STEP_ROWS = 1024
PREP_ROWS = 512
PREP_LANES = 128
COMBINE_ROWS = 256
DISPATCH_ROWS = 128
DISPATCH_PARTS = 2
SC_CORES = 2
SC_SUBCORES = 16
SB_BLOCK = 128
SB_CUTOFF = 104.0

_F32 = jnp.float32
_BF16 = jnp.bfloat16


def _params(*sem):
    return pltpu.CompilerParams(dimension_semantics=sem, vmem_limit_bytes=VMEM_LIMIT_BYTES)


def _const_spec(shape):
    nd = len(shape)
    return pl.BlockSpec(shape, lambda *_: (0,) * nd)


def _layer_norm_rows(y, g, b):
    mu = jnp.mean(y, axis=-1, keepdims=True)
    yc = y - mu
    var = jnp.mean(yc * yc, axis=-1, keepdims=True)
    return yc * lax.rsqrt(var + LN_EPS) * g + b


def _silu(x):
    return x * jax.nn.sigmoid(x)


def _pack_bf16_pairs(y):
    half = y.shape[1] // 2
    lo = lax.bitcast_convert_type(y[:, :half].astype(_BF16).astype(_F32), jnp.uint32)
    hi = lax.bitcast_convert_type(y[:, half:].astype(_BF16).astype(_F32), jnp.uint32)
    return (lo >> 16) | (hi & jnp.uint32(0xFFFF0000))


def _unpack_bf16_pairs(w):
    lo = lax.bitcast_convert_type(w << 16, _F32)
    hi = lax.bitcast_convert_type(w & jnp.uint32(0xFFFF0000), _F32)
    return jnp.concatenate([lo, hi], axis=1).astype(_BF16)


def _proj_kernel(x_ref, w_ref, o_ref):
    o_ref[...] = jnp.dot(x_ref[...].astype(_BF16), w_ref[...],
                         preferred_element_type=_F32).astype(o_ref.dtype)


def _proj(x, w_bf16):
    n, d = x.shape
    width = w_bf16.shape[1]
    return pl.pallas_call(
        _proj_kernel,
        grid=(n // PROJ_ROWS,),
        in_specs=[pl.BlockSpec((PROJ_ROWS, d), lambda i: (i, 0)), _const_spec((d, width))],
        out_specs=pl.BlockSpec((PROJ_ROWS, width), lambda i: (i, 0)),
        out_shape=jax.ShapeDtypeStruct((n, width), _BF16),
        compiler_params=_params("arbitrary"),
        name="proj",
    )(x, w_bf16)


def _proj_res_ln_kernel(a_ref, w_ref, res_ref, g_ref, b_ref, o_ref, words_ref):
    m = jnp.dot(a_ref[...], w_ref[...], preferred_element_type=_F32)
    y = _layer_norm_rows(DEEPNORM_ALPHA * res_ref[...] + m, g_ref[...], b_ref[...])
    o_ref[...] = y
    words_ref[...] = _pack_bf16_pairs(y)


def _proj_res_ln(a_bf16, w_bf16, res, g, b):
    n, d = res.shape
    k = a_bf16.shape[1]
    row = lambda i: (i, 0)
    return pl.pallas_call(
        _proj_res_ln_kernel,
        grid=(n // PROJ_ROWS,),
        in_specs=[pl.BlockSpec((PROJ_ROWS, k), row), _const_spec((k, d)),
                  pl.BlockSpec((PROJ_ROWS, d), row), _const_spec((1, d)), _const_spec((1, d))],
        out_specs=[pl.BlockSpec((PROJ_ROWS, d), row), pl.BlockSpec((PROJ_ROWS, d // 2), row)],
        out_shape=[jax.ShapeDtypeStruct((n, d), _F32), jax.ShapeDtypeStruct((n, d // 2), jnp.uint32)],
        compiler_params=_params("arbitrary"),
        name="proj_res_ln",
    )(a_bf16, w_bf16, res, g.reshape(1, d), b.reshape(1, d))


def _l0_mixer_kernel(h_ref, cos_ref, sin_ref, xi_ref, zeta_ref, decay_ref, gstate_ref, gn_g_ref,
                     conv_w_ref, conv_b_ref, cln_g_ref, cln_b_ref, o_ref, state_ref, u_ref, shift_ref):
    rows = MIX_ROWS
    step = pl.program_id(1)

    @pl.when(step == 0)
    def _():
        state_ref[...] = jnp.zeros_like(state_ref)
        u_ref[0:CONV_HALO, :] = jnp.zeros((CONV_HALO, CONV_CH), _F32)

    lane = lax.broadcasted_iota(jnp.int32, (rows, RET_QK_WIDTH), 1)
    first_half = (lane % RET_DK) < (RET_DK // 2)
    cos = cos_ref[...]
    sin = sin_ref[...]

    def rotary(t):
        partner = jnp.where(first_half,
                            pltpu.roll(t, RET_QK_WIDTH - RET_DK // 2, axis=1),
                            pltpu.roll(t, RET_DK // 2, axis=1))
        return t * cos + partner * sin

    q = rotary(h_ref[:, 0:RET_QK_WIDTH].astype(_F32))
    k = rotary(h_ref[:, RET_QK_WIDTH:2 * RET_QK_WIDTH].astype(_F32)) * (RET_DK ** -0.5)
    q_in = (q * xi_ref[...]).astype(_BF16)
    k_out = (k * zeta_ref[...]).astype(_BF16)
    q = q.astype(_BF16)
    k = k.astype(_BF16)
    v_off = 2 * RET_QK_WIDTH
    g_off = v_off + RET_WIDTH
    for hd in range(RET_HEADS):
        qk = slice(hd * RET_DK, (hd + 1) * RET_DK)
        vs = slice(hd * RET_DV, (hd + 1) * RET_DV)
        v = h_ref[:, v_off + hd * RET_DV:v_off + (hd + 1) * RET_DV]
        s = lax.dot_general(q[:, qk], k[:, qk], (((1,), (1,)), ((), ())),
                            preferred_element_type=_F32) * decay_ref[hd]
        r = jnp.dot(s.astype(_BF16), v, preferred_element_type=_F32)
        r += jnp.dot(q_in[:, qk], state_ref[hd].astype(_BF16), preferred_element_type=_F32)
        kv = lax.dot_general(k_out[:, qk], v, (((0,), (0,)), ((), ())),
                             preferred_element_type=_F32)
        state_ref[hd] = gstate_ref[hd] * state_ref[hd] + kv
        mu = jnp.mean(r, axis=-1, keepdims=True)
        rc = r - mu
        var = jnp.mean(rc * rc, axis=-1, keepdims=True)
        rn = rc * lax.rsqrt(var + LN_EPS) * gn_g_ref[:, vs]
        gate = h_ref[:, g_off + hd * RET_DV:g_off + (hd + 1) * RET_DV].astype(_F32)
        o_ref[:, vs] = (_silu(gate) * rn).astype(o_ref.dtype)

    a_off = g_off + RET_WIDTH
    glu_a = h_ref[:, a_off:a_off + CONV_CH].astype(_F32)
    glu_b = h_ref[:, a_off + CONV_CH:a_off + 2 * CONV_CH].astype(_F32)
    u_ref[CONV_HALO:CONV_HALO + rows, :] = glu_a * jax.nn.sigmoid(glu_b)
    acc = jnp.broadcast_to(conv_b_ref[...], (rows, CONV_CH))
    first = CONV_HALO - (CONV_WIDTH - 1)
    for phase in range(F32_SUBLANES):
        offsets = [first + j for j in range(CONV_WIDTH) if (first + j) % F32_SUBLANES == phase]
        span = max(offsets) - phase + rows
        if phase == 0:
            src = u_ref
        else:
            shift_ref[0:span, :] = u_ref[phase:phase + span, :]
            src = shift_ref
        for off in offsets:
            j = off - first
            acc = acc + conv_w_ref[j:j + 1, :] * src[off - phase:off - phase + rows, :]
    u_ref[0:CONV_HALO, :] = u_ref[rows:rows + CONV_HALO, :]
    c = _silu(_layer_norm_rows(acc, cln_g_ref[...], cln_b_ref[...]))
    o_ref[:, RET_WIDTH:RET_WIDTH + CONV_CH] = c.astype(o_ref.dtype)


def _l0_mixer_tables(seq):
    half = RET_DK // 2
    inv = ROPE_BASE ** (-jnp.arange(half, dtype=_F32) / half)
    ang = jnp.arange(seq).astype(_F32)[:, None] * inv[None, :]
    cos = jnp.tile(jnp.cos(ang), (1, 2 * RET_HEADS))
    sin = jnp.tile(jnp.concatenate([-jnp.sin(ang), jnp.sin(ang)], axis=1), (1, RET_HEADS))
    log_g = jnp.log(1.0 - 2.0 ** (-5.0 - jnp.arange(RET_HEADS, dtype=_F32)))
    n = jnp.arange(MIX_ROWS, dtype=_F32)
    diff = n[:, None] - n[None, :]
    chunk = jnp.arange(MIX_ROWS) // CHUNK
    expo = jnp.where(chunk[:, None] == chunk[None, :], jnp.abs(diff), diff)
    decay = jnp.where((chunk[:, None] >= chunk[None, :])[None],
                      jnp.exp(expo[None] * log_g[:, None, None]), 0.0)
    xi = jnp.repeat(jnp.exp((n[:, None] + 1.0) * log_g[None, :]), RET_DK, axis=1)
    zeta = jnp.repeat(jnp.exp((MIX_ROWS - 1.0 - n[:, None]) * log_g[None, :]), RET_DK, axis=1)
    g_state = jnp.exp(MIX_ROWS * log_g)
    return cos, sin, xi, zeta, decay, g_state


def _l0_mixer(h, batch, seq, gn_g, conv_w, conv_b, cln_g, cln_b):
    n = h.shape[0]
    steps = seq // MIX_ROWS
    cos, sin, xi, zeta, decay, g_state = _l0_mixer_tables(seq)
    row = lambda b, i: (b * steps + i, 0)
    pos = lambda b, i: (i, 0)
    return pl.pallas_call(
        _l0_mixer_kernel,
        grid=(batch, steps),
        in_specs=[pl.BlockSpec((MIX_ROWS, L0_IN), row),
                  pl.BlockSpec((MIX_ROWS, RET_QK_WIDTH), pos),
                  pl.BlockSpec((MIX_ROWS, RET_QK_WIDTH), pos),
                  _const_spec((MIX_ROWS, RET_QK_WIDTH)), _const_spec((MIX_ROWS, RET_QK_WIDTH)),
                  _const_spec((RET_HEADS, MIX_ROWS, MIX_ROWS)),
                  pl.BlockSpec(memory_space=pltpu.SMEM),
                  _const_spec((1, RET_WIDTH)), _const_spec((CONV_WIDTH, CONV_CH)),
                  _const_spec((1, CONV_CH)), _const_spec((1, CONV_CH)), _const_spec((1, CONV_CH))],
        out_specs=pl.BlockSpec((MIX_ROWS, D_MODEL), row),
        out_shape=jax.ShapeDtypeStruct((n, D_MODEL), _BF16),
        scratch_shapes=[pltpu.VMEM((RET_HEADS, RET_DK, RET_DV), _F32),
                        pltpu.VMEM((MIX_ROWS + CONV_HALO, CONV_CH), _F32),
                        pltpu.VMEM((MIX_ROWS + CONV_HALO, CONV_CH), _F32)],
        compiler_params=_params("arbitrary", "arbitrary"),
        name="l0_mixer",
    )(h, cos, sin, xi, zeta, decay, g_state, gn_g.reshape(1, -1), conv_w,
      conv_b.reshape(1, -1), cln_g.reshape(1, -1), cln_b.reshape(1, -1))


def _sb_attention_kernel(q_ref, k_ref, v_ref, tri_ref, o_ref, acc_ref, carry_ref):
    blk = SB_BLOCK
    qi = pl.program_id(1)
    t_idx = lax.broadcasted_iota(jnp.int32, (blk, blk), 0)
    s_idx = lax.broadcasted_iota(jnp.int32, (blk, blk), 1)
    keep = s_idx < t_idx

    heads = [slice(hd * SB_HEAD_DIM, (hd + 1) * SB_HEAD_DIM) for hd in range(SB_HEADS)]

    def sweep(j, carry):
        masked = carry is None
        start = pl.multiple_of(j * blk, blk)
        zs = [lax.dot_general(q_ref[:, hs], k_ref[pl.ds(start, blk), hs], (((1,), (1,)), ((), ())),
                              preferred_element_type=_F32) for hs in heads]
        log_betas, addends = [], []
        for z in zs:
            softplus = jnp.log(1.0 + jnp.exp(-jnp.abs(z)))
            log_beta = jnp.minimum(z, 0.0) - softplus
            log_rest = log_beta - z
            if masked:
                log_rest = jnp.where(keep, log_rest, 0.0)
            hi = log_rest.astype(_BF16)
            lo = (log_rest - hi.astype(_F32)).astype(_BF16)
            log_betas.append(log_beta)
            addends.append(jnp.concatenate([hi, lo], axis=1))
        sums = [jnp.dot(t, tri_ref[...], preferred_element_type=_F32) for t in addends]
        weights = []
        for hd, (log_beta, s) in enumerate(zip(log_betas, sums)):
            if masked:
                a = jnp.where(keep, jnp.exp(log_beta + s[:, :blk]), 0.0)
            else:
                a = jnp.exp(log_beta + s[:, :blk] + carry[:, heads[hd]])
            weights.append(a.astype(_BF16))
        outs = [jnp.dot(a, v_ref[pl.ds(start, blk), hs], preferred_element_type=_F32)
                for a, hs in zip(weights, heads)]
        return jnp.concatenate(outs, axis=1), jnp.concatenate([s[:, blk:] for s in sums], axis=1)

    out, carry = sweep(qi, None)
    acc_ref[...] = out
    carry_ref[...] = carry

    def cond(state):
        j, alive = state
        return jnp.logical_and(j >= 0, alive)

    def body(state):
        j, _ = state
        carry = carry_ref[...]
        out, total = sweep(j, carry)
        carry = carry + total
        acc_ref[...] += out
        carry_ref[...] = carry
        return j - 1, jnp.max(carry) > -SB_CUTOFF

    lax.while_loop(cond, body, (qi - 1, jnp.max(carry) > -SB_CUTOFF))
    o_ref[...] = acc_ref[...].astype(o_ref.dtype)


def _sb_attention(qkv, batch, seq):
    n = qkv.shape[0]
    nq = seq // SB_BLOCK
    j = jnp.arange(2 * SB_BLOCK)[:, None] % SB_BLOCK
    s = jnp.arange(SB_BLOCK)[None, :]
    tri = jnp.concatenate([(j > s).astype(_BF16), jnp.ones((2 * SB_BLOCK, SB_BLOCK), _BF16)], axis=1)
    whole_seq = lambda col: pl.BlockSpec((seq, D_MODEL), lambda b, i: (b, col),
                                         pipeline_mode=pl.Buffered(1))
    return pl.pallas_call(
        _sb_attention_kernel,
        grid=(batch, nq),
        in_specs=[pl.BlockSpec((SB_BLOCK, D_MODEL), lambda b, i: (b * nq + i, 0)),
                  whole_seq(1), whole_seq(2), _const_spec((2 * SB_BLOCK, 2 * SB_BLOCK))],
        out_specs=pl.BlockSpec((SB_BLOCK, D_MODEL), lambda b, i: (b * nq + i, 0)),
        out_shape=jax.ShapeDtypeStruct((n, D_MODEL), _BF16),
        scratch_shapes=[pltpu.VMEM((SB_BLOCK, D_MODEL), _F32), pltpu.VMEM((SB_BLOCK, D_MODEL), _F32)],
        compiler_params=_params("arbitrary", "arbitrary"),
        name="sb_attention",
    )(qkv, qkv, qkv, tri)


def _router_kernel(x_ref, w_ref, b_ref, tri_ref, idx_ref, gate_ref, rank_ref, count_ref, base_ref):
    @pl.when(pl.program_id(0) == 0)
    def _():
        base_ref[...] = jnp.zeros_like(base_ref)

    x = x_ref[...]
    w = w_ref[...]
    x_hi = x.astype(_BF16)
    x_lo = (x - x_hi.astype(_F32)).astype(_BF16)
    w_hi = w.astype(_BF16)
    w_lo = (w - w_hi.astype(_F32)).astype(_BF16)
    nt = (((1,), (1,)), ((), ()))
    logits = (lax.dot_general(w_hi, x_hi, nt, preferred_element_type=_F32)
              + lax.dot_general(w_lo, x_hi, nt, preferred_element_type=_F32)
              + lax.dot_general(w_hi, x_lo, nt, preferred_element_type=_F32)) + b_ref[...]
    expert = lax.broadcasted_iota(jnp.int32, logits.shape, 0)
    vals, idxs = [], []
    for _ in range(TOP_K):
        m = jnp.max(logits, axis=0, keepdims=True)
        sel = jnp.min(jnp.where(logits == m, expert, N_EXPERTS), axis=0, keepdims=True)
        vals.append(m)
        idxs.append(sel)
        logits = jnp.where(expert == sel, -jnp.inf, logits)
    vals = jnp.concatenate(vals, axis=0)
    e = jnp.exp(vals - vals[0:1])
    gate_ref[...] = e / jnp.sum(e, axis=0, keepdims=True)
    idx_ref[...] = jnp.concatenate(idxs, axis=0)

    member = jnp.zeros(logits.shape, _F32)
    for sel in idxs:
        member += (expert == sel).astype(_F32)
    before = jnp.dot(member.astype(_BF16), tri_ref[...], preferred_element_type=_F32) + base_ref[...]
    ranks = [jnp.sum(jnp.where(expert == sel, before, 0.0), axis=0, keepdims=True) for sel in idxs]
    rank_ref[...] = jnp.concatenate(ranks, axis=0).astype(jnp.int32)
    base_ref[...] += jnp.sum(member, axis=1, keepdims=True)
    count_ref[...] = base_ref[...]


def _router(x, router_w, router_b):
    n, d = x.shape
    t = jnp.arange(ROUTER_ROWS)
    tri = (t[:, None] < t[None, :]).astype(_BF16)
    tok = pl.BlockSpec((TOP_K, ROUTER_ROWS), lambda i: (0, i))
    return pl.pallas_call(
        _router_kernel,
        grid=(n // ROUTER_ROWS,),
        in_specs=[pl.BlockSpec((ROUTER_ROWS, d), lambda i: (i, 0)),
                  _const_spec((N_EXPERTS, d)), _const_spec((N_EXPERTS, 1)),
                  _const_spec((ROUTER_ROWS, ROUTER_ROWS))],
        out_specs=[tok, tok, tok, _const_spec((N_EXPERTS, 1))],
        out_shape=[jax.ShapeDtypeStruct((TOP_K, n), jnp.int32),
                   jax.ShapeDtypeStruct((TOP_K, n), _F32),
                   jax.ShapeDtypeStruct((TOP_K, n), jnp.int32),
                   jax.ShapeDtypeStruct((N_EXPERTS, 1), _F32)],
        scratch_shapes=[pltpu.VMEM((N_EXPERTS, 1), _F32)],
        compiler_params=_params("arbitrary"),
        name="router",
    )(x, router_w.T, router_b.reshape(N_EXPERTS, 1), tri)


def _prep_up_kernel(w_ref, sel_ref, o_ref):
    group = 2 * PREP_LANES
    for c in range(w_ref.shape[2] // group):
        t = jnp.dot(w_ref[0, :, c * group:(c + 1) * group].astype(_BF16), sel_ref[...],
                    preferred_element_type=_F32)
        o_ref[0, :, c * PREP_LANES:(c + 1) * PREP_LANES] = t[:, :PREP_LANES].astype(_BF16)
        o_ref[0, :, D_FF + c * PREP_LANES:D_FF + (c + 1) * PREP_LANES] = t[:, PREP_LANES:].astype(_BF16)


def _prep_up(w_up):
    e, d, ff2 = w_up.shape
    i = jnp.arange(2 * PREP_LANES)[:, None]
    j = jnp.arange(2 * PREP_LANES)[None, :]
    sel = (i == jnp.where(j < PREP_LANES, 2 * j, 2 * (j - PREP_LANES) + 1)).astype(_BF16)
    blk = lambda x, r: (x, r, 0)
    return pl.pallas_call(
        _prep_up_kernel,
        grid=(e, d // PREP_ROWS),
        in_specs=[pl.BlockSpec((1, PREP_ROWS, ff2), blk), _const_spec((2 * PREP_LANES, 2 * PREP_LANES))],
        out_specs=pl.BlockSpec((1, PREP_ROWS, ff2), blk),
        out_shape=jax.ShapeDtypeStruct((e, d, ff2), _BF16),
        compiler_params=_params("arbitrary", "arbitrary"),
        name="prep_up",
    )(w_up, sel)


def _dispatch(x_words, slot_of, p):
    n, words = x_words.shape
    nk = slot_of.shape[0]
    tiles = n // DISPATCH_ROWS
    part = words // DISPATCH_PARTS
    mesh = plsc.VectorSubcoreMesh(core_axis_name="core", subcore_axis_name="subcore",
                                  num_cores=SC_CORES, num_subcores=SC_SUBCORES)
    out = jax.ShapeDtypeStruct((p, part), x_words.dtype)

    @functools.partial(
        pl.kernel, out_type=[out] * DISPATCH_PARTS, mesh=mesh, scratch_types=[],
        compiler_params=pltpu.CompilerParams(use_tc_tiling_on_sc=True), name="moe_dispatch")
    def dispatch(x_hbm, slot_hbm, *o_hbm):
        for c in range(DISPATCH_PARTS):
            def body(x_vmem, slot_vmem, o_ref=o_hbm[c]):
                pltpu.sync_copy(x_vmem, o_ref.at[slot_vmem.at[0]])

            pltpu.emit_pipeline(
                body,
                grid=(nk // DISPATCH_ROWS,),
                in_specs=[pl.BlockSpec((DISPATCH_ROWS, part), lambda i, c=c: (i % tiles, c)),
                          pl.BlockSpec((1, DISPATCH_ROWS), lambda i: (0, i))],
                out_specs=[],
                core_axis_name=("core", "subcore"),
                dimension_semantics=(pltpu.PARALLEL,),
            )(x_hbm, slot_hbm)

    return dispatch(x_words, slot_of.reshape(1, nk))


def _combine_gather(y_parts, slot_of):
    nk = slot_of.shape[0]
    part = y_parts[0].shape[1]
    mesh = plsc.VectorSubcoreMesh(core_axis_name="core", subcore_axis_name="subcore",
                                  num_cores=SC_CORES, num_subcores=SC_SUBCORES)
    out = jax.ShapeDtypeStruct((nk, part), y_parts[0].dtype)

    @functools.partial(
        pl.kernel, out_type=[out] * DISPATCH_PARTS, mesh=mesh, scratch_types=[],
        compiler_params=pltpu.CompilerParams(use_tc_tiling_on_sc=True), name="moe_combine_gather")
    def gather(*refs):
        y_hbm, slot_hbm, o_hbm = refs[:DISPATCH_PARTS], refs[DISPATCH_PARTS], refs[DISPATCH_PARTS + 1:]
        for c in range(DISPATCH_PARTS):
            def body(slot_vmem, o_vmem, y_ref=y_hbm[c]):
                pltpu.sync_copy(y_ref.at[slot_vmem.at[0]], o_vmem)

            pltpu.emit_pipeline(
                body,
                grid=(nk // DISPATCH_ROWS,),
                in_specs=[pl.BlockSpec((1, DISPATCH_ROWS), lambda i: (0, i))],
                out_specs=[pl.BlockSpec((DISPATCH_ROWS, part), lambda i: (i, 0))],
                core_axis_name=("core", "subcore"),
                dimension_semantics=(pltpu.PARALLEL,),
            )(slot_hbm, o_hbm[c])

    return gather(*y_parts, slot_of.reshape(1, nk))


def _moe_ffn_kernel(block_e_ref, block_rows_ref, xa_ref, xb_ref, wu_ref, bu_ref, wd_f32_ref, bd_ref,
                    oa_ref, ob_ref, wd_ref):
    step = pl.program_id(0)
    rows_used = block_rows_ref[step]
    part = oa_ref.shape[1]
    new_expert = jnp.logical_or(step == 0,
                                block_e_ref[step] != block_e_ref[jnp.maximum(step - 1, 0)])

    @pl.when(jnp.logical_and(new_expert, rows_used > 0))
    def _():
        wd_ref[...] = wd_f32_ref[0].astype(_BF16)

    for sub in range(MOE_STEP_ROWS // MOE_ROWS):
        rs = slice(sub * MOE_ROWS, (sub + 1) * MOE_ROWS)

        @pl.when(sub * MOE_ROWS < rows_used)
        def _():
            x = _unpack_bf16_pairs(jnp.concatenate([xa_ref[rs, :], xb_ref[rs, :]], axis=1))
            h = jnp.dot(x, wu_ref[0], preferred_element_type=_F32) + bu_ref[0]
            h_glu = jnp.minimum(h[:, :D_FF], SWIGLU_LIMIT)
            h_lin = jnp.clip(h[:, D_FF:], -SWIGLU_LIMIT, SWIGLU_LIMIT)
            act = h_glu * jax.nn.sigmoid(SWIGLU_ALPHA * h_glu) * (h_lin + 1.0)
            y = jnp.dot(act.astype(_BF16), wd_ref[...], preferred_element_type=_F32) + bd_ref[0]
            words = _pack_bf16_pairs(y)
            oa_ref[rs, :] = words[:, :part]
            ob_ref[rs, :] = words[:, part:]

        @pl.when(sub * MOE_ROWS >= rows_used)
        def _():
            oa_ref[rs, :] = jnp.zeros((MOE_ROWS, part), oa_ref.dtype)
            ob_ref[rs, :] = jnp.zeros((MOE_ROWS, part), ob_ref.dtype)


def _moe_ffn(xs, block_e, block_rows, wu, bu, wd_f32, bd):
    xa, xb = xs
    p, part = xa.shape
    ff, d = wd_f32.shape[1:]
    ff2 = wu.shape[2]
    row = lambda i, be, br: (i, 0)
    exp3 = lambda i, be, br: (be[i], 0, 0)
    rows_spec = pl.BlockSpec((MOE_STEP_ROWS, part), row)
    grid_spec = pltpu.PrefetchScalarGridSpec(
        num_scalar_prefetch=2,
        grid=(p // MOE_STEP_ROWS,),
        in_specs=[rows_spec, rows_spec,
                  pl.BlockSpec((1, d, ff2), exp3), pl.BlockSpec((1, 1, ff2), exp3),
                  pl.BlockSpec((1, ff, d), exp3), pl.BlockSpec((1, 1, d), exp3)],
        out_specs=[rows_spec, rows_spec],
        scratch_shapes=[pltpu.VMEM((ff, d), _BF16)],
    )
    out = jax.ShapeDtypeStruct((p, part), xa.dtype)
    return pl.pallas_call(
        _moe_ffn_kernel,
        grid_spec=grid_spec,
        out_shape=[out, out],
        compiler_params=_params("arbitrary"),
        name="moe_ffn",
    )(block_e, block_rows, xa, xb, wu, bu, wd_f32, bd)


def _combine_ln_kernel(ya_ref, yb_ref, gate_ref, res_ref, g_ref, b_ref, o_ref, obf_ref):
    gate = gate_ref[...]
    moe = None
    for k in range(TOP_K):
        words = jnp.concatenate([ya_ref[k], yb_ref[k]], axis=1)
        term = _unpack_bf16_pairs(words).astype(_F32) * gate[:, k:k + 1]
        moe = term if moe is None else moe + term
    y = _layer_norm_rows(DEEPNORM_ALPHA * res_ref[...] + moe, g_ref[...], b_ref[...])
    o_ref[...] = y
    obf_ref[...] = y.astype(_BF16)


def _combine_ln(y4_parts, gates, res, g, b):
    n, d = res.shape
    ya, yb = (t.reshape(TOP_K, n, t.shape[1]) for t in y4_parts)
    part = ya.shape[2]
    row = lambda i: (i, 0)
    y_spec = pl.BlockSpec((TOP_K, COMBINE_ROWS, part), lambda i: (0, i, 0))
    return pl.pallas_call(
        _combine_ln_kernel,
        grid=(n // COMBINE_ROWS,),
        in_specs=[y_spec, y_spec, pl.BlockSpec((COMBINE_ROWS, TOP_K), row),
                  pl.BlockSpec((COMBINE_ROWS, d), row), _const_spec((1, d)), _const_spec((1, d))],
        out_specs=[pl.BlockSpec((COMBINE_ROWS, d), row), pl.BlockSpec((COMBINE_ROWS, d), row)],
        out_shape=[jax.ShapeDtypeStruct((n, d), _F32), jax.ShapeDtypeStruct((n, d), _BF16)],
        compiler_params=_params("arbitrary"),
        name="combine_ln",
    )(ya, yb, gates, res, g.reshape(1, d), b.reshape(1, d))


def _moe_layer(x, x_words, router_w, router_b, w_up, b_up, w_down, b_down, ln_g, ln_b):
    n, d = x.shape
    nk = n * TOP_K
    idx_t, gate_t, rank_t, count = _router(x, router_w, router_b)
    counts = count[:, 0].astype(jnp.int32)
    padded = (counts + MOE_STEP_ROWS - 1) // MOE_STEP_ROWS * MOE_STEP_ROWS
    padded_ends = jnp.cumsum(padded)
    padded_starts = padded_ends - padded
    n_blocks = -(-(nk + N_EXPERTS * (MOE_STEP_ROWS - 1)) // MOE_STEP_ROWS)
    p = n_blocks * MOE_STEP_ROWS
    experts = jnp.arange(N_EXPERTS, dtype=jnp.int32)
    start_of = jnp.sum(jnp.where(idx_t[None] == experts[:, None, None],
                                 padded_starts[:, None, None], 0), axis=0)
    slot_of = (start_of + rank_t).reshape(nk)
    block_start = jnp.arange(n_blocks, dtype=jnp.int32) * MOE_STEP_ROWS
    block_e = jnp.minimum(jnp.sum(padded_ends[None, :] <= block_start[:, None], axis=1),
                          N_EXPERTS - 1).astype(jnp.int32)
    block_rows = jnp.where(block_start < padded_ends[-1],
                           jnp.clip(counts[block_e] - (block_start - padded_starts[block_e]),
                                    0, MOE_STEP_ROWS), 0).astype(jnp.int32)
    xs = _dispatch(x_words, slot_of, p)
    bu = jnp.concatenate([b_up[:, 0::2], b_up[:, 1::2]], axis=1).reshape(N_EXPERTS, 1, 2 * D_FF)
    ys = _moe_ffn(xs, block_e, block_rows, _prep_up(w_up), bu, w_down,
                  b_down.reshape(N_EXPERTS, 1, d))
    return _combine_ln(_combine_gather(ys, slot_of), gate_t.T, x, ln_g, ln_b)


def kernel(x, l0_w_in, l0_ret_gn_g, l0_conv_w, l0_conv_b, l0_conv_ln_g, l0_conv_ln_b, l0_w_out, l0_ln1_g, l0_ln1_b, l0_router_w, l0_router_b, l0_w_up, l0_b_up, l0_w_down, l0_b_down, l0_ln2_g, l0_ln2_b, l1_w_in, l1_w_out, l1_ln1_g, l1_ln1_b, l1_router_w, l1_router_b, l1_w_up, l1_b_up, l1_w_down, l1_b_down, l1_ln2_g, l1_ln2_b):
    batch, seq, d = x.shape
    n = batch * seq
    x0 = x.reshape(n, d)

    h = _proj(x0, l0_w_in.astype(_BF16))
    mix = _l0_mixer(h, batch, seq, l0_ret_gn_g, l0_conv_w, l0_conv_b, l0_conv_ln_g, l0_conv_ln_b)
    x1, x1_words = _proj_res_ln(mix, l0_w_out.astype(_BF16), x0, l0_ln1_g, l0_ln1_b)
    x2, x2_bf16 = _moe_layer(x1, x1_words, l0_router_w, l0_router_b, l0_w_up, l0_b_up, l0_w_down,
                             l0_b_down, l0_ln2_g, l0_ln2_b)

    q_scale = jnp.where(jnp.arange(3 * d) < d, SB_HEAD_DIM ** -0.5, 1.0)
    qkv = _proj(x2_bf16, (l1_w_in * q_scale).astype(_BF16))
    att = _sb_attention(qkv, batch, seq)
    x3, x3_words = _proj_res_ln(att, l1_w_out.astype(_BF16), x2, l1_ln1_g, l1_ln1_b)
    x4, _ = _moe_layer(x3, x3_words, l1_router_w, l1_router_b, l1_w_up, l1_b_up, l1_w_down,
                       l1_b_down, l1_ln2_g, l1_ln2_b)
    return x4.reshape(batch, seq, d)
```

```python
import functools

import jax
import jax.numpy as jnp
from jax import lax
from jax.experimental import pallas as pl
from jax.experimental.pallas import tpu as pltpu
from jax.experimental.pallas import tpu_sc as plsc

D_MODEL = 1024
CHUNK = 64
RET_HEADS = 4
RET_DK = 64
RET_DV = 128
RET_QK_WIDTH = RET_HEADS * RET_DK
RET_WIDTH = RET_HEADS * RET_DV
ROPE_BASE = 10000.0
CONV_CH = D_MODEL // 2
CONV_WIDTH = 31
L0_IN = 2 * RET_QK_WIDTH + 2 * RET_WIDTH + 2 * CONV_CH
SB_HEADS = 8
SB_HEAD_DIM = D_MODEL // SB_HEADS
N_EXPERTS = 32
TOP_K = 4
D_FF = D_MODEL
SWIGLU_LIMIT = 7.0
SWIGLU_ALPHA = 1.702
LN_EPS = 1e-5
DEPTH = 2
DEEPNORM_ALPHA = (2 * DEPTH) ** 0.25

VMEM_LIMIT_BYTES = 48 * 1024 * 1024
F32_SUBLANES = 8
PROJ_ROWS = 512
MIX_ROWS = 256
CONV_HALO = 32
ROUTER_ROWS = 512
MOE_ROWS = 512
MOE_STEP_ROWS = 1024
PREP_ROWS = 512
PREP_LANES = 128
COMBINE_ROWS = 256
DISPATCH_ROWS = 128
DISPATCH_PARTS = 2
SC_CORES = 2
SC_SUBCORES = 16
SB_BLOCK = 128
SB_CUTOFF = 104.0

_F32 = jnp.float32
_BF16 = jnp.bfloat16


def _params(*sem):
    return pltpu.CompilerParams(dimension_semantics=sem, vmem_limit_bytes=VMEM_LIMIT_BYTES)


def _const_spec(shape):
    nd = len(shape)
    return pl.BlockSpec(shape, lambda *_: (0,) * nd)


def _layer_norm_rows(y, g, b):
    mu = jnp.mean(y, axis=-1, keepdims=True)
    yc = y - mu
    var = jnp.mean(yc * yc, axis=-1, keepdims=True)
    return yc * lax.rsqrt(var + LN_EPS) * g + b


def _silu(x):
    return x * jax.nn.sigmoid(x)


def _pack_bf16_pairs(y):
    half = y.shape[1] // 2
    lo = lax.bitcast_convert_type(y[:, :half].astype(_BF16).astype(_F32), jnp.uint32)
    hi = lax.bitcast_convert_type(y[:, half:].astype(_BF16).astype(_F32), jnp.uint32)
    return (lo >> 16) | (hi & jnp.uint32(0xFFFF0000))


def _unpack_bf16_pairs(w):
    lo = lax.bitcast_convert_type(w << 16, _F32)
    hi = lax.bitcast_convert_type(w & jnp.uint32(0xFFFF0000), _F32)
    return jnp.concatenate([lo, hi], axis=1).astype(_BF16)


def _proj_kernel(x_ref, w_ref, o_ref):
    o_ref[...] = jnp.dot(x_ref[...].astype(_BF16), w_ref[...],
                         preferred_element_type=_F32).astype(o_ref.dtype)


def _proj(x, w_bf16):
    n, d = x.shape
    width = w_bf16.shape[1]
    return pl.pallas_call(
        _proj_kernel,
        grid=(n // PROJ_ROWS,),
        in_specs=[pl.BlockSpec((PROJ_ROWS, d), lambda i: (i, 0)), _const_spec((d, width))],
        out_specs=pl.BlockSpec((PROJ_ROWS, width), lambda i: (i, 0)),
        out_shape=jax.ShapeDtypeStruct((n, width), _BF16),
        compiler_params=_params("arbitrary"),
        name="proj",
    )(x, w_bf16)


def _proj_res_ln_kernel(a_ref, w_ref, res_ref, g_ref, b_ref, o_ref, words_ref):
    m = jnp.dot(a_ref[...], w_ref[...], preferred_element_type=_F32)
    y = _layer_norm_rows(DEEPNORM_ALPHA * res_ref[...] + m, g_ref[...], b_ref[...])
    o_ref[...] = y
    words_ref[...] = _pack_bf16_pairs(y)


def _proj_res_ln(a_bf16, w_bf16, res, g, b):
    n, d = res.shape
    k = a_bf16.shape[1]
    row = lambda i: (i, 0)
    return pl.pallas_call(
        _proj_res_ln_kernel,
        grid=(n // PROJ_ROWS,),
        in_specs=[pl.BlockSpec((PROJ_ROWS, k), row), _const_spec((k, d)),
                  pl.BlockSpec((PROJ_ROWS, d), row), _const_spec((1, d)), _const_spec((1, d))],
        out_specs=[pl.BlockSpec((PROJ_ROWS, d), row), pl.BlockSpec((PROJ_ROWS, d // 2), row)],
        out_shape=[jax.ShapeDtypeStruct((n, d), _F32), jax.ShapeDtypeStruct((n, d // 2), jnp.uint32)],
        compiler_params=_params("arbitrary"),
        name="proj_res_ln",
    )(a_bf16, w_bf16, res, g.reshape(1, d), b.reshape(1, d))


def _l0_block_kernel(x_ref, w_in_ref, cos_ref, sin_ref, xi_ref, zeta_ref, decay_ref, gstate_ref,
                     gn_g_ref, conv_w_ref, conv_b_ref, cln_g_ref, cln_b_ref, w_out_ref, ln_g_ref,
                     ln_b_ref, o_ref, words_ref, state_ref, u_ref, shift_ref, h_ref, mix_ref):
    rows = MIX_ROWS
    step = pl.program_id(1)
    v_off = 2 * RET_QK_WIDTH
    g_off = v_off + RET_WIDTH
    a_off = g_off + RET_WIDTH

    @pl.when(step == 0)
    def _():
        state_ref[...] = jnp.zeros_like(state_ref)
        u_ref[0:CONV_HALO, :] = jnp.zeros((CONV_HALO, CONV_CH), _F32)

    x_bf16 = x_ref[...].astype(_BF16)
    h_conv = jnp.dot(x_bf16, w_in_ref[:, a_off:a_off + 2 * CONV_CH], preferred_element_type=_F32)
    h_ref[...] = jnp.dot(x_bf16, w_in_ref[:, 0:a_off], preferred_element_type=_F32)

    u_ref[CONV_HALO:CONV_HALO + rows, :] = h_conv[:, :CONV_CH] * jax.nn.sigmoid(h_conv[:, CONV_CH:])
    acc = jnp.broadcast_to(conv_b_ref[...], (rows, CONV_CH))
    first = CONV_HALO - (CONV_WIDTH - 1)
    for phase in range(F32_SUBLANES):
        offsets = [first + j for j in range(CONV_WIDTH) if (first + j) % F32_SUBLANES == phase]
        span = max(offsets) - phase + rows
        if phase == 0:
            src = u_ref
        else:
            shift_ref[0:span, :] = u_ref[phase:phase + span, :]
            src = shift_ref
        for off in offsets:
            j = off - first
            acc = acc + conv_w_ref[j:j + 1, :] * src[off - phase:off - phase + rows, :]
    u_ref[0:CONV_HALO, :] = u_ref[rows:rows + CONV_HALO, :]
    c = _silu(_layer_norm_rows(acc, cln_g_ref[...], cln_b_ref[...]))
    mix_ref[:, RET_WIDTH:RET_WIDTH + CONV_CH] = c.astype(_BF16)

    lane = lax.broadcasted_iota(jnp.int32, (rows, RET_QK_WIDTH), 1)
    first_half = (lane % RET_DK) < (RET_DK // 2)
    cos = cos_ref[...]
    sin = sin_ref[...]

    def rotary(t):
        partner = jnp.where(first_half,
                            pltpu.roll(t, RET_QK_WIDTH - RET_DK // 2, axis=1),
                            pltpu.roll(t, RET_DK // 2, axis=1))
        return t * cos + partner * sin

    q = rotary(h_ref[:, 0:RET_QK_WIDTH])
    k = rotary(h_ref[:, RET_QK_WIDTH:2 * RET_QK_WIDTH]) * (RET_DK ** -0.5)
    q_in = (q * xi_ref[...]).astype(_BF16)
    k_out = (k * zeta_ref[...]).astype(_BF16)
    q = q.astype(_BF16)
    k = k.astype(_BF16)
    for hd in range(RET_HEADS):
        qk = slice(hd * RET_DK, (hd + 1) * RET_DK)
        vs = slice(hd * RET_DV, (hd + 1) * RET_DV)
        v = h_ref[:, v_off + hd * RET_DV:v_off + (hd + 1) * RET_DV].astype(_BF16)
        s = lax.dot_general(q[:, qk], k[:, qk], (((1,), (1,)), ((), ())),
                            preferred_element_type=_F32) * decay_ref[hd]
        r = jnp.dot(s.astype(_BF16), v, preferred_element_type=_F32)
        r += jnp.dot(q_in[:, qk], state_ref[hd].astype(_BF16), preferred_element_type=_F32)
        kv = lax.dot_general(k_out[:, qk], v, (((0,), (0,)), ((), ())),
                             preferred_element_type=_F32)
        state_ref[hd] = gstate_ref[hd] * state_ref[hd] + kv
        mu = jnp.mean(r, axis=-1, keepdims=True)
        rc = r - mu
        var = jnp.mean(rc * rc, axis=-1, keepdims=True)
        rn = rc * lax.rsqrt(var + LN_EPS) * gn_g_ref[:, vs]
        gate = h_ref[:, g_off + hd * RET_DV:g_off + (hd + 1) * RET_DV]
        mix_ref[:, vs] = (_silu(gate) * rn).astype(_BF16)

    m = jnp.dot(mix_ref[...], w_out_ref[...], preferred_element_type=_F32)
    y = _layer_norm_rows(DEEPNORM_ALPHA * x_ref[...] + m, ln_g_ref[...], ln_b_ref[...])
    o_ref[...] = y
    words_ref[...] = _pack_bf16_pairs(y)


def _l0_mixer_tables(seq):
    half = RET_DK // 2
    inv = ROPE_BASE ** (-jnp.arange(half, dtype=_F32) / half)
    ang = jnp.arange(seq).astype(_F32)[:, None] * inv[None, :]
    cos = jnp.tile(jnp.cos(ang), (1, 2 * RET_HEADS))
    sin = jnp.tile(jnp.concatenate([-jnp.sin(ang), jnp.sin(ang)], axis=1), (1, RET_HEADS))
    log_g = jnp.log(1.0 - 2.0 ** (-5.0 - jnp.arange(RET_HEADS, dtype=_F32)))
    n = jnp.arange(MIX_ROWS, dtype=_F32)
    diff = n[:, None] - n[None, :]
    chunk = jnp.arange(MIX_ROWS) // CHUNK
    expo = jnp.where(chunk[:, None] == chunk[None, :], jnp.abs(diff), diff)
    decay = jnp.where((chunk[:, None] >= chunk[None, :])[None],
                      jnp.exp(expo[None] * log_g[:, None, None]), 0.0)
    xi = jnp.repeat(jnp.exp((n[:, None] + 1.0) * log_g[None, :]), RET_DK, axis=1)
    zeta = jnp.repeat(jnp.exp((MIX_ROWS - 1.0 - n[:, None]) * log_g[None, :]), RET_DK, axis=1)
    g_state = jnp.exp(MIX_ROWS * log_g)
    return cos, sin, xi, zeta, decay, g_state


def _l0_block(x, batch, seq, w_in, gn_g, conv_w, conv_b, cln_g, cln_b, w_out, ln_g, ln_b):
    n, d = x.shape
    steps = seq // MIX_ROWS
    cos, sin, xi, zeta, decay, g_state = _l0_mixer_tables(seq)
    row = lambda b, i: (b * steps + i, 0)
    pos = lambda b, i: (i, 0)
    vec = lambda t: t.reshape(1, -1)
    return pl.pallas_call(
        _l0_block_kernel,
        grid=(batch, steps),
        in_specs=[pl.BlockSpec((MIX_ROWS, d), row), _const_spec((d, L0_IN)),
                  pl.BlockSpec((MIX_ROWS, RET_QK_WIDTH), pos),
                  pl.BlockSpec((MIX_ROWS, RET_QK_WIDTH), pos),
                  _const_spec((MIX_ROWS, RET_QK_WIDTH)), _const_spec((MIX_ROWS, RET_QK_WIDTH)),
                  _const_spec((RET_HEADS, MIX_ROWS, MIX_ROWS)),
                  pl.BlockSpec(memory_space=pltpu.SMEM),
                  _const_spec((1, RET_WIDTH)), _const_spec((CONV_WIDTH, CONV_CH)),
                  _const_spec((1, CONV_CH)), _const_spec((1, CONV_CH)), _const_spec((1, CONV_CH)),
                  _const_spec((RET_WIDTH + CONV_CH, d)), _const_spec((1, d)), _const_spec((1, d))],
        out_specs=[pl.BlockSpec((MIX_ROWS, d), row), pl.BlockSpec((MIX_ROWS, d // 2), row)],
        out_shape=[jax.ShapeDtypeStruct((n, d), _F32), jax.ShapeDtypeStruct((n, d // 2), jnp.uint32)],
        scratch_shapes=[pltpu.VMEM((RET_HEADS, RET_DK, RET_DV), _F32),
                        pltpu.VMEM((MIX_ROWS + CONV_HALO, CONV_CH), _F32),
                        pltpu.VMEM((MIX_ROWS + CONV_HALO, CONV_CH), _F32),
                        pltpu.VMEM((MIX_ROWS, 2 * RET_QK_WIDTH + 2 * RET_WIDTH), _F32),
                        pltpu.VMEM((MIX_ROWS, RET_WIDTH + CONV_CH), _BF16)],
        compiler_params=_params("arbitrary", "arbitrary"),
        name="l0_block",
    )(x, w_in.astype(_BF16), cos, sin, xi, zeta, decay, g_state, vec(gn_g), conv_w, vec(conv_b),
      vec(cln_g), vec(cln_b), w_out.astype(_BF16), vec(ln_g), vec(ln_b))


def _sb_attention_kernel(q_ref, k_ref, v_ref, tri_ref, o_ref, acc_ref, carry_ref):
    blk = SB_BLOCK
    qi = pl.program_id(1)
    t_idx = lax.broadcasted_iota(jnp.int32, (blk, blk), 0)
    s_idx = lax.broadcasted_iota(jnp.int32, (blk, blk), 1)
    keep = s_idx < t_idx

    heads = [slice(hd * SB_HEAD_DIM, (hd + 1) * SB_HEAD_DIM) for hd in range(SB_HEADS)]

    def sweep(j, carry):
        masked = carry is None
        start = pl.multiple_of(j * blk, blk)
        zs = [lax.dot_general(q_ref[:, hs], k_ref[pl.ds(start, blk), hs], (((1,), (1,)), ((), ())),
                              preferred_element_type=_F32) for hs in heads]
        log_betas, addends = [], []
        for z in zs:
            softplus = jnp.log(1.0 + jnp.exp(-jnp.abs(z)))
            log_beta = jnp.minimum(z, 0.0) - softplus
            log_rest = log_beta - z
            if masked:
                log_rest = jnp.where(keep, log_rest, 0.0)
            hi = log_rest.astype(_BF16)
            lo = (log_rest - hi.astype(_F32)).astype(_BF16)
            log_betas.append(log_beta)
            addends.append(jnp.concatenate([hi, lo], axis=1))
        sums = [jnp.dot(t, tri_ref[...], preferred_element_type=_F32) for t in addends]
        weights = []
        for hd, (log_beta, s) in enumerate(zip(log_betas, sums)):
            if masked:
                a = jnp.where(keep, jnp.exp(log_beta + s[:, :blk]), 0.0)
            else:
                a = jnp.exp(log_beta + s[:, :blk] + carry[:, heads[hd]])
            weights.append(a.astype(_BF16))
        outs = [jnp.dot(a, v_ref[pl.ds(start, blk), hs], preferred_element_type=_F32)
                for a, hs in zip(weights, heads)]
        return jnp.concatenate(outs, axis=1), jnp.concatenate([s[:, blk:] for s in sums], axis=1)

    out, carry = sweep(qi, None)
    acc_ref[...] = out
    carry_ref[...] = carry

    def cond(state):
        j, alive = state
        return jnp.logical_and(j >= 0, alive)

    def body(state):
        j, _ = state
        carry = carry_ref[...]
        out, total = sweep(j, carry)
        carry = carry + total
        acc_ref[...] += out
        carry_ref[...] = carry
        return j - 1, jnp.max(carry) > -SB_CUTOFF

    lax.while_loop(cond, body, (qi - 1, jnp.max(carry) > -SB_CUTOFF))
    o_ref[...] = acc_ref[...].astype(o_ref.dtype)


def _sb_attention(qkv, batch, seq):
    n = qkv.shape[0]
    nq = seq // SB_BLOCK
    j = jnp.arange(2 * SB_BLOCK)[:, None] % SB_BLOCK
    s = jnp.arange(SB_BLOCK)[None, :]
    tri = jnp.concatenate([(j > s).astype(_BF16), jnp.ones((2 * SB_BLOCK, SB_BLOCK), _BF16)], axis=1)
    whole_seq = lambda col: pl.BlockSpec((seq, D_MODEL), lambda b, i: (b, col),
                                         pipeline_mode=pl.Buffered(1))
    return pl.pallas_call(
        _sb_attention_kernel,
        grid=(batch, nq),
        in_specs=[pl.BlockSpec((SB_BLOCK, D_MODEL), lambda b, i: (b * nq + i, 0)),
                  whole_seq(1), whole_seq(2), _const_spec((2 * SB_BLOCK, 2 * SB_BLOCK))],
        out_specs=pl.BlockSpec((SB_BLOCK, D_MODEL), lambda b, i: (b * nq + i, 0)),
        out_shape=jax.ShapeDtypeStruct((n, D_MODEL), _BF16),
        scratch_shapes=[pltpu.VMEM((SB_BLOCK, D_MODEL), _F32), pltpu.VMEM((SB_BLOCK, D_MODEL), _F32)],
        compiler_params=_params("arbitrary", "arbitrary"),
        name="sb_attention",
    )(qkv, qkv, qkv, tri)


def _router_kernel(x_ref, w_ref, b_ref, tri_ref, idx_ref, gate_ref, rank_ref, count_ref, base_ref):
    @pl.when(pl.program_id(0) == 0)
    def _():
        base_ref[...] = jnp.zeros_like(base_ref)

    x = x_ref[...]
    w = w_ref[...]
    x_hi = x.astype(_BF16)
    x_lo = (x - x_hi.astype(_F32)).astype(_BF16)
    w_hi = w.astype(_BF16)
    w_lo = (w - w_hi.astype(_F32)).astype(_BF16)
    nt = (((1,), (1,)), ((), ()))
    logits = (lax.dot_general(w_hi, x_hi, nt, preferred_element_type=_F32)
              + lax.dot_general(w_lo, x_hi, nt, preferred_element_type=_F32)
              + lax.dot_general(w_hi, x_lo, nt, preferred_element_type=_F32)) + b_ref[...]
    expert = lax.broadcasted_iota(jnp.int32, logits.shape, 0)
    vals, idxs = [], []
    for _ in range(TOP_K):
        m = jnp.max(logits, axis=0, keepdims=True)
        sel = jnp.min(jnp.where(logits == m, expert, N_EXPERTS), axis=0, keepdims=True)
        vals.append(m)
        idxs.append(sel)
        logits = jnp.where(expert == sel, -jnp.inf, logits)
    vals = jnp.concatenate(vals, axis=0)
    e = jnp.exp(vals - vals[0:1])
    gate_ref[...] = e / jnp.sum(e, axis=0, keepdims=True)
    idx_ref[...] = jnp.concatenate(idxs, axis=0)

    member = jnp.zeros(logits.shape, _F32)
    for sel in idxs:
        member += (expert == sel).astype(_F32)
    before = jnp.dot(member.astype(_BF16), tri_ref[...], preferred_element_type=_F32) + base_ref[...]
    ranks = [jnp.sum(jnp.where(expert == sel, before, 0.0), axis=0, keepdims=True) for sel in idxs]
    rank_ref[...] = jnp.concatenate(ranks, axis=0).astype(jnp.int32)
    base_ref[...] += jnp.sum(member, axis=1, keepdims=True)
    count_ref[...] = base_ref[...]


def _router(x, router_w, router_b):
    n, d = x.shape
    t = jnp.arange(ROUTER_ROWS)
    tri = (t[:, None] < t[None, :]).astype(_BF16)
    tok = pl.BlockSpec((TOP_K, ROUTER_ROWS), lambda i: (0, i))
    return pl.pallas_call(
        _router_kernel,
        grid=(n // ROUTER_ROWS,),
        in_specs=[pl.BlockSpec((ROUTER_ROWS, d), lambda i: (i, 0)),
                  _const_spec((N_EXPERTS, d)), _const_spec((N_EXPERTS, 1)),
                  _const_spec((ROUTER_ROWS, ROUTER_ROWS))],
        out_specs=[tok, tok, tok, _const_spec((N_EXPERTS, 1))],
        out_shape=[jax.ShapeDtypeStruct((TOP_K, n), jnp.int32),
                   jax.ShapeDtypeStruct((TOP_K, n), _F32),
                   jax.ShapeDtypeStruct((TOP_K, n), jnp.int32),
                   jax.ShapeDtypeStruct((N_EXPERTS, 1), _F32)],
        scratch_shapes=[pltpu.VMEM((N_EXPERTS, 1), _F32)],
        compiler_params=_params("arbitrary"),
        name="router",
    )(x, router_w.T, router_b.reshape(N_EXPERTS, 1), tri)


def _prep_up_kernel(w_ref, sel_ref, o_ref):
    group = 2 * PREP_LANES
    for c in range(w_ref.shape[2] // group):
        t = jnp.dot(w_ref[0, :, c * group:(c + 1) * group].astype(_BF16), sel_ref[...],
                    preferred_element_type=_F32)
        o_ref[0, :, c * PREP_LANES:(c + 1) * PREP_LANES] = t[:, :PREP_LANES].astype(_BF16)
        o_ref[0, :, D_FF + c * PREP_LANES:D_FF + (c + 1) * PREP_LANES] = t[:, PREP_LANES:].astype(_BF16)


def _prep_up(w_up):
    e, d, ff2 = w_up.shape
    i = jnp.arange(2 * PREP_LANES)[:, None]
    j = jnp.arange(2 * PREP_LANES)[None, :]
    sel = (i == jnp.where(j < PREP_LANES, 2 * j, 2 * (j - PREP_LANES) + 1)).astype(_BF16)
    blk = lambda x, r: (x, r, 0)
    return pl.pallas_call(
        _prep_up_kernel,
        grid=(e, d // PREP_ROWS),
        in_specs=[pl.BlockSpec((1, PREP_ROWS, ff2), blk), _const_spec((2 * PREP_LANES, 2 * PREP_LANES))],
        out_specs=pl.BlockSpec((1, PREP_ROWS, ff2), blk),
        out_shape=jax.ShapeDtypeStruct((e, d, ff2), _BF16),
        compiler_params=_params("arbitrary", "arbitrary"),
        name="prep_up",
    )(w_up, sel)


def _dispatch(x_words, slot_of, p):
    n, words = x_words.shape
    part = words // DISPATCH_PARTS
    mesh = plsc.VectorSubcoreMesh(core_axis_name="core", subcore_axis_name="subcore",
                                  num_cores=SC_CORES, num_subcores=SC_SUBCORES)
    out = jax.ShapeDtypeStruct((p, part), x_words.dtype)

    @functools.partial(
        pl.kernel, out_type=[out] * DISPATCH_PARTS, mesh=mesh, scratch_types=[],
        compiler_params=pltpu.CompilerParams(use_tc_tiling_on_sc=True), name="moe_dispatch")
    def dispatch(x_hbm, slot_hbm, *o_hbm):
        for c in range(DISPATCH_PARTS):
            def body(x_vmem, slot_vmem, o_ref=o_hbm[c]):
                for k in range(TOP_K):
                    pltpu.sync_copy(x_vmem, o_ref.at[slot_vmem.at[k]])

            pltpu.emit_pipeline(
                body,
                grid=(n // DISPATCH_ROWS,),
                in_specs=[pl.BlockSpec((DISPATCH_ROWS, part), lambda i, c=c: (i, c)),
                          pl.BlockSpec((TOP_K, DISPATCH_ROWS), lambda i: (0, i))],
                out_specs=[],
                core_axis_name=("core", "subcore"),
                dimension_semantics=(pltpu.PARALLEL,),
            )(x_hbm, slot_hbm)

    return dispatch(x_words, slot_of.reshape(TOP_K, n))


def _combine_gather(y_parts, slot_of):
    nk = slot_of.shape[0]
    part = y_parts[0].shape[1]
    mesh = plsc.VectorSubcoreMesh(core_axis_name="core", subcore_axis_name="subcore",
                                  num_cores=SC_CORES, num_subcores=SC_SUBCORES)
    out = jax.ShapeDtypeStruct((nk, part), y_parts[0].dtype)

    @functools.partial(
        pl.kernel, out_type=[out] * DISPATCH_PARTS, mesh=mesh, scratch_types=[],
        compiler_params=pltpu.CompilerParams(use_tc_tiling_on_sc=True), name="moe_combine_gather")
    def gather(*refs):
        y_hbm, slot_hbm, o_hbm = refs[:DISPATCH_PARTS], refs[DISPATCH_PARTS], refs[DISPATCH_PARTS + 1:]
        for c in range(DISPATCH_PARTS):
            def body(slot_vmem, o_vmem, y_ref=y_hbm[c]):
                pltpu.sync_copy(y_ref.at[slot_vmem.at[0]], o_vmem)

            pltpu.emit_pipeline(
                body,
                grid=(nk // DISPATCH_ROWS,),
                in_specs=[pl.BlockSpec((1, DISPATCH_ROWS), lambda i: (0, i))],
                out_specs=[pl.BlockSpec((DISPATCH_ROWS, part), lambda i: (i, 0))],
                core_axis_name=("core", "subcore"),
                dimension_semantics=(pltpu.PARALLEL,),
            )(slot_hbm, o_hbm[c])

    return gather(*y_parts, slot_of.reshape(1, nk))


def _moe_ffn_kernel(block_e_ref, block_rows_ref, xa_ref, xb_ref, wu_ref, bu_ref, wd_f32_ref, bd_ref,
                    oa_ref, ob_ref, wd_ref):
    step = pl.program_id(0)
    rows_used = block_rows_ref[step]
    part = oa_ref.shape[1]
    new_expert = jnp.logical_or(step == 0,
                                block_e_ref[step] != block_e_ref[jnp.maximum(step - 1, 0)])

    @pl.when(jnp.logical_and(new_expert, rows_used > 0))
    def _():
        wd_ref[...] = wd_f32_ref[0].astype(_BF16)

    for sub in range(MOE_STEP_ROWS // MOE_ROWS):
        rs = slice(sub * MOE_ROWS, (sub + 1) * MOE_ROWS)

        @pl.when(sub * MOE_ROWS < rows_used)
        def _():
            x = _unpack_bf16_pairs(jnp.concatenate([xa_ref[rs, :], xb_ref[rs, :]], axis=1))
            h = jnp.dot(x, wu_ref[0], preferred_element_type=_F32) + bu_ref[0]
            h_glu = jnp.minimum(h[:, :D_FF], SWIGLU_LIMIT)
            h_lin = jnp.clip(h[:, D_FF:], -SWIGLU_LIMIT, SWIGLU_LIMIT)
            act = h_glu * jax.nn.sigmoid(SWIGLU_ALPHA * h_glu) * (h_lin + 1.0)
            y = jnp.dot(act.astype(_BF16), wd_ref[...], preferred_element_type=_F32) + bd_ref[0]
            words = _pack_bf16_pairs(y)
            oa_ref[rs, :] = words[:, :part]
            ob_ref[rs, :] = words[:, part:]

        @pl.when(sub * MOE_ROWS >= rows_used)
        def _():
            oa_ref[rs, :] = jnp.zeros((MOE_ROWS, part), oa_ref.dtype)
            ob_ref[rs, :] = jnp.zeros((MOE_ROWS, part), ob_ref.dtype)


def _moe_ffn(xs, block_e, block_rows, wu, bu, wd_f32, bd):
    xa, xb = xs
    p, part = xa.shape
    ff, d = wd_f32.shape[1:]
    ff2 = wu.shape[2]
    row = lambda i, be, br: (i, 0)
    exp3 = lambda i, be, br: (be[i], 0, 0)
    rows_spec = pl.BlockSpec((MOE_STEP_ROWS, part), row)
    grid_spec = pltpu.PrefetchScalarGridSpec(
        num_scalar_prefetch=2,
        grid=(p // MOE_STEP_ROWS,),
        in_specs=[rows_spec, rows_spec,
                  pl.BlockSpec((1, d, ff2), exp3), pl.BlockSpec((1, 1, ff2), exp3),
                  pl.BlockSpec((1, ff, d), exp3), pl.BlockSpec((1, 1, d), exp3)],
        out_specs=[rows_spec, rows_spec],
        scratch_shapes=[pltpu.VMEM((ff, d), _BF16)],
    )
    out = jax.ShapeDtypeStruct((p, part), xa.dtype)
    return pl.pallas_call(
        _moe_ffn_kernel,
        grid_spec=grid_spec,
        out_shape=[out, out],
        compiler_params=_params("arbitrary"),
        name="moe_ffn",
    )(block_e, block_rows, xa, xb, wu, bu, wd_f32, bd)


def _combine_ln_kernel(ya_ref, yb_ref, gate_ref, res_ref, g_ref, b_ref, o_ref, obf_ref):
    gate = gate_ref[...]
    moe = None
    for k in range(TOP_K):
        words = jnp.concatenate([ya_ref[k], yb_ref[k]], axis=1)
        term = _unpack_bf16_pairs(words).astype(_F32) * gate[:, k:k + 1]
        moe = term if moe is None else moe + term
    y = _layer_norm_rows(DEEPNORM_ALPHA * res_ref[...] + moe, g_ref[...], b_ref[...])
    o_ref[...] = y
    obf_ref[...] = y.astype(_BF16)


def _combine_ln(y4_parts, gates, res, g, b):
    n, d = res.shape
    ya, yb = (t.reshape(TOP_K, n, t.shape[1]) for t in y4_parts)
    part = ya.shape[2]
    row = lambda i: (i, 0)
    y_spec = pl.BlockSpec((TOP_K, COMBINE_ROWS, part), lambda i: (0, i, 0))
    return pl.pallas_call(
        _combine_ln_kernel,
        grid=(n // COMBINE_ROWS,),
        in_specs=[y_spec, y_spec, pl.BlockSpec((COMBINE_ROWS, TOP_K), row),
                  pl.BlockSpec((COMBINE_ROWS, d), row), _const_spec((1, d)), _const_spec((1, d))],
        out_specs=[pl.BlockSpec((COMBINE_ROWS, d), row), pl.BlockSpec((COMBINE_ROWS, d), row)],
        out_shape=[jax.ShapeDtypeStruct((n, d), _F32), jax.ShapeDtypeStruct((n, d), _BF16)],
        compiler_params=_params("arbitrary"),
        name="combine_ln",
    )(ya, yb, gates, res, g.reshape(1, d), b.reshape(1, d))


def _moe_layer(x, x_words, router_w, router_b, w_up, b_up, w_down, b_down, ln_g, ln_b):
    n, d = x.shape
    nk = n * TOP_K
    idx_t, gate_t, rank_t, count = _router(x, router_w, router_b)
    counts = count[:, 0].astype(jnp.int32)
    padded = (counts + MOE_STEP_ROWS - 1) // MOE_STEP_ROWS * MOE_STEP_ROWS
    padded_ends = jnp.cumsum(padded)
    padded_starts = padded_ends - padded
    n_blocks = -(-(nk + N_EXPERTS * (MOE_STEP_ROWS - 1)) // MOE_STEP_ROWS)
    p = n_blocks * MOE_STEP_ROWS
    experts = jnp.arange(N_EXPERTS, dtype=jnp.int32)
    start_of = jnp.sum(jnp.where(idx_t[None] == experts[:, None, None],
                                 padded_starts[:, None, None], 0), axis=0)
    slot_of = (start_of + rank_t).reshape(nk)
    block_start = jnp.arange(n_blocks, dtype=jnp.int32) * MOE_STEP_ROWS
    block_e = jnp.minimum(jnp.sum(padded_ends[None, :] <= block_start[:, None], axis=1),
                          N_EXPERTS - 1).astype(jnp.int32)
    block_rows = jnp.where(block_start < padded_ends[-1],
                           jnp.clip(counts[block_e] - (block_start - padded_starts[block_e]),
                                    0, MOE_STEP_ROWS), 0).astype(jnp.int32)
    xs = _dispatch(x_words, slot_of, p)
    bu = jnp.concatenate([b_up[:, 0::2], b_up[:, 1::2]], axis=1).reshape(N_EXPERTS, 1, 2 * D_FF)
    ys = _moe_ffn(xs, block_e, block_rows, _prep_up(w_up), bu, w_down,
                  b_down.reshape(N_EXPERTS, 1, d))
    return _combine_ln(_combine_gather(ys, slot_of), gate_t.T, x, ln_g, ln_b)


def kernel(x, l0_w_in, l0_ret_gn_g, l0_conv_w, l0_conv_b, l0_conv_ln_g, l0_conv_ln_b, l0_w_out, l0_ln1_g, l0_ln1_b, l0_router_w, l0_router_b, l0_w_up, l0_b_up, l0_w_down, l0_b_down, l0_ln2_g, l0_ln2_b, l1_w_in, l1_w_out, l1_ln1_g, l1_ln1_b, l1_router_w, l1_router_b, l1_w_up, l1_b_up, l1_w_down, l1_b_down, l1_ln2_g, l1_ln2_b):
    batch, seq, d = x.shape
    n = batch * seq
    x0 = x.reshape(n, d)

    x1, x1_words = _l0_block(x0, batch, seq, l0_w_in, l0_ret_gn_g, l0_conv_w, l0_conv_b,
                             l0_conv_ln_g, l0_conv_ln_b, l0_w_out, l0_ln1_g, l0_ln1_b)
    x2, x2_bf16 = _moe_layer(x1, x1_words, l0_router_w, l0_router_b, l0_w_up, l0_b_up, l0_w_down,
                             l0_b_down, l0_ln2_g, l0_ln2_b)

    q_scale = jnp.where(jnp.arange(3 * d) < d, SB_HEAD_DIM ** -0.5, 1.0)
    qkv = _proj(x2_bf16, (l1_w_in * q_scale).astype(_BF16))
    att = _sb_attention(qkv, batch, seq)
    x3, x3_words = _proj_res_ln(att, l1_w_out.astype(_BF16), x2, l1_ln1_g, l1_ln1_b)
    x4, _ = _moe_layer(x3, x3_words, l1_router_w, l1_router_b, l1_w_up, l1_b_up, l1_w_down,
                       l1_b_down, l1_ln2_g, l1_ln2_b)
    return x4.reshape(batch, seq, d)
```

```python
import functools

import jax
import jax.numpy as jnp
from jax import lax
from jax.experimental import pallas as pl
from jax.experimental.pallas import tpu as pltpu
from jax.experimental.pallas import tpu_sc as plsc

D_MODEL = 1024
CHUNK = 64
RET_HEADS = 4
RET_DK = 64
RET_DV = 128
RET_QK_WIDTH = RET_HEADS * RET_DK
RET_WIDTH = RET_HEADS * RET_DV
ROPE_BASE = 10000.0
CONV_CH = D_MODEL // 2
CONV_WIDTH = 31
L0_IN = 2 * RET_QK_WIDTH + 2 * RET_WIDTH + 2 * CONV_CH
SB_HEADS = 8
SB_HEAD_DIM = D_MODEL // SB_HEADS
N_EXPERTS = 32
TOP_K = 4
D_FF = D_MODEL
SWIGLU_LIMIT = 7.0
SWIGLU_ALPHA = 1.702
LN_EPS = 1e-5
DEPTH = 2
DEEPNORM_ALPHA = (2 * DEPTH) ** 0.25

VMEM_LIMIT_BYTES = 48 * 1024 * 1024
F32_SUBLANES = 8
PROJ_ROWS = 512
MIX_ROWS = 256
CONV_HALO = 32
ROUTER_ROWS = 512
MOE_ROWS = 512
MOE_STEP_ROWS = 1024
PREP_ROWS = 512
PREP_LANES = 128
COMBINE_ROWS = 256
COMBINE_CHUNKS = 2
DISPATCH_ROWS = 128
DISPATCH_PARTS = 2
SC_CORES = 2
SC_SUBCORES = 16
SB_BLOCK = 128
SB_CUTOFF = 150.0
LOG2_E = 1.4426950408889634

_F32 = jnp.float32
_BF16 = jnp.bfloat16


def _params(*sem):
    return pltpu.CompilerParams(dimension_semantics=sem, vmem_limit_bytes=VMEM_LIMIT_BYTES)


def _const_spec(shape):
    nd = len(shape)
    return pl.BlockSpec(shape, lambda *_: (0,) * nd)


def _layer_norm_rows(y, g, b):
    mu = jnp.mean(y, axis=-1, keepdims=True)
    yc = y - mu
    var = jnp.mean(yc * yc, axis=-1, keepdims=True)
    return yc * lax.rsqrt(var + LN_EPS) * g + b


def _silu(x):
    return x * jax.nn.sigmoid(x)


def _pack_bf16_pairs(y):
    half = y.shape[1] // 2
    lo = lax.bitcast_convert_type(y[:, :half].astype(_BF16).astype(_F32), jnp.uint32)
    hi = lax.bitcast_convert_type(y[:, half:].astype(_BF16).astype(_F32), jnp.uint32)
    return (lo >> 16) | (hi & jnp.uint32(0xFFFF0000))


def _unpack_bf16_pairs(w):
    lo = lax.bitcast_convert_type(w << 16, _F32)
    hi = lax.bitcast_convert_type(w & jnp.uint32(0xFFFF0000), _F32)
    return jnp.concatenate([lo, hi], axis=1).astype(_BF16)


def _proj_kernel(x_ref, w_ref, o_ref):
    o_ref[...] = jnp.dot(x_ref[...].astype(_BF16), w_ref[...],
                         preferred_element_type=_F32).astype(o_ref.dtype)


def _proj(x, w_bf16):
    n, d = x.shape
    width = w_bf16.shape[1]
    return pl.pallas_call(
        _proj_kernel,
        grid=(n // PROJ_ROWS,),
        in_specs=[pl.BlockSpec((PROJ_ROWS, d), lambda i: (i, 0)), _const_spec((d, width))],
        out_specs=pl.BlockSpec((PROJ_ROWS, width), lambda i: (i, 0)),
        out_shape=jax.ShapeDtypeStruct((n, width), _BF16),
        compiler_params=_params("arbitrary"),
        name="proj",
    )(x, w_bf16)


def _proj_res_ln_kernel(a_ref, w_ref, res_ref, g_ref, b_ref, o_ref, words_ref):
    m = jnp.dot(a_ref[...], w_ref[...], preferred_element_type=_F32)
    y = _layer_norm_rows(DEEPNORM_ALPHA * res_ref[...] + m, g_ref[...], b_ref[...])
    o_ref[...] = y
    words_ref[...] = _pack_bf16_pairs(y)


def _proj_res_ln(a_bf16, w_bf16, res, g, b):
    n, d = res.shape
    k = a_bf16.shape[1]
    row = lambda i: (i, 0)
    return pl.pallas_call(
        _proj_res_ln_kernel,
        grid=(n // PROJ_ROWS,),
        in_specs=[pl.BlockSpec((PROJ_ROWS, k), row), _const_spec((k, d)),
                  pl.BlockSpec((PROJ_ROWS, d), row), _const_spec((1, d)), _const_spec((1, d))],
        out_specs=[pl.BlockSpec((PROJ_ROWS, d), row), pl.BlockSpec((PROJ_ROWS, d // 2), row)],
        out_shape=[jax.ShapeDtypeStruct((n, d), _F32), jax.ShapeDtypeStruct((n, d // 2), jnp.uint32)],
        compiler_params=_params("arbitrary"),
        name="proj_res_ln",
    )(a_bf16, w_bf16, res, g.reshape(1, d), b.reshape(1, d))


def _l0_block_kernel(x_ref, w_in_ref, cos_ref, sin_ref, xi_ref, zeta_ref, decay_ref, gstate_ref,
                     gn_g_ref, conv_w_ref, conv_b_ref, cln_g_ref, cln_b_ref, w_out_ref, ln_g_ref,
                     ln_b_ref, o_ref, words_ref, state_ref, u_ref, shift_ref, h_ref, mix_ref):
    rows = MIX_ROWS
    step = pl.program_id(1)
    v_off = 2 * RET_QK_WIDTH
    g_off = v_off + RET_WIDTH
    a_off = g_off + RET_WIDTH

    @pl.when(step == 0)
    def _():
        state_ref[...] = jnp.zeros_like(state_ref)
        u_ref[0:CONV_HALO, :] = jnp.zeros((CONV_HALO, CONV_CH), _F32)

    x_bf16 = x_ref[...].astype(_BF16)
    h_conv = jnp.dot(x_bf16, w_in_ref[:, a_off:a_off + 2 * CONV_CH], preferred_element_type=_F32)
    h_ref[...] = jnp.dot(x_bf16, w_in_ref[:, 0:a_off], preferred_element_type=_F32)

    u_ref[CONV_HALO:CONV_HALO + rows, :] = h_conv[:, :CONV_CH] * jax.nn.sigmoid(h_conv[:, CONV_CH:])
    acc = jnp.broadcast_to(conv_b_ref[...], (rows, CONV_CH))
    first = CONV_HALO - (CONV_WIDTH - 1)
    for phase in range(F32_SUBLANES):
        offsets = [first + j for j in range(CONV_WIDTH) if (first + j) % F32_SUBLANES == phase]
        span = max(offsets) - phase + rows
        if phase == 0:
            src = u_ref
        else:
            shift_ref[0:span, :] = u_ref[phase:phase + span, :]
            src = shift_ref
        for off in offsets:
            j = off - first
            acc = acc + conv_w_ref[j:j + 1, :] * src[off - phase:off - phase + rows, :]
    u_ref[0:CONV_HALO, :] = u_ref[rows:rows + CONV_HALO, :]
    c = _silu(_layer_norm_rows(acc, cln_g_ref[...], cln_b_ref[...]))
    mix_ref[:, RET_WIDTH:RET_WIDTH + CONV_CH] = c.astype(_BF16)

    lane = lax.broadcasted_iota(jnp.int32, (rows, RET_QK_WIDTH), 1)
    first_half = (lane % RET_DK) < (RET_DK // 2)
    cos = cos_ref[...]
    sin = sin_ref[...]

    def rotary(t):
        partner = jnp.where(first_half,
                            pltpu.roll(t, RET_QK_WIDTH - RET_DK // 2, axis=1),
                            pltpu.roll(t, RET_DK // 2, axis=1))
        return t * cos + partner * sin

    q = rotary(h_ref[:, 0:RET_QK_WIDTH])
    k = rotary(h_ref[:, RET_QK_WIDTH:2 * RET_QK_WIDTH]) * (RET_DK ** -0.5)
    q_in = (q * xi_ref[...]).astype(_BF16)
    k_out = (k * zeta_ref[...]).astype(_BF16)
    q = q.astype(_BF16)
    k = k.astype(_BF16)
    for hd in range(RET_HEADS):
        qk = slice(hd * RET_DK, (hd + 1) * RET_DK)
        vs = slice(hd * RET_DV, (hd + 1) * RET_DV)
        v = h_ref[:, v_off + hd * RET_DV:v_off + (hd + 1) * RET_DV].astype(_BF16)
        s = lax.dot_general(q[:, qk], k[:, qk], (((1,), (1,)), ((), ())),
                            preferred_element_type=_F32) * decay_ref[hd]
        r = jnp.dot(s.astype(_BF16), v, preferred_element_type=_F32)
        r += jnp.dot(q_in[:, qk], state_ref[hd].astype(_BF16), preferred_element_type=_F32)
        kv = lax.dot_general(k_out[:, qk], v, (((0,), (0,)), ((), ())),
                             preferred_element_type=_F32)
        state_ref[hd] = gstate_ref[hd] * state_ref[hd] + kv
        mu = jnp.mean(r, axis=-1, keepdims=True)
        rc = r - mu
        var = jnp.mean(rc * rc, axis=-1, keepdims=True)
        rn = rc * lax.rsqrt(var + LN_EPS) * gn_g_ref[:, vs]
        gate = h_ref[:, g_off + hd * RET_DV:g_off + (hd + 1) * RET_DV]
        mix_ref[:, vs] = (_silu(gate) * rn).astype(_BF16)

    m = jnp.dot(mix_ref[...], w_out_ref[...], preferred_element_type=_F32)
    y = _layer_norm_rows(DEEPNORM_ALPHA * x_ref[...] + m, ln_g_ref[...], ln_b_ref[...])
    o_ref[...] = y
    words_ref[...] = _pack_bf16_pairs(y)


def _l0_mixer_tables(seq):
    half = RET_DK // 2
    inv = ROPE_BASE ** (-jnp.arange(half, dtype=_F32) / half)
    ang = jnp.arange(seq).astype(_F32)[:, None] * inv[None, :]
    cos = jnp.tile(jnp.cos(ang), (1, 2 * RET_HEADS))
    sin = jnp.tile(jnp.concatenate([-jnp.sin(ang), jnp.sin(ang)], axis=1), (1, RET_HEADS))
    log_g = jnp.log(1.0 - 2.0 ** (-5.0 - jnp.arange(RET_HEADS, dtype=_F32)))
    n = jnp.arange(MIX_ROWS, dtype=_F32)
    diff = n[:, None] - n[None, :]
    chunk = jnp.arange(MIX_ROWS) // CHUNK
    expo = jnp.where(chunk[:, None] == chunk[None, :], jnp.abs(diff), diff)
    decay = jnp.where((chunk[:, None] >= chunk[None, :])[None],
                      jnp.exp(expo[None] * log_g[:, None, None]), 0.0)
    xi = jnp.repeat(jnp.exp((n[:, None] + 1.0) * log_g[None, :]), RET_DK, axis=1)
    zeta = jnp.repeat(jnp.exp((MIX_ROWS - 1.0 - n[:, None]) * log_g[None, :]), RET_DK, axis=1)
    g_state = jnp.exp(MIX_ROWS * log_g)
    return cos, sin, xi, zeta, decay, g_state


def _l0_block(x, batch, seq, w_in, gn_g, conv_w, conv_b, cln_g, cln_b, w_out, ln_g, ln_b):
    n, d = x.shape
    steps = seq // MIX_ROWS
    cos, sin, xi, zeta, decay, g_state = _l0_mixer_tables(seq)
    row = lambda b, i: (b * steps + i, 0)
    pos = lambda b, i: (i, 0)
    vec = lambda t: t.reshape(1, -1)
    return pl.pallas_call(
        _l0_block_kernel,
        grid=(batch, steps),
        in_specs=[pl.BlockSpec((MIX_ROWS, d), row), _const_spec((d, L0_IN)),
                  pl.BlockSpec((MIX_ROWS, RET_QK_WIDTH), pos),
                  pl.BlockSpec((MIX_ROWS, RET_QK_WIDTH), pos),
                  _const_spec((MIX_ROWS, RET_QK_WIDTH)), _const_spec((MIX_ROWS, RET_QK_WIDTH)),
                  _const_spec((RET_HEADS, MIX_ROWS, MIX_ROWS)),
                  pl.BlockSpec(memory_space=pltpu.SMEM),
                  _const_spec((1, RET_WIDTH)), _const_spec((CONV_WIDTH, CONV_CH)),
                  _const_spec((1, CONV_CH)), _const_spec((1, CONV_CH)), _const_spec((1, CONV_CH)),
                  _const_spec((RET_WIDTH + CONV_CH, d)), _const_spec((1, d)), _const_spec((1, d))],
        out_specs=[pl.BlockSpec((MIX_ROWS, d), row), pl.BlockSpec((MIX_ROWS, d // 2), row)],
        out_shape=[jax.ShapeDtypeStruct((n, d), _F32), jax.ShapeDtypeStruct((n, d // 2), jnp.uint32)],
        scratch_shapes=[pltpu.VMEM((RET_HEADS, RET_DK, RET_DV), _F32),
                        pltpu.VMEM((MIX_ROWS + CONV_HALO, CONV_CH), _F32),
                        pltpu.VMEM((MIX_ROWS + CONV_HALO, CONV_CH), _F32),
                        pltpu.VMEM((MIX_ROWS, 2 * RET_QK_WIDTH + 2 * RET_WIDTH), _F32),
                        pltpu.VMEM((MIX_ROWS, RET_WIDTH + CONV_CH), _BF16)],
        compiler_params=_params("arbitrary", "arbitrary"),
        name="l0_block",
    )(x, w_in.astype(_BF16), cos, sin, xi, zeta, decay, g_state, vec(gn_g), conv_w, vec(conv_b),
      vec(cln_g), vec(cln_b), w_out.astype(_BF16), vec(ln_g), vec(ln_b))


def _sb_attention_kernel(q_ref, k_ref, v_ref, tri_ref, o_ref, acc_ref, carry_ref):
    blk = SB_BLOCK
    qi = pl.program_id(1)
    t_idx = lax.broadcasted_iota(jnp.int32, (blk, blk), 0)
    s_idx = lax.broadcasted_iota(jnp.int32, (blk, blk), 1)
    keep = s_idx < t_idx

    heads = [slice(hd * SB_HEAD_DIM, (hd + 1) * SB_HEAD_DIM) for hd in range(SB_HEADS)]

    def sweep(blocks, carry):
        starts = [pl.multiple_of(j * blk, blk) for j, _ in blocks]
        zs = [[lax.dot_general(q_ref[:, hs], k_ref[pl.ds(st, blk), hs], (((1,), (1,)), ((), ())),
                               preferred_element_type=_F32) for hs in heads] for st in starts]
        log_betas, addends = [], []
        for (_, mask), z_heads in zip(blocks, zs):
            for z in z_heads:
                softplus = jnp.log2(1.0 + jnp.exp2(-jnp.abs(z)))
                log_beta = jnp.minimum(z, 0.0) - softplus
                log_rest = log_beta - z
                if mask is not None:
                    log_rest = jnp.where(mask, log_rest, 0.0)
                log_betas.append(log_beta)
                addends.append(log_rest.astype(_BF16))
        sums = [jnp.dot(t, tri_ref[...], preferred_element_type=_F32) for t in addends]
        weights, total = [], None
        for b, (_, mask) in enumerate(blocks):
            block_sums = sums[b * SB_HEADS:(b + 1) * SB_HEADS]
            before = carry if total is None else (total if carry is None else carry + total)
            for hd, s in enumerate(block_sums):
                expo = log_betas[b * SB_HEADS + hd] + s[:, :blk]
                if before is not None:
                    expo = expo + before[:, heads[hd]]
                a = jnp.exp2(expo)
                if mask is not None:
                    a = jnp.where(mask, a, 0.0)
                weights.append(a.astype(_BF16))
            block_total = jnp.concatenate([s[:, blk:] for s in block_sums], axis=1)
            total = block_total if total is None else total + block_total
        outs = []
        for hd, hs in enumerate(heads):
            o = None
            for b, st in enumerate(starts):
                t = jnp.dot(weights[b * SB_HEADS + hd], v_ref[pl.ds(st, blk), hs],
                            preferred_element_type=_F32)
                o = t if o is None else o + t
            outs.append(o)
        return jnp.concatenate(outs, axis=1), total

    out, carry = sweep([(qi, keep)], None)
    acc_ref[...] = out
    carry_ref[...] = carry

    def cond(state):
        j, alive = state
        return jnp.logical_and(j >= 0, alive)

    def body(state):
        j, _ = state
        carry = carry_ref[...]
        out, total = sweep([(j, None), (jnp.maximum(j - 1, 0), j >= 1)], carry)
        carry = carry + total
        acc_ref[...] += out
        carry_ref[...] = carry
        return j - 2, jnp.max(carry) > -SB_CUTOFF

    lax.while_loop(cond, body, (qi - 1, jnp.max(carry) > -SB_CUTOFF))
    o_ref[...] = acc_ref[...].astype(o_ref.dtype)


def _sb_attention(qkv, batch, seq):
    n = qkv.shape[0]
    nq = seq // SB_BLOCK
    j = jnp.arange(SB_BLOCK)[:, None]
    s = jnp.arange(SB_BLOCK)[None, :]
    tri = jnp.concatenate([(j > s).astype(_BF16), jnp.ones((SB_BLOCK, SB_BLOCK), _BF16)], axis=1)
    whole_seq = lambda col: pl.BlockSpec((seq, D_MODEL), lambda b, i: (b, col),
                                         pipeline_mode=pl.Buffered(1))
    return pl.pallas_call(
        _sb_attention_kernel,
        grid=(batch, nq),
        in_specs=[pl.BlockSpec((SB_BLOCK, D_MODEL), lambda b, i: (b * nq + i, 0)),
                  whole_seq(1), whole_seq(2), _const_spec((SB_BLOCK, 2 * SB_BLOCK))],
        out_specs=pl.BlockSpec((SB_BLOCK, D_MODEL), lambda b, i: (b * nq + i, 0)),
        out_shape=jax.ShapeDtypeStruct((n, D_MODEL), _BF16),
        scratch_shapes=[pltpu.VMEM((SB_BLOCK, D_MODEL), _F32), pltpu.VMEM((SB_BLOCK, D_MODEL), _F32)],
        compiler_params=_params("arbitrary", "arbitrary"),
        name="sb_attention",
    )(qkv, qkv, qkv, tri)


def _router_kernel(x_ref, w_ref, b_ref, tri_ref, idx_ref, gate_ref, rank_ref, count_ref, base_ref):
    @pl.when(pl.program_id(0) == 0)
    def _():
        base_ref[...] = jnp.zeros_like(base_ref)

    x = x_ref[...]
    w = w_ref[...]
    x_hi = x.astype(_BF16)
    x_lo = (x - x_hi.astype(_F32)).astype(_BF16)
    w_hi = w.astype(_BF16)
    w_lo = (w - w_hi.astype(_F32)).astype(_BF16)
    nt = (((1,), (1,)), ((), ()))
    logits = (lax.dot_general(w_hi, x_hi, nt, preferred_element_type=_F32)
              + lax.dot_general(w_lo, x_hi, nt, preferred_element_type=_F32)
              + lax.dot_general(w_hi, x_lo, nt, preferred_element_type=_F32)) + b_ref[...]
    expert = lax.broadcasted_iota(jnp.int32, logits.shape, 0)
    vals, idxs = [], []
    for _ in range(TOP_K):
        m = jnp.max(logits, axis=0, keepdims=True)
        sel = jnp.min(jnp.where(logits == m, expert, N_EXPERTS), axis=0, keepdims=True)
        vals.append(m)
        idxs.append(sel)
        logits = jnp.where(expert == sel, -jnp.inf, logits)
    vals = jnp.concatenate(vals, axis=0)
    e = jnp.exp(vals - vals[0:1])
    gate_ref[...] = e / jnp.sum(e, axis=0, keepdims=True)
    idx_ref[...] = jnp.concatenate(idxs, axis=0)

    member = jnp.zeros(logits.shape, _F32)
    for sel in idxs:
        member += (expert == sel).astype(_F32)
    before = jnp.dot(member.astype(_BF16), tri_ref[...], preferred_element_type=_F32) + base_ref[...]
    ranks = [jnp.sum(jnp.where(expert == sel, before, 0.0), axis=0, keepdims=True) for sel in idxs]
    rank_ref[...] = jnp.concatenate(ranks, axis=0).astype(jnp.int32)
    base_ref[...] += jnp.sum(member, axis=1, keepdims=True)
    count_ref[...] = base_ref[...]


def _router(x, router_w, router_b):
    n, d = x.shape
    t = jnp.arange(ROUTER_ROWS)
    tri = (t[:, None] < t[None, :]).astype(_BF16)
    tok = pl.BlockSpec((TOP_K, ROUTER_ROWS), lambda i: (0, i))
    return pl.pallas_call(
        _router_kernel,
        grid=(n // ROUTER_ROWS,),
        in_specs=[pl.BlockSpec((ROUTER_ROWS, d), lambda i: (i, 0)),
                  _const_spec((N_EXPERTS, d)), _const_spec((N_EXPERTS, 1)),
                  _const_spec((ROUTER_ROWS, ROUTER_ROWS))],
        out_specs=[tok, tok, tok, _const_spec((N_EXPERTS, 1))],
        out_shape=[jax.ShapeDtypeStruct((TOP_K, n), jnp.int32),
                   jax.ShapeDtypeStruct((TOP_K, n), _F32),
                   jax.ShapeDtypeStruct((TOP_K, n), jnp.int32),
                   jax.ShapeDtypeStruct((N_EXPERTS, 1), _F32)],
        scratch_shapes=[pltpu.VMEM((N_EXPERTS, 1), _F32)],
        compiler_params=_params("arbitrary"),
        name="router",
    )(x, router_w.T, router_b.reshape(N_EXPERTS, 1), tri)


def _prep_up_kernel(w_ref, sel_ref, o_ref):
    group = 2 * PREP_LANES
    for c in range(w_ref.shape[2] // group):
        t = jnp.dot(w_ref[0, :, c * group:(c + 1) * group].astype(_BF16), sel_ref[...],
                    preferred_element_type=_F32)
        o_ref[0, :, c * PREP_LANES:(c + 1) * PREP_LANES] = t[:, :PREP_LANES].astype(_BF16)
        o_ref[0, :, D_FF + c * PREP_LANES:D_FF + (c + 1) * PREP_LANES] = t[:, PREP_LANES:].astype(_BF16)


def _prep_up(w_up):
    e, d, ff2 = w_up.shape
    i = jnp.arange(2 * PREP_LANES)[:, None]
    j = jnp.arange(2 * PREP_LANES)[None, :]
    sel = (i == jnp.where(j < PREP_LANES, 2 * j, 2 * (j - PREP_LANES) + 1)).astype(_BF16)
    blk = lambda x, r: (x, r, 0)
    return pl.pallas_call(
        _prep_up_kernel,
        grid=(e, d // PREP_ROWS),
        in_specs=[pl.BlockSpec((1, PREP_ROWS, ff2), blk), _const_spec((2 * PREP_LANES, 2 * PREP_LANES))],
        out_specs=pl.BlockSpec((1, PREP_ROWS, ff2), blk),
        out_shape=jax.ShapeDtypeStruct((e, d, ff2), _BF16),
        compiler_params=_params("arbitrary", "arbitrary"),
        name="prep_up",
    )(w_up, sel)


def _dispatch(x_words, slot_of, p):
    n, words = x_words.shape
    part = words // DISPATCH_PARTS
    mesh = plsc.VectorSubcoreMesh(core_axis_name="core", subcore_axis_name="subcore",
                                  num_cores=SC_CORES, num_subcores=SC_SUBCORES)
    out = jax.ShapeDtypeStruct((p, part), x_words.dtype)

    @functools.partial(
        pl.kernel, out_type=[out] * DISPATCH_PARTS, mesh=mesh, scratch_types=[],
        compiler_params=pltpu.CompilerParams(use_tc_tiling_on_sc=True), name="moe_dispatch")
    def dispatch(x_hbm, slot_hbm, *o_hbm):
        for c in range(DISPATCH_PARTS):
            def body(x_vmem, slot_vmem, o_ref=o_hbm[c]):
                for k in range(TOP_K):
                    pltpu.sync_copy(x_vmem, o_ref.at[slot_vmem.at[k]])

            pltpu.emit_pipeline(
                body,
                grid=(n // DISPATCH_ROWS,),
                in_specs=[pl.BlockSpec((DISPATCH_ROWS, part), lambda i, c=c: (i, c)),
                          pl.BlockSpec((TOP_K, DISPATCH_ROWS), lambda i: (0, i))],
                out_specs=[],
                core_axis_name=("core", "subcore"),
                dimension_semantics=(pltpu.PARALLEL,),
            )(x_hbm, slot_hbm)

    return dispatch(x_words, slot_of.reshape(TOP_K, n))


def _combine_gather(y_parts, slot_of):
    nk = slot_of.shape[0]
    part = y_parts[0].shape[1]
    mesh = plsc.VectorSubcoreMesh(core_axis_name="core", subcore_axis_name="subcore",
                                  num_cores=SC_CORES, num_subcores=SC_SUBCORES)
    out = jax.ShapeDtypeStruct((nk, part), y_parts[0].dtype)

    @functools.partial(
        pl.kernel, out_type=[out] * DISPATCH_PARTS, mesh=mesh, scratch_types=[],
        compiler_params=pltpu.CompilerParams(use_tc_tiling_on_sc=True), name="moe_combine_gather")
    def gather(*refs):
        y_hbm, slot_hbm, o_hbm = refs[:DISPATCH_PARTS], refs[DISPATCH_PARTS], refs[DISPATCH_PARTS + 1:]
        for c in range(DISPATCH_PARTS):
            def body(slot_vmem, o_vmem, y_ref=y_hbm[c]):
                pltpu.sync_copy(y_ref.at[slot_vmem.at[0]], o_vmem)

            pltpu.emit_pipeline(
                body,
                grid=(nk // DISPATCH_ROWS,),
                in_specs=[pl.BlockSpec((1, DISPATCH_ROWS), lambda i: (0, i))],
                out_specs=[pl.BlockSpec((DISPATCH_ROWS, part), lambda i: (i, 0))],
                core_axis_name=("core", "subcore"),
                dimension_semantics=(pltpu.PARALLEL,),
            )(slot_hbm, o_hbm[c])

    return gather(*y_parts, slot_of.reshape(1, nk))


def _moe_ffn_kernel(block_e_ref, block_rows_ref, xa_ref, xb_ref, wu_ref, bu_ref, wd_f32_ref, bd_ref,
                    oa_ref, ob_ref, wd_ref):
    step = pl.program_id(0)
    rows_used = block_rows_ref[step]
    part = oa_ref.shape[1]
    new_expert = jnp.logical_or(step == 0,
                                block_e_ref[step] != block_e_ref[jnp.maximum(step - 1, 0)])

    @pl.when(jnp.logical_and(new_expert, rows_used > 0))
    def _():
        wd_ref[...] = wd_f32_ref[0].astype(_BF16)

    for sub in range(MOE_STEP_ROWS // MOE_ROWS):
        rs = slice(sub * MOE_ROWS, (sub + 1) * MOE_ROWS)

        @pl.when(sub * MOE_ROWS < rows_used)
        def _():
            x = _unpack_bf16_pairs(jnp.concatenate([xa_ref[rs, :], xb_ref[rs, :]], axis=1))
            h = jnp.dot(x, wu_ref[0], preferred_element_type=_F32) + bu_ref[0]
            h_glu = jnp.minimum(h[:, :D_FF], SWIGLU_LIMIT)
            h_lin = jnp.clip(h[:, D_FF:], -SWIGLU_LIMIT, SWIGLU_LIMIT)
            act = h_glu * jax.nn.sigmoid(SWIGLU_ALPHA * h_glu) * (h_lin + 1.0)
            y = jnp.dot(act.astype(_BF16), wd_ref[...], preferred_element_type=_F32) + bd_ref[0]
            words = _pack_bf16_pairs(y)
            oa_ref[rs, :] = words[:, :part]
            ob_ref[rs, :] = words[:, part:]

        @pl.when(sub * MOE_ROWS >= rows_used)
        def _():
            oa_ref[rs, :] = jnp.zeros((MOE_ROWS, part), oa_ref.dtype)
            ob_ref[rs, :] = jnp.zeros((MOE_ROWS, part), ob_ref.dtype)


def _moe_ffn(xs, block_e, block_rows, wu, bu, wd_f32, bd):
    xa, xb = xs
    p, part = xa.shape
    ff, d = wd_f32.shape[1:]
    ff2 = wu.shape[2]
    row = lambda i, be, br: (i, 0)
    exp3 = lambda i, be, br: (be[i], 0, 0)
    rows_spec = pl.BlockSpec((MOE_STEP_ROWS, part), row)
    grid_spec = pltpu.PrefetchScalarGridSpec(
        num_scalar_prefetch=2,
        grid=(p // MOE_STEP_ROWS,),
        in_specs=[rows_spec, rows_spec,
                  pl.BlockSpec((1, d, ff2), exp3), pl.BlockSpec((1, 1, ff2), exp3),
                  pl.BlockSpec((1, ff, d), exp3), pl.BlockSpec((1, 1, d), exp3)],
        out_specs=[rows_spec, rows_spec],
        scratch_shapes=[pltpu.VMEM((ff, d), _BF16)],
    )
    out = jax.ShapeDtypeStruct((p, part), xa.dtype)
    return pl.pallas_call(
        _moe_ffn_kernel,
        grid_spec=grid_spec,
        out_shape=[out, out],
        compiler_params=_params("arbitrary"),
        name="moe_ffn",
    )(block_e, block_rows, xa, xb, wu, bu, wd_f32, bd)


def _combine_ln_kernel(ya_ref, yb_ref, gate_ref, res_ref, g_ref, b_ref, *rest):
    o_ref, obf_ref = rest[-2:]
    gate = gate_ref[...]
    moe = None
    for k in range(TOP_K):
        words = jnp.concatenate([ya_ref[k], yb_ref[k]], axis=1)
        term = _unpack_bf16_pairs(words).astype(_F32) * gate[:, k:k + 1]
        moe = term if moe is None else moe + term
    y = _layer_norm_rows(DEEPNORM_ALPHA * res_ref[...] + moe, g_ref[...], b_ref[...])
    o_ref[...] = y
    obf_ref[...] = y.astype(_BF16)


def _combine_ln(y4_parts, gates, res, g, b, first_row, previous):
    n, d = res.shape
    rows = y4_parts[0].shape[0] // TOP_K
    ya, yb = (t.reshape(TOP_K, rows, t.shape[1]) for t in y4_parts)
    part = ya.shape[2]
    first_block = first_row // COMBINE_ROWS
    local = lambda i: (i, 0)
    row = lambda i: (first_block + i, 0)
    y_spec = pl.BlockSpec((TOP_K, COMBINE_ROWS, part), lambda i: (0, i, 0))
    in_specs = [y_spec, y_spec, pl.BlockSpec((COMBINE_ROWS, TOP_K), local),
                pl.BlockSpec((COMBINE_ROWS, d), row), _const_spec((1, d)), _const_spec((1, d))]
    args = [ya, yb, gates, res, g.reshape(1, d), b.reshape(1, d)]
    aliases = {}
    if previous is not None:
        aliases = {len(args): 0, len(args) + 1: 1}
        in_specs += [pl.BlockSpec(memory_space=pl.ANY)] * 2
        args += list(previous)
    return pl.pallas_call(
        _combine_ln_kernel,
        grid=(rows // COMBINE_ROWS,),
        in_specs=in_specs,
        out_specs=[pl.BlockSpec((COMBINE_ROWS, d), row), pl.BlockSpec((COMBINE_ROWS, d), row)],
        out_shape=[jax.ShapeDtypeStruct((n, d), _F32), jax.ShapeDtypeStruct((n, d), _BF16)],
        input_output_aliases=aliases,
        compiler_params=_params("arbitrary"),
        name="combine_ln",
    )(*args)


def _moe_layer(x, x_words, router_w, router_b, w_up, b_up, w_down, b_down, ln_g, ln_b):
    n, d = x.shape
    nk = n * TOP_K
    idx_t, gate_t, rank_t, count = _router(x, router_w, router_b)
    counts = count[:, 0].astype(jnp.int32)
    padded = (counts + MOE_STEP_ROWS - 1) // MOE_STEP_ROWS * MOE_STEP_ROWS
    padded_ends = jnp.cumsum(padded)
    padded_starts = padded_ends - padded
    n_blocks = -(-(nk + N_EXPERTS * (MOE_STEP_ROWS - 1)) // MOE_STEP_ROWS)
    p = n_blocks * MOE_STEP_ROWS
    experts = jnp.arange(N_EXPERTS, dtype=jnp.int32)
    start_of = jnp.sum(jnp.where(idx_t[None] == experts[:, None, None],
                                 padded_starts[:, None, None], 0), axis=0)
    slot_of = (start_of + rank_t).reshape(nk)
    block_start = jnp.arange(n_blocks, dtype=jnp.int32) * MOE_STEP_ROWS
    block_e = jnp.minimum(jnp.sum(padded_ends[None, :] <= block_start[:, None], axis=1),
                          N_EXPERTS - 1).astype(jnp.int32)
    block_rows = jnp.where(block_start < padded_ends[-1],
                           jnp.clip(counts[block_e] - (block_start - padded_starts[block_e]),
                                    0, MOE_STEP_ROWS), 0).astype(jnp.int32)
    xs = _dispatch(x_words, slot_of, p)
    bu = jnp.concatenate([b_up[:, 0::2], b_up[:, 1::2]], axis=1).reshape(N_EXPERTS, 1, 2 * D_FF)
    ys = _moe_ffn(xs, block_e, block_rows, _prep_up(w_up), bu, w_down,
                  b_down.reshape(N_EXPERTS, 1, d))
    slot_kn = slot_of.reshape(TOP_K, n)
    gates = gate_t.T
    chunk = n // COMBINE_CHUNKS
    outs = None
    for c in range(COMBINE_CHUNKS):
        tokens = slice(c * chunk, (c + 1) * chunk)
        y4 = _combine_gather(ys, slot_kn[:, tokens].reshape(TOP_K * chunk))
        outs = _combine_ln(y4, gates[tokens], x, ln_g, ln_b, c * chunk, outs)
    return outs


def kernel(x, l0_w_in, l0_ret_gn_g, l0_conv_w, l0_conv_b, l0_conv_ln_g, l0_conv_ln_b, l0_w_out, l0_ln1_g, l0_ln1_b, l0_router_w, l0_router_b, l0_w_up, l0_b_up, l0_w_down, l0_b_down, l0_ln2_g, l0_ln2_b, l1_w_in, l1_w_out, l1_ln1_g, l1_ln1_b, l1_router_w, l1_router_b, l1_w_up, l1_b_up, l1_w_down, l1_b_down, l1_ln2_g, l1_ln2_b):
    batch, seq, d = x.shape
    n = batch * seq
    x0 = x.reshape(n, d)

    x1, x1_words = _l0_block(x0, batch, seq, l0_w_in, l0_ret_gn_g, l0_conv_w, l0_conv_b,
                             l0_conv_ln_g, l0_conv_ln_b, l0_w_out, l0_ln1_g, l0_ln1_b)
    x2, x2_bf16 = _moe_layer(x1, x1_words, l0_router_w, l0_router_b, l0_w_up, l0_b_up, l0_w_down,
                             l0_b_down, l0_ln2_g, l0_ln2_b)

    q_scale = jnp.where(jnp.arange(3 * d) < d, LOG2_E * SB_HEAD_DIM ** -0.5, 1.0)
    qkv = _proj(x2_bf16, (l1_w_in * q_scale).astype(_BF16))
    att = _sb_attention(qkv, batch, seq)
    x3, x3_words = _proj_res_ln(att, l1_w_out.astype(_BF16), x2, l1_ln1_g, l1_ln1_b)
    x4, _ = _moe_layer(x3, x3_words, l1_router_w, l1_router_b, l1_w_up, l1_b_up, l1_w_down,
                       l1_b_down, l1_ln2_g, l1_ln2_b)
    return x4.reshape(batch, seq, d)
```

```python
import functools

import jax
import jax.numpy as jnp
from jax import lax
from jax.experimental import pallas as pl
from jax.experimental.pallas import tpu as pltpu
from jax.experimental.pallas import tpu_sc as plsc

D_MODEL = 1024
CHUNK = 64
RET_HEADS = 4
RET_DK = 64
RET_DV = 128
RET_QK_WIDTH = RET_HEADS * RET_DK
RET_WIDTH = RET_HEADS * RET_DV
ROPE_BASE = 10000.0
CONV_CH = D_MODEL // 2
CONV_WIDTH = 31
L0_IN = 2 * RET_QK_WIDTH + 2 * RET_WIDTH + 2 * CONV_CH
SB_HEADS = 8
SB_HEAD_DIM = D_MODEL // SB_HEADS
N_EXPERTS = 32
TOP_K = 4
D_FF = D_MODEL
SWIGLU_LIMIT = 7.0
SWIGLU_ALPHA = 1.702
LN_EPS = 1e-5
DEPTH = 2
DEEPNORM_ALPHA = (2 * DEPTH) ** 0.25

VMEM_LIMIT_BYTES = 48 * 1024 * 1024
F32_SUBLANES = 8
PROJ_ROWS = 512
MIX_ROWS = 256
CONV_HALO = 32
ROUTER_ROWS = 512
MOE_ROWS = 512
MOE_TAIL_ROWS = 256
MOE_STEP_ROWS = 1024
PREP_ROWS = 512
PREP_LANES = 128
COMBINE_ROWS = 256
DISPATCH_ROWS = 128
DISPATCH_PARTS = 2
SC_CORES = 2
SC_SUBCORES = 16
SB_BLOCK = 128
SB_CUTOFF = 150.0
LOG2_E = 1.4426950408889634

_F32 = jnp.float32
_BF16 = jnp.bfloat16


def _params(*sem):
    return pltpu.CompilerParams(dimension_semantics=sem, vmem_limit_bytes=VMEM_LIMIT_BYTES)


def _const_spec(shape):
    nd = len(shape)
    return pl.BlockSpec(shape, lambda *_: (0,) * nd)


def _layer_norm_rows(y, g, b):
    mu = jnp.mean(y, axis=-1, keepdims=True)
    yc = y - mu
    var = jnp.mean(yc * yc, axis=-1, keepdims=True)
    return yc * lax.rsqrt(var + LN_EPS) * g + b


def _silu(x):
    return x * jax.nn.sigmoid(x)


def _pack_bf16_pairs(y):
    half = y.shape[1] // 2
    lo = lax.bitcast_convert_type(y[:, :half].astype(_BF16).astype(_F32), jnp.uint32)
    hi = lax.bitcast_convert_type(y[:, half:].astype(_BF16).astype(_F32), jnp.uint32)
    return (lo >> 16) | (hi & jnp.uint32(0xFFFF0000))


def _unpack_bf16_pairs(w):
    lo = lax.bitcast_convert_type(w << 16, _F32)
    hi = lax.bitcast_convert_type(w & jnp.uint32(0xFFFF0000), _F32)
    return jnp.concatenate([lo, hi], axis=1).astype(_BF16)


def _proj_kernel(x_ref, w_ref, scale_ref, o_ref):
    o_ref[...] = (jnp.dot(x_ref[...].astype(_BF16), w_ref[...], preferred_element_type=_F32)
                  * scale_ref[...]).astype(o_ref.dtype)


def _proj(x, w_bf16, col_scale):
    n, d = x.shape
    width = w_bf16.shape[1]
    return pl.pallas_call(
        _proj_kernel,
        grid=(n // PROJ_ROWS,),
        in_specs=[pl.BlockSpec((PROJ_ROWS, d), lambda i: (i, 0)), _const_spec((d, width)),
                  _const_spec((1, width))],
        out_specs=pl.BlockSpec((PROJ_ROWS, width), lambda i: (i, 0)),
        out_shape=jax.ShapeDtypeStruct((n, width), _BF16),
        compiler_params=_params("arbitrary"),
        name="proj",
    )(x, w_bf16, col_scale.reshape(1, width))


def _proj_res_ln_kernel(a_ref, w_ref, res_ref, g_ref, b_ref, o_ref, words_ref):
    m = jnp.dot(a_ref[...], w_ref[...], preferred_element_type=_F32)
    y = _layer_norm_rows(DEEPNORM_ALPHA * res_ref[...] + m, g_ref[...], b_ref[...])
    o_ref[...] = y
    words_ref[...] = _pack_bf16_pairs(y)


def _proj_res_ln(a_bf16, w_bf16, res, g, b):
    n, d = res.shape
    k = a_bf16.shape[1]
    row = lambda i: (i, 0)
    return pl.pallas_call(
        _proj_res_ln_kernel,
        grid=(n // PROJ_ROWS,),
        in_specs=[pl.BlockSpec((PROJ_ROWS, k), row), _const_spec((k, d)),
                  pl.BlockSpec((PROJ_ROWS, d), row), _const_spec((1, d)), _const_spec((1, d))],
        out_specs=[pl.BlockSpec((PROJ_ROWS, d), row), pl.BlockSpec((PROJ_ROWS, d // 2), row)],
        out_shape=[jax.ShapeDtypeStruct((n, d), _F32), jax.ShapeDtypeStruct((n, d // 2), jnp.uint32)],
        compiler_params=_params("arbitrary"),
        name="proj_res_ln",
    )(a_bf16, w_bf16, res, g.reshape(1, d), b.reshape(1, d))


def _l0_block_kernel(x_ref, w_in_ref, cos_ref, sin_ref, xi_ref, zeta_ref, decay_ref, gstate_ref,
                     gn_g_ref, conv_w_ref, conv_b_ref, cln_g_ref, cln_b_ref, w_out_ref, ln_g_ref,
                     ln_b_ref, o_ref, words_ref, state_ref, u_ref, shift_ref, h_ref, mix_ref):
    rows = MIX_ROWS
    step = pl.program_id(1)
    v_off = 2 * RET_QK_WIDTH
    g_off = v_off + RET_WIDTH
    a_off = g_off + RET_WIDTH

    @pl.when(step == 0)
    def _():
        state_ref[...] = jnp.zeros_like(state_ref)
        u_ref[0:CONV_HALO, :] = jnp.zeros((CONV_HALO, CONV_CH), _F32)

    x_bf16 = x_ref[...].astype(_BF16)
    h_conv = jnp.dot(x_bf16, w_in_ref[:, a_off:a_off + 2 * CONV_CH], preferred_element_type=_F32)
    h_ref[...] = jnp.dot(x_bf16, w_in_ref[:, 0:a_off], preferred_element_type=_F32)

    u_ref[CONV_HALO:CONV_HALO + rows, :] = h_conv[:, :CONV_CH] * jax.nn.sigmoid(h_conv[:, CONV_CH:])
    acc = jnp.broadcast_to(conv_b_ref[...], (rows, CONV_CH))
    first = CONV_HALO - (CONV_WIDTH - 1)
    for phase in range(F32_SUBLANES):
        offsets = [first + j for j in range(CONV_WIDTH) if (first + j) % F32_SUBLANES == phase]
        span = max(offsets) - phase + rows
        if phase == 0:
            src = u_ref
        else:
            shift_ref[0:span, :] = u_ref[phase:phase + span, :]
            src = shift_ref
        for off in offsets:
            j = off - first
            acc = acc + conv_w_ref[j:j + 1, :] * src[off - phase:off - phase + rows, :]
    u_ref[0:CONV_HALO, :] = u_ref[rows:rows + CONV_HALO, :]
    c = _silu(_layer_norm_rows(acc, cln_g_ref[...], cln_b_ref[...]))
    mix_ref[:, RET_WIDTH:RET_WIDTH + CONV_CH] = c.astype(_BF16)

    lane = lax.broadcasted_iota(jnp.int32, (rows, RET_QK_WIDTH), 1)
    first_half = (lane % RET_DK) < (RET_DK // 2)
    cos = cos_ref[...]
    sin = sin_ref[...]

    def rotary(t):
        partner = jnp.where(first_half,
                            pltpu.roll(t, RET_QK_WIDTH - RET_DK // 2, axis=1),
                            pltpu.roll(t, RET_DK // 2, axis=1))
        return t * cos + partner * sin

    q = rotary(h_ref[:, 0:RET_QK_WIDTH])
    k = rotary(h_ref[:, RET_QK_WIDTH:2 * RET_QK_WIDTH]) * (RET_DK ** -0.5)
    q_in = (q * xi_ref[...]).astype(_BF16)
    k_out = (k * zeta_ref[...]).astype(_BF16)
    q = q.astype(_BF16)
    k = k.astype(_BF16)
    for hd in range(RET_HEADS):
        qk = slice(hd * RET_DK, (hd + 1) * RET_DK)
        vs = slice(hd * RET_DV, (hd + 1) * RET_DV)
        v = h_ref[:, v_off + hd * RET_DV:v_off + (hd + 1) * RET_DV].astype(_BF16)
        s = lax.dot_general(q[:, qk], k[:, qk], (((1,), (1,)), ((), ())),
                            preferred_element_type=_F32) * decay_ref[hd]
        r = jnp.dot(s.astype(_BF16), v, preferred_element_type=_F32)
        r += jnp.dot(q_in[:, qk], state_ref[hd].astype(_BF16), preferred_element_type=_F32)
        kv = lax.dot_general(k_out[:, qk], v, (((0,), (0,)), ((), ())),
                             preferred_element_type=_F32)
        state_ref[hd] = gstate_ref[hd] * state_ref[hd] + kv
        mu = jnp.mean(r, axis=-1, keepdims=True)
        rc = r - mu
        var = jnp.mean(rc * rc, axis=-1, keepdims=True)
        rn = rc * lax.rsqrt(var + LN_EPS) * gn_g_ref[:, vs]
        gate = h_ref[:, g_off + hd * RET_DV:g_off + (hd + 1) * RET_DV]
        mix_ref[:, vs] = (_silu(gate) * rn).astype(_BF16)

    m = jnp.dot(mix_ref[...], w_out_ref[...], preferred_element_type=_F32)
    y = _layer_norm_rows(DEEPNORM_ALPHA * x_ref[...] + m, ln_g_ref[...], ln_b_ref[...])
    o_ref[...] = y
    words_ref[...] = _pack_bf16_pairs(y)


def _l0_mixer_tables(seq):
    half = RET_DK // 2
    inv = ROPE_BASE ** (-jnp.arange(half, dtype=_F32) / half)
    ang = jnp.arange(seq).astype(_F32)[:, None] * inv[None, :]
    cos = jnp.tile(jnp.cos(ang), (1, 2 * RET_HEADS))
    sin = jnp.tile(jnp.concatenate([-jnp.sin(ang), jnp.sin(ang)], axis=1), (1, RET_HEADS))
    log_g = jnp.log(1.0 - 2.0 ** (-5.0 - jnp.arange(RET_HEADS, dtype=_F32)))
    n = jnp.arange(MIX_ROWS, dtype=_F32)
    diff = n[:, None] - n[None, :]
    chunk = jnp.arange(MIX_ROWS) // CHUNK
    expo = jnp.where(chunk[:, None] == chunk[None, :], jnp.abs(diff), diff)
    decay = jnp.where((chunk[:, None] >= chunk[None, :])[None],
                      jnp.exp(expo[None] * log_g[:, None, None]), 0.0)
    xi = jnp.repeat(jnp.exp((n[:, None] + 1.0) * log_g[None, :]), RET_DK, axis=1)
    zeta = jnp.repeat(jnp.exp((MIX_ROWS - 1.0 - n[:, None]) * log_g[None, :]), RET_DK, axis=1)
    g_state = jnp.exp(MIX_ROWS * log_g)
    return cos, sin, xi, zeta, decay, g_state


def _l0_block(x, batch, seq, w_in, gn_g, conv_w, conv_b, cln_g, cln_b, w_out, ln_g, ln_b):
    n, d = x.shape
    steps = seq // MIX_ROWS
    cos, sin, xi, zeta, decay, g_state = _l0_mixer_tables(seq)
    row = lambda b, i: (b * steps + i, 0)
    pos = lambda b, i: (i, 0)
    vec = lambda t: t.reshape(1, -1)
    return pl.pallas_call(
        _l0_block_kernel,
        grid=(batch, steps),
        in_specs=[pl.BlockSpec((MIX_ROWS, d), row), _const_spec((d, L0_IN)),
                  pl.BlockSpec((MIX_ROWS, RET_QK_WIDTH), pos),
                  pl.BlockSpec((MIX_ROWS, RET_QK_WIDTH), pos),
                  _const_spec((MIX_ROWS, RET_QK_WIDTH)), _const_spec((MIX_ROWS, RET_QK_WIDTH)),
                  _const_spec((RET_HEADS, MIX_ROWS, MIX_ROWS)),
                  pl.BlockSpec(memory_space=pltpu.SMEM),
                  _const_spec((1, RET_WIDTH)), _const_spec((CONV_WIDTH, CONV_CH)),
                  _const_spec((1, CONV_CH)), _const_spec((1, CONV_CH)), _const_spec((1, CONV_CH)),
                  _const_spec((RET_WIDTH + CONV_CH, d)), _const_spec((1, d)), _const_spec((1, d))],
        out_specs=[pl.BlockSpec((MIX_ROWS, d), row), pl.BlockSpec((MIX_ROWS, d // 2), row)],
        out_shape=[jax.ShapeDtypeStruct((n, d), _F32), jax.ShapeDtypeStruct((n, d // 2), jnp.uint32)],
        scratch_shapes=[pltpu.VMEM((RET_HEADS, RET_DK, RET_DV), _F32),
                        pltpu.VMEM((MIX_ROWS + CONV_HALO, CONV_CH), _F32),
                        pltpu.VMEM((MIX_ROWS + CONV_HALO, CONV_CH), _F32),
                        pltpu.VMEM((MIX_ROWS, 2 * RET_QK_WIDTH + 2 * RET_WIDTH), _F32),
                        pltpu.VMEM((MIX_ROWS, RET_WIDTH + CONV_CH), _BF16)],
        compiler_params=_params("arbitrary", "arbitrary"),
        name="l0_block",
    )(x, w_in.astype(_BF16), cos, sin, xi, zeta, decay, g_state, vec(gn_g), conv_w, vec(conv_b),
      vec(cln_g), vec(cln_b), w_out.astype(_BF16), vec(ln_g), vec(ln_b))


def _sb_attention_kernel(q_ref, k_ref, v_ref, tri_ref, o_ref, acc_ref, carry_ref):
    blk = SB_BLOCK
    qi = pl.program_id(1)
    t_idx = lax.broadcasted_iota(jnp.int32, (blk, blk), 0)
    s_idx = lax.broadcasted_iota(jnp.int32, (blk, blk), 1)
    keep = s_idx < t_idx

    heads = [slice(hd * SB_HEAD_DIM, (hd + 1) * SB_HEAD_DIM) for hd in range(SB_HEADS)]

    def sweep(blocks, carry):
        starts = [pl.multiple_of(j * blk, blk) for j, _ in blocks]
        zs = [[lax.dot_general(q_ref[:, hs], k_ref[pl.ds(st, blk), hs], (((1,), (1,)), ((), ())),
                               preferred_element_type=_F32) for hs in heads] for st in starts]
        log_betas, addends = [], []
        for (_, mask), z_heads in zip(blocks, zs):
            for z in z_heads:
                softplus = jnp.log2(1.0 + jnp.exp2(-jnp.abs(z)))
                log_beta = jnp.minimum(z, 0.0) - softplus
                log_rest = log_beta - z
                if mask is not None:
                    log_rest = jnp.where(mask, log_rest, 0.0)
                log_betas.append(log_beta)
                addends.append(log_rest.astype(_BF16))
        sums = [jnp.dot(t, tri_ref[...], preferred_element_type=_F32) for t in addends]
        weights, total = [], None
        for b, (_, mask) in enumerate(blocks):
            block_sums = sums[b * SB_HEADS:(b + 1) * SB_HEADS]
            before = carry if total is None else (total if carry is None else carry + total)
            for hd, s in enumerate(block_sums):
                expo = log_betas[b * SB_HEADS + hd] + s[:, :blk]
                if before is not None:
                    expo = expo + before[:, heads[hd]]
                a = jnp.exp2(expo)
                if mask is not None:
                    a = jnp.where(mask, a, 0.0)
                weights.append(a.astype(_BF16))
            block_total = jnp.concatenate([s[:, blk:] for s in block_sums], axis=1)
            total = block_total if total is None else total + block_total
        outs = []
        for hd, hs in enumerate(heads):
            o = None
            for b, st in enumerate(starts):
                t = jnp.dot(weights[b * SB_HEADS + hd], v_ref[pl.ds(st, blk), hs],
                            preferred_element_type=_F32)
                o = t if o is None else o + t
            outs.append(o)
        return jnp.concatenate(outs, axis=1), total

    out, carry = sweep([(qi, keep)], None)
    acc_ref[...] = out
    carry_ref[...] = carry

    def cond(state):
        j, alive = state
        return jnp.logical_and(j >= 0, alive)

    def body(state):
        j, _ = state
        carry = carry_ref[...]
        out, total = sweep([(j, None), (jnp.maximum(j - 1, 0), j >= 1)], carry)
        carry = carry + total
        acc_ref[...] += out
        carry_ref[...] = carry
        return j - 2, jnp.max(carry) > -SB_CUTOFF

    lax.while_loop(cond, body, (qi - 1, jnp.max(carry) > -SB_CUTOFF))
    o_ref[...] = acc_ref[...].astype(o_ref.dtype)


def _sb_attention(qkv, batch, seq):
    n = qkv.shape[0]
    nq = seq // SB_BLOCK
    j = jnp.arange(SB_BLOCK)[:, None]
    s = jnp.arange(SB_BLOCK)[None, :]
    tri = jnp.concatenate([(j > s).astype(_BF16), jnp.ones((SB_BLOCK, SB_BLOCK), _BF16)], axis=1)
    whole_seq = lambda col: pl.BlockSpec((seq, D_MODEL), lambda b, i: (b, col),
                                         pipeline_mode=pl.Buffered(1))
    return pl.pallas_call(
        _sb_attention_kernel,
        grid=(batch, nq),
        in_specs=[pl.BlockSpec((SB_BLOCK, D_MODEL), lambda b, i: (b * nq + i, 0)),
                  whole_seq(1), whole_seq(2), _const_spec((SB_BLOCK, 2 * SB_BLOCK))],
        out_specs=pl.BlockSpec((SB_BLOCK, D_MODEL), lambda b, i: (b * nq + i, 0)),
        out_shape=jax.ShapeDtypeStruct((n, D_MODEL), _BF16),
        scratch_shapes=[pltpu.VMEM((SB_BLOCK, D_MODEL), _F32), pltpu.VMEM((SB_BLOCK, D_MODEL), _F32)],
        compiler_params=_params("arbitrary", "arbitrary"),
        name="sb_attention",
    )(qkv, qkv, qkv, tri)


def _router_kernel(x_ref, w_ref, b_ref, tri_ref, idx_ref, gate_ref, rank_ref, count_ref, base_ref):
    @pl.when(pl.program_id(0) == 0)
    def _():
        base_ref[...] = jnp.zeros_like(base_ref)

    x = x_ref[...]
    w = w_ref[...]
    x_hi = x.astype(_BF16)
    x_lo = (x - x_hi.astype(_F32)).astype(_BF16)
    w_hi = w.astype(_BF16)
    w_lo = (w - w_hi.astype(_F32)).astype(_BF16)
    nt = (((1,), (1,)), ((), ()))
    logits = (lax.dot_general(w_hi, x_hi, nt, preferred_element_type=_F32)
              + lax.dot_general(w_lo, x_hi, nt, preferred_element_type=_F32)
              + lax.dot_general(w_hi, x_lo, nt, preferred_element_type=_F32)) + b_ref[...]
    expert = lax.broadcasted_iota(jnp.int32, logits.shape, 0)
    vals, idxs = [], []
    for _ in range(TOP_K):
        m = jnp.max(logits, axis=0, keepdims=True)
        sel = jnp.min(jnp.where(logits == m, expert, N_EXPERTS), axis=0, keepdims=True)
        vals.append(m)
        idxs.append(sel)
        logits = jnp.where(expert == sel, -jnp.inf, logits)
    vals = jnp.concatenate(vals, axis=0)
    e = jnp.exp(vals - vals[0:1])
    gate_ref[...] = e / jnp.sum(e, axis=0, keepdims=True)
    idx_ref[...] = jnp.concatenate(idxs, axis=0)

    member = jnp.zeros(logits.shape, _F32)
    for sel in idxs:
        member += (expert == sel).astype(_F32)
    before = jnp.dot(member.astype(_BF16), tri_ref[...], preferred_element_type=_F32) + base_ref[...]
    ranks = [jnp.sum(jnp.where(expert == sel, before, 0.0), axis=0, keepdims=True) for sel in idxs]
    rank_ref[...] = jnp.concatenate(ranks, axis=0).astype(jnp.int32)
    base_ref[...] += jnp.sum(member, axis=1, keepdims=True)
    count_ref[...] = base_ref[...]


def _router(x, router_w, router_b):
    n, d = x.shape
    t = jnp.arange(ROUTER_ROWS)
    tri = (t[:, None] < t[None, :]).astype(_BF16)
    tok = pl.BlockSpec((TOP_K, ROUTER_ROWS), lambda i: (0, i))
    return pl.pallas_call(
        _router_kernel,
        grid=(n // ROUTER_ROWS,),
        in_specs=[pl.BlockSpec((ROUTER_ROWS, d), lambda i: (i, 0)),
                  _const_spec((N_EXPERTS, d)), _const_spec((N_EXPERTS, 1)),
                  _const_spec((ROUTER_ROWS, ROUTER_ROWS))],
        out_specs=[tok, tok, tok, _const_spec((N_EXPERTS, 1))],
        out_shape=[jax.ShapeDtypeStruct((TOP_K, n), jnp.int32),
                   jax.ShapeDtypeStruct((TOP_K, n), _F32),
                   jax.ShapeDtypeStruct((TOP_K, n), jnp.int32),
                   jax.ShapeDtypeStruct((N_EXPERTS, 1), _F32)],
        scratch_shapes=[pltpu.VMEM((N_EXPERTS, 1), _F32)],
        compiler_params=_params("arbitrary"),
        name="router",
    )(x, router_w.T, router_b.reshape(N_EXPERTS, 1), tri)


def _prep_up_kernel(w_ref, sel_ref, o_ref):
    group = 2 * PREP_LANES
    for c in range(w_ref.shape[2] // group):
        t = jnp.dot(w_ref[0, :, c * group:(c + 1) * group].astype(_BF16), sel_ref[...],
                    preferred_element_type=_F32)
        o_ref[0, :, c * PREP_LANES:(c + 1) * PREP_LANES] = t[:, :PREP_LANES].astype(_BF16)
        o_ref[0, :, D_FF + c * PREP_LANES:D_FF + (c + 1) * PREP_LANES] = t[:, PREP_LANES:].astype(_BF16)


def _prep_up(w_up):
    e, d, ff2 = w_up.shape
    i = jnp.arange(2 * PREP_LANES)[:, None]
    j = jnp.arange(2 * PREP_LANES)[None, :]
    sel = (i == jnp.where(j < PREP_LANES, 2 * j, 2 * (j - PREP_LANES) + 1)).astype(_BF16)
    blk = lambda x, r: (x, r, 0)
    return pl.pallas_call(
        _prep_up_kernel,
        grid=(e, d // PREP_ROWS),
        in_specs=[pl.BlockSpec((1, PREP_ROWS, ff2), blk), _const_spec((2 * PREP_LANES, 2 * PREP_LANES))],
        out_specs=pl.BlockSpec((1, PREP_ROWS, ff2), blk),
        out_shape=jax.ShapeDtypeStruct((e, d, ff2), _BF16),
        compiler_params=_params("arbitrary", "arbitrary"),
        name="prep_up",
    )(w_up, sel)


def _dispatch(x_words, slot_of, p):
    n, words = x_words.shape
    part = words // DISPATCH_PARTS
    mesh = plsc.VectorSubcoreMesh(core_axis_name="core", subcore_axis_name="subcore",
                                  num_cores=SC_CORES, num_subcores=SC_SUBCORES)
    out = jax.ShapeDtypeStruct((p, part), x_words.dtype)

    @functools.partial(
        pl.kernel, out_type=[out] * DISPATCH_PARTS, mesh=mesh, scratch_types=[],
        compiler_params=pltpu.CompilerParams(use_tc_tiling_on_sc=True), name="moe_dispatch")
    def dispatch(x_hbm, slot_hbm, *o_hbm):
        for c in range(DISPATCH_PARTS):
            def body(x_vmem, slot_vmem, o_ref=o_hbm[c]):
                for k in range(TOP_K):
                    pltpu.sync_copy(x_vmem, o_ref.at[slot_vmem.at[k]])

            pltpu.emit_pipeline(
                body,
                grid=(n // DISPATCH_ROWS,),
                in_specs=[pl.BlockSpec((DISPATCH_ROWS, part), lambda i, c=c: (i, c)),
                          pl.BlockSpec((TOP_K, DISPATCH_ROWS), lambda i: (0, i))],
                out_specs=[],
                core_axis_name=("core", "subcore"),
                dimension_semantics=(pltpu.PARALLEL,),
            )(x_hbm, slot_hbm)

    return dispatch(x_words, slot_of.reshape(TOP_K, n))


def _combine_gather(y_parts, slot_of):
    nk = slot_of.shape[0]
    part = y_parts[0].shape[1]
    mesh = plsc.VectorSubcoreMesh(core_axis_name="core", subcore_axis_name="subcore",
                                  num_cores=SC_CORES, num_subcores=SC_SUBCORES)
    out = jax.ShapeDtypeStruct((nk, part), y_parts[0].dtype)

    @functools.partial(
        pl.kernel, out_type=[out] * DISPATCH_PARTS, mesh=mesh, scratch_types=[],
        compiler_params=pltpu.CompilerParams(use_tc_tiling_on_sc=True), name="moe_combine_gather")
    def gather(*refs):
        y_hbm, slot_hbm, o_hbm = refs[:DISPATCH_PARTS], refs[DISPATCH_PARTS], refs[DISPATCH_PARTS + 1:]
        for c in range(DISPATCH_PARTS):
            def body(slot_vmem, o_vmem, y_ref=y_hbm[c]):
                pltpu.sync_copy(y_ref.at[slot_vmem.at[0]], o_vmem)

            pltpu.emit_pipeline(
                body,
                grid=(nk // DISPATCH_ROWS,),
                in_specs=[pl.BlockSpec((1, DISPATCH_ROWS), lambda i: (0, i))],
                out_specs=[pl.BlockSpec((DISPATCH_ROWS, part), lambda i: (i, 0))],
                core_axis_name=("core", "subcore"),
                dimension_semantics=(pltpu.PARALLEL,),
            )(slot_hbm, o_hbm[c])

    return gather(*y_parts, slot_of.reshape(1, nk))


def _moe_ffn_kernel(block_e_ref, block_rows_ref, xa_ref, xb_ref, wu_ref, bu_ref, wd_f32_ref, bd_ref,
                    oa_ref, ob_ref, wd_ref):
    step = pl.program_id(0)
    rows_used = block_rows_ref[step]
    part = oa_ref.shape[1]
    new_expert = jnp.logical_or(step == 0,
                                block_e_ref[step] != block_e_ref[jnp.maximum(step - 1, 0)])

    @pl.when(jnp.logical_and(new_expert, rows_used > 0))
    def _():
        wd_ref[...] = wd_f32_ref[0].astype(_BF16)

    def ffn_rows(lo, hi):
        rs = slice(lo, hi)
        x = _unpack_bf16_pairs(jnp.concatenate([xa_ref[rs, :], xb_ref[rs, :]], axis=1))
        h = jnp.dot(x, wu_ref[0], preferred_element_type=_F32) + bu_ref[0]
        h_glu = jnp.minimum(h[:, :D_FF], SWIGLU_LIMIT)
        h_lin = jnp.clip(h[:, D_FF:], -SWIGLU_LIMIT, SWIGLU_LIMIT)
        act = h_glu * jax.nn.sigmoid(SWIGLU_ALPHA * h_glu) * (h_lin + 1.0)
        y = jnp.dot(act.astype(_BF16), wd_ref[...], preferred_element_type=_F32) + bd_ref[0]
        words = _pack_bf16_pairs(y)
        oa_ref[rs, :] = words[:, :part]
        ob_ref[rs, :] = words[:, part:]

    def zero_rows(lo, hi):
        oa_ref[lo:hi, :] = jnp.zeros((hi - lo, part), oa_ref.dtype)
        ob_ref[lo:hi, :] = jnp.zeros((hi - lo, part), ob_ref.dtype)

    for lo in range(0, MOE_STEP_ROWS, MOE_ROWS):
        mid, hi = lo + MOE_TAIL_ROWS, lo + MOE_ROWS

        @pl.when(rows_used > mid)
        def _():
            ffn_rows(lo, hi)

        @pl.when(jnp.logical_and(rows_used > lo, rows_used <= mid))
        def _():
            ffn_rows(lo, mid)
            zero_rows(mid, hi)

        @pl.when(rows_used <= lo)
        def _():
            zero_rows(lo, hi)


def _moe_ffn(xs, block_e, block_rows, wu, bu, wd_f32, bd):
    xa, xb = xs
    p, part = xa.shape
    ff, d = wd_f32.shape[1:]
    ff2 = wu.shape[2]
    row = lambda i, be, br: (i, 0)
    exp3 = lambda i, be, br: (be[i], 0, 0)
    rows_spec = pl.BlockSpec((MOE_STEP_ROWS, part), row)
    grid_spec = pltpu.PrefetchScalarGridSpec(
        num_scalar_prefetch=2,
        grid=(p // MOE_STEP_ROWS,),
        in_specs=[rows_spec, rows_spec,
                  pl.BlockSpec((1, d, ff2), exp3), pl.BlockSpec((1, 1, ff2), exp3),
                  pl.BlockSpec((1, ff, d), exp3), pl.BlockSpec((1, 1, d), exp3)],
        out_specs=[rows_spec, rows_spec],
        scratch_shapes=[pltpu.VMEM((ff, d), _BF16)],
    )
    out = jax.ShapeDtypeStruct((p, part), xa.dtype)
    return pl.pallas_call(
        _moe_ffn_kernel,
        grid_spec=grid_spec,
        out_shape=[out, out],
        compiler_params=_params("arbitrary"),
        name="moe_ffn",
    )(block_e, block_rows, xa, xb, wu, bu, wd_f32, bd)


def _combine_ln_kernel(ya_ref, yb_ref, gate_ref, res_ref, g_ref, b_ref, o_ref, obf_ref):
    gate = gate_ref[...]
    moe = None
    for k in range(TOP_K):
        words = jnp.concatenate([ya_ref[k], yb_ref[k]], axis=1)
        term = _unpack_bf16_pairs(words).astype(_F32) * gate[:, k:k + 1]
        moe = term if moe is None else moe + term
    y = _layer_norm_rows(DEEPNORM_ALPHA * res_ref[...] + moe, g_ref[...], b_ref[...])
    o_ref[...] = y
    obf_ref[...] = y.astype(_BF16)


def _combine_ln(y4_parts, gates, res, g, b):
    n, d = res.shape
    ya, yb = (t.reshape(TOP_K, n, t.shape[1]) for t in y4_parts)
    part = ya.shape[2]
    row = lambda i: (i, 0)
    y_spec = pl.BlockSpec((TOP_K, COMBINE_ROWS, part), lambda i: (0, i, 0))
    return pl.pallas_call(
        _combine_ln_kernel,
        grid=(n // COMBINE_ROWS,),
        in_specs=[y_spec, y_spec, pl.BlockSpec((COMBINE_ROWS, TOP_K), row),
                  pl.BlockSpec((COMBINE_ROWS, d), row), _const_spec((1, d)), _const_spec((1, d))],
        out_specs=[pl.BlockSpec((COMBINE_ROWS, d), row), pl.BlockSpec((COMBINE_ROWS, d), row)],
        out_shape=[jax.ShapeDtypeStruct((n, d), _F32), jax.ShapeDtypeStruct((n, d), _BF16)],
        compiler_params=_params("arbitrary"),
        name="combine_ln",
    )(ya, yb, gates, res, g.reshape(1, d), b.reshape(1, d))


def _moe_layer(x, x_words, router_w, router_b, w_up, b_up, w_down, b_down, ln_g, ln_b):
    n, d = x.shape
    nk = n * TOP_K
    idx_t, gate_t, rank_t, count = _router(x, router_w, router_b)
    counts = count[:, 0].astype(jnp.int32)
    padded = (counts + MOE_STEP_ROWS - 1) // MOE_STEP_ROWS * MOE_STEP_ROWS
    padded_ends = jnp.cumsum(padded)
    padded_starts = padded_ends - padded
    n_blocks = -(-(nk + N_EXPERTS * (MOE_STEP_ROWS - 1)) // MOE_STEP_ROWS)
    p = n_blocks * MOE_STEP_ROWS
    experts = jnp.arange(N_EXPERTS, dtype=jnp.int32)
    start_of = jnp.sum(jnp.where(idx_t[None] == experts[:, None, None],
                                 padded_starts[:, None, None], 0), axis=0)
    slot_of = (start_of + rank_t).reshape(nk)
    block_start = jnp.arange(n_blocks, dtype=jnp.int32) * MOE_STEP_ROWS
    block_e = jnp.minimum(jnp.sum(padded_ends[None, :] <= block_start[:, None], axis=1),
                          N_EXPERTS - 1).astype(jnp.int32)
    block_rows = jnp.where(block_start < padded_ends[-1],
                           jnp.clip(counts[block_e] - (block_start - padded_starts[block_e]),
                                    0, MOE_STEP_ROWS), 0).astype(jnp.int32)
    xs = _dispatch(x_words, slot_of, p)
    bu = jnp.concatenate([b_up[:, 0::2], b_up[:, 1::2]], axis=1).reshape(N_EXPERTS, 1, 2 * D_FF)
    ys = _moe_ffn(xs, block_e, block_rows, _prep_up(w_up), bu, w_down,
                  b_down.reshape(N_EXPERTS, 1, d))
    return _combine_ln(_combine_gather(ys, slot_of), gate_t.T, x, ln_g, ln_b)


def kernel(x, l0_w_in, l0_ret_gn_g, l0_conv_w, l0_conv_b, l0_conv_ln_g, l0_conv_ln_b, l0_w_out, l0_ln1_g, l0_ln1_b, l0_router_w, l0_router_b, l0_w_up, l0_b_up, l0_w_down, l0_b_down, l0_ln2_g, l0_ln2_b, l1_w_in, l1_w_out, l1_ln1_g, l1_ln1_b, l1_router_w, l1_router_b, l1_w_up, l1_b_up, l1_w_down, l1_b_down, l1_ln2_g, l1_ln2_b):
    batch, seq, d = x.shape
    n = batch * seq
    x0 = x.reshape(n, d)

    x1, x1_words = _l0_block(x0, batch, seq, l0_w_in, l0_ret_gn_g, l0_conv_w, l0_conv_b,
                             l0_conv_ln_g, l0_conv_ln_b, l0_w_out, l0_ln1_g, l0_ln1_b)
    x2, x2_bf16 = _moe_layer(x1, x1_words, l0_router_w, l0_router_b, l0_w_up, l0_b_up, l0_w_down,
                             l0_b_down, l0_ln2_g, l0_ln2_b)

    q_scale = jnp.where(jnp.arange(3 * d) < d, LOG2_E * SB_HEAD_DIM ** -0.5, 1.0).astype(_F32)
    qkv = _proj(x2_bf16, l1_w_in.astype(_BF16), q_scale)
    att = _sb_attention(qkv, batch, seq)
    x3, x3_words = _proj_res_ln(att, l1_w_out.astype(_BF16), x2, l1_ln1_g, l1_ln1_b)
    x4, _ = _moe_layer(x3, x3_words, l1_router_w, l1_router_b, l1_w_up, l1_b_up, l1_w_down,
                       l1_b_down, l1_ln2_g, l1_ln2_b)
    return x4.reshape(batch, seq, d)
```

```python
import functools

import jax
import jax.numpy as jnp
from jax import lax
from jax.experimental import pallas as pl
from jax.experimental.pallas import tpu as pltpu
from jax.experimental.pallas import tpu_sc as plsc

D_MODEL = 1024
CHUNK = 64
RET_HEADS = 4
RET_DK = 64
RET_DV = 128
RET_QK_WIDTH = RET_HEADS * RET_DK
RET_WIDTH = RET_HEADS * RET_DV
ROPE_BASE = 10000.0
CONV_CH = D_MODEL // 2
CONV_WIDTH = 31
L0_IN = 2 * RET_QK_WIDTH + 2 * RET_WIDTH + 2 * CONV_CH
SB_HEADS = 8
SB_HEAD_DIM = D_MODEL // SB_HEADS
N_EXPERTS = 32
TOP_K = 4
D_FF = D_MODEL
SWIGLU_LIMIT = 7.0
SWIGLU_ALPHA = 1.702
LN_EPS = 1e-5
DEPTH = 2
DEEPNORM_ALPHA = (2 * DEPTH) ** 0.25

VMEM_LIMIT_BYTES = 48 * 1024 * 1024
F32_SUBLANES = 8
PROJ_ROWS = 512
MIX_ROWS = 256
CONV_HALO = 32
ROUTER_ROWS = 512
MOE_ROWS = 512
MOE_TAIL_ROWS = 256
MOE_STEP_ROWS = 1024
PREP_ROWS = 512
PREP_LANES = 128
COMBINE_ROWS = 256
DISPATCH_ROWS = 128
DISPATCH_PARTS = 2
SC_CORES = 2
SC_SUBCORES = 16
SB_BLOCK = 128
SB_CUTOFF = 150.0
LOG2_E = 1.4426950408889634

_F32 = jnp.float32
_BF16 = jnp.bfloat16


def _params(*sem):
    return pltpu.CompilerParams(dimension_semantics=sem, vmem_limit_bytes=VMEM_LIMIT_BYTES)


def _const_spec(shape):
    nd = len(shape)
    return pl.BlockSpec(shape, lambda *_: (0,) * nd)


def _layer_norm_rows(y, g, b):
    mu = jnp.mean(y, axis=-1, keepdims=True)
    yc = y - mu
    var = jnp.mean(yc * yc, axis=-1, keepdims=True)
    return yc * lax.rsqrt(var + LN_EPS) * g + b


def _silu(x):
    return x * jax.nn.sigmoid(x)


def _pack_bf16_pairs(y):
    half = y.shape[1] // 2
    lo = lax.bitcast_convert_type(y[:, :half].astype(_BF16).astype(_F32), jnp.uint32)
    hi = lax.bitcast_convert_type(y[:, half:].astype(_BF16).astype(_F32), jnp.uint32)
    return (lo >> 16) | (hi & jnp.uint32(0xFFFF0000))


def _unpack_bf16_pairs(w):
    lo = lax.bitcast_convert_type(w << 16, _F32)
    hi = lax.bitcast_convert_type(w & jnp.uint32(0xFFFF0000), _F32)
    return jnp.concatenate([lo, hi], axis=1).astype(_BF16)


def _proj_kernel(x_ref, w_ref, scale_ref, o_ref):
    o_ref[...] = (jnp.dot(x_ref[...].astype(_BF16), w_ref[...], preferred_element_type=_F32)
                  * scale_ref[...]).astype(o_ref.dtype)


def _proj(x, w_bf16, col_scale):
    n, d = x.shape
    width = w_bf16.shape[1]
    return pl.pallas_call(
        _proj_kernel,
        grid=(n // PROJ_ROWS,),
        in_specs=[pl.BlockSpec((PROJ_ROWS, d), lambda i: (i, 0)), _const_spec((d, width)),
                  _const_spec((1, width))],
        out_specs=pl.BlockSpec((PROJ_ROWS, width), lambda i: (i, 0)),
        out_shape=jax.ShapeDtypeStruct((n, width), _BF16),
        compiler_params=_params("arbitrary"),
        name="proj",
    )(x, w_bf16, col_scale.reshape(1, width))


def _proj_res_ln_kernel(a_ref, w_ref, res_ref, g_ref, b_ref, o_ref, words_ref):
    m = jnp.dot(a_ref[...], w_ref[...], preferred_element_type=_F32)
    y = _layer_norm_rows(DEEPNORM_ALPHA * res_ref[...] + m, g_ref[...], b_ref[...])
    o_ref[...] = y
    words_ref[...] = _pack_bf16_pairs(y)


def _proj_res_ln(a_bf16, w_bf16, res, g, b):
    n, d = res.shape
    k = a_bf16.shape[1]
    row = lambda i: (i, 0)
    return pl.pallas_call(
        _proj_res_ln_kernel,
        grid=(n // PROJ_ROWS,),
        in_specs=[pl.BlockSpec((PROJ_ROWS, k), row), _const_spec((k, d)),
                  pl.BlockSpec((PROJ_ROWS, d), row), _const_spec((1, d)), _const_spec((1, d))],
        out_specs=[pl.BlockSpec((PROJ_ROWS, d), row), pl.BlockSpec((PROJ_ROWS, d // 2), row)],
        out_shape=[jax.ShapeDtypeStruct((n, d), _F32), jax.ShapeDtypeStruct((n, d // 2), jnp.uint32)],
        compiler_params=_params("arbitrary"),
        name="proj_res_ln",
    )(a_bf16, w_bf16, res, g.reshape(1, d), b.reshape(1, d))


def _l0_block_kernel(x_ref, w_in_ref, cos_ref, sin_ref, xi_ref, zeta_ref, decay_ref, gstate_ref,
                     gn_g_ref, conv_w_ref, conv_b_ref, cln_g_ref, cln_b_ref, w_out_ref, ln_g_ref,
                     ln_b_ref, o_ref, words_ref, state_ref, u_ref, shift_ref, h_ref, mix_ref):
    rows = MIX_ROWS
    step = pl.program_id(1)
    v_off = 2 * RET_QK_WIDTH
    g_off = v_off + RET_WIDTH
    a_off = g_off + RET_WIDTH

    @pl.when(step == 0)
    def _():
        state_ref[...] = jnp.zeros_like(state_ref)
        u_ref[0:CONV_HALO, :] = jnp.zeros((CONV_HALO, CONV_CH), _F32)

    x_bf16 = x_ref[...].astype(_BF16)
    h_conv = jnp.dot(x_bf16, w_in_ref[:, a_off:a_off + 2 * CONV_CH], preferred_element_type=_F32)
    h_ref[...] = jnp.dot(x_bf16, w_in_ref[:, 0:a_off], preferred_element_type=_F32)

    u_ref[CONV_HALO:CONV_HALO + rows, :] = h_conv[:, :CONV_CH] * jax.nn.sigmoid(h_conv[:, CONV_CH:])
    acc = jnp.broadcast_to(conv_b_ref[...], (rows, CONV_CH))
    first = CONV_HALO - (CONV_WIDTH - 1)
    for phase in range(F32_SUBLANES):
        offsets = [first + j for j in range(CONV_WIDTH) if (first + j) % F32_SUBLANES == phase]
        span = max(offsets) - phase + rows
        if phase == 0:
            src = u_ref
        else:
            shift_ref[0:span, :] = u_ref[phase:phase + span, :]
            src = shift_ref
        for off in offsets:
            j = off - first
            acc = acc + conv_w_ref[j:j + 1, :] * src[off - phase:off - phase + rows, :]
    u_ref[0:CONV_HALO, :] = u_ref[rows:rows + CONV_HALO, :]
    c = _silu(_layer_norm_rows(acc, cln_g_ref[...], cln_b_ref[...]))
    mix_ref[:, RET_WIDTH:RET_WIDTH + CONV_CH] = c.astype(_BF16)

    lane = lax.broadcasted_iota(jnp.int32, (rows, RET_QK_WIDTH), 1)
    first_half = (lane % RET_DK) < (RET_DK // 2)
    cos = cos_ref[...]
    sin = sin_ref[...]

    def rotary(t):
        partner = jnp.where(first_half,
                            pltpu.roll(t, RET_QK_WIDTH - RET_DK // 2, axis=1),
                            pltpu.roll(t, RET_DK // 2, axis=1))
        return t * cos + partner * sin

    q = rotary(h_ref[:, 0:RET_QK_WIDTH])
    k = rotary(h_ref[:, RET_QK_WIDTH:2 * RET_QK_WIDTH]) * (RET_DK ** -0.5)
    q_in = (q * xi_ref[...]).astype(_BF16)
    k_out = (k * zeta_ref[...]).astype(_BF16)
    q = q.astype(_BF16)
    k = k.astype(_BF16)
    for hd in range(RET_HEADS):
        qk = slice(hd * RET_DK, (hd + 1) * RET_DK)
        vs = slice(hd * RET_DV, (hd + 1) * RET_DV)
        v = h_ref[:, v_off + hd * RET_DV:v_off + (hd + 1) * RET_DV].astype(_BF16)
        s = lax.dot_general(q[:, qk], k[:, qk], (((1,), (1,)), ((), ())),
                            preferred_element_type=_F32) * decay_ref[hd]
        r = jnp.dot(s.astype(_BF16), v, preferred_element_type=_F32)
        r += jnp.dot(q_in[:, qk], state_ref[hd].astype(_BF16), preferred_element_type=_F32)
        kv = lax.dot_general(k_out[:, qk], v, (((0,), (0,)), ((), ())),
                             preferred_element_type=_F32)
        state_ref[hd] = gstate_ref[hd] * state_ref[hd] + kv
        mu = jnp.mean(r, axis=-1, keepdims=True)
        rc = r - mu
        var = jnp.mean(rc * rc, axis=-1, keepdims=True)
        rn = rc * lax.rsqrt(var + LN_EPS) * gn_g_ref[:, vs]
        gate = h_ref[:, g_off + hd * RET_DV:g_off + (hd + 1) * RET_DV]
        mix_ref[:, vs] = (_silu(gate) * rn).astype(_BF16)

    m = jnp.dot(mix_ref[...], w_out_ref[...], preferred_element_type=_F32)
    y = _layer_norm_rows(DEEPNORM_ALPHA * x_ref[...] + m, ln_g_ref[...], ln_b_ref[...])
    o_ref[...] = y
    words_ref[...] = _pack_bf16_pairs(y)


def _l0_mixer_tables(seq):
    half = RET_DK // 2
    inv = ROPE_BASE ** (-jnp.arange(half, dtype=_F32) / half)
    ang = jnp.arange(seq).astype(_F32)[:, None] * inv[None, :]
    cos = jnp.tile(jnp.cos(ang), (1, 2 * RET_HEADS))
    sin = jnp.tile(jnp.concatenate([-jnp.sin(ang), jnp.sin(ang)], axis=1), (1, RET_HEADS))
    log_g = jnp.log(1.0 - 2.0 ** (-5.0 - jnp.arange(RET_HEADS, dtype=_F32)))
    n = jnp.arange(MIX_ROWS, dtype=_F32)
    diff = n[:, None] - n[None, :]
    chunk = jnp.arange(MIX_ROWS) // CHUNK
    expo = jnp.where(chunk[:, None] == chunk[None, :], jnp.abs(diff), diff)
    decay = jnp.where((chunk[:, None] >= chunk[None, :])[None],
                      jnp.exp(expo[None] * log_g[:, None, None]), 0.0)
    xi = jnp.repeat(jnp.exp((n[:, None] + 1.0) * log_g[None, :]), RET_DK, axis=1)
    zeta = jnp.repeat(jnp.exp((MIX_ROWS - 1.0 - n[:, None]) * log_g[None, :]), RET_DK, axis=1)
    g_state = jnp.exp(MIX_ROWS * log_g)
    return cos, sin, xi, zeta, decay, g_state


def _l0_block(x, batch, seq, w_in, gn_g, conv_w, conv_b, cln_g, cln_b, w_out, ln_g, ln_b):
    n, d = x.shape
    steps = seq // MIX_ROWS
    cos, sin, xi, zeta, decay, g_state = _l0_mixer_tables(seq)
    row = lambda b, i: (b * steps + i, 0)
    pos = lambda b, i: (i, 0)
    vec = lambda t: t.reshape(1, -1)
    return pl.pallas_call(
        _l0_block_kernel,
        grid=(batch, steps),
        in_specs=[pl.BlockSpec((MIX_ROWS, d), row), _const_spec((d, L0_IN)),
                  pl.BlockSpec((MIX_ROWS, RET_QK_WIDTH), pos),
                  pl.BlockSpec((MIX_ROWS, RET_QK_WIDTH), pos),
                  _const_spec((MIX_ROWS, RET_QK_WIDTH)), _const_spec((MIX_ROWS, RET_QK_WIDTH)),
                  _const_spec((RET_HEADS, MIX_ROWS, MIX_ROWS)),
                  pl.BlockSpec(memory_space=pltpu.SMEM),
                  _const_spec((1, RET_WIDTH)), _const_spec((CONV_WIDTH, CONV_CH)),
                  _const_spec((1, CONV_CH)), _const_spec((1, CONV_CH)), _const_spec((1, CONV_CH)),
                  _const_spec((RET_WIDTH + CONV_CH, d)), _const_spec((1, d)), _const_spec((1, d))],
        out_specs=[pl.BlockSpec((MIX_ROWS, d), row), pl.BlockSpec((MIX_ROWS, d // 2), row)],
        out_shape=[jax.ShapeDtypeStruct((n, d), _F32), jax.ShapeDtypeStruct((n, d // 2), jnp.uint32)],
        scratch_shapes=[pltpu.VMEM((RET_HEADS, RET_DK, RET_DV), _F32),
                        pltpu.VMEM((MIX_ROWS + CONV_HALO, CONV_CH), _F32),
                        pltpu.VMEM((MIX_ROWS + CONV_HALO, CONV_CH), _F32),
                        pltpu.VMEM((MIX_ROWS, 2 * RET_QK_WIDTH + 2 * RET_WIDTH), _F32),
                        pltpu.VMEM((MIX_ROWS, RET_WIDTH + CONV_CH), _BF16)],
        compiler_params=_params("arbitrary", "arbitrary"),
        name="l0_block",
    )(x, w_in.astype(_BF16), cos, sin, xi, zeta, decay, g_state, vec(gn_g), conv_w, vec(conv_b),
      vec(cln_g), vec(cln_b), w_out.astype(_BF16), vec(ln_g), vec(ln_b))


def _sb_attention_kernel(q_ref, k_ref, v_ref, tri_ref, o_ref, acc_ref, carry_ref):
    blk = SB_BLOCK
    qi = pl.program_id(1)
    t_idx = lax.broadcasted_iota(jnp.int32, (blk, blk), 0)
    s_idx = lax.broadcasted_iota(jnp.int32, (blk, blk), 1)
    keep = s_idx < t_idx

    heads = [slice(hd * SB_HEAD_DIM, (hd + 1) * SB_HEAD_DIM) for hd in range(SB_HEADS)]

    def sweep(blocks, carry):
        starts = [pl.multiple_of(j * blk, blk) for j, _ in blocks]
        zs = [[lax.dot_general(q_ref[:, hs], k_ref[pl.ds(st, blk), hs], (((1,), (1,)), ((), ())),
                               preferred_element_type=_F32) for hs in heads] for st in starts]
        log_betas, addends = [], []
        for (_, mask), z_heads in zip(blocks, zs):
            for z in z_heads:
                softplus = jnp.log2(1.0 + jnp.exp2(-jnp.abs(z)))
                log_beta = jnp.minimum(z, 0.0) - softplus
                log_rest = log_beta - z
                if mask is not None:
                    log_rest = jnp.where(mask, log_rest, 0.0)
                log_betas.append(log_beta)
                addends.append(log_rest.astype(_BF16))
        sums = [jnp.dot(t, tri_ref[...], preferred_element_type=_F32) for t in addends]
        weights, total = [], None
        for b, (_, mask) in enumerate(blocks):
            block_sums = sums[b * SB_HEADS:(b + 1) * SB_HEADS]
            before = carry if total is None else (total if carry is None else carry + total)
            for hd, s in enumerate(block_sums):
                expo = log_betas[b * SB_HEADS + hd] + s[:, :blk]
                if before is not None:
                    expo = expo + before[:, heads[hd]]
                a = jnp.exp2(expo)
                if mask is not None:
                    a = jnp.where(mask, a, 0.0)
                weights.append(a.astype(_BF16))
            block_total = jnp.concatenate([s[:, blk:] for s in block_sums], axis=1)
            total = block_total if total is None else total + block_total
        outs = []
        for hd, hs in enumerate(heads):
            o = None
            for b, st in enumerate(starts):
                t = jnp.dot(weights[b * SB_HEADS + hd], v_ref[pl.ds(st, blk), hs],
                            preferred_element_type=_F32)
                o = t if o is None else o + t
            outs.append(o)
        return jnp.concatenate(outs, axis=1), total

    out, carry = sweep([(qi, keep)], None)
    acc_ref[...] = out
    carry_ref[...] = carry

    def cond(state):
        j, alive = state
        return jnp.logical_and(j >= 0, alive)

    def body(state):
        j, _ = state
        carry = carry_ref[...]
        out, total = sweep([(j, None), (jnp.maximum(j - 1, 0), j >= 1)], carry)
        carry = carry + total
        acc_ref[...] += out
        carry_ref[...] = carry
        return j - 2, jnp.max(carry) > -SB_CUTOFF

    lax.while_loop(cond, body, (qi - 1, jnp.max(carry) > -SB_CUTOFF))
    o_ref[...] = acc_ref[...].astype(o_ref.dtype)


def _sb_attention(qkv, batch, seq):
    n = qkv.shape[0]
    nq = seq // SB_BLOCK
    j = jnp.arange(SB_BLOCK)[:, None]
    s = jnp.arange(SB_BLOCK)[None, :]
    tri = jnp.concatenate([(j > s).astype(_BF16), jnp.ones((SB_BLOCK, SB_BLOCK), _BF16)], axis=1)
    whole_seq = lambda col: pl.BlockSpec((seq, D_MODEL), lambda b, i: (b, col),
                                         pipeline_mode=pl.Buffered(1))
    return pl.pallas_call(
        _sb_attention_kernel,
        grid=(batch, nq),
        in_specs=[pl.BlockSpec((SB_BLOCK, D_MODEL), lambda b, i: (b * nq + i, 0)),
                  whole_seq(1), whole_seq(2), _const_spec((SB_BLOCK, 2 * SB_BLOCK))],
        out_specs=pl.BlockSpec((SB_BLOCK, D_MODEL), lambda b, i: (b * nq + i, 0)),
        out_shape=jax.ShapeDtypeStruct((n, D_MODEL), _BF16),
        scratch_shapes=[pltpu.VMEM((SB_BLOCK, D_MODEL), _F32), pltpu.VMEM((SB_BLOCK, D_MODEL), _F32)],
        compiler_params=_params("arbitrary", "arbitrary"),
        name="sb_attention",
    )(qkv, qkv, qkv, tri)


def _router_kernel(x_ref, w_ref, b_ref, tri_ref, idx_ref, gate_ref, rank_ref, count_ref, base_ref):
    @pl.when(pl.program_id(0) == 0)
    def _():
        base_ref[...] = jnp.zeros_like(base_ref)

    x = x_ref[...]
    w = w_ref[...]
    x_hi = x.astype(_BF16)
    x_lo = (x - x_hi.astype(_F32)).astype(_BF16)
    w_hi = w.astype(_BF16)
    w_lo = (w - w_hi.astype(_F32)).astype(_BF16)
    nt = (((1,), (1,)), ((), ()))
    logits = (lax.dot_general(w_hi, x_hi, nt, preferred_element_type=_F32)
              + lax.dot_general(w_lo, x_hi, nt, preferred_element_type=_F32)
              + lax.dot_general(w_hi, x_lo, nt, preferred_element_type=_F32)) + b_ref[...]
    expert = lax.broadcasted_iota(jnp.int32, logits.shape, 0)
    vals, idxs = [], []
    for _ in range(TOP_K):
        m = jnp.max(logits, axis=0, keepdims=True)
        sel = jnp.min(jnp.where(logits == m, expert, N_EXPERTS), axis=0, keepdims=True)
        vals.append(m)
        idxs.append(sel)
        logits = jnp.where(expert == sel, -jnp.inf, logits)
    vals = jnp.concatenate(vals, axis=0)
    e = jnp.exp(vals - vals[0:1])
    gate_ref[...] = e / jnp.sum(e, axis=0, keepdims=True)
    idx_ref[...] = jnp.concatenate(idxs, axis=0)

    member = jnp.zeros(logits.shape, _F32)
    for sel in idxs:
        member += (expert == sel).astype(_F32)
    before = jnp.dot(member.astype(_BF16), tri_ref[...], preferred_element_type=_F32) + base_ref[...]
    ranks = [jnp.sum(jnp.where(expert == sel, before, 0.0), axis=0, keepdims=True) for sel in idxs]
    rank_ref[...] = jnp.concatenate(ranks, axis=0).astype(jnp.int32)
    base_ref[...] += jnp.sum(member, axis=1, keepdims=True)
    count_ref[...] = base_ref[...]


def _router(x, router_w, router_b):
    n, d = x.shape
    t = jnp.arange(ROUTER_ROWS)
    tri = (t[:, None] < t[None, :]).astype(_BF16)
    tok = pl.BlockSpec((TOP_K, ROUTER_ROWS), lambda i: (0, i))
    return pl.pallas_call(
        _router_kernel,
        grid=(n // ROUTER_ROWS,),
        in_specs=[pl.BlockSpec((ROUTER_ROWS, d), lambda i: (i, 0)),
                  _const_spec((N_EXPERTS, d)), _const_spec((N_EXPERTS, 1)),
                  _const_spec((ROUTER_ROWS, ROUTER_ROWS))],
        out_specs=[tok, tok, tok, _const_spec((N_EXPERTS, 1))],
        out_shape=[jax.ShapeDtypeStruct((TOP_K, n), jnp.int32),
                   jax.ShapeDtypeStruct((TOP_K, n), _F32),
                   jax.ShapeDtypeStruct((TOP_K, n), jnp.int32),
                   jax.ShapeDtypeStruct((N_EXPERTS, 1), _F32)],
        scratch_shapes=[pltpu.VMEM((N_EXPERTS, 1), _F32)],
        compiler_params=_params("arbitrary"),
        name="router",
    )(x, router_w.T, router_b.reshape(N_EXPERTS, 1), tri)


def _prep_up_kernel(w_ref, sel_ref, o_ref):
    group = 2 * PREP_LANES
    for c in range(w_ref.shape[2] // group):
        t = jnp.dot(w_ref[0, :, c * group:(c + 1) * group].astype(_BF16), sel_ref[...],
                    preferred_element_type=_F32)
        o_ref[0, :, c * PREP_LANES:(c + 1) * PREP_LANES] = t[:, :PREP_LANES].astype(_BF16)
        o_ref[0, :, D_FF + c * PREP_LANES:D_FF + (c + 1) * PREP_LANES] = t[:, PREP_LANES:].astype(_BF16)


def _prep_up(w_up):
    e, d, ff2 = w_up.shape
    i = jnp.arange(2 * PREP_LANES)[:, None]
    j = jnp.arange(2 * PREP_LANES)[None, :]
    sel = (i == jnp.where(j < PREP_LANES, 2 * j, 2 * (j - PREP_LANES) + 1)).astype(_BF16)
    blk = lambda x, r: (x, r, 0)
    return pl.pallas_call(
        _prep_up_kernel,
        grid=(e, d // PREP_ROWS),
        in_specs=[pl.BlockSpec((1, PREP_ROWS, ff2), blk), _const_spec((2 * PREP_LANES, 2 * PREP_LANES))],
        out_specs=pl.BlockSpec((1, PREP_ROWS, ff2), blk),
        out_shape=jax.ShapeDtypeStruct((e, d, ff2), _BF16),
        compiler_params=_params("arbitrary", "arbitrary"),
        name="prep_up",
    )(w_up, sel)


def _dispatch(x_words, slot_of, p):
    n, words = x_words.shape
    part = words // DISPATCH_PARTS
    mesh = plsc.VectorSubcoreMesh(core_axis_name="core", subcore_axis_name="subcore",
                                  num_cores=SC_CORES, num_subcores=SC_SUBCORES)
    out = jax.ShapeDtypeStruct((p, part), x_words.dtype)

    @functools.partial(
        pl.kernel, out_type=[out] * DISPATCH_PARTS, mesh=mesh, scratch_types=[],
        compiler_params=pltpu.CompilerParams(use_tc_tiling_on_sc=True), name="moe_dispatch")
    def dispatch(x_hbm, slot_hbm, *o_hbm):
        for c in range(DISPATCH_PARTS):
            def body(x_vmem, slot_vmem, o_ref=o_hbm[c]):
                for k in range(TOP_K):
                    pltpu.sync_copy(x_vmem, o_ref.at[slot_vmem.at[k]])

            pltpu.emit_pipeline(
                body,
                grid=(n // DISPATCH_ROWS,),
                in_specs=[pl.BlockSpec((DISPATCH_ROWS, part), lambda i, c=c: (i, c)),
                          pl.BlockSpec((TOP_K, DISPATCH_ROWS), lambda i: (0, i))],
                out_specs=[],
                core_axis_name=("core", "subcore"),
                dimension_semantics=(pltpu.PARALLEL,),
            )(x_hbm, slot_hbm)

    return dispatch(x_words, slot_of.reshape(TOP_K, n))


def _combine_gather(y_parts, slot_of):
    nk = slot_of.shape[0]
    part = y_parts[0].shape[1]
    mesh = plsc.VectorSubcoreMesh(core_axis_name="core", subcore_axis_name="subcore",
                                  num_cores=SC_CORES, num_subcores=SC_SUBCORES)
    out = jax.ShapeDtypeStruct((nk, part), y_parts[0].dtype)

    @functools.partial(
        pl.kernel, out_type=[out] * DISPATCH_PARTS, mesh=mesh, scratch_types=[],
        compiler_params=pltpu.CompilerParams(use_tc_tiling_on_sc=True), name="moe_combine_gather")
    def gather(*refs):
        y_hbm, slot_hbm, o_hbm = refs[:DISPATCH_PARTS], refs[DISPATCH_PARTS], refs[DISPATCH_PARTS + 1:]
        for c in range(DISPATCH_PARTS):
            def body(slot_vmem, o_vmem, y_ref=y_hbm[c]):
                pltpu.sync_copy(y_ref.at[slot_vmem.at[0]], o_vmem)

            pltpu.emit_pipeline(
                body,
                grid=(nk // DISPATCH_ROWS,),
                in_specs=[pl.BlockSpec((1, DISPATCH_ROWS), lambda i: (0, i))],
                out_specs=[pl.BlockSpec((DISPATCH_ROWS, part), lambda i: (i, 0))],
                core_axis_name=("core", "subcore"),
                dimension_semantics=(pltpu.PARALLEL,),
            )(slot_hbm, o_hbm[c])

    return gather(*y_parts, slot_of.reshape(1, nk))


def _moe_ffn_kernel(block_e_ref, block_rows_ref, last_block_ref, xa_ref, xb_ref, wu_ref, bu_ref,
                    wd_f32_ref, bd_ref, oa_ref, ob_ref, wd_ref):
    step = pl.program_id(0)
    rows_used = block_rows_ref[step]
    part = oa_ref.shape[1]
    new_expert = jnp.logical_or(step == 0,
                                block_e_ref[step] != block_e_ref[jnp.maximum(step - 1, 0)])

    @pl.when(jnp.logical_and(new_expert, rows_used > 0))
    def _():
        wd_ref[...] = wd_f32_ref[0].astype(_BF16)

    def ffn_rows(lo, hi):
        rs = slice(lo, hi)
        x = _unpack_bf16_pairs(jnp.concatenate([xa_ref[rs, :], xb_ref[rs, :]], axis=1))
        h = jnp.dot(x, wu_ref[0], preferred_element_type=_F32) + bu_ref[0]
        h_glu = jnp.minimum(h[:, :D_FF], SWIGLU_LIMIT)
        h_lin = jnp.clip(h[:, D_FF:], -SWIGLU_LIMIT, SWIGLU_LIMIT)
        act = h_glu * jax.nn.sigmoid(SWIGLU_ALPHA * h_glu) * (h_lin + 1.0)
        y = jnp.dot(act.astype(_BF16), wd_ref[...], preferred_element_type=_F32) + bd_ref[0]
        words = _pack_bf16_pairs(y)
        oa_ref[rs, :] = words[:, :part]
        ob_ref[rs, :] = words[:, part:]

    def zero_rows(lo, hi):
        oa_ref[lo:hi, :] = jnp.zeros((hi - lo, part), oa_ref.dtype)
        ob_ref[lo:hi, :] = jnp.zeros((hi - lo, part), ob_ref.dtype)

    for lo in range(0, MOE_STEP_ROWS, MOE_ROWS):
        mid, hi = lo + MOE_TAIL_ROWS, lo + MOE_ROWS

        @pl.when(rows_used > mid)
        def _():
            ffn_rows(lo, hi)

        @pl.when(jnp.logical_and(rows_used > lo, rows_used <= mid))
        def _():
            ffn_rows(lo, mid)
            zero_rows(mid, hi)

        @pl.when(rows_used <= lo)
        def _():
            zero_rows(lo, hi)


def _moe_ffn(xs, block_e, block_rows, last_block, wu, bu, wd_f32, bd):
    xa, xb = xs
    p, part = xa.shape
    ff, d = wd_f32.shape[1:]
    ff2 = wu.shape[2]
    row = lambda i, be, br, lb: (jnp.minimum(i, lb[0]), 0)
    exp3 = lambda i, be, br, lb: (be[i], 0, 0)
    rows_spec = pl.BlockSpec((MOE_STEP_ROWS, part), row)
    grid_spec = pltpu.PrefetchScalarGridSpec(
        num_scalar_prefetch=3,
        grid=(p // MOE_STEP_ROWS,),
        in_specs=[rows_spec, rows_spec,
                  pl.BlockSpec((1, d, ff2), exp3), pl.BlockSpec((1, 1, ff2), exp3),
                  pl.BlockSpec((1, ff, d), exp3), pl.BlockSpec((1, 1, d), exp3)],
        out_specs=[rows_spec, rows_spec],
        scratch_shapes=[pltpu.VMEM((ff, d), _BF16)],
    )
    out = jax.ShapeDtypeStruct((p, part), xa.dtype)
    return pl.pallas_call(
        _moe_ffn_kernel,
        grid_spec=grid_spec,
        out_shape=[out, out],
        compiler_params=_params("arbitrary"),
        name="moe_ffn",
    )(block_e, block_rows, last_block, xa, xb, wu, bu, wd_f32, bd)


def _combine_ln_kernel(ya_ref, yb_ref, gate_ref, res_ref, g_ref, b_ref, o_ref, obf_ref):
    gate = gate_ref[...]
    moe = None
    for k in range(TOP_K):
        words = jnp.concatenate([ya_ref[k], yb_ref[k]], axis=1)
        term = _unpack_bf16_pairs(words).astype(_F32) * gate[:, k:k + 1]
        moe = term if moe is None else moe + term
    y = _layer_norm_rows(DEEPNORM_ALPHA * res_ref[...] + moe, g_ref[...], b_ref[...])
    o_ref[...] = y
    obf_ref[...] = y.astype(_BF16)


def _combine_ln(y4_parts, gates, res, g, b):
    n, d = res.shape
    ya, yb = (t.reshape(TOP_K, n, t.shape[1]) for t in y4_parts)
    part = ya.shape[2]
    row = lambda i: (i, 0)
    y_spec = pl.BlockSpec((TOP_K, COMBINE_ROWS, part), lambda i: (0, i, 0))
    return pl.pallas_call(
        _combine_ln_kernel,
        grid=(n // COMBINE_ROWS,),
        in_specs=[y_spec, y_spec, pl.BlockSpec((COMBINE_ROWS, TOP_K), row),
                  pl.BlockSpec((COMBINE_ROWS, d), row), _const_spec((1, d)), _const_spec((1, d))],
        out_specs=[pl.BlockSpec((COMBINE_ROWS, d), row), pl.BlockSpec((COMBINE_ROWS, d), row)],
        out_shape=[jax.ShapeDtypeStruct((n, d), _F32), jax.ShapeDtypeStruct((n, d), _BF16)],
        compiler_params=_params("arbitrary"),
        name="combine_ln",
    )(ya, yb, gates, res, g.reshape(1, d), b.reshape(1, d))


def _moe_layer(x, x_words, router_w, router_b, w_up, b_up, w_down, b_down, ln_g, ln_b):
    n, d = x.shape
    nk = n * TOP_K
    idx_t, gate_t, rank_t, count = _router(x, router_w, router_b)
    counts = count[:, 0].astype(jnp.int32)
    padded = (counts + MOE_STEP_ROWS - 1) // MOE_STEP_ROWS * MOE_STEP_ROWS
    padded_ends = jnp.cumsum(padded)
    padded_starts = padded_ends - padded
    n_blocks = -(-(nk + N_EXPERTS * (MOE_STEP_ROWS - 1)) // MOE_STEP_ROWS)
    p = n_blocks * MOE_STEP_ROWS
    short = counts % MOE_STEP_ROWS
    experts = jnp.arange(N_EXPERTS, dtype=jnp.int32)
    chosen = idx_t[None] == experts[:, None, None]
    start_of = jnp.sum(jnp.where(chosen, padded_starts[:, None, None], 0), axis=0)
    short_of = jnp.sum(jnp.where(chosen, short[:, None, None], 0), axis=0)
    gap_of = jnp.where(jnp.logical_and(short_of > 0, rank_t >= short_of), MOE_STEP_ROWS - short_of, 0)
    slot_of = (start_of + rank_t + gap_of).reshape(nk)
    block_start = jnp.arange(n_blocks, dtype=jnp.int32) * MOE_STEP_ROWS
    block_e = jnp.minimum(jnp.sum(padded_ends[None, :] <= block_start[:, None], axis=1),
                          N_EXPERTS - 1).astype(jnp.int32)
    first_of_expert = block_start == padded_starts[block_e]
    block_rows = jnp.where(block_start >= padded_ends[-1], 0,
                           jnp.where(jnp.logical_and(first_of_expert, short[block_e] > 0),
                                     short[block_e], MOE_STEP_ROWS)).astype(jnp.int32)
    last_block = jnp.minimum(padded_ends[-1] // MOE_STEP_ROWS, n_blocks - 1).astype(jnp.int32)
    xs = _dispatch(x_words, slot_of, p)
    bu = jnp.concatenate([b_up[:, 0::2], b_up[:, 1::2]], axis=1).reshape(N_EXPERTS, 1, 2 * D_FF)
    ys = _moe_ffn(xs, block_e, block_rows, last_block.reshape(1), _prep_up(w_up), bu, w_down,
                  b_down.reshape(N_EXPERTS, 1, d))
    return _combine_ln(_combine_gather(ys, slot_of), gate_t.T, x, ln_g, ln_b)


def kernel(x, l0_w_in, l0_ret_gn_g, l0_conv_w, l0_conv_b, l0_conv_ln_g, l0_conv_ln_b, l0_w_out, l0_ln1_g, l0_ln1_b, l0_router_w, l0_router_b, l0_w_up, l0_b_up, l0_w_down, l0_b_down, l0_ln2_g, l0_ln2_b, l1_w_in, l1_w_out, l1_ln1_g, l1_ln1_b, l1_router_w, l1_router_b, l1_w_up, l1_b_up, l1_w_down, l1_b_down, l1_ln2_g, l1_ln2_b):
    batch, seq, d = x.shape
    n = batch * seq
    x0 = x.reshape(n, d)

    x1, x1_words = _l0_block(x0, batch, seq, l0_w_in, l0_ret_gn_g, l0_conv_w, l0_conv_b,
                             l0_conv_ln_g, l0_conv_ln_b, l0_w_out, l0_ln1_g, l0_ln1_b)
    x2, x2_bf16 = _moe_layer(x1, x1_words, l0_router_w, l0_router_b, l0_w_up, l0_b_up, l0_w_down,
                             l0_b_down, l0_ln2_g, l0_ln2_b)

    q_scale = jnp.where(jnp.arange(3 * d) < d, LOG2_E * SB_HEAD_DIM ** -0.5, 1.0).astype(_F32)
    qkv = _proj(x2_bf16, l1_w_in.astype(_BF16), q_scale)
    att = _sb_attention(qkv, batch, seq)
    x3, x3_words = _proj_res_ln(att, l1_w_out.astype(_BF16), x2, l1_ln1_g, l1_ln1_b)
    x4, _ = _moe_layer(x3, x3_words, l1_router_w, l1_router_b, l1_w_up, l1_b_up, l1_w_down,
                       l1_b_down, l1_ln2_g, l1_ln2_b)
    return x4.reshape(batch, seq, d)
```

```python
import functools

import jax
import jax.numpy as jnp
from jax import lax
from jax.experimental import pallas as pl
from jax.experimental.pallas import tpu as pltpu
from jax.experimental.pallas import tpu_sc as plsc

D_MODEL = 1024
CHUNK = 64
RET_HEADS = 4
RET_DK = 64
RET_DV = 128
RET_QK_WIDTH = RET_HEADS * RET_DK
RET_WIDTH = RET_HEADS * RET_DV
ROPE_BASE = 10000.0
CONV_CH = D_MODEL // 2
CONV_WIDTH = 31
L0_IN = 2 * RET_QK_WIDTH + 2 * RET_WIDTH + 2 * CONV_CH
SB_HEADS = 8
SB_HEAD_DIM = D_MODEL // SB_HEADS
N_EXPERTS = 32
TOP_K = 4
D_FF = D_MODEL
SWIGLU_LIMIT = 7.0
SWIGLU_ALPHA = 1.702
LN_EPS = 1e-5
DEPTH = 2
DEEPNORM_ALPHA = (2 * DEPTH) ** 0.25

VMEM_LIMIT_BYTES = 48 * 1024 * 1024
F32_SUBLANES = 8
PROJ_ROWS = 512
MIX_ROWS = 256
CONV_HALO = 32
ROUTER_ROWS = 512
MOE_ROWS = 512
MOE_TAIL_ROWS = 256
MOE_STEP_ROWS = 1024
PREP_ROWS = 512
PREP_LANES = 128
COMBINE_ROWS = 256
DISPATCH_ROWS = 128
DISPATCH_PARTS = 2
SC_CORES = 2
SC_SUBCORES = 16
SB_BLOCK = 128
SB_CUTOFF = 150.0
LOG2_E = 1.4426950408889634

_F32 = jnp.float32
_BF16 = jnp.bfloat16


def _params(*sem):
    return pltpu.CompilerParams(dimension_semantics=sem, vmem_limit_bytes=VMEM_LIMIT_BYTES)


def _const_spec(shape):
    nd = len(shape)
    return pl.BlockSpec(shape, lambda *_: (0,) * nd)


def _layer_norm_rows(y, g, b):
    mu = jnp.mean(y, axis=-1, keepdims=True)
    yc = y - mu
    var = jnp.mean(yc * yc, axis=-1, keepdims=True)
    return yc * lax.rsqrt(var + LN_EPS) * g + b


def _silu(x):
    return x * jax.nn.sigmoid(x)


def _pack_bf16_pairs(y):
    half = y.shape[1] // 2
    lo = lax.bitcast_convert_type(y[:, :half].astype(_BF16).astype(_F32), jnp.uint32)
    hi = lax.bitcast_convert_type(y[:, half:].astype(_BF16).astype(_F32), jnp.uint32)
    return (lo >> 16) | (hi & jnp.uint32(0xFFFF0000))


def _unpack_bf16_pairs(w):
    lo = lax.bitcast_convert_type(w << 16, _F32)
    hi = lax.bitcast_convert_type(w & jnp.uint32(0xFFFF0000), _F32)
    return jnp.concatenate([lo, hi], axis=1).astype(_BF16)


def _proj_kernel(x_ref, w_ref, scale_ref, o_ref):
    o_ref[...] = (jnp.dot(x_ref[...].astype(_BF16), w_ref[...], preferred_element_type=_F32)
                  * scale_ref[...]).astype(o_ref.dtype)


def _proj(x, w_bf16, col_scale):
    n, d = x.shape
    width = w_bf16.shape[1]
    return pl.pallas_call(
        _proj_kernel,
        grid=(n // PROJ_ROWS,),
        in_specs=[pl.BlockSpec((PROJ_ROWS, d), lambda i: (i, 0)), _const_spec((d, width)),
                  _const_spec((1, width))],
        out_specs=pl.BlockSpec((PROJ_ROWS, width), lambda i: (i, 0)),
        out_shape=jax.ShapeDtypeStruct((n, width), _BF16),
        compiler_params=_params("arbitrary"),
        name="proj",
    )(x, w_bf16, col_scale.reshape(1, width))


def _proj_res_ln_kernel(a_ref, w_ref, res_ref, g_ref, b_ref, o_ref, words_ref):
    m = jnp.dot(a_ref[...], w_ref[...], preferred_element_type=_F32)
    y = _layer_norm_rows(DEEPNORM_ALPHA * res_ref[...] + m, g_ref[...], b_ref[...])
    o_ref[...] = y
    words_ref[...] = _pack_bf16_pairs(y)


def _proj_res_ln(a_bf16, w_bf16, res, g, b):
    n, d = res.shape
    k = a_bf16.shape[1]
    row = lambda i: (i, 0)
    return pl.pallas_call(
        _proj_res_ln_kernel,
        grid=(n // PROJ_ROWS,),
        in_specs=[pl.BlockSpec((PROJ_ROWS, k), row), _const_spec((k, d)),
                  pl.BlockSpec((PROJ_ROWS, d), row), _const_spec((1, d)), _const_spec((1, d))],
        out_specs=[pl.BlockSpec((PROJ_ROWS, d), row), pl.BlockSpec((PROJ_ROWS, d // 2), row)],
        out_shape=[jax.ShapeDtypeStruct((n, d), _F32), jax.ShapeDtypeStruct((n, d // 2), jnp.uint32)],
        compiler_params=_params("arbitrary"),
        name="proj_res_ln",
    )(a_bf16, w_bf16, res, g.reshape(1, d), b.reshape(1, d))


def _l0_block_kernel(x_ref, w_in_ref, cos_ref, sin_ref, xi_ref, zeta_ref, decay_ref, gstate_ref,
                     gn_g_ref, conv_w_ref, conv_b_ref, cln_g_ref, cln_b_ref, w_out_ref, ln_g_ref,
                     ln_b_ref, o_ref, words_ref, state_ref, u_ref, shift_ref, h_ref, mix_ref):
    rows = MIX_ROWS
    step = pl.program_id(1)
    v_off = 2 * RET_QK_WIDTH
    g_off = v_off + RET_WIDTH
    a_off = g_off + RET_WIDTH

    @pl.when(step == 0)
    def _():
        state_ref[...] = jnp.zeros_like(state_ref)
        u_ref[0:CONV_HALO, :] = jnp.zeros((CONV_HALO, CONV_CH), _F32)

    x_bf16 = x_ref[...].astype(_BF16)
    h_conv = jnp.dot(x_bf16, w_in_ref[:, a_off:a_off + 2 * CONV_CH], preferred_element_type=_F32)
    h_ref[...] = jnp.dot(x_bf16, w_in_ref[:, 0:a_off], preferred_element_type=_F32)

    u_ref[CONV_HALO:CONV_HALO + rows, :] = h_conv[:, :CONV_CH] * jax.nn.sigmoid(h_conv[:, CONV_CH:])
    acc = jnp.broadcast_to(conv_b_ref[...], (rows, CONV_CH))
    first = CONV_HALO - (CONV_WIDTH - 1)
    for phase in range(F32_SUBLANES):
        offsets = [first + j for j in range(CONV_WIDTH) if (first + j) % F32_SUBLANES == phase]
        span = max(offsets) - phase + rows
        if phase == 0:
            src = u_ref
        else:
            shift_ref[0:span, :] = u_ref[phase:phase + span, :]
            src = shift_ref
        for off in offsets:
            j = off - first
            acc = acc + conv_w_ref[j:j + 1, :] * src[off - phase:off - phase + rows, :]
    u_ref[0:CONV_HALO, :] = u_ref[rows:rows + CONV_HALO, :]
    c = _silu(_layer_norm_rows(acc, cln_g_ref[...], cln_b_ref[...]))
    mix_ref[:, RET_WIDTH:RET_WIDTH + CONV_CH] = c.astype(_BF16)

    lane = lax.broadcasted_iota(jnp.int32, (rows, RET_QK_WIDTH), 1)
    first_half = (lane % RET_DK) < (RET_DK // 2)
    cos = cos_ref[...]
    sin = sin_ref[...]

    def rotary(t):
        partner = jnp.where(first_half,
                            pltpu.roll(t, RET_QK_WIDTH - RET_DK // 2, axis=1),
                            pltpu.roll(t, RET_DK // 2, axis=1))
        return t * cos + partner * sin

    q = rotary(h_ref[:, 0:RET_QK_WIDTH])
    k = rotary(h_ref[:, RET_QK_WIDTH:2 * RET_QK_WIDTH]) * (RET_DK ** -0.5)
    q_in = (q * xi_ref[...]).astype(_BF16)
    k_out = (k * zeta_ref[...]).astype(_BF16)
    q = q.astype(_BF16)
    k = k.astype(_BF16)
    for hd in range(RET_HEADS):
        qk = slice(hd * RET_DK, (hd + 1) * RET_DK)
        vs = slice(hd * RET_DV, (hd + 1) * RET_DV)
        v = h_ref[:, v_off + hd * RET_DV:v_off + (hd + 1) * RET_DV].astype(_BF16)
        s = lax.dot_general(q[:, qk], k[:, qk], (((1,), (1,)), ((), ())),
                            preferred_element_type=_F32) * decay_ref[hd]
        r = jnp.dot(s.astype(_BF16), v, preferred_element_type=_F32)
        r += jnp.dot(q_in[:, qk], state_ref[hd].astype(_BF16), preferred_element_type=_F32)
        kv = lax.dot_general(k_out[:, qk], v, (((0,), (0,)), ((), ())),
                             preferred_element_type=_F32)
        state_ref[hd] = gstate_ref[hd] * state_ref[hd] + kv
        mu = jnp.mean(r, axis=-1, keepdims=True)
        rc = r - mu
        var = jnp.mean(rc * rc, axis=-1, keepdims=True)
        rn = rc * lax.rsqrt(var + LN_EPS) * gn_g_ref[:, vs]
        gate = h_ref[:, g_off + hd * RET_DV:g_off + (hd + 1) * RET_DV]
        mix_ref[:, vs] = (_silu(gate) * rn).astype(_BF16)

    m = jnp.dot(mix_ref[...], w_out_ref[...], preferred_element_type=_F32)
    y = _layer_norm_rows(DEEPNORM_ALPHA * x_ref[...] + m, ln_g_ref[...], ln_b_ref[...])
    o_ref[...] = y
    words_ref[...] = _pack_bf16_pairs(y)


def _l0_mixer_tables(seq):
    half = RET_DK // 2
    inv = ROPE_BASE ** (-jnp.arange(half, dtype=_F32) / half)
    ang = jnp.arange(seq).astype(_F32)[:, None] * inv[None, :]
    cos = jnp.tile(jnp.cos(ang), (1, 2 * RET_HEADS))
    sin = jnp.tile(jnp.concatenate([-jnp.sin(ang), jnp.sin(ang)], axis=1), (1, RET_HEADS))
    log_g = jnp.log(1.0 - 2.0 ** (-5.0 - jnp.arange(RET_HEADS, dtype=_F32)))
    n = jnp.arange(MIX_ROWS, dtype=_F32)
    diff = n[:, None] - n[None, :]
    chunk = jnp.arange(MIX_ROWS) // CHUNK
    expo = jnp.where(chunk[:, None] == chunk[None, :], jnp.abs(diff), diff)
    decay = jnp.where((chunk[:, None] >= chunk[None, :])[None],
                      jnp.exp(expo[None] * log_g[:, None, None]), 0.0)
    xi = jnp.repeat(jnp.exp((n[:, None] + 1.0) * log_g[None, :]), RET_DK, axis=1)
    zeta = jnp.repeat(jnp.exp((MIX_ROWS - 1.0 - n[:, None]) * log_g[None, :]), RET_DK, axis=1)
    g_state = jnp.exp(MIX_ROWS * log_g)
    return cos, sin, xi, zeta, decay, g_state


def _l0_block(x, batch, seq, w_in, gn_g, conv_w, conv_b, cln_g, cln_b, w_out, ln_g, ln_b):
    n, d = x.shape
    steps = seq // MIX_ROWS
    cos, sin, xi, zeta, decay, g_state = _l0_mixer_tables(seq)
    row = lambda b, i: (b * steps + i, 0)
    pos = lambda b, i: (i, 0)
    vec = lambda t: t.reshape(1, -1)
    return pl.pallas_call(
        _l0_block_kernel,
        grid=(batch, steps),
        in_specs=[pl.BlockSpec((MIX_ROWS, d), row), _const_spec((d, L0_IN)),
                  pl.BlockSpec((MIX_ROWS, RET_QK_WIDTH), pos),
                  pl.BlockSpec((MIX_ROWS, RET_QK_WIDTH), pos),
                  _const_spec((MIX_ROWS, RET_QK_WIDTH)), _const_spec((MIX_ROWS, RET_QK_WIDTH)),
                  _const_spec((RET_HEADS, MIX_ROWS, MIX_ROWS)),
                  pl.BlockSpec(memory_space=pltpu.SMEM),
                  _const_spec((1, RET_WIDTH)), _const_spec((CONV_WIDTH, CONV_CH)),
                  _const_spec((1, CONV_CH)), _const_spec((1, CONV_CH)), _const_spec((1, CONV_CH)),
                  _const_spec((RET_WIDTH + CONV_CH, d)), _const_spec((1, d)), _const_spec((1, d))],
        out_specs=[pl.BlockSpec((MIX_ROWS, d), row), pl.BlockSpec((MIX_ROWS, d // 2), row)],
        out_shape=[jax.ShapeDtypeStruct((n, d), _F32), jax.ShapeDtypeStruct((n, d // 2), jnp.uint32)],
        scratch_shapes=[pltpu.VMEM((RET_HEADS, RET_DK, RET_DV), _F32),
                        pltpu.VMEM((MIX_ROWS + CONV_HALO, CONV_CH), _F32),
                        pltpu.VMEM((MIX_ROWS + CONV_HALO, CONV_CH), _F32),
                        pltpu.VMEM((MIX_ROWS, 2 * RET_QK_WIDTH + 2 * RET_WIDTH), _F32),
                        pltpu.VMEM((MIX_ROWS, RET_WIDTH + CONV_CH), _BF16)],
        compiler_params=_params("arbitrary", "arbitrary"),
        name="l0_block",
    )(x, w_in.astype(_BF16), cos, sin, xi, zeta, decay, g_state, vec(gn_g), conv_w, vec(conv_b),
      vec(cln_g), vec(cln_b), w_out.astype(_BF16), vec(ln_g), vec(ln_b))


def _sb_attention_kernel(q_ref, k_ref, v_ref, tri_ref, o_ref, acc_ref, carry_ref):
    blk = SB_BLOCK
    qi = pl.program_id(1)
    t_idx = lax.broadcasted_iota(jnp.int32, (blk, blk), 0)
    s_idx = lax.broadcasted_iota(jnp.int32, (blk, blk), 1)
    keep = s_idx < t_idx

    heads = [slice(hd * SB_HEAD_DIM, (hd + 1) * SB_HEAD_DIM) for hd in range(SB_HEADS)]

    def sweep(blocks, carry):
        starts = [pl.multiple_of(j * blk, blk) for j, _ in blocks]
        zs = [[lax.dot_general(q_ref[:, hs], k_ref[pl.ds(st, blk), hs], (((1,), (1,)), ((), ())),
                               preferred_element_type=_F32) for hs in heads] for st in starts]
        log_betas, addends = [], []
        for (_, mask), z_heads in zip(blocks, zs):
            for z in z_heads:
                softplus = jnp.log2(1.0 + jnp.exp2(-jnp.abs(z)))
                log_beta = jnp.minimum(z, 0.0) - softplus
                log_rest = log_beta - z
                if mask is not None:
                    log_rest = jnp.where(mask, log_rest, 0.0)
                log_betas.append(log_beta)
                addends.append(log_rest.astype(_BF16))
        sums = [jnp.dot(t, tri_ref[...], preferred_element_type=_F32) for t in addends]
        weights, total = [], None
        for b, (_, mask) in enumerate(blocks):
            block_sums = sums[b * SB_HEADS:(b + 1) * SB_HEADS]
            before = carry if total is None else (total if carry is None else carry + total)
            for hd, s in enumerate(block_sums):
                expo = log_betas[b * SB_HEADS + hd] + s[:, :blk]
                if before is not None:
                    expo = expo + before[:, heads[hd]]
                a = jnp.exp2(expo)
                if mask is not None:
                    a = jnp.where(mask, a, 0.0)
                weights.append(a.astype(_BF16))
            block_total = jnp.concatenate([s[:, blk:] for s in block_sums], axis=1)
            total = block_total if total is None else total + block_total
        outs = []
        for hd, hs in enumerate(heads):
            o = None
            for b, st in enumerate(starts):
                t = jnp.dot(weights[b * SB_HEADS + hd], v_ref[pl.ds(st, blk), hs],
                            preferred_element_type=_F32)
                o = t if o is None else o + t
            outs.append(o)
        return jnp.concatenate(outs, axis=1), total

    out, carry = sweep([(qi, keep)], None)
    acc_ref[...] = out
    carry_ref[...] = carry

    def cond(state):
        j, alive = state
        return jnp.logical_and(j >= 0, alive)

    def body(state):
        j, _ = state
        carry = carry_ref[...]
        out, total = sweep([(j, None), (jnp.maximum(j - 1, 0), j >= 1)], carry)
        carry = carry + total
        acc_ref[...] += out
        carry_ref[...] = carry
        return j - 2, jnp.max(carry) > -SB_CUTOFF

    lax.while_loop(cond, body, (qi - 1, jnp.max(carry) > -SB_CUTOFF))
    o_ref[...] = acc_ref[...].astype(o_ref.dtype)


def _sb_attention(qkv, batch, seq):
    n = qkv.shape[0]
    nq = seq // SB_BLOCK
    j = jnp.arange(SB_BLOCK)[:, None]
    s = jnp.arange(SB_BLOCK)[None, :]
    tri = jnp.concatenate([(j > s).astype(_BF16), jnp.ones((SB_BLOCK, SB_BLOCK), _BF16)], axis=1)
    whole_seq = lambda col: pl.BlockSpec((seq, D_MODEL), lambda b, i: (b, col),
                                         pipeline_mode=pl.Buffered(1))
    return pl.pallas_call(
        _sb_attention_kernel,
        grid=(batch, nq),
        in_specs=[pl.BlockSpec((SB_BLOCK, D_MODEL), lambda b, i: (b * nq + i, 0)),
                  whole_seq(1), whole_seq(2), _const_spec((SB_BLOCK, 2 * SB_BLOCK))],
        out_specs=pl.BlockSpec((SB_BLOCK, D_MODEL), lambda b, i: (b * nq + i, 0)),
        out_shape=jax.ShapeDtypeStruct((n, D_MODEL), _BF16),
        scratch_shapes=[pltpu.VMEM((SB_BLOCK, D_MODEL), _F32), pltpu.VMEM((SB_BLOCK, D_MODEL), _F32)],
        compiler_params=_params("arbitrary", "arbitrary"),
        name="sb_attention",
    )(qkv, qkv, qkv, tri)


def _router_kernel(x_ref, w_ref, b_ref, tri_ref, idx_ref, gate_ref, rank_ref, count_ref, base_ref):
    @pl.when(pl.program_id(0) == 0)
    def _():
        base_ref[...] = jnp.zeros_like(base_ref)

    x = x_ref[...]
    w = w_ref[...]
    x_hi = x.astype(_BF16)
    x_lo = (x - x_hi.astype(_F32)).astype(_BF16)
    w_hi = w.astype(_BF16)
    w_lo = (w - w_hi.astype(_F32)).astype(_BF16)
    nt = (((1,), (1,)), ((), ()))
    logits = (lax.dot_general(w_hi, x_hi, nt, preferred_element_type=_F32)
              + lax.dot_general(w_lo, x_hi, nt, preferred_element_type=_F32)
              + lax.dot_general(w_hi, x_lo, nt, preferred_element_type=_F32)) + b_ref[...]
    expert = lax.broadcasted_iota(jnp.int32, logits.shape, 0)
    vals, idxs = [], []
    for _ in range(TOP_K):
        m = jnp.max(logits, axis=0, keepdims=True)
        sel = jnp.min(jnp.where(logits == m, expert, N_EXPERTS), axis=0, keepdims=True)
        vals.append(m)
        idxs.append(sel)
        logits = jnp.where(expert == sel, -jnp.inf, logits)
    vals = jnp.concatenate(vals, axis=0)
    e = jnp.exp(vals - vals[0:1])
    gate_ref[...] = e / jnp.sum(e, axis=0, keepdims=True)
    idx_ref[...] = jnp.concatenate(idxs, axis=0)

    member = jnp.zeros(logits.shape, _F32)
    for sel in idxs:
        member += (expert == sel).astype(_F32)
    before = jnp.dot(member.astype(_BF16), tri_ref[...], preferred_element_type=_F32) + base_ref[...]
    ranks = [jnp.sum(jnp.where(expert == sel, before, 0.0), axis=0, keepdims=True) for sel in idxs]
    rank_ref[...] = jnp.concatenate(ranks, axis=0).astype(jnp.int32)
    base_ref[...] += jnp.sum(member, axis=1, keepdims=True)
    count_ref[...] = base_ref[...]


def _router(x, router_w, router_b):
    n, d = x.shape
    t = jnp.arange(ROUTER_ROWS)
    tri = (t[:, None] < t[None, :]).astype(_BF16)
    tok = pl.BlockSpec((TOP_K, ROUTER_ROWS), lambda i: (0, i))
    return pl.pallas_call(
        _router_kernel,
        grid=(n // ROUTER_ROWS,),
        in_specs=[pl.BlockSpec((ROUTER_ROWS, d), lambda i: (i, 0)),
                  _const_spec((N_EXPERTS, d)), _const_spec((N_EXPERTS, 1)),
                  _const_spec((ROUTER_ROWS, ROUTER_ROWS))],
        out_specs=[tok, tok, tok, _const_spec((N_EXPERTS, 1))],
        out_shape=[jax.ShapeDtypeStruct((TOP_K, n), jnp.int32),
                   jax.ShapeDtypeStruct((TOP_K, n), _F32),
                   jax.ShapeDtypeStruct((TOP_K, n), jnp.int32),
                   jax.ShapeDtypeStruct((N_EXPERTS, 1), _F32)],
        scratch_shapes=[pltpu.VMEM((N_EXPERTS, 1), _F32)],
        compiler_params=_params("arbitrary"),
        name="router",
    )(x, router_w.T, router_b.reshape(N_EXPERTS, 1), tri)


def _prep_up_kernel(w_ref, sel_ref, o_ref):
    group = 2 * PREP_LANES
    for c in range(w_ref.shape[2] // group):
        t = jnp.dot(w_ref[0, :, c * group:(c + 1) * group].astype(_BF16), sel_ref[...],
                    preferred_element_type=_F32)
        o_ref[0, :, c * PREP_LANES:(c + 1) * PREP_LANES] = t[:, :PREP_LANES].astype(_BF16)
        o_ref[0, :, D_FF + c * PREP_LANES:D_FF + (c + 1) * PREP_LANES] = t[:, PREP_LANES:].astype(_BF16)


def _prep_up(w_up):
    e, d, ff2 = w_up.shape
    i = jnp.arange(2 * PREP_LANES)[:, None]
    j = jnp.arange(2 * PREP_LANES)[None, :]
    sel = (i == jnp.where(j < PREP_LANES, 2 * j, 2 * (j - PREP_LANES) + 1)).astype(_BF16)
    blk = lambda x, r: (x, r, 0)
    return pl.pallas_call(
        _prep_up_kernel,
        grid=(e, d // PREP_ROWS),
        in_specs=[pl.BlockSpec((1, PREP_ROWS, ff2), blk), _const_spec((2 * PREP_LANES, 2 * PREP_LANES))],
        out_specs=pl.BlockSpec((1, PREP_ROWS, ff2), blk),
        out_shape=jax.ShapeDtypeStruct((e, d, ff2), _BF16),
        compiler_params=_params("arbitrary", "arbitrary"),
        name="prep_up",
    )(w_up, sel)


def _dispatch(x_words, slot_of, p):
    n, words = x_words.shape
    part = words // DISPATCH_PARTS
    mesh = plsc.VectorSubcoreMesh(core_axis_name="core", subcore_axis_name="subcore",
                                  num_cores=SC_CORES, num_subcores=SC_SUBCORES)
    out = jax.ShapeDtypeStruct((p, part), x_words.dtype)

    @functools.partial(
        pl.kernel, out_type=[out] * DISPATCH_PARTS, mesh=mesh, scratch_types=[],
        compiler_params=pltpu.CompilerParams(use_tc_tiling_on_sc=True), name="moe_dispatch")
    def dispatch(x_hbm, slot_hbm, *o_hbm):
        for c in range(DISPATCH_PARTS):
            def body(x_vmem, slot_vmem, o_ref=o_hbm[c]):
                for k in range(TOP_K):
                    pltpu.sync_copy(x_vmem, o_ref.at[slot_vmem.at[k]])

            pltpu.emit_pipeline(
                body,
                grid=(n // DISPATCH_ROWS,),
                in_specs=[pl.BlockSpec((DISPATCH_ROWS, part), lambda i, c=c: (i, c)),
                          pl.BlockSpec((TOP_K, DISPATCH_ROWS), lambda i: (0, i))],
                out_specs=[],
                core_axis_name=("core", "subcore"),
                dimension_semantics=(pltpu.PARALLEL,),
            )(x_hbm, slot_hbm)

    return dispatch(x_words, slot_of.reshape(TOP_K, n))


def _combine_gather(y_parts, slot_of):
    nk = slot_of.shape[0]
    part = y_parts[0].shape[1]
    mesh = plsc.VectorSubcoreMesh(core_axis_name="core", subcore_axis_name="subcore",
                                  num_cores=SC_CORES, num_subcores=SC_SUBCORES)
    out = jax.ShapeDtypeStruct((nk, part), y_parts[0].dtype)

    @functools.partial(
        pl.kernel, out_type=[out] * DISPATCH_PARTS, mesh=mesh, scratch_types=[],
        compiler_params=pltpu.CompilerParams(use_tc_tiling_on_sc=True), name="moe_combine_gather")
    def gather(*refs):
        y_hbm, slot_hbm, o_hbm = refs[:DISPATCH_PARTS], refs[DISPATCH_PARTS], refs[DISPATCH_PARTS + 1:]
        for c in range(DISPATCH_PARTS):
            def body(slot_vmem, o_vmem, y_ref=y_hbm[c]):
                pltpu.sync_copy(y_ref.at[slot_vmem.at[0]], o_vmem)

            pltpu.emit_pipeline(
                body,
                grid=(nk // DISPATCH_ROWS,),
                in_specs=[pl.BlockSpec((1, DISPATCH_ROWS), lambda i: (0, i))],
                out_specs=[pl.BlockSpec((DISPATCH_ROWS, part), lambda i: (i, 0))],
                core_axis_name=("core", "subcore"),
                dimension_semantics=(pltpu.PARALLEL,),
            )(slot_hbm, o_hbm[c])

    return gather(*y_parts, slot_of.reshape(1, nk))


def _moe_ffn_kernel(block_e_ref, block_rows_ref, last_block_ref, xa_ref, xb_ref, wu_ref, bu_ref,
                    wd_f32_ref, bd_ref, oa_ref, ob_ref, wd_ref):
    step = pl.program_id(0)
    rows_used = block_rows_ref[step]
    part = oa_ref.shape[1]
    new_expert = jnp.logical_or(step == 0,
                                block_e_ref[step] != block_e_ref[jnp.maximum(step - 1, 0)])

    @pl.when(jnp.logical_and(new_expert, rows_used > 0))
    def _():
        wd_ref[...] = wd_f32_ref[0].astype(_BF16)

    def ffn_rows(*ranges):
        xs = [_unpack_bf16_pairs(jnp.concatenate([xa_ref[lo:hi, :], xb_ref[lo:hi, :]], axis=1))
              for lo, hi in ranges]
        hs = [jnp.dot(x, wu_ref[0], preferred_element_type=_F32) + bu_ref[0] for x in xs]
        acts = []
        for h in hs:
            h_glu = jnp.minimum(h[:, :D_FF], SWIGLU_LIMIT)
            h_lin = jnp.clip(h[:, D_FF:], -SWIGLU_LIMIT, SWIGLU_LIMIT)
            acts.append((h_glu * jax.nn.sigmoid(SWIGLU_ALPHA * h_glu) * (h_lin + 1.0)).astype(_BF16))
        ys = [jnp.dot(a, wd_ref[...], preferred_element_type=_F32) + bd_ref[0] for a in acts]
        for (lo, hi), y in zip(ranges, ys):
            words = _pack_bf16_pairs(y)
            oa_ref[lo:hi, :] = words[:, :part]
            ob_ref[lo:hi, :] = words[:, part:]

    def zero_rows(lo, hi):
        oa_ref[lo:hi, :] = jnp.zeros((hi - lo, part), oa_ref.dtype)
        ob_ref[lo:hi, :] = jnp.zeros((hi - lo, part), ob_ref.dtype)

    sub_blocks = [(lo, lo + MOE_ROWS) for lo in range(0, MOE_STEP_ROWS, MOE_ROWS)]
    whole_step = rows_used > MOE_STEP_ROWS - MOE_ROWS + MOE_TAIL_ROWS

    @pl.when(whole_step)
    def _():
        ffn_rows(*sub_blocks)

    @pl.when(jnp.logical_not(whole_step))
    def _():
        for lo, hi in sub_blocks:
            mid = lo + MOE_TAIL_ROWS

            @pl.when(rows_used > mid)
            def _():
                ffn_rows((lo, hi))

            @pl.when(jnp.logical_and(rows_used > lo, rows_used <= mid))
            def _():
                ffn_rows((lo, mid))
                zero_rows(mid, hi)

            @pl.when(rows_used <= lo)
            def _():
                zero_rows(lo, hi)


def _moe_ffn(xs, block_e, block_rows, last_block, wu, bu, wd_f32, bd):
    xa, xb = xs
    p, part = xa.shape
    ff, d = wd_f32.shape[1:]
    ff2 = wu.shape[2]
    row = lambda i, be, br, lb: (jnp.minimum(i, lb[0]), 0)
    exp3 = lambda i, be, br, lb: (be[i], 0, 0)
    rows_spec = pl.BlockSpec((MOE_STEP_ROWS, part), row)
    grid_spec = pltpu.PrefetchScalarGridSpec(
        num_scalar_prefetch=3,
        grid=(p // MOE_STEP_ROWS,),
        in_specs=[rows_spec, rows_spec,
                  pl.BlockSpec((1, d, ff2), exp3), pl.BlockSpec((1, 1, ff2), exp3),
                  pl.BlockSpec((1, ff, d), exp3), pl.BlockSpec((1, 1, d), exp3)],
        out_specs=[rows_spec, rows_spec],
        scratch_shapes=[pltpu.VMEM((ff, d), _BF16)],
    )
    out = jax.ShapeDtypeStruct((p, part), xa.dtype)
    return pl.pallas_call(
        _moe_ffn_kernel,
        grid_spec=grid_spec,
        out_shape=[out, out],
        compiler_params=_params("arbitrary"),
        name="moe_ffn",
    )(block_e, block_rows, last_block, xa, xb, wu, bu, wd_f32, bd)


def _combine_ln_kernel(ya_ref, yb_ref, gate_ref, res_ref, g_ref, b_ref, o_ref, *maybe_obf_ref):
    gate = gate_ref[...]
    moe = None
    for k in range(TOP_K):
        words = jnp.concatenate([ya_ref[k], yb_ref[k]], axis=1)
        term = _unpack_bf16_pairs(words).astype(_F32) * gate[:, k:k + 1]
        moe = term if moe is None else moe + term
    y = _layer_norm_rows(DEEPNORM_ALPHA * res_ref[...] + moe, g_ref[...], b_ref[...])
    o_ref[...] = y
    for obf_ref in maybe_obf_ref:
        obf_ref[...] = y.astype(_BF16)


def _combine_ln(y4_parts, gates, res, g, b, with_bf16):
    n, d = res.shape
    ya, yb = (t.reshape(TOP_K, n, t.shape[1]) for t in y4_parts)
    part = ya.shape[2]
    row = lambda i: (i, 0)
    y_spec = pl.BlockSpec((TOP_K, COMBINE_ROWS, part), lambda i: (0, i, 0))
    return pl.pallas_call(
        _combine_ln_kernel,
        grid=(n // COMBINE_ROWS,),
        in_specs=[y_spec, y_spec, pl.BlockSpec((COMBINE_ROWS, TOP_K), row),
                  pl.BlockSpec((COMBINE_ROWS, d), row), _const_spec((1, d)), _const_spec((1, d))],
        out_specs=[pl.BlockSpec((COMBINE_ROWS, d), row)] * (2 if with_bf16 else 1),
        out_shape=[jax.ShapeDtypeStruct((n, d), _F32)]
        + ([jax.ShapeDtypeStruct((n, d), _BF16)] if with_bf16 else []),
        compiler_params=_params("arbitrary"),
        name="combine_ln",
    )(ya, yb, gates, res, g.reshape(1, d), b.reshape(1, d))


def _moe_layer(x, x_words, router_w, router_b, w_up, b_up, w_down, b_down, ln_g, ln_b, with_bf16):
    n, d = x.shape
    nk = n * TOP_K
    idx_t, gate_t, rank_t, count = _router(x, router_w, router_b)
    counts = count[:, 0].astype(jnp.int32)
    padded = (counts + MOE_STEP_ROWS - 1) // MOE_STEP_ROWS * MOE_STEP_ROWS
    padded_ends = jnp.cumsum(padded)
    padded_starts = padded_ends - padded
    n_blocks = -(-(nk + N_EXPERTS * (MOE_STEP_ROWS - 1)) // MOE_STEP_ROWS)
    p = n_blocks * MOE_STEP_ROWS
    short = counts % MOE_STEP_ROWS
    experts = jnp.arange(N_EXPERTS, dtype=jnp.int32)
    chosen = idx_t[None] == experts[:, None, None]
    start_of = jnp.sum(jnp.where(chosen, padded_starts[:, None, None], 0), axis=0)
    short_of = jnp.sum(jnp.where(chosen, short[:, None, None], 0), axis=0)
    gap_of = jnp.where(jnp.logical_and(short_of > 0, rank_t >= short_of), MOE_STEP_ROWS - short_of, 0)
    slot_of = (start_of + rank_t + gap_of).reshape(nk)
    block_start = jnp.arange(n_blocks, dtype=jnp.int32) * MOE_STEP_ROWS
    block_e = jnp.minimum(jnp.sum(padded_ends[None, :] <= block_start[:, None], axis=1),
                          N_EXPERTS - 1).astype(jnp.int32)
    first_of_expert = block_start == padded_starts[block_e]
    block_rows = jnp.where(block_start >= padded_ends[-1], 0,
                           jnp.where(jnp.logical_and(first_of_expert, short[block_e] > 0),
                                     short[block_e], MOE_STEP_ROWS)).astype(jnp.int32)
    last_block = jnp.minimum(padded_ends[-1] // MOE_STEP_ROWS, n_blocks - 1).astype(jnp.int32)
    xs = _dispatch(x_words, slot_of, p)
    bu = b_up.reshape(N_EXPERTS, D_FF, 2).transpose(0, 2, 1).reshape(N_EXPERTS, 1, 2 * D_FF)
    ys = _moe_ffn(xs, block_e, block_rows, last_block.reshape(1), _prep_up(w_up), bu, w_down,
                  b_down.reshape(N_EXPERTS, 1, d))
    return _combine_ln(_combine_gather(ys, slot_of), gate_t.T, x, ln_g, ln_b, with_bf16)


def kernel(x, l0_w_in, l0_ret_gn_g, l0_conv_w, l0_conv_b, l0_conv_ln_g, l0_conv_ln_b, l0_w_out, l0_ln1_g, l0_ln1_b, l0_router_w, l0_router_b, l0_w_up, l0_b_up, l0_w_down, l0_b_down, l0_ln2_g, l0_ln2_b, l1_w_in, l1_w_out, l1_ln1_g, l1_ln1_b, l1_router_w, l1_router_b, l1_w_up, l1_b_up, l1_w_down, l1_b_down, l1_ln2_g, l1_ln2_b):
    batch, seq, d = x.shape
    n = batch * seq
    x0 = x.reshape(n, d)

    x1, x1_words = _l0_block(x0, batch, seq, l0_w_in, l0_ret_gn_g, l0_conv_w, l0_conv_b,
                             l0_conv_ln_g, l0_conv_ln_b, l0_w_out, l0_ln1_g, l0_ln1_b)
    x2, x2_bf16 = _moe_layer(x1, x1_words, l0_router_w, l0_router_b, l0_w_up, l0_b_up, l0_w_down,
                             l0_b_down, l0_ln2_g, l0_ln2_b, True)

    q_scale = jnp.where(jnp.arange(3 * d) < d, LOG2_E * SB_HEAD_DIM ** -0.5, 1.0).astype(_F32)
    qkv = _proj(x2_bf16, l1_w_in.astype(_BF16), q_scale)
    att = _sb_attention(qkv, batch, seq)
    x3, x3_words = _proj_res_ln(att, l1_w_out.astype(_BF16), x2, l1_ln1_g, l1_ln1_b)
    (x4,) = _moe_layer(x3, x3_words, l1_router_w, l1_router_b, l1_w_up, l1_b_up, l1_w_down,
                       l1_b_down, l1_ln2_g, l1_ln2_b, False)
    return x4.reshape(batch, seq, d)
```

```python
import functools

import jax
import jax.numpy as jnp
from jax import lax
from jax.experimental import pallas as pl
from jax.experimental.pallas import tpu as pltpu
from jax.experimental.pallas import tpu_sc as plsc

D_MODEL = 1024
CHUNK = 64
RET_HEADS = 4
RET_DK = 64
RET_DV = 128
RET_QK_WIDTH = RET_HEADS * RET_DK
RET_WIDTH = RET_HEADS * RET_DV
ROPE_BASE = 10000.0
CONV_CH = D_MODEL // 2
CONV_WIDTH = 31
L0_IN = 2 * RET_QK_WIDTH + 2 * RET_WIDTH + 2 * CONV_CH
SB_HEADS = 8
SB_HEAD_DIM = D_MODEL // SB_HEADS
N_EXPERTS = 32
TOP_K = 4
D_FF = D_MODEL
SWIGLU_LIMIT = 7.0
SWIGLU_ALPHA = 1.702
LN_EPS = 1e-5
DEPTH = 2
DEEPNORM_ALPHA = (2 * DEPTH) ** 0.25

VMEM_LIMIT_BYTES = 48 * 1024 * 1024
F32_SUBLANES = 8
PROJ_ROWS = 512
MIX_ROWS = 256
CONV_HALO = 32
ROUTER_ROWS = 512
MOE_ROWS = 512
MOE_TAIL_ROWS = 256
MOE_STEP_ROWS = 1024
PREP_ROWS = 512
PREP_LANES = 128
COMBINE_ROWS = 256
DISPATCH_ROWS = 128
DISPATCH_PARTS = 2
SC_CORES = 2
SC_SUBCORES = 16
SB_BLOCK = 128
SB_STEP_BLOCKS = 2
SB_CUTOFF = 150.0
LOG2_E = 1.4426950408889634

_F32 = jnp.float32
_BF16 = jnp.bfloat16


def _params(*sem):
    return pltpu.CompilerParams(dimension_semantics=sem, vmem_limit_bytes=VMEM_LIMIT_BYTES)


def _const_spec(shape):
    nd = len(shape)
    return pl.BlockSpec(shape, lambda *_: (0,) * nd)


def _weight_spec(shape):
    nd = len(shape)
    return pl.BlockSpec(shape, lambda *_: (0,) * nd, pipeline_mode=pl.Buffered(1))


def _layer_norm_rows(y, g, b):
    mu = jnp.mean(y, axis=-1, keepdims=True)
    yc = y - mu
    var = jnp.mean(yc * yc, axis=-1, keepdims=True)
    return yc * lax.rsqrt(var + LN_EPS) * g + b


def _silu(x):
    return x * jax.nn.sigmoid(x)


def _pack_bf16_pairs(y):
    half = y.shape[1] // 2
    lo = lax.bitcast_convert_type(y[:, :half].astype(_BF16).astype(_F32), jnp.uint32)
    hi = lax.bitcast_convert_type(y[:, half:].astype(_BF16).astype(_F32), jnp.uint32)
    return (lo >> 16) | (hi & jnp.uint32(0xFFFF0000))


def _unpack_bf16_pairs(w):
    lo = lax.bitcast_convert_type(w << 16, _F32)
    hi = lax.bitcast_convert_type(w & jnp.uint32(0xFFFF0000), _F32)
    return jnp.concatenate([lo, hi], axis=1).astype(_BF16)


def _proj_kernel(x_ref, w_f32_ref, scale_ref, o_ref, w_ref):
    @pl.when(pl.program_id(0) == 0)
    def _():
        w_ref[...] = w_f32_ref[...].astype(_BF16)

    o_ref[...] = (jnp.dot(x_ref[...].astype(_BF16), w_ref[...], preferred_element_type=_F32)
                  * scale_ref[...]).astype(o_ref.dtype)


def _proj(x, w, col_scale):
    n, d = x.shape
    width = w.shape[1]
    return pl.pallas_call(
        _proj_kernel,
        grid=(n // PROJ_ROWS,),
        in_specs=[pl.BlockSpec((PROJ_ROWS, d), lambda i: (i, 0)), _weight_spec((d, width)),
                  _const_spec((1, width))],
        out_specs=pl.BlockSpec((PROJ_ROWS, width), lambda i: (i, 0)),
        out_shape=jax.ShapeDtypeStruct((n, width), _BF16),
        scratch_shapes=[pltpu.VMEM((d, width), _BF16)],
        compiler_params=_params("arbitrary"),
        name="proj",
    )(x, w, col_scale.reshape(1, width))


def _proj_res_ln_kernel(a_ref, w_f32_ref, res_ref, g_ref, b_ref, o_ref, words_ref, w_ref):
    @pl.when(pl.program_id(0) == 0)
    def _():
        w_ref[...] = w_f32_ref[...].astype(_BF16)

    m = jnp.dot(a_ref[...], w_ref[...], preferred_element_type=_F32)
    y = _layer_norm_rows(DEEPNORM_ALPHA * res_ref[...] + m, g_ref[...], b_ref[...])
    o_ref[...] = y
    words_ref[...] = _pack_bf16_pairs(y)


def _proj_res_ln(a_bf16, w, res, g, b):
    n, d = res.shape
    k = a_bf16.shape[1]
    row = lambda i: (i, 0)
    return pl.pallas_call(
        _proj_res_ln_kernel,
        grid=(n // PROJ_ROWS,),
        in_specs=[pl.BlockSpec((PROJ_ROWS, k), row), _weight_spec((k, d)),
                  pl.BlockSpec((PROJ_ROWS, d), row), _const_spec((1, d)), _const_spec((1, d))],
        out_specs=[pl.BlockSpec((PROJ_ROWS, d), row), pl.BlockSpec((PROJ_ROWS, d // 2), row)],
        out_shape=[jax.ShapeDtypeStruct((n, d), _F32), jax.ShapeDtypeStruct((n, d // 2), jnp.uint32)],
        scratch_shapes=[pltpu.VMEM((k, d), _BF16)],
        compiler_params=_params("arbitrary"),
        name="proj_res_ln",
    )(a_bf16, w, res, g.reshape(1, d), b.reshape(1, d))


def _l0_block_kernel(x_ref, w_in_f32_ref, cos_ref, sin_ref, xi_ref, zeta_ref, decay_ref, gstate_ref,
                     gn_g_ref, conv_w_ref, conv_b_ref, cln_g_ref, cln_b_ref, w_out_f32_ref, ln_g_ref,
                     ln_b_ref, o_ref, words_ref, state_ref, u_ref, shift_ref, h_ref, mix_ref,
                     w_in_ref, w_out_ref):
    rows = MIX_ROWS
    step = pl.program_id(1)
    v_off = 2 * RET_QK_WIDTH
    g_off = v_off + RET_WIDTH
    a_off = g_off + RET_WIDTH

    @pl.when(jnp.logical_and(pl.program_id(0) == 0, step == 0))
    def _():
        w_in_ref[...] = w_in_f32_ref[...].astype(_BF16)
        w_out_ref[...] = w_out_f32_ref[...].astype(_BF16)

    @pl.when(step == 0)
    def _():
        state_ref[...] = jnp.zeros_like(state_ref)
        u_ref[0:CONV_HALO, :] = jnp.zeros((CONV_HALO, CONV_CH), _F32)

    x_bf16 = x_ref[...].astype(_BF16)
    h_conv = jnp.dot(x_bf16, w_in_ref[:, a_off:a_off + 2 * CONV_CH], preferred_element_type=_F32)
    h_ref[...] = jnp.dot(x_bf16, w_in_ref[:, 0:a_off], preferred_element_type=_F32)

    u_ref[CONV_HALO:CONV_HALO + rows, :] = h_conv[:, :CONV_CH] * jax.nn.sigmoid(h_conv[:, CONV_CH:])
    acc = jnp.broadcast_to(conv_b_ref[...], (rows, CONV_CH))
    first = CONV_HALO - (CONV_WIDTH - 1)
    for phase in range(F32_SUBLANES):
        offsets = [first + j for j in range(CONV_WIDTH) if (first + j) % F32_SUBLANES == phase]
        span = max(offsets) - phase + rows
        if phase == 0:
            src = u_ref
        else:
            shift_ref[0:span, :] = u_ref[phase:phase + span, :]
            src = shift_ref
        for off in offsets:
            j = off - first
            acc = acc + conv_w_ref[j:j + 1, :] * src[off - phase:off - phase + rows, :]
    u_ref[0:CONV_HALO, :] = u_ref[rows:rows + CONV_HALO, :]
    c = _silu(_layer_norm_rows(acc, cln_g_ref[...], cln_b_ref[...]))
    mix_ref[:, RET_WIDTH:RET_WIDTH + CONV_CH] = c.astype(_BF16)

    lane = lax.broadcasted_iota(jnp.int32, (rows, RET_QK_WIDTH), 1)
    first_half = (lane % RET_DK) < (RET_DK // 2)
    cos = cos_ref[...]
    sin = sin_ref[...]

    def rotary(t):
        partner = jnp.where(first_half,
                            pltpu.roll(t, RET_QK_WIDTH - RET_DK // 2, axis=1),
                            pltpu.roll(t, RET_DK // 2, axis=1))
        return t * cos + partner * sin

    q = rotary(h_ref[:, 0:RET_QK_WIDTH])
    k = rotary(h_ref[:, RET_QK_WIDTH:2 * RET_QK_WIDTH]) * (RET_DK ** -0.5)
    q_in = (q * xi_ref[...]).astype(_BF16)
    k_out = (k * zeta_ref[...]).astype(_BF16)
    q = q.astype(_BF16)
    k = k.astype(_BF16)
    for hd in range(RET_HEADS):
        qk = slice(hd * RET_DK, (hd + 1) * RET_DK)
        vs = slice(hd * RET_DV, (hd + 1) * RET_DV)
        v = h_ref[:, v_off + hd * RET_DV:v_off + (hd + 1) * RET_DV].astype(_BF16)
        s = lax.dot_general(q[:, qk], k[:, qk], (((1,), (1,)), ((), ())),
                            preferred_element_type=_F32) * decay_ref[hd]
        r = jnp.dot(s.astype(_BF16), v, preferred_element_type=_F32)
        r += jnp.dot(q_in[:, qk], state_ref[hd].astype(_BF16), preferred_element_type=_F32)
        kv = lax.dot_general(k_out[:, qk], v, (((0,), (0,)), ((), ())),
                             preferred_element_type=_F32)
        state_ref[hd] = gstate_ref[hd] * state_ref[hd] + kv
        mu = jnp.mean(r, axis=-1, keepdims=True)
        rc = r - mu
        var = jnp.mean(rc * rc, axis=-1, keepdims=True)
        rn = rc * lax.rsqrt(var + LN_EPS) * gn_g_ref[:, vs]
        gate = h_ref[:, g_off + hd * RET_DV:g_off + (hd + 1) * RET_DV]
        mix_ref[:, vs] = (_silu(gate) * rn).astype(_BF16)

    m = jnp.dot(mix_ref[...], w_out_ref[...], preferred_element_type=_F32)
    y = _layer_norm_rows(DEEPNORM_ALPHA * x_ref[...] + m, ln_g_ref[...], ln_b_ref[...])
    o_ref[...] = y
    words_ref[...] = _pack_bf16_pairs(y)


def _l0_mixer_tables(seq):
    half = RET_DK // 2
    inv = ROPE_BASE ** (-jnp.arange(half, dtype=_F32) / half)
    ang = jnp.arange(seq).astype(_F32)[:, None] * inv[None, :]
    cos = jnp.tile(jnp.cos(ang), (1, 2 * RET_HEADS))
    sin = jnp.tile(jnp.concatenate([-jnp.sin(ang), jnp.sin(ang)], axis=1), (1, RET_HEADS))
    log_g = jnp.log(1.0 - 2.0 ** (-5.0 - jnp.arange(RET_HEADS, dtype=_F32)))
    n = jnp.arange(MIX_ROWS, dtype=_F32)
    diff = n[:, None] - n[None, :]
    chunk = jnp.arange(MIX_ROWS) // CHUNK
    expo = jnp.where(chunk[:, None] == chunk[None, :], jnp.abs(diff), diff)
    decay = jnp.where((chunk[:, None] >= chunk[None, :])[None],
                      jnp.exp(expo[None] * log_g[:, None, None]), 0.0)
    xi = jnp.repeat(jnp.exp((n[:, None] + 1.0) * log_g[None, :]), RET_DK, axis=1)
    zeta = jnp.repeat(jnp.exp((MIX_ROWS - 1.0 - n[:, None]) * log_g[None, :]), RET_DK, axis=1)
    g_state = jnp.exp(MIX_ROWS * log_g)
    return cos, sin, xi, zeta, decay, g_state


def _l0_block(x, batch, seq, w_in, gn_g, conv_w, conv_b, cln_g, cln_b, w_out, ln_g, ln_b):
    n, d = x.shape
    steps = seq // MIX_ROWS
    cos, sin, xi, zeta, decay, g_state = _l0_mixer_tables(seq)
    row = lambda b, i: (b * steps + i, 0)
    pos = lambda b, i: (i, 0)
    vec = lambda t: t.reshape(1, -1)
    return pl.pallas_call(
        _l0_block_kernel,
        grid=(batch, steps),
        in_specs=[pl.BlockSpec((MIX_ROWS, d), row), _weight_spec((d, L0_IN)),
                  pl.BlockSpec((MIX_ROWS, RET_QK_WIDTH), pos),
                  pl.BlockSpec((MIX_ROWS, RET_QK_WIDTH), pos),
                  _const_spec((MIX_ROWS, RET_QK_WIDTH)), _const_spec((MIX_ROWS, RET_QK_WIDTH)),
                  _const_spec((RET_HEADS, MIX_ROWS, MIX_ROWS)),
                  pl.BlockSpec(memory_space=pltpu.SMEM),
                  _const_spec((1, RET_WIDTH)), _const_spec((CONV_WIDTH, CONV_CH)),
                  _const_spec((1, CONV_CH)), _const_spec((1, CONV_CH)), _const_spec((1, CONV_CH)),
                  _weight_spec((RET_WIDTH + CONV_CH, d)), _const_spec((1, d)), _const_spec((1, d))],
        out_specs=[pl.BlockSpec((MIX_ROWS, d), row), pl.BlockSpec((MIX_ROWS, d // 2), row)],
        out_shape=[jax.ShapeDtypeStruct((n, d), _F32), jax.ShapeDtypeStruct((n, d // 2), jnp.uint32)],
        scratch_shapes=[pltpu.VMEM((RET_HEADS, RET_DK, RET_DV), _F32),
                        pltpu.VMEM((MIX_ROWS + CONV_HALO, CONV_CH), _F32),
                        pltpu.VMEM((MIX_ROWS + CONV_HALO, CONV_CH), _F32),
                        pltpu.VMEM((MIX_ROWS, 2 * RET_QK_WIDTH + 2 * RET_WIDTH), _F32),
                        pltpu.VMEM((MIX_ROWS, RET_WIDTH + CONV_CH), _BF16),
                        pltpu.VMEM((d, L0_IN), _BF16), pltpu.VMEM((RET_WIDTH + CONV_CH, d), _BF16)],
        compiler_params=_params("arbitrary", "arbitrary"),
        name="l0_block",
    )(x, w_in, cos, sin, xi, zeta, decay, g_state, vec(gn_g), conv_w, vec(conv_b),
      vec(cln_g), vec(cln_b), w_out, vec(ln_g), vec(ln_b))


def _sb_attention_kernel(q_ref, k_ref, v_ref, tri_ref, o_ref, acc_ref, carry_ref):
    blk = SB_BLOCK
    t_idx = lax.broadcasted_iota(jnp.int32, (blk, blk), 0)
    s_idx = lax.broadcasted_iota(jnp.int32, (blk, blk), 1)
    keep = s_idx < t_idx

    heads = [slice(hd * SB_HEAD_DIM, (hd + 1) * SB_HEAD_DIM) for hd in range(SB_HEADS)]

    def sweep(q_rows, blocks, carry):
        starts = [pl.multiple_of(j * blk, blk) for j, _ in blocks]
        zs = [[lax.dot_general(q_ref[q_rows, hs], k_ref[pl.ds(st, blk), hs], (((1,), (1,)), ((), ())),
                               preferred_element_type=_F32) for hs in heads] for st in starts]
        log_betas, addends = [], []
        for (_, mask), z_heads in zip(blocks, zs):
            for z in z_heads:
                softplus = jnp.log2(1.0 + jnp.exp2(-jnp.abs(z)))
                log_beta = jnp.minimum(z, 0.0) - softplus
                log_rest = log_beta - z
                if mask is not None:
                    log_rest = jnp.where(mask, log_rest, 0.0)
                log_betas.append(log_beta)
                addends.append(log_rest.astype(_BF16))
        sums = [jnp.dot(t, tri_ref[...], preferred_element_type=_F32) for t in addends]
        weights, total = [], None
        for b, (_, mask) in enumerate(blocks):
            block_sums = sums[b * SB_HEADS:(b + 1) * SB_HEADS]
            before = carry if total is None else (total if carry is None else carry + total)
            for hd, s in enumerate(block_sums):
                expo = log_betas[b * SB_HEADS + hd] + s[:, :blk]
                if before is not None:
                    expo = expo + before[:, heads[hd]]
                a = jnp.exp2(expo)
                if mask is not None:
                    a = jnp.where(mask, a, 0.0)
                weights.append(a.astype(_BF16))
            block_total = jnp.concatenate([s[:, blk:] for s in block_sums], axis=1)
            total = block_total if total is None else total + block_total
        outs = []
        for hd, hs in enumerate(heads):
            o = None
            for b, st in enumerate(starts):
                t = jnp.dot(weights[b * SB_HEADS + hd], v_ref[pl.ds(st, blk), hs],
                            preferred_element_type=_F32)
                o = t if o is None else o + t
            outs.append(o)
        return jnp.concatenate(outs, axis=1), total

    def query_block(sub):
        q_rows = slice(sub * blk, (sub + 1) * blk)
        qi = pl.program_id(1) * SB_STEP_BLOCKS + sub
        out, carry = sweep(q_rows, [(qi, keep)], None)
        acc_ref[...] = out
        carry_ref[...] = carry

        def cond(state):
            j, alive = state
            return jnp.logical_and(j >= 0, alive)

        def body(state):
            j, _ = state
            carry = carry_ref[...]
            out, total = sweep(q_rows, [(j, None), (jnp.maximum(j - 1, 0), j >= 1)], carry)
            carry = carry + total
            acc_ref[...] += out
            carry_ref[...] = carry
            return j - 2, jnp.max(carry) > -SB_CUTOFF

        lax.while_loop(cond, body, (qi - 1, jnp.max(carry) > -SB_CUTOFF))
        o_ref[q_rows, :] = acc_ref[...].astype(o_ref.dtype)

    for sub in range(SB_STEP_BLOCKS):
        query_block(sub)


def _sb_attention(qkv, batch, seq):
    n = qkv.shape[0]
    rows = SB_STEP_BLOCKS * SB_BLOCK
    nq = seq // rows
    j = jnp.arange(SB_BLOCK)[:, None]
    s = jnp.arange(SB_BLOCK)[None, :]
    tri = jnp.concatenate([(j > s).astype(_BF16), jnp.ones((SB_BLOCK, SB_BLOCK), _BF16)], axis=1)
    whole_seq = lambda col: pl.BlockSpec((seq, D_MODEL), lambda b, i: (b, col),
                                         pipeline_mode=pl.Buffered(1))
    return pl.pallas_call(
        _sb_attention_kernel,
        grid=(batch, nq),
        in_specs=[pl.BlockSpec((rows, D_MODEL), lambda b, i: (b * nq + i, 0)),
                  whole_seq(1), whole_seq(2), _const_spec((SB_BLOCK, 2 * SB_BLOCK))],
        out_specs=pl.BlockSpec((rows, D_MODEL), lambda b, i: (b * nq + i, 0)),
        out_shape=jax.ShapeDtypeStruct((n, D_MODEL), _BF16),
        scratch_shapes=[pltpu.VMEM((SB_BLOCK, D_MODEL), _F32), pltpu.VMEM((SB_BLOCK, D_MODEL), _F32)],
        compiler_params=_params("arbitrary", "arbitrary"),
        name="sb_attention",
    )(qkv, qkv, qkv, tri)


def _router_kernel(x_ref, w_ref, b_ref, tri_ref, idx_ref, gate_ref, rank_ref, count_ref, base_ref):
    @pl.when(pl.program_id(0) == 0)
    def _():
        base_ref[...] = jnp.zeros_like(base_ref)

    x = x_ref[...]
    w = w_ref[...]
    x_hi = x.astype(_BF16)
    x_lo = (x - x_hi.astype(_F32)).astype(_BF16)
    w_hi = w.astype(_BF16)
    w_lo = (w - w_hi.astype(_F32)).astype(_BF16)
    nt = (((1,), (1,)), ((), ()))
    logits = (lax.dot_general(w_hi, x_hi, nt, preferred_element_type=_F32)
              + lax.dot_general(w_lo, x_hi, nt, preferred_element_type=_F32)
              + lax.dot_general(w_hi, x_lo, nt, preferred_element_type=_F32)) + b_ref[...]
    expert = lax.broadcasted_iota(jnp.int32, logits.shape, 0)
    vals, idxs = [], []
    for _ in range(TOP_K):
        m = jnp.max(logits, axis=0, keepdims=True)
        sel = jnp.min(jnp.where(logits == m, expert, N_EXPERTS), axis=0, keepdims=True)
        vals.append(m)
        idxs.append(sel)
        logits = jnp.where(expert == sel, -jnp.inf, logits)
    vals = jnp.concatenate(vals, axis=0)
    e = jnp.exp(vals - vals[0:1])
    gate_ref[...] = e / jnp.sum(e, axis=0, keepdims=True)
    idx_ref[...] = jnp.concatenate(idxs, axis=0)

    member = jnp.zeros(logits.shape, _F32)
    for sel in idxs:
        member += (expert == sel).astype(_F32)
    before = jnp.dot(member.astype(_BF16), tri_ref[...], preferred_element_type=_F32) + base_ref[...]
    ranks = [jnp.sum(jnp.where(expert == sel, before, 0.0), axis=0, keepdims=True) for sel in idxs]
    rank_ref[...] = jnp.concatenate(ranks, axis=0).astype(jnp.int32)
    base_ref[...] += jnp.sum(member, axis=1, keepdims=True)
    count_ref[...] = base_ref[...]


def _router(x, router_w, router_b):
    n, d = x.shape
    t = jnp.arange(ROUTER_ROWS)
    tri = (t[:, None] < t[None, :]).astype(_BF16)
    tok = pl.BlockSpec((TOP_K, ROUTER_ROWS), lambda i: (0, i))
    return pl.pallas_call(
        _router_kernel,
        grid=(n // ROUTER_ROWS,),
        in_specs=[pl.BlockSpec((ROUTER_ROWS, d), lambda i: (i, 0)),
                  _const_spec((N_EXPERTS, d)), _const_spec((N_EXPERTS, 1)),
                  _const_spec((ROUTER_ROWS, ROUTER_ROWS))],
        out_specs=[tok, tok, tok, _const_spec((N_EXPERTS, 1))],
        out_shape=[jax.ShapeDtypeStruct((TOP_K, n), jnp.int32),
                   jax.ShapeDtypeStruct((TOP_K, n), _F32),
                   jax.ShapeDtypeStruct((TOP_K, n), jnp.int32),
                   jax.ShapeDtypeStruct((N_EXPERTS, 1), _F32)],
        scratch_shapes=[pltpu.VMEM((N_EXPERTS, 1), _F32)],
        compiler_params=_params("arbitrary"),
        name="router",
    )(x, router_w.T, router_b.reshape(N_EXPERTS, 1), tri)


def _prep_up_kernel(w_ref, sel_ref, o_ref):
    group = 2 * PREP_LANES
    for c in range(w_ref.shape[2] // group):
        t = jnp.dot(w_ref[0, :, c * group:(c + 1) * group].astype(_BF16), sel_ref[...],
                    preferred_element_type=_F32)
        o_ref[0, :, c * PREP_LANES:(c + 1) * PREP_LANES] = t[:, :PREP_LANES].astype(_BF16)
        o_ref[0, :, D_FF + c * PREP_LANES:D_FF + (c + 1) * PREP_LANES] = t[:, PREP_LANES:].astype(_BF16)


def _prep_up(w_up):
    e, d, ff2 = w_up.shape
    i = jnp.arange(2 * PREP_LANES)[:, None]
    j = jnp.arange(2 * PREP_LANES)[None, :]
    sel = (i == jnp.where(j < PREP_LANES, 2 * j, 2 * (j - PREP_LANES) + 1)).astype(_BF16)
    blk = lambda x, r: (x, r, 0)
    return pl.pallas_call(
        _prep_up_kernel,
        grid=(e, d // PREP_ROWS),
        in_specs=[pl.BlockSpec((1, PREP_ROWS, ff2), blk), _const_spec((2 * PREP_LANES, 2 * PREP_LANES))],
        out_specs=pl.BlockSpec((1, PREP_ROWS, ff2), blk),
        out_shape=jax.ShapeDtypeStruct((e, d, ff2), _BF16),
        compiler_params=_params("arbitrary", "arbitrary"),
        name="prep_up",
    )(w_up, sel)


def _dispatch(x_words, slot_of, p):
    n, words = x_words.shape
    part = words // DISPATCH_PARTS
    mesh = plsc.VectorSubcoreMesh(core_axis_name="core", subcore_axis_name="subcore",
                                  num_cores=SC_CORES, num_subcores=SC_SUBCORES)
    out = jax.ShapeDtypeStruct((p, part), x_words.dtype)

    @functools.partial(
        pl.kernel, out_type=[out] * DISPATCH_PARTS, mesh=mesh, scratch_types=[],
        compiler_params=pltpu.CompilerParams(use_tc_tiling_on_sc=True), name="moe_dispatch")
    def dispatch(x_hbm, slot_hbm, *o_hbm):
        for c in range(DISPATCH_PARTS):
            def body(x_vmem, slot_vmem, o_ref=o_hbm[c]):
                for k in range(TOP_K):
                    pltpu.sync_copy(x_vmem, o_ref.at[slot_vmem.at[k]])

            pltpu.emit_pipeline(
                body,
                grid=(n // DISPATCH_ROWS,),
                in_specs=[pl.BlockSpec((DISPATCH_ROWS, part), lambda i, c=c: (i, c)),
                          pl.BlockSpec((TOP_K, DISPATCH_ROWS), lambda i: (0, i))],
                out_specs=[],
                core_axis_name=("core", "subcore"),
                dimension_semantics=(pltpu.PARALLEL,),
            )(x_hbm, slot_hbm)

    return dispatch(x_words, slot_of.reshape(TOP_K, n))


def _combine_gather(y_parts, slot_of):
    nk = slot_of.shape[0]
    part = y_parts[0].shape[1]
    mesh = plsc.VectorSubcoreMesh(core_axis_name="core", subcore_axis_name="subcore",
                                  num_cores=SC_CORES, num_subcores=SC_SUBCORES)
    out = jax.ShapeDtypeStruct((nk, part), y_parts[0].dtype)

    @functools.partial(
        pl.kernel, out_type=[out] * DISPATCH_PARTS, mesh=mesh, scratch_types=[],
        compiler_params=pltpu.CompilerParams(use_tc_tiling_on_sc=True), name="moe_combine_gather")
    def gather(*refs):
        y_hbm, slot_hbm, o_hbm = refs[:DISPATCH_PARTS], refs[DISPATCH_PARTS], refs[DISPATCH_PARTS + 1:]
        for c in range(DISPATCH_PARTS):
            def body(slot_vmem, o_vmem, y_ref=y_hbm[c]):
                pltpu.sync_copy(y_ref.at[slot_vmem.at[0]], o_vmem)

            pltpu.emit_pipeline(
                body,
                grid=(nk // DISPATCH_ROWS,),
                in_specs=[pl.BlockSpec((1, DISPATCH_ROWS), lambda i: (0, i))],
                out_specs=[pl.BlockSpec((DISPATCH_ROWS, part), lambda i: (i, 0))],
                core_axis_name=("core", "subcore"),
                dimension_semantics=(pltpu.PARALLEL,),
            )(slot_hbm, o_hbm[c])

    return gather(*y_parts, slot_of.reshape(1, nk))


def _moe_ffn_kernel(block_e_ref, block_rows_ref, last_block_ref, xa_ref, xb_ref, wu_ref, bu_ref,
                    wd_f32_ref, bd_ref, oa_ref, ob_ref, wd_ref):
    step = pl.program_id(0)
    rows_used = block_rows_ref[step]
    part = oa_ref.shape[1]
    new_expert = jnp.logical_or(step == 0,
                                block_e_ref[step] != block_e_ref[jnp.maximum(step - 1, 0)])

    @pl.when(jnp.logical_and(new_expert, rows_used > 0))
    def _():
        wd_ref[...] = wd_f32_ref[0].astype(_BF16)

    def ffn_rows(*ranges):
        xs = [_unpack_bf16_pairs(jnp.concatenate([xa_ref[lo:hi, :], xb_ref[lo:hi, :]], axis=1))
              for lo, hi in ranges]
        hs = [jnp.dot(x, wu_ref[0], preferred_element_type=_F32) + bu_ref[0] for x in xs]
        acts = []
        for h in hs:
            h_glu = jnp.minimum(h[:, :D_FF], SWIGLU_LIMIT)
            h_lin = jnp.clip(h[:, D_FF:], -SWIGLU_LIMIT, SWIGLU_LIMIT)
            acts.append((h_glu * jax.nn.sigmoid(SWIGLU_ALPHA * h_glu) * (h_lin + 1.0)).astype(_BF16))
        ys = [jnp.dot(a, wd_ref[...], preferred_element_type=_F32) + bd_ref[0] for a in acts]
        for (lo, hi), y in zip(ranges, ys):
            words = _pack_bf16_pairs(y)
            oa_ref[lo:hi, :] = words[:, :part]
            ob_ref[lo:hi, :] = words[:, part:]

    def zero_rows(lo, hi):
        oa_ref[lo:hi, :] = jnp.zeros((hi - lo, part), oa_ref.dtype)
        ob_ref[lo:hi, :] = jnp.zeros((hi - lo, part), ob_ref.dtype)

    sub_blocks = [(lo, lo + MOE_ROWS) for lo in range(0, MOE_STEP_ROWS, MOE_ROWS)]
    whole_step = rows_used > MOE_STEP_ROWS - MOE_ROWS + MOE_TAIL_ROWS

    @pl.when(whole_step)
    def _():
        ffn_rows(*sub_blocks)

    @pl.when(jnp.logical_not(whole_step))
    def _():
        for lo, hi in sub_blocks:
            mid = lo + MOE_TAIL_ROWS

            @pl.when(rows_used > mid)
            def _():
                ffn_rows((lo, hi))

            @pl.when(jnp.logical_and(rows_used > lo, rows_used <= mid))
            def _():
                ffn_rows((lo, mid))
                zero_rows(mid, hi)

            @pl.when(rows_used <= lo)
            def _():
                zero_rows(lo, hi)


def _moe_ffn(xs, block_e, block_rows, last_block, wu, bu, wd_f32, bd):
    xa, xb = xs
    p, part = xa.shape
    ff, d = wd_f32.shape[1:]
    ff2 = wu.shape[2]
    row = lambda i, be, br, lb: (jnp.minimum(i, lb[0]), 0)
    exp3 = lambda i, be, br, lb: (be[i], 0, 0)
    rows_spec = pl.BlockSpec((MOE_STEP_ROWS, part), row)
    grid_spec = pltpu.PrefetchScalarGridSpec(
        num_scalar_prefetch=3,
        grid=(p // MOE_STEP_ROWS,),
        in_specs=[rows_spec, rows_spec,
                  pl.BlockSpec((1, d, ff2), exp3), pl.BlockSpec((1, 1, ff2), exp3),
                  pl.BlockSpec((1, ff, d), exp3), pl.BlockSpec((1, 1, d), exp3)],
        out_specs=[rows_spec, rows_spec],
        scratch_shapes=[pltpu.VMEM((ff, d), _BF16)],
    )
    out = jax.ShapeDtypeStruct((p, part), xa.dtype)
    return pl.pallas_call(
        _moe_ffn_kernel,
        grid_spec=grid_spec,
        out_shape=[out, out],
        compiler_params=_params("arbitrary"),
        name="moe_ffn",
    )(block_e, block_rows, last_block, xa, xb, wu, bu, wd_f32, bd)


def _combine_ln_kernel(ya_ref, yb_ref, gate_ref, res_ref, g_ref, b_ref, o_ref, *maybe_obf_ref):
    gate = gate_ref[...]
    moe = None
    for k in range(TOP_K):
        words = jnp.concatenate([ya_ref[k], yb_ref[k]], axis=1)
        term = _unpack_bf16_pairs(words).astype(_F32) * gate[:, k:k + 1]
        moe = term if moe is None else moe + term
    y = _layer_norm_rows(DEEPNORM_ALPHA * res_ref[...] + moe, g_ref[...], b_ref[...])
    o_ref[...] = y
    for obf_ref in maybe_obf_ref:
        obf_ref[...] = y.astype(_BF16)


def _combine_ln(y4_parts, gates, res, g, b, with_bf16):
    n, d = res.shape
    ya, yb = (t.reshape(TOP_K, n, t.shape[1]) for t in y4_parts)
    part = ya.shape[2]
    row = lambda i: (i, 0)
    y_spec = pl.BlockSpec((TOP_K, COMBINE_ROWS, part), lambda i: (0, i, 0))
    return pl.pallas_call(
        _combine_ln_kernel,
        grid=(n // COMBINE_ROWS,),
        in_specs=[y_spec, y_spec, pl.BlockSpec((COMBINE_ROWS, TOP_K), row),
                  pl.BlockSpec((COMBINE_ROWS, d), row), _const_spec((1, d)), _const_spec((1, d))],
        out_specs=[pl.BlockSpec((COMBINE_ROWS, d), row)] * (2 if with_bf16 else 1),
        out_shape=[jax.ShapeDtypeStruct((n, d), _F32)]
        + ([jax.ShapeDtypeStruct((n, d), _BF16)] if with_bf16 else []),
        compiler_params=_params("arbitrary"),
        name="combine_ln",
    )(ya, yb, gates, res, g.reshape(1, d), b.reshape(1, d))


def _moe_layer(x, x_words, router_w, router_b, w_up, b_up, w_down, b_down, ln_g, ln_b, with_bf16):
    n, d = x.shape
    nk = n * TOP_K
    idx_t, gate_t, rank_t, count = _router(x, router_w, router_b)
    counts = count[:, 0].astype(jnp.int32)
    padded = (counts + MOE_STEP_ROWS - 1) // MOE_STEP_ROWS * MOE_STEP_ROWS
    padded_ends = jnp.cumsum(padded)
    padded_starts = padded_ends - padded
    n_blocks = -(-(nk + N_EXPERTS * (MOE_STEP_ROWS - 1)) // MOE_STEP_ROWS)
    p = n_blocks * MOE_STEP_ROWS
    short = counts % MOE_STEP_ROWS
    experts = jnp.arange(N_EXPERTS, dtype=jnp.int32)
    chosen = idx_t[None] == experts[:, None, None]
    start_of = jnp.sum(jnp.where(chosen, padded_starts[:, None, None], 0), axis=0)
    short_of = jnp.sum(jnp.where(chosen, short[:, None, None], 0), axis=0)
    gap_of = jnp.where(jnp.logical_and(short_of > 0, rank_t >= short_of), MOE_STEP_ROWS - short_of, 0)
    slot_of = (start_of + rank_t + gap_of).reshape(nk)
    block_start = jnp.arange(n_blocks, dtype=jnp.int32) * MOE_STEP_ROWS
    block_e = jnp.minimum(jnp.sum(padded_ends[None, :] <= block_start[:, None], axis=1),
                          N_EXPERTS - 1).astype(jnp.int32)
    first_of_expert = block_start == padded_starts[block_e]
    block_rows = jnp.where(block_start >= padded_ends[-1], 0,
                           jnp.where(jnp.logical_and(first_of_expert, short[block_e] > 0),
                                     short[block_e], MOE_STEP_ROWS)).astype(jnp.int32)
    last_block = jnp.minimum(padded_ends[-1] // MOE_STEP_ROWS, n_blocks - 1).astype(jnp.int32)
    xs = _dispatch(x_words, slot_of, p)
    bu = b_up.reshape(N_EXPERTS, D_FF, 2).transpose(0, 2, 1).reshape(N_EXPERTS, 1, 2 * D_FF)
    ys = _moe_ffn(xs, block_e, block_rows, last_block.reshape(1), _prep_up(w_up), bu, w_down,
                  b_down.reshape(N_EXPERTS, 1, d))
    return _combine_ln(_combine_gather(ys, slot_of), gate_t.T, x, ln_g, ln_b, with_bf16)


def kernel(x, l0_w_in, l0_ret_gn_g, l0_conv_w, l0_conv_b, l0_conv_ln_g, l0_conv_ln_b, l0_w_out, l0_ln1_g, l0_ln1_b, l0_router_w, l0_router_b, l0_w_up, l0_b_up, l0_w_down, l0_b_down, l0_ln2_g, l0_ln2_b, l1_w_in, l1_w_out, l1_ln1_g, l1_ln1_b, l1_router_w, l1_router_b, l1_w_up, l1_b_up, l1_w_down, l1_b_down, l1_ln2_g, l1_ln2_b):
    batch, seq, d = x.shape
    n = batch * seq
    x0 = x.reshape(n, d)

    x1, x1_words = _l0_block(x0, batch, seq, l0_w_in, l0_ret_gn_g, l0_conv_w, l0_conv_b,
                             l0_conv_ln_g, l0_conv_ln_b, l0_w_out, l0_ln1_g, l0_ln1_b)
    x2, x2_bf16 = _moe_layer(x1, x1_words, l0_router_w, l0_router_b, l0_w_up, l0_b_up, l0_w_down,
                             l0_b_down, l0_ln2_g, l0_ln2_b, True)

    q_scale = jnp.where(jnp.arange(3 * d) < d, LOG2_E * SB_HEAD_DIM ** -0.5, 1.0).astype(_F32)
    qkv = _proj(x2_bf16, l1_w_in, q_scale)
    att = _sb_attention(qkv, batch, seq)
    x3, x3_words = _proj_res_ln(att, l1_w_out, x2, l1_ln1_g, l1_ln1_b)
    (x4,) = _moe_layer(x3, x3_words, l1_router_w, l1_router_b, l1_w_up, l1_b_up, l1_w_down,
                       l1_b_down, l1_ln2_g, l1_ln2_b, False)
    return x4.reshape(batch, seq, d)
```

```python
import functools

import jax
import jax.numpy as jnp
from jax import lax
from jax.experimental import pallas as pl
from jax.experimental.pallas import tpu as pltpu
from jax.experimental.pallas import tpu_sc as plsc

D_MODEL = 1024
CHUNK = 64
RET_HEADS = 4
RET_DK = 64
RET_DV = 128
RET_QK_WIDTH = RET_HEADS * RET_DK
RET_WIDTH = RET_HEADS * RET_DV
ROPE_BASE = 10000.0
CONV_CH = D_MODEL // 2
CONV_WIDTH = 31
L0_IN = 2 * RET_QK_WIDTH + 2 * RET_WIDTH + 2 * CONV_CH
SB_HEADS = 8
SB_HEAD_DIM = D_MODEL // SB_HEADS
N_EXPERTS = 32
TOP_K = 4
D_FF = D_MODEL
SWIGLU_LIMIT = 7.0
SWIGLU_ALPHA = 1.702
LN_EPS = 1e-5
DEPTH = 2
DEEPNORM_ALPHA = (2 * DEPTH) ** 0.25

VMEM_LIMIT_BYTES = 48 * 1024 * 1024
F32_SUBLANES = 8
PROJ_ROWS = 1024
MIX_ROWS = 256
CONV_HALO = 32
ROUTER_ROWS = 1024
MOE_ROWS = 512
MOE_TAIL_ROWS = 256
MOE_STEP_ROWS = 1024
PREP_ROWS = 1024
PREP_LANES = 128
COMBINE_ROWS = 512
DISPATCH_ROWS = 128
DISPATCH_PARTS = 2
SC_CORES = 2
SC_SUBCORES = 16
SB_BLOCK = 128
SB_STEP_BLOCKS = 2
SB_CUTOFF = 150.0
LOG2_E = 1.4426950408889634

_F32 = jnp.float32
_BF16 = jnp.bfloat16


def _params(*sem):
    return pltpu.CompilerParams(dimension_semantics=sem, vmem_limit_bytes=VMEM_LIMIT_BYTES)


def _const_spec(shape):
    nd = len(shape)
    return pl.BlockSpec(shape, lambda *_: (0,) * nd)


def _weight_spec(shape):
    nd = len(shape)
    return pl.BlockSpec(shape, lambda *_: (0,) * nd, pipeline_mode=pl.Buffered(1))


def _layer_norm_rows(y, g, b):
    mu = jnp.mean(y, axis=-1, keepdims=True)
    yc = y - mu
    var = jnp.mean(yc * yc, axis=-1, keepdims=True)
    return yc * lax.rsqrt(var + LN_EPS) * g + b


def _silu(x):
    return x * jax.nn.sigmoid(x)


def _pack_bf16_pairs(y):
    half = y.shape[1] // 2
    lo = lax.bitcast_convert_type(y[:, :half].astype(_BF16).astype(_F32), jnp.uint32)
    hi = lax.bitcast_convert_type(y[:, half:].astype(_BF16).astype(_F32), jnp.uint32)
    return (lo >> 16) | (hi & jnp.uint32(0xFFFF0000))


def _unpack_bf16_pairs(w):
    lo = lax.bitcast_convert_type(w << 16, _F32)
    hi = lax.bitcast_convert_type(w & jnp.uint32(0xFFFF0000), _F32)
    return jnp.concatenate([lo, hi], axis=1).astype(_BF16)


def _proj_kernel(x_ref, w_f32_ref, scale_ref, o_ref, w_ref):
    @pl.when(pl.program_id(0) == 0)
    def _():
        w_ref[...] = w_f32_ref[...].astype(_BF16)

    o_ref[...] = (jnp.dot(x_ref[...].astype(_BF16), w_ref[...], preferred_element_type=_F32)
                  * scale_ref[...]).astype(o_ref.dtype)


def _proj(x, w, col_scale):
    n, d = x.shape
    width = w.shape[1]
    return pl.pallas_call(
        _proj_kernel,
        grid=(n // PROJ_ROWS,),
        in_specs=[pl.BlockSpec((PROJ_ROWS, d), lambda i: (i, 0)), _weight_spec((d, width)),
                  _const_spec((1, width))],
        out_specs=pl.BlockSpec((PROJ_ROWS, width), lambda i: (i, 0)),
        out_shape=jax.ShapeDtypeStruct((n, width), _BF16),
        scratch_shapes=[pltpu.VMEM((d, width), _BF16)],
        compiler_params=_params("arbitrary"),
        name="proj",
    )(x, w, col_scale.reshape(1, width))


def _proj_res_ln_kernel(a_ref, w_f32_ref, res_ref, g_ref, b_ref, o_ref, words_ref, w_ref):
    @pl.when(pl.program_id(0) == 0)
    def _():
        w_ref[...] = w_f32_ref[...].astype(_BF16)

    m = jnp.dot(a_ref[...], w_ref[...], preferred_element_type=_F32)
    y = _layer_norm_rows(DEEPNORM_ALPHA * res_ref[...] + m, g_ref[...], b_ref[...])
    o_ref[...] = y
    words_ref[...] = _pack_bf16_pairs(y)


def _proj_res_ln(a_bf16, w, res, g, b):
    n, d = res.shape
    k = a_bf16.shape[1]
    row = lambda i: (i, 0)
    return pl.pallas_call(
        _proj_res_ln_kernel,
        grid=(n // PROJ_ROWS,),
        in_specs=[pl.BlockSpec((PROJ_ROWS, k), row), _weight_spec((k, d)),
                  pl.BlockSpec((PROJ_ROWS, d), row), _const_spec((1, d)), _const_spec((1, d))],
        out_specs=[pl.BlockSpec((PROJ_ROWS, d), row), pl.BlockSpec((PROJ_ROWS, d // 2), row)],
        out_shape=[jax.ShapeDtypeStruct((n, d), _F32), jax.ShapeDtypeStruct((n, d // 2), jnp.uint32)],
        scratch_shapes=[pltpu.VMEM((k, d), _BF16)],
        compiler_params=_params("arbitrary"),
        name="proj_res_ln",
    )(a_bf16, w, res, g.reshape(1, d), b.reshape(1, d))


def _l0_block_kernel(x_ref, w_in_f32_ref, cos_ref, sin_ref, xi_ref, zeta_ref, decay_ref, gstate_ref,
                     gn_g_ref, conv_w_ref, conv_b_ref, cln_g_ref, cln_b_ref, w_out_f32_ref, ln_g_ref,
                     ln_b_ref, o_ref, words_ref, state_ref, u_ref, shift_ref, h_ref, mix_ref,
                     w_in_ref, w_out_ref):
    rows = MIX_ROWS
    step = pl.program_id(1)
    v_off = 2 * RET_QK_WIDTH
    g_off = v_off + RET_WIDTH
    a_off = g_off + RET_WIDTH

    @pl.when(jnp.logical_and(pl.program_id(0) == 0, step == 0))
    def _():
        w_in_ref[...] = w_in_f32_ref[...].astype(_BF16)
        w_out_ref[...] = w_out_f32_ref[...].astype(_BF16)

    @pl.when(step == 0)
    def _():
        state_ref[...] = jnp.zeros_like(state_ref)
        u_ref[0:CONV_HALO, :] = jnp.zeros((CONV_HALO, CONV_CH), _F32)

    x_bf16 = x_ref[...].astype(_BF16)
    h_conv = jnp.dot(x_bf16, w_in_ref[:, a_off:a_off + 2 * CONV_CH], preferred_element_type=_F32)
    h_ref[...] = jnp.dot(x_bf16, w_in_ref[:, 0:a_off], preferred_element_type=_F32)

    u_ref[CONV_HALO:CONV_HALO + rows, :] = h_conv[:, :CONV_CH] * jax.nn.sigmoid(h_conv[:, CONV_CH:])
    acc = jnp.broadcast_to(conv_b_ref[...], (rows, CONV_CH))
    first = CONV_HALO - (CONV_WIDTH - 1)
    for phase in range(F32_SUBLANES):
        offsets = [first + j for j in range(CONV_WIDTH) if (first + j) % F32_SUBLANES == phase]
        span = max(offsets) - phase + rows
        if phase == 0:
            src = u_ref
        else:
            shift_ref[0:span, :] = u_ref[phase:phase + span, :]
            src = shift_ref
        for off in offsets:
            j = off - first
            acc = acc + conv_w_ref[j:j + 1, :] * src[off - phase:off - phase + rows, :]
    u_ref[0:CONV_HALO, :] = u_ref[rows:rows + CONV_HALO, :]
    c = _silu(_layer_norm_rows(acc, cln_g_ref[...], cln_b_ref[...]))
    mix_ref[:, RET_WIDTH:RET_WIDTH + CONV_CH] = c.astype(_BF16)

    lane = lax.broadcasted_iota(jnp.int32, (rows, RET_QK_WIDTH), 1)
    first_half = (lane % RET_DK) < (RET_DK // 2)
    cos = cos_ref[...]
    sin = sin_ref[...]

    def rotary(t):
        partner = jnp.where(first_half,
                            pltpu.roll(t, RET_QK_WIDTH - RET_DK // 2, axis=1),
                            pltpu.roll(t, RET_DK // 2, axis=1))
        return t * cos + partner * sin

    q = rotary(h_ref[:, 0:RET_QK_WIDTH])
    k = rotary(h_ref[:, RET_QK_WIDTH:2 * RET_QK_WIDTH]) * (RET_DK ** -0.5)
    q_in = (q * xi_ref[...]).astype(_BF16)
    k_out = (k * zeta_ref[...]).astype(_BF16)
    q = q.astype(_BF16)
    k = k.astype(_BF16)
    for hd in range(RET_HEADS):
        qk = slice(hd * RET_DK, (hd + 1) * RET_DK)
        vs = slice(hd * RET_DV, (hd + 1) * RET_DV)
        v = h_ref[:, v_off + hd * RET_DV:v_off + (hd + 1) * RET_DV].astype(_BF16)
        s = lax.dot_general(q[:, qk], k[:, qk], (((1,), (1,)), ((), ())),
                            preferred_element_type=_F32) * decay_ref[hd]
        r = jnp.dot(s.astype(_BF16), v, preferred_element_type=_F32)
        r += jnp.dot(q_in[:, qk], state_ref[hd].astype(_BF16), preferred_element_type=_F32)
        kv = lax.dot_general(k_out[:, qk], v, (((0,), (0,)), ((), ())),
                             preferred_element_type=_F32)
        state_ref[hd] = gstate_ref[hd] * state_ref[hd] + kv
        mu = jnp.mean(r, axis=-1, keepdims=True)
        rc = r - mu
        var = jnp.mean(rc * rc, axis=-1, keepdims=True)
        rn = rc * lax.rsqrt(var + LN_EPS) * gn_g_ref[:, vs]
        gate = h_ref[:, g_off + hd * RET_DV:g_off + (hd + 1) * RET_DV]
        mix_ref[:, vs] = (_silu(gate) * rn).astype(_BF16)

    m = jnp.dot(mix_ref[...], w_out_ref[...], preferred_element_type=_F32)
    y = _layer_norm_rows(DEEPNORM_ALPHA * x_ref[...] + m, ln_g_ref[...], ln_b_ref[...])
    o_ref[...] = y
    words_ref[...] = _pack_bf16_pairs(y)


def _l0_mixer_tables(seq):
    half = RET_DK // 2
    inv = ROPE_BASE ** (-jnp.arange(half, dtype=_F32) / half)
    ang = jnp.arange(seq).astype(_F32)[:, None] * inv[None, :]
    cos = jnp.tile(jnp.cos(ang), (1, 2 * RET_HEADS))
    sin = jnp.tile(jnp.concatenate([-jnp.sin(ang), jnp.sin(ang)], axis=1), (1, RET_HEADS))
    log_g = jnp.log(1.0 - 2.0 ** (-5.0 - jnp.arange(RET_HEADS, dtype=_F32)))
    n = jnp.arange(MIX_ROWS, dtype=_F32)
    diff = n[:, None] - n[None, :]
    chunk = jnp.arange(MIX_ROWS) // CHUNK
    expo = jnp.where(chunk[:, None] == chunk[None, :], jnp.abs(diff), diff)
    decay = jnp.where((chunk[:, None] >= chunk[None, :])[None],
                      jnp.exp(expo[None] * log_g[:, None, None]), 0.0)
    xi = jnp.repeat(jnp.exp((n[:, None] + 1.0) * log_g[None, :]), RET_DK, axis=1)
    zeta = jnp.repeat(jnp.exp((MIX_ROWS - 1.0 - n[:, None]) * log_g[None, :]), RET_DK, axis=1)
    g_state = jnp.exp(MIX_ROWS * log_g)
    return cos, sin, xi, zeta, decay, g_state


def _l0_block(x, batch, seq, w_in, gn_g, conv_w, conv_b, cln_g, cln_b, w_out, ln_g, ln_b):
    n, d = x.shape
    steps = seq // MIX_ROWS
    cos, sin, xi, zeta, decay, g_state = _l0_mixer_tables(seq)
    row = lambda b, i: (b * steps + i, 0)
    pos = lambda b, i: (i, 0)
    vec = lambda t: t.reshape(1, -1)
    return pl.pallas_call(
        _l0_block_kernel,
        grid=(batch, steps),
        in_specs=[pl.BlockSpec((MIX_ROWS, d), row), _weight_spec((d, L0_IN)),
                  pl.BlockSpec((MIX_ROWS, RET_QK_WIDTH), pos),
                  pl.BlockSpec((MIX_ROWS, RET_QK_WIDTH), pos),
                  _const_spec((MIX_ROWS, RET_QK_WIDTH)), _const_spec((MIX_ROWS, RET_QK_WIDTH)),
                  _const_spec((RET_HEADS, MIX_ROWS, MIX_ROWS)),
                  pl.BlockSpec(memory_space=pltpu.SMEM),
                  _const_spec((1, RET_WIDTH)), _const_spec((CONV_WIDTH, CONV_CH)),
                  _const_spec((1, CONV_CH)), _const_spec((1, CONV_CH)), _const_spec((1, CONV_CH)),
                  _weight_spec((RET_WIDTH + CONV_CH, d)), _const_spec((1, d)), _const_spec((1, d))],
        out_specs=[pl.BlockSpec((MIX_ROWS, d), row), pl.BlockSpec((MIX_ROWS, d // 2), row)],
        out_shape=[jax.ShapeDtypeStruct((n, d), _F32), jax.ShapeDtypeStruct((n, d // 2), jnp.uint32)],
        scratch_shapes=[pltpu.VMEM((RET_HEADS, RET_DK, RET_DV), _F32),
                        pltpu.VMEM((MIX_ROWS + CONV_HALO, CONV_CH), _F32),
                        pltpu.VMEM((MIX_ROWS + CONV_HALO, CONV_CH), _F32),
                        pltpu.VMEM((MIX_ROWS, 2 * RET_QK_WIDTH + 2 * RET_WIDTH), _F32),
                        pltpu.VMEM((MIX_ROWS, RET_WIDTH + CONV_CH), _BF16),
                        pltpu.VMEM((d, L0_IN), _BF16), pltpu.VMEM((RET_WIDTH + CONV_CH, d), _BF16)],
        compiler_params=_params("arbitrary", "arbitrary"),
        name="l0_block",
    )(x, w_in, cos, sin, xi, zeta, decay, g_state, vec(gn_g), conv_w, vec(conv_b),
      vec(cln_g), vec(cln_b), w_out, vec(ln_g), vec(ln_b))


def _sb_attention_kernel(q_ref, k_ref, v_ref, tri_ref, o_ref, acc_ref, carry_ref):
    blk = SB_BLOCK
    t_idx = lax.broadcasted_iota(jnp.int32, (blk, blk), 0)
    s_idx = lax.broadcasted_iota(jnp.int32, (blk, blk), 1)
    keep = s_idx < t_idx

    heads = [slice(hd * SB_HEAD_DIM, (hd + 1) * SB_HEAD_DIM) for hd in range(SB_HEADS)]

    def sweep(q_rows, blocks, carry):
        starts = [pl.multiple_of(j * blk, blk) for j, _ in blocks]
        zs = [[lax.dot_general(q_ref[q_rows, hs], k_ref[pl.ds(st, blk), hs], (((1,), (1,)), ((), ())),
                               preferred_element_type=_F32) for hs in heads] for st in starts]
        log_betas, addends = [], []
        for (_, mask), z_heads in zip(blocks, zs):
            for z in z_heads:
                softplus = jnp.log2(1.0 + jnp.exp2(-jnp.abs(z)))
                log_beta = jnp.minimum(z, 0.0) - softplus
                log_rest = log_beta - z
                if mask is not None:
                    log_rest = jnp.where(mask, log_rest, 0.0)
                log_betas.append(log_beta)
                addends.append(log_rest.astype(_BF16))
        sums = [jnp.dot(t, tri_ref[...], preferred_element_type=_F32) for t in addends]
        weights, total = [], None
        for b, (_, mask) in enumerate(blocks):
            block_sums = sums[b * SB_HEADS:(b + 1) * SB_HEADS]
            before = carry if total is None else (total if carry is None else carry + total)
            for hd, s in enumerate(block_sums):
                expo = log_betas[b * SB_HEADS + hd] + s[:, :blk]
                if before is not None:
                    expo = expo + before[:, heads[hd]]
                a = jnp.exp2(expo)
                if mask is not None:
                    a = jnp.where(mask, a, 0.0)
                weights.append(a.astype(_BF16))
            block_total = jnp.concatenate([s[:, blk:] for s in block_sums], axis=1)
            total = block_total if total is None else total + block_total
        outs = []
        for hd, hs in enumerate(heads):
            o = None
            for b, st in enumerate(starts):
                t = jnp.dot(weights[b * SB_HEADS + hd], v_ref[pl.ds(st, blk), hs],
                            preferred_element_type=_F32)
                o = t if o is None else o + t
            outs.append(o)
        return jnp.concatenate(outs, axis=1), total

    def query_block(sub):
        q_rows = slice(sub * blk, (sub + 1) * blk)
        qi = pl.program_id(1) * SB_STEP_BLOCKS + sub
        out, carry = sweep(q_rows, [(qi, keep)], None)
        acc_ref[...] = out
        carry_ref[...] = carry

        def cond(state):
            j, alive = state
            return jnp.logical_and(j >= 0, alive)

        def body(state):
            j, _ = state
            carry = carry_ref[...]
            out, total = sweep(q_rows, [(j, None), (jnp.maximum(j - 1, 0), j >= 1)], carry)
            carry = carry + total
            acc_ref[...] += out
            carry_ref[...] = carry
            return j - 2, jnp.max(carry) > -SB_CUTOFF

        lax.while_loop(cond, body, (qi - 1, jnp.max(carry) > -SB_CUTOFF))
        o_ref[q_rows, :] = acc_ref[...].astype(o_ref.dtype)

    for sub in range(SB_STEP_BLOCKS):
        query_block(sub)


def _sb_attention(qkv, batch, seq):
    n = qkv.shape[0]
    rows = SB_STEP_BLOCKS * SB_BLOCK
    nq = seq // rows
    j = jnp.arange(SB_BLOCK)[:, None]
    s = jnp.arange(SB_BLOCK)[None, :]
    tri = jnp.concatenate([(j > s).astype(_BF16), jnp.ones((SB_BLOCK, SB_BLOCK), _BF16)], axis=1)
    whole_seq = lambda col: pl.BlockSpec((seq, D_MODEL), lambda b, i: (b, col),
                                         pipeline_mode=pl.Buffered(1))
    return pl.pallas_call(
        _sb_attention_kernel,
        grid=(batch, nq),
        in_specs=[pl.BlockSpec((rows, D_MODEL), lambda b, i: (b * nq + i, 0)),
                  whole_seq(1), whole_seq(2), _const_spec((SB_BLOCK, 2 * SB_BLOCK))],
        out_specs=pl.BlockSpec((rows, D_MODEL), lambda b, i: (b * nq + i, 0)),
        out_shape=jax.ShapeDtypeStruct((n, D_MODEL), _BF16),
        scratch_shapes=[pltpu.VMEM((SB_BLOCK, D_MODEL), _F32), pltpu.VMEM((SB_BLOCK, D_MODEL), _F32)],
        compiler_params=_params("arbitrary", "arbitrary"),
        name="sb_attention",
    )(qkv, qkv, qkv, tri)


def _router_kernel(x_ref, w_ref, b_ref, tri_ref, idx_ref, gate_ref, rank_ref, count_ref, base_ref):
    @pl.when(pl.program_id(0) == 0)
    def _():
        base_ref[...] = jnp.zeros_like(base_ref)

    x = x_ref[...]
    w = w_ref[...]
    x_hi = x.astype(_BF16)
    x_lo = (x - x_hi.astype(_F32)).astype(_BF16)
    w_hi = w.astype(_BF16)
    w_lo = (w - w_hi.astype(_F32)).astype(_BF16)
    nt = (((1,), (1,)), ((), ()))
    logits = (lax.dot_general(w_hi, x_hi, nt, preferred_element_type=_F32)
              + lax.dot_general(w_lo, x_hi, nt, preferred_element_type=_F32)
              + lax.dot_general(w_hi, x_lo, nt, preferred_element_type=_F32)) + b_ref[...]
    expert = lax.broadcasted_iota(jnp.int32, logits.shape, 0)
    vals, idxs = [], []
    for _ in range(TOP_K):
        m = jnp.max(logits, axis=0, keepdims=True)
        sel = jnp.min(jnp.where(logits == m, expert, N_EXPERTS), axis=0, keepdims=True)
        vals.append(m)
        idxs.append(sel)
        logits = jnp.where(expert == sel, -jnp.inf, logits)
    vals = jnp.concatenate(vals, axis=0)
    e = jnp.exp(vals - vals[0:1])
    gate_ref[...] = e / jnp.sum(e, axis=0, keepdims=True)
    idx_ref[...] = jnp.concatenate(idxs, axis=0)

    member = jnp.zeros(logits.shape, _F32)
    for sel in idxs:
        member += (expert == sel).astype(_F32)
    before = jnp.dot(member.astype(_BF16), tri_ref[...], preferred_element_type=_F32) + base_ref[...]
    ranks = [jnp.sum(jnp.where(expert == sel, before, 0.0), axis=0, keepdims=True) for sel in idxs]
    rank_ref[...] = jnp.concatenate(ranks, axis=0).astype(jnp.int32)
    base_ref[...] += jnp.sum(member, axis=1, keepdims=True)
    count_ref[...] = base_ref[...]


def _router(x, router_w, router_b):
    n, d = x.shape
    t = jnp.arange(ROUTER_ROWS)
    tri = (t[:, None] < t[None, :]).astype(_BF16)
    tok = pl.BlockSpec((TOP_K, ROUTER_ROWS), lambda i: (0, i))
    return pl.pallas_call(
        _router_kernel,
        grid=(n // ROUTER_ROWS,),
        in_specs=[pl.BlockSpec((ROUTER_ROWS, d), lambda i: (i, 0)),
                  _const_spec((N_EXPERTS, d)), _const_spec((N_EXPERTS, 1)),
                  _const_spec((ROUTER_ROWS, ROUTER_ROWS))],
        out_specs=[tok, tok, tok, _const_spec((N_EXPERTS, 1))],
        out_shape=[jax.ShapeDtypeStruct((TOP_K, n), jnp.int32),
                   jax.ShapeDtypeStruct((TOP_K, n), _F32),
                   jax.ShapeDtypeStruct((TOP_K, n), jnp.int32),
                   jax.ShapeDtypeStruct((N_EXPERTS, 1), _F32)],
        scratch_shapes=[pltpu.VMEM((N_EXPERTS, 1), _F32)],
        compiler_params=_params("arbitrary"),
        name="router",
    )(x, router_w.T, router_b.reshape(N_EXPERTS, 1), tri)


def _prep_up_kernel(w_ref, sel_ref, o_ref):
    group = 2 * PREP_LANES
    for c in range(w_ref.shape[2] // group):
        t = jnp.dot(w_ref[0, :, c * group:(c + 1) * group].astype(_BF16), sel_ref[...],
                    preferred_element_type=_F32)
        o_ref[0, :, c * PREP_LANES:(c + 1) * PREP_LANES] = t[:, :PREP_LANES].astype(_BF16)
        o_ref[0, :, D_FF + c * PREP_LANES:D_FF + (c + 1) * PREP_LANES] = t[:, PREP_LANES:].astype(_BF16)


def _prep_up(w_up):
    e, d, ff2 = w_up.shape
    i = jnp.arange(2 * PREP_LANES)[:, None]
    j = jnp.arange(2 * PREP_LANES)[None, :]
    sel = (i == jnp.where(j < PREP_LANES, 2 * j, 2 * (j - PREP_LANES) + 1)).astype(_BF16)
    blk = lambda x, r: (x, r, 0)
    return pl.pallas_call(
        _prep_up_kernel,
        grid=(e, d // PREP_ROWS),
        in_specs=[pl.BlockSpec((1, PREP_ROWS, ff2), blk), _const_spec((2 * PREP_LANES, 2 * PREP_LANES))],
        out_specs=pl.BlockSpec((1, PREP_ROWS, ff2), blk),
        out_shape=jax.ShapeDtypeStruct((e, d, ff2), _BF16),
        compiler_params=_params("arbitrary", "arbitrary"),
        name="prep_up",
    )(w_up, sel)


def _dispatch(x_words, slot_of, p):
    n, words = x_words.shape
    part = words // DISPATCH_PARTS
    mesh = plsc.VectorSubcoreMesh(core_axis_name="core", subcore_axis_name="subcore",
                                  num_cores=SC_CORES, num_subcores=SC_SUBCORES)
    out = jax.ShapeDtypeStruct((p, part), x_words.dtype)

    @functools.partial(
        pl.kernel, out_type=[out] * DISPATCH_PARTS, mesh=mesh, scratch_types=[],
        compiler_params=pltpu.CompilerParams(use_tc_tiling_on_sc=True), name="moe_dispatch")
    def dispatch(x_hbm, slot_hbm, *o_hbm):
        for c in range(DISPATCH_PARTS):
            def body(x_vmem, slot_vmem, o_ref=o_hbm[c]):
                for k in range(TOP_K):
                    pltpu.sync_copy(x_vmem, o_ref.at[slot_vmem.at[k]])

            pltpu.emit_pipeline(
                body,
                grid=(n // DISPATCH_ROWS,),
                in_specs=[pl.BlockSpec((DISPATCH_ROWS, part), lambda i, c=c: (i, c)),
                          pl.BlockSpec((TOP_K, DISPATCH_ROWS), lambda i: (0, i))],
                out_specs=[],
                core_axis_name=("core", "subcore"),
                dimension_semantics=(pltpu.PARALLEL,),
            )(x_hbm, slot_hbm)

    return dispatch(x_words, slot_of.reshape(TOP_K, n))


def _combine_gather(y_parts, slot_of):
    nk = slot_of.shape[0]
    part = y_parts[0].shape[1]
    mesh = plsc.VectorSubcoreMesh(core_axis_name="core", subcore_axis_name="subcore",
                                  num_cores=SC_CORES, num_subcores=SC_SUBCORES)
    out = jax.ShapeDtypeStruct((nk, part), y_parts[0].dtype)

    @functools.partial(
        pl.kernel, out_type=[out] * DISPATCH_PARTS, mesh=mesh, scratch_types=[],
        compiler_params=pltpu.CompilerParams(use_tc_tiling_on_sc=True), name="moe_combine_gather")
    def gather(*refs):
        y_hbm, slot_hbm, o_hbm = refs[:DISPATCH_PARTS], refs[DISPATCH_PARTS], refs[DISPATCH_PARTS + 1:]
        for c in range(DISPATCH_PARTS):
            def body(slot_vmem, o_vmem, y_ref=y_hbm[c]):
                pltpu.sync_copy(y_ref.at[slot_vmem.at[0]], o_vmem)

            pltpu.emit_pipeline(
                body,
                grid=(nk // DISPATCH_ROWS,),
                in_specs=[pl.BlockSpec((1, DISPATCH_ROWS), lambda i: (0, i))],
                out_specs=[pl.BlockSpec((DISPATCH_ROWS, part), lambda i: (i, 0))],
                core_axis_name=("core", "subcore"),
                dimension_semantics=(pltpu.PARALLEL,),
            )(slot_hbm, o_hbm[c])

    return gather(*y_parts, slot_of.reshape(1, nk))


def _moe_ffn_kernel(block_e_ref, block_rows_ref, last_block_ref, xa_ref, xb_ref, wu_ref, bu_ref,
                    wd_f32_ref, bd_ref, oa_ref, ob_ref, wd_ref):
    step = pl.program_id(0)
    rows_used = block_rows_ref[step]
    part = oa_ref.shape[1]
    new_expert = jnp.logical_or(step == 0,
                                block_e_ref[step] != block_e_ref[jnp.maximum(step - 1, 0)])

    @pl.when(jnp.logical_and(new_expert, rows_used > 0))
    def _():
        wd_ref[...] = wd_f32_ref[0].astype(_BF16)

    def ffn_rows(*ranges):
        xs = [_unpack_bf16_pairs(jnp.concatenate([xa_ref[lo:hi, :], xb_ref[lo:hi, :]], axis=1))
              for lo, hi in ranges]
        hs = [jnp.dot(x, wu_ref[0], preferred_element_type=_F32) + bu_ref[0] for x in xs]
        acts = []
        for h in hs:
            h_glu = jnp.minimum(h[:, :D_FF], SWIGLU_LIMIT)
            h_lin = jnp.clip(h[:, D_FF:], -SWIGLU_LIMIT, SWIGLU_LIMIT)
            acts.append((h_glu * jax.nn.sigmoid(SWIGLU_ALPHA * h_glu) * (h_lin + 1.0)).astype(_BF16))
        ys = [jnp.dot(a, wd_ref[...], preferred_element_type=_F32) + bd_ref[0] for a in acts]
        for (lo, hi), y in zip(ranges, ys):
            words = _pack_bf16_pairs(y)
            oa_ref[lo:hi, :] = words[:, :part]
            ob_ref[lo:hi, :] = words[:, part:]

    def zero_rows(lo, hi):
        oa_ref[lo:hi, :] = jnp.zeros((hi - lo, part), oa_ref.dtype)
        ob_ref[lo:hi, :] = jnp.zeros((hi - lo, part), ob_ref.dtype)

    sub_blocks = [(lo, lo + MOE_ROWS) for lo in range(0, MOE_STEP_ROWS, MOE_ROWS)]
    whole_step = rows_used > MOE_STEP_ROWS - MOE_ROWS + MOE_TAIL_ROWS

    @pl.when(whole_step)
    def _():
        ffn_rows(*sub_blocks)

    @pl.when(jnp.logical_not(whole_step))
    def _():
        for lo, hi in sub_blocks:
            mid = lo + MOE_TAIL_ROWS

            @pl.when(rows_used > mid)
            def _():
                ffn_rows((lo, hi))

            @pl.when(jnp.logical_and(rows_used > lo, rows_used <= mid))
            def _():
                ffn_rows((lo, mid))
                zero_rows(mid, hi)

            @pl.when(rows_used <= lo)
            def _():
                zero_rows(lo, hi)


def _moe_ffn(xs, block_e, block_rows, last_block, wu, bu, wd_f32, bd):
    xa, xb = xs
    p, part = xa.shape
    ff, d = wd_f32.shape[1:]
    ff2 = wu.shape[2]
    row = lambda i, be, br, lb: (jnp.minimum(i, lb[0]), 0)
    exp3 = lambda i, be, br, lb: (be[i], 0, 0)
    rows_spec = pl.BlockSpec((MOE_STEP_ROWS, part), row)
    grid_spec = pltpu.PrefetchScalarGridSpec(
        num_scalar_prefetch=3,
        grid=(p // MOE_STEP_ROWS,),
        in_specs=[rows_spec, rows_spec,
                  pl.BlockSpec((1, d, ff2), exp3), pl.BlockSpec((1, 1, ff2), exp3),
                  pl.BlockSpec((1, ff, d), exp3), pl.BlockSpec((1, 1, d), exp3)],
        out_specs=[rows_spec, rows_spec],
        scratch_shapes=[pltpu.VMEM((ff, d), _BF16)],
    )
    out = jax.ShapeDtypeStruct((p, part), xa.dtype)
    return pl.pallas_call(
        _moe_ffn_kernel,
        grid_spec=grid_spec,
        out_shape=[out, out],
        compiler_params=_params("arbitrary"),
        name="moe_ffn",
    )(block_e, block_rows, last_block, xa, xb, wu, bu, wd_f32, bd)


def _combine_ln_kernel(ya_ref, yb_ref, gate_ref, res_ref, g_ref, b_ref, o_ref, *maybe_obf_ref):
    gate = gate_ref[...]
    moe = None
    for k in range(TOP_K):
        words = jnp.concatenate([ya_ref[k], yb_ref[k]], axis=1)
        term = _unpack_bf16_pairs(words).astype(_F32) * gate[:, k:k + 1]
        moe = term if moe is None else moe + term
    y = _layer_norm_rows(DEEPNORM_ALPHA * res_ref[...] + moe, g_ref[...], b_ref[...])
    o_ref[...] = y
    for obf_ref in maybe_obf_ref:
        obf_ref[...] = y.astype(_BF16)


def _combine_ln(y4_parts, gates, res, g, b, with_bf16):
    n, d = res.shape
    ya, yb = (t.reshape(TOP_K, n, t.shape[1]) for t in y4_parts)
    part = ya.shape[2]
    row = lambda i: (i, 0)
    y_spec = pl.BlockSpec((TOP_K, COMBINE_ROWS, part), lambda i: (0, i, 0))
    return pl.pallas_call(
        _combine_ln_kernel,
        grid=(n // COMBINE_ROWS,),
        in_specs=[y_spec, y_spec, pl.BlockSpec((COMBINE_ROWS, TOP_K), row),
                  pl.BlockSpec((COMBINE_ROWS, d), row), _const_spec((1, d)), _const_spec((1, d))],
        out_specs=[pl.BlockSpec((COMBINE_ROWS, d), row)] * (2 if with_bf16 else 1),
        out_shape=[jax.ShapeDtypeStruct((n, d), _F32)]
        + ([jax.ShapeDtypeStruct((n, d), _BF16)] if with_bf16 else []),
        compiler_params=_params("arbitrary"),
        name="combine_ln",
    )(ya, yb, gates, res, g.reshape(1, d), b.reshape(1, d))


def _moe_layer(x, x_words, router_w, router_b, w_up, b_up, w_down, b_down, ln_g, ln_b, with_bf16):
    n, d = x.shape
    nk = n * TOP_K
    idx_t, gate_t, rank_t, count = _router(x, router_w, router_b)
    counts = count[:, 0].astype(jnp.int32)
    padded = (counts + MOE_STEP_ROWS - 1) // MOE_STEP_ROWS * MOE_STEP_ROWS
    padded_ends = jnp.cumsum(padded)
    padded_starts = padded_ends - padded
    n_blocks = -(-(nk + N_EXPERTS * (MOE_STEP_ROWS - 1)) // MOE_STEP_ROWS)
    p = n_blocks * MOE_STEP_ROWS
    short = counts % MOE_STEP_ROWS
    experts = jnp.arange(N_EXPERTS, dtype=jnp.int32)
    chosen = idx_t[None] == experts[:, None, None]
    start_of = jnp.sum(jnp.where(chosen, padded_starts[:, None, None], 0), axis=0)
    short_of = jnp.sum(jnp.where(chosen, short[:, None, None], 0), axis=0)
    gap_of = jnp.where(jnp.logical_and(short_of > 0, rank_t >= short_of), MOE_STEP_ROWS - short_of, 0)
    slot_of = (start_of + rank_t + gap_of).reshape(nk)
    block_start = jnp.arange(n_blocks, dtype=jnp.int32) * MOE_STEP_ROWS
    block_e = jnp.minimum(jnp.sum(padded_ends[None, :] <= block_start[:, None], axis=1),
                          N_EXPERTS - 1).astype(jnp.int32)
    first_of_expert = block_start == padded_starts[block_e]
    block_rows = jnp.where(block_start >= padded_ends[-1], 0,
                           jnp.where(jnp.logical_and(first_of_expert, short[block_e] > 0),
                                     short[block_e], MOE_STEP_ROWS)).astype(jnp.int32)
    last_block = jnp.minimum(padded_ends[-1] // MOE_STEP_ROWS, n_blocks - 1).astype(jnp.int32)
    xs = _dispatch(x_words, slot_of, p)
    bu = b_up.reshape(N_EXPERTS, D_FF, 2).transpose(0, 2, 1).reshape(N_EXPERTS, 1, 2 * D_FF)
    ys = _moe_ffn(xs, block_e, block_rows, last_block.reshape(1), _prep_up(w_up), bu, w_down,
                  b_down.reshape(N_EXPERTS, 1, d))
    return _combine_ln(_combine_gather(ys, slot_of), gate_t.T, x, ln_g, ln_b, with_bf16)


def kernel(x, l0_w_in, l0_ret_gn_g, l0_conv_w, l0_conv_b, l0_conv_ln_g, l0_conv_ln_b, l0_w_out, l0_ln1_g, l0_ln1_b, l0_router_w, l0_router_b, l0_w_up, l0_b_up, l0_w_down, l0_b_down, l0_ln2_g, l0_ln2_b, l1_w_in, l1_w_out, l1_ln1_g, l1_ln1_b, l1_router_w, l1_router_b, l1_w_up, l1_b_up, l1_w_down, l1_b_down, l1_ln2_g, l1_ln2_b):
    batch, seq, d = x.shape
    n = batch * seq
    x0 = x.reshape(n, d)

    x1, x1_words = _l0_block(x0, batch, seq, l0_w_in, l0_ret_gn_g, l0_conv_w, l0_conv_b,
                             l0_conv_ln_g, l0_conv_ln_b, l0_w_out, l0_ln1_g, l0_ln1_b)
    x2, x2_bf16 = _moe_layer(x1, x1_words, l0_router_w, l0_router_b, l0_w_up, l0_b_up, l0_w_down,
                             l0_b_down, l0_ln2_g, l0_ln2_b, True)

    q_scale = jnp.where(jnp.arange(3 * d) < d, LOG2_E * SB_HEAD_DIM ** -0.5, 1.0).astype(_F32)
    qkv = _proj(x2_bf16, l1_w_in, q_scale)
    att = _sb_attention(qkv, batch, seq)
    x3, x3_words = _proj_res_ln(att, l1_w_out, x2, l1_ln1_g, l1_ln1_b)
    (x4,) = _moe_layer(x3, x3_words, l1_router_w, l1_router_b, l1_w_up, l1_b_up, l1_w_down,
                       l1_b_down, l1_ln2_g, l1_ln2_b, False)
    return x4.reshape(batch, seq, d)
```

```python
import functools

import jax
import jax.numpy as jnp
from jax import lax
from jax.experimental import pallas as pl
from jax.experimental.pallas import tpu as pltpu
from jax.experimental.pallas import tpu_sc as plsc

D_MODEL = 1024
CHUNK = 64
RET_HEADS = 4
RET_DK = 64
RET_DV = 128
RET_QK_WIDTH = RET_HEADS * RET_DK
RET_WIDTH = RET_HEADS * RET_DV
ROPE_BASE = 10000.0
CONV_CH = D_MODEL // 2
CONV_WIDTH = 31
L0_IN = 2 * RET_QK_WIDTH + 2 * RET_WIDTH + 2 * CONV_CH
SB_HEADS = 8
SB_HEAD_DIM = D_MODEL // SB_HEADS
N_EXPERTS = 32
TOP_K = 4
D_FF = D_MODEL
SWIGLU_LIMIT = 7.0
SWIGLU_ALPHA = 1.702
LN_EPS = 1e-5
DEPTH = 2
DEEPNORM_ALPHA = (2 * DEPTH) ** 0.25

VMEM_LIMIT_BYTES = 48 * 1024 * 1024
F32_SUBLANES = 8
PROJ_ROWS = 1024
MIX_ROWS = 256
CONV_HALO = 32
ROUTER_ROWS = 1024
MOE_ROWS = 512
MOE_TAIL_ROWS = (128, 256)
MOE_STEP_ROWS = 1024
PREP_ROWS = 1024
PREP_LANES = 128
COMBINE_ROWS = 512
DISPATCH_ROWS = 128
DISPATCH_PARTS = 2
SC_CORES = 2
SC_SUBCORES = 16
SB_BLOCK = 128
SB_STEP_BLOCKS = 4
SB_CUTOFF = 150.0
LOG2_E = 1.4426950408889634

_F32 = jnp.float32
_BF16 = jnp.bfloat16


def _params(*sem):
    return pltpu.CompilerParams(dimension_semantics=sem, vmem_limit_bytes=VMEM_LIMIT_BYTES)


def _const_spec(shape):
    nd = len(shape)
    return pl.BlockSpec(shape, lambda *_: (0,) * nd)


def _weight_spec(shape):
    nd = len(shape)
    return pl.BlockSpec(shape, lambda *_: (0,) * nd, pipeline_mode=pl.Buffered(1))


def _layer_norm_rows(y, g, b):
    mu = jnp.mean(y, axis=-1, keepdims=True)
    yc = y - mu
    var = jnp.mean(yc * yc, axis=-1, keepdims=True)
    return yc * lax.rsqrt(var + LN_EPS) * g + b


def _silu(x):
    return x * jax.nn.sigmoid(x)


def _pack_bf16_pairs(y):
    half = y.shape[1] // 2
    lo = lax.bitcast_convert_type(y[:, :half].astype(_BF16).astype(_F32), jnp.uint32)
    hi = lax.bitcast_convert_type(y[:, half:].astype(_BF16).astype(_F32), jnp.uint32)
    return (lo >> 16) | (hi & jnp.uint32(0xFFFF0000))


def _unpack_bf16_pairs(w):
    lo = lax.bitcast_convert_type(w << 16, _F32)
    hi = lax.bitcast_convert_type(w & jnp.uint32(0xFFFF0000), _F32)
    return jnp.concatenate([lo, hi], axis=1).astype(_BF16)


def _proj_kernel(x_ref, w_f32_ref, scale_ref, o_ref, w_ref):
    @pl.when(pl.program_id(0) == 0)
    def _():
        w_ref[...] = w_f32_ref[...].astype(_BF16)

    o_ref[...] = (jnp.dot(x_ref[...].astype(_BF16), w_ref[...], preferred_element_type=_F32)
                  * scale_ref[...]).astype(o_ref.dtype)


def _proj(x, w, col_scale):
    n, d = x.shape
    width = w.shape[1]
    return pl.pallas_call(
        _proj_kernel,
        grid=(n // PROJ_ROWS,),
        in_specs=[pl.BlockSpec((PROJ_ROWS, d), lambda i: (i, 0)), _weight_spec((d, width)),
                  _const_spec((1, width))],
        out_specs=pl.BlockSpec((PROJ_ROWS, width), lambda i: (i, 0)),
        out_shape=jax.ShapeDtypeStruct((n, width), _BF16),
        scratch_shapes=[pltpu.VMEM((d, width), _BF16)],
        compiler_params=_params("arbitrary"),
        name="proj",
    )(x, w, col_scale.reshape(1, width))


def _proj_res_ln_kernel(a_ref, w_f32_ref, res_ref, g_ref, b_ref, o_ref, words_ref, w_ref):
    @pl.when(pl.program_id(0) == 0)
    def _():
        w_ref[...] = w_f32_ref[...].astype(_BF16)

    m = jnp.dot(a_ref[...], w_ref[...], preferred_element_type=_F32)
    y = _layer_norm_rows(DEEPNORM_ALPHA * res_ref[...] + m, g_ref[...], b_ref[...])
    o_ref[...] = y
    words_ref[...] = _pack_bf16_pairs(y)


def _proj_res_ln(a_bf16, w, res, g, b):
    n, d = res.shape
    k = a_bf16.shape[1]
    row = lambda i: (i, 0)
    return pl.pallas_call(
        _proj_res_ln_kernel,
        grid=(n // PROJ_ROWS,),
        in_specs=[pl.BlockSpec((PROJ_ROWS, k), row), _weight_spec((k, d)),
                  pl.BlockSpec((PROJ_ROWS, d), row), _const_spec((1, d)), _const_spec((1, d))],
        out_specs=[pl.BlockSpec((PROJ_ROWS, d), row), pl.BlockSpec((PROJ_ROWS, d // 2), row)],
        out_shape=[jax.ShapeDtypeStruct((n, d), _F32), jax.ShapeDtypeStruct((n, d // 2), jnp.uint32)],
        scratch_shapes=[pltpu.VMEM((k, d), _BF16)],
        compiler_params=_params("arbitrary"),
        name="proj_res_ln",
    )(a_bf16, w, res, g.reshape(1, d), b.reshape(1, d))


def _l0_block_kernel(x_ref, w_in_f32_ref, cos_ref, sin_ref, xi_ref, zeta_ref, decay_ref, gstate_ref,
                     gn_g_ref, conv_w_ref, conv_b_ref, cln_g_ref, cln_b_ref, w_out_f32_ref, ln_g_ref,
                     ln_b_ref, o_ref, words_ref, state_ref, u_ref, shift_ref, h_ref, mix_ref,
                     w_in_ref, w_out_ref):
    rows = MIX_ROWS
    step = pl.program_id(1)
    v_off = 2 * RET_QK_WIDTH
    g_off = v_off + RET_WIDTH
    a_off = g_off + RET_WIDTH

    @pl.when(jnp.logical_and(pl.program_id(0) == 0, step == 0))
    def _():
        w_in_ref[...] = w_in_f32_ref[...].astype(_BF16)
        w_out_ref[...] = w_out_f32_ref[...].astype(_BF16)

    @pl.when(step == 0)
    def _():
        state_ref[...] = jnp.zeros_like(state_ref)
        u_ref[0:CONV_HALO, :] = jnp.zeros((CONV_HALO, CONV_CH), _F32)

    x_bf16 = x_ref[...].astype(_BF16)
    h_conv = jnp.dot(x_bf16, w_in_ref[:, a_off:a_off + 2 * CONV_CH], preferred_element_type=_F32)
    h_ref[...] = jnp.dot(x_bf16, w_in_ref[:, 0:a_off], preferred_element_type=_F32)

    u_ref[CONV_HALO:CONV_HALO + rows, :] = h_conv[:, :CONV_CH] * jax.nn.sigmoid(h_conv[:, CONV_CH:])
    acc = jnp.broadcast_to(conv_b_ref[...], (rows, CONV_CH))
    first = CONV_HALO - (CONV_WIDTH - 1)
    for phase in range(F32_SUBLANES):
        offsets = [first + j for j in range(CONV_WIDTH) if (first + j) % F32_SUBLANES == phase]
        span = max(offsets) - phase + rows
        if phase == 0:
            src = u_ref
        else:
            shift_ref[0:span, :] = u_ref[phase:phase + span, :]
            src = shift_ref
        for off in offsets:
            j = off - first
            acc = acc + conv_w_ref[j:j + 1, :] * src[off - phase:off - phase + rows, :]
    u_ref[0:CONV_HALO, :] = u_ref[rows:rows + CONV_HALO, :]
    c = _silu(_layer_norm_rows(acc, cln_g_ref[...], cln_b_ref[...]))
    mix_ref[:, RET_WIDTH:RET_WIDTH + CONV_CH] = c.astype(_BF16)

    lane = lax.broadcasted_iota(jnp.int32, (rows, RET_QK_WIDTH), 1)
    first_half = (lane % RET_DK) < (RET_DK // 2)
    cos = cos_ref[...]
    sin = sin_ref[...]

    def rotary(t):
        partner = jnp.where(first_half,
                            pltpu.roll(t, RET_QK_WIDTH - RET_DK // 2, axis=1),
                            pltpu.roll(t, RET_DK // 2, axis=1))
        return t * cos + partner * sin

    q = rotary(h_ref[:, 0:RET_QK_WIDTH])
    k = rotary(h_ref[:, RET_QK_WIDTH:2 * RET_QK_WIDTH]) * (RET_DK ** -0.5)
    q_in = (q * xi_ref[...]).astype(_BF16)
    k_out = (k * zeta_ref[...]).astype(_BF16)
    q = q.astype(_BF16)
    k = k.astype(_BF16)
    for hd in range(RET_HEADS):
        qk = slice(hd * RET_DK, (hd + 1) * RET_DK)
        vs = slice(hd * RET_DV, (hd + 1) * RET_DV)
        v = h_ref[:, v_off + hd * RET_DV:v_off + (hd + 1) * RET_DV].astype(_BF16)
        s = lax.dot_general(q[:, qk], k[:, qk], (((1,), (1,)), ((), ())),
                            preferred_element_type=_F32) * decay_ref[hd]
        r = jnp.dot(s.astype(_BF16), v, preferred_element_type=_F32)
        r += jnp.dot(q_in[:, qk], state_ref[hd].astype(_BF16), preferred_element_type=_F32)
        kv = lax.dot_general(k_out[:, qk], v, (((0,), (0,)), ((), ())),
                             preferred_element_type=_F32)
        state_ref[hd] = gstate_ref[hd] * state_ref[hd] + kv
        mu = jnp.mean(r, axis=-1, keepdims=True)
        rc = r - mu
        var = jnp.mean(rc * rc, axis=-1, keepdims=True)
        rn = rc * lax.rsqrt(var + LN_EPS) * gn_g_ref[:, vs]
        gate = h_ref[:, g_off + hd * RET_DV:g_off + (hd + 1) * RET_DV]
        mix_ref[:, vs] = (_silu(gate) * rn).astype(_BF16)

    m = jnp.dot(mix_ref[...], w_out_ref[...], preferred_element_type=_F32)
    y = _layer_norm_rows(DEEPNORM_ALPHA * x_ref[...] + m, ln_g_ref[...], ln_b_ref[...])
    o_ref[...] = y
    words_ref[...] = _pack_bf16_pairs(y)


def _l0_mixer_tables(seq):
    half = RET_DK // 2
    inv = ROPE_BASE ** (-jnp.arange(half, dtype=_F32) / half)
    ang = jnp.arange(seq).astype(_F32)[:, None] * inv[None, :]
    cos = jnp.tile(jnp.cos(ang), (1, 2 * RET_HEADS))
    sin = jnp.tile(jnp.concatenate([-jnp.sin(ang), jnp.sin(ang)], axis=1), (1, RET_HEADS))
    log_g = jnp.log(1.0 - 2.0 ** (-5.0 - jnp.arange(RET_HEADS, dtype=_F32)))
    n = jnp.arange(MIX_ROWS, dtype=_F32)
    diff = n[:, None] - n[None, :]
    chunk = jnp.arange(MIX_ROWS) // CHUNK
    expo = jnp.where(chunk[:, None] == chunk[None, :], jnp.abs(diff), diff)
    decay = jnp.where((chunk[:, None] >= chunk[None, :])[None],
                      jnp.exp(expo[None] * log_g[:, None, None]), 0.0)
    xi = jnp.repeat(jnp.exp((n[:, None] + 1.0) * log_g[None, :]), RET_DK, axis=1)
    zeta = jnp.repeat(jnp.exp((MIX_ROWS - 1.0 - n[:, None]) * log_g[None, :]), RET_DK, axis=1)
    g_state = jnp.exp(MIX_ROWS * log_g)
    return cos, sin, xi, zeta, decay, g_state


def _l0_block(x, batch, seq, w_in, gn_g, conv_w, conv_b, cln_g, cln_b, w_out, ln_g, ln_b):
    n, d = x.shape
    steps = seq // MIX_ROWS
    cos, sin, xi, zeta, decay, g_state = _l0_mixer_tables(seq)
    row = lambda b, i: (b * steps + i, 0)
    pos = lambda b, i: (i, 0)
    vec = lambda t: t.reshape(1, -1)
    return pl.pallas_call(
        _l0_block_kernel,
        grid=(batch, steps),
        in_specs=[pl.BlockSpec((MIX_ROWS, d), row), _weight_spec((d, L0_IN)),
                  pl.BlockSpec((MIX_ROWS, RET_QK_WIDTH), pos),
                  pl.BlockSpec((MIX_ROWS, RET_QK_WIDTH), pos),
                  _const_spec((MIX_ROWS, RET_QK_WIDTH)), _const_spec((MIX_ROWS, RET_QK_WIDTH)),
                  _const_spec((RET_HEADS, MIX_ROWS, MIX_ROWS)),
                  pl.BlockSpec(memory_space=pltpu.SMEM),
                  _const_spec((1, RET_WIDTH)), _const_spec((CONV_WIDTH, CONV_CH)),
                  _const_spec((1, CONV_CH)), _const_spec((1, CONV_CH)), _const_spec((1, CONV_CH)),
                  _weight_spec((RET_WIDTH + CONV_CH, d)), _const_spec((1, d)), _const_spec((1, d))],
        out_specs=[pl.BlockSpec((MIX_ROWS, d), row), pl.BlockSpec((MIX_ROWS, d // 2), row)],
        out_shape=[jax.ShapeDtypeStruct((n, d), _F32), jax.ShapeDtypeStruct((n, d // 2), jnp.uint32)],
        scratch_shapes=[pltpu.VMEM((RET_HEADS, RET_DK, RET_DV), _F32),
                        pltpu.VMEM((MIX_ROWS + CONV_HALO, CONV_CH), _F32),
                        pltpu.VMEM((MIX_ROWS + CONV_HALO, CONV_CH), _F32),
                        pltpu.VMEM((MIX_ROWS, 2 * RET_QK_WIDTH + 2 * RET_WIDTH), _F32),
                        pltpu.VMEM((MIX_ROWS, RET_WIDTH + CONV_CH), _BF16),
                        pltpu.VMEM((d, L0_IN), _BF16), pltpu.VMEM((RET_WIDTH + CONV_CH, d), _BF16)],
        compiler_params=_params("arbitrary", "arbitrary"),
        name="l0_block",
    )(x, w_in, cos, sin, xi, zeta, decay, g_state, vec(gn_g), conv_w, vec(conv_b),
      vec(cln_g), vec(cln_b), w_out, vec(ln_g), vec(ln_b))


def _sb_attention_kernel(q_ref, k_ref, v_ref, tri_ref, o_ref, acc_ref, carry_ref):
    blk = SB_BLOCK
    t_idx = lax.broadcasted_iota(jnp.int32, (blk, blk), 0)
    s_idx = lax.broadcasted_iota(jnp.int32, (blk, blk), 1)
    keep = s_idx < t_idx

    heads = [slice(hd * SB_HEAD_DIM, (hd + 1) * SB_HEAD_DIM) for hd in range(SB_HEADS)]

    def sweep(q_rows, blocks, carry):
        starts = [pl.multiple_of(j * blk, blk) for j, _ in blocks]
        zs = [[lax.dot_general(q_ref[q_rows, hs], k_ref[pl.ds(st, blk), hs], (((1,), (1,)), ((), ())),
                               preferred_element_type=_F32) for hs in heads] for st in starts]
        log_betas, addends = [], []
        for (_, mask), z_heads in zip(blocks, zs):
            for z in z_heads:
                softplus = jnp.log2(1.0 + jnp.exp2(-jnp.abs(z)))
                log_beta = jnp.minimum(z, 0.0) - softplus
                log_rest = log_beta - z
                if mask is not None:
                    log_rest = jnp.where(mask, log_rest, 0.0)
                log_betas.append(log_beta)
                addends.append(log_rest.astype(_BF16))
        sums = [jnp.dot(t, tri_ref[...], preferred_element_type=_F32) for t in addends]
        weights, total = [], None
        for b, (_, mask) in enumerate(blocks):
            block_sums = sums[b * SB_HEADS:(b + 1) * SB_HEADS]
            before = carry if total is None else (total if carry is None else carry + total)
            for hd, s in enumerate(block_sums):
                expo = log_betas[b * SB_HEADS + hd] + s[:, :blk]
                if before is not None:
                    expo = expo + before[:, heads[hd]]
                a = jnp.exp2(expo)
                if mask is not None:
                    a = jnp.where(mask, a, 0.0)
                weights.append(a.astype(_BF16))
            block_total = jnp.concatenate([s[:, blk:] for s in block_sums], axis=1)
            total = block_total if total is None else total + block_total
        outs = []
        for hd, hs in enumerate(heads):
            o = None
            for b, st in enumerate(starts):
                t = jnp.dot(weights[b * SB_HEADS + hd], v_ref[pl.ds(st, blk), hs],
                            preferred_element_type=_F32)
                o = t if o is None else o + t
            outs.append(o)
        return jnp.concatenate(outs, axis=1), total

    def query_block(sub):
        q_rows = slice(sub * blk, (sub + 1) * blk)
        qi = pl.program_id(1) * SB_STEP_BLOCKS + sub
        out, carry = sweep(q_rows, [(qi, keep)], None)
        acc_ref[...] = out
        carry_ref[...] = carry

        def cond(state):
            j, alive = state
            return jnp.logical_and(j >= 0, alive)

        def body(state):
            j, _ = state
            carry = carry_ref[...]
            out, total = sweep(q_rows, [(j, None), (jnp.maximum(j - 1, 0), j >= 1)], carry)
            carry = carry + total
            acc_ref[...] += out
            carry_ref[...] = carry
            return j - 2, jnp.max(carry) > -SB_CUTOFF

        lax.while_loop(cond, body, (qi - 1, jnp.max(carry) > -SB_CUTOFF))
        o_ref[q_rows, :] = acc_ref[...].astype(o_ref.dtype)

    for sub in range(SB_STEP_BLOCKS):
        query_block(sub)


def _sb_attention(qkv, batch, seq):
    n = qkv.shape[0]
    rows = SB_STEP_BLOCKS * SB_BLOCK
    nq = seq // rows
    j = jnp.arange(SB_BLOCK)[:, None]
    s = jnp.arange(SB_BLOCK)[None, :]
    tri = jnp.concatenate([(j > s).astype(_BF16), jnp.ones((SB_BLOCK, SB_BLOCK), _BF16)], axis=1)
    whole_seq = lambda col: pl.BlockSpec((seq, D_MODEL), lambda b, i: (b, col),
                                         pipeline_mode=pl.Buffered(1))
    return pl.pallas_call(
        _sb_attention_kernel,
        grid=(batch, nq),
        in_specs=[pl.BlockSpec((rows, D_MODEL), lambda b, i: (b * nq + i, 0)),
                  whole_seq(1), whole_seq(2), _const_spec((SB_BLOCK, 2 * SB_BLOCK))],
        out_specs=pl.BlockSpec((rows, D_MODEL), lambda b, i: (b * nq + i, 0)),
        out_shape=jax.ShapeDtypeStruct((n, D_MODEL), _BF16),
        scratch_shapes=[pltpu.VMEM((SB_BLOCK, D_MODEL), _F32), pltpu.VMEM((SB_BLOCK, D_MODEL), _F32)],
        compiler_params=_params("arbitrary", "arbitrary"),
        name="sb_attention",
    )(qkv, qkv, qkv, tri)


def _router_kernel(x_ref, w_ref, b_ref, tri_ref, idx_ref, gate_ref, rank_ref, count_ref, base_ref):
    @pl.when(pl.program_id(0) == 0)
    def _():
        base_ref[...] = jnp.zeros_like(base_ref)

    x = x_ref[...]
    w = w_ref[...]
    x_hi = x.astype(_BF16)
    x_lo = (x - x_hi.astype(_F32)).astype(_BF16)
    w_hi = w.astype(_BF16)
    w_lo = (w - w_hi.astype(_F32)).astype(_BF16)
    nt = (((1,), (1,)), ((), ()))
    logits = (lax.dot_general(w_hi, x_hi, nt, preferred_element_type=_F32)
              + lax.dot_general(w_lo, x_hi, nt, preferred_element_type=_F32)
              + lax.dot_general(w_hi, x_lo, nt, preferred_element_type=_F32)) + b_ref[...]
    expert = lax.broadcasted_iota(jnp.int32, logits.shape, 0)
    vals, idxs = [], []
    for _ in range(TOP_K):
        m = jnp.max(logits, axis=0, keepdims=True)
        sel = jnp.min(jnp.where(logits == m, expert, N_EXPERTS), axis=0, keepdims=True)
        vals.append(m)
        idxs.append(sel)
        logits = jnp.where(expert == sel, -jnp.inf, logits)
    vals = jnp.concatenate(vals, axis=0)
    e = jnp.exp(vals - vals[0:1])
    gate_ref[...] = e / jnp.sum(e, axis=0, keepdims=True)
    idx_ref[...] = jnp.concatenate(idxs, axis=0)

    member = jnp.zeros(logits.shape, _F32)
    for sel in idxs:
        member += (expert == sel).astype(_F32)
    before = jnp.dot(member.astype(_BF16), tri_ref[...], preferred_element_type=_F32) + base_ref[...]
    ranks = [jnp.sum(jnp.where(expert == sel, before, 0.0), axis=0, keepdims=True) for sel in idxs]
    rank_ref[...] = jnp.concatenate(ranks, axis=0).astype(jnp.int32)
    base_ref[...] += jnp.sum(member, axis=1, keepdims=True)
    count_ref[...] = base_ref[...]


def _router(x, router_w, router_b):
    n, d = x.shape
    t = jnp.arange(ROUTER_ROWS)
    tri = (t[:, None] < t[None, :]).astype(_BF16)
    tok = pl.BlockSpec((TOP_K, ROUTER_ROWS), lambda i: (0, i))
    return pl.pallas_call(
        _router_kernel,
        grid=(n // ROUTER_ROWS,),
        in_specs=[pl.BlockSpec((ROUTER_ROWS, d), lambda i: (i, 0)),
                  _const_spec((N_EXPERTS, d)), _const_spec((N_EXPERTS, 1)),
                  _const_spec((ROUTER_ROWS, ROUTER_ROWS))],
        out_specs=[tok, tok, tok, _const_spec((N_EXPERTS, 1))],
        out_shape=[jax.ShapeDtypeStruct((TOP_K, n), jnp.int32),
                   jax.ShapeDtypeStruct((TOP_K, n), _F32),
                   jax.ShapeDtypeStruct((TOP_K, n), jnp.int32),
                   jax.ShapeDtypeStruct((N_EXPERTS, 1), _F32)],
        scratch_shapes=[pltpu.VMEM((N_EXPERTS, 1), _F32)],
        compiler_params=_params("arbitrary"),
        name="router",
    )(x, router_w.T, router_b.reshape(N_EXPERTS, 1), tri)


def _prep_up_kernel(w_ref, sel_ref, o_ref):
    group = 2 * PREP_LANES
    for c in range(w_ref.shape[2] // group):
        t = jnp.dot(w_ref[0, :, c * group:(c + 1) * group].astype(_BF16), sel_ref[...],
                    preferred_element_type=_F32)
        o_ref[0, :, c * PREP_LANES:(c + 1) * PREP_LANES] = t[:, :PREP_LANES].astype(_BF16)
        o_ref[0, :, D_FF + c * PREP_LANES:D_FF + (c + 1) * PREP_LANES] = t[:, PREP_LANES:].astype(_BF16)


def _prep_up(w_up):
    e, d, ff2 = w_up.shape
    i = jnp.arange(2 * PREP_LANES)[:, None]
    j = jnp.arange(2 * PREP_LANES)[None, :]
    sel = (i == jnp.where(j < PREP_LANES, 2 * j, 2 * (j - PREP_LANES) + 1)).astype(_BF16)
    blk = lambda x, r: (x, r, 0)
    return pl.pallas_call(
        _prep_up_kernel,
        grid=(e, d // PREP_ROWS),
        in_specs=[pl.BlockSpec((1, PREP_ROWS, ff2), blk), _const_spec((2 * PREP_LANES, 2 * PREP_LANES))],
        out_specs=pl.BlockSpec((1, PREP_ROWS, ff2), blk),
        out_shape=jax.ShapeDtypeStruct((e, d, ff2), _BF16),
        compiler_params=_params("arbitrary", "arbitrary"),
        name="prep_up",
    )(w_up, sel)


def _dispatch(x_words, slot_of, p):
    n, words = x_words.shape
    part = words // DISPATCH_PARTS
    mesh = plsc.VectorSubcoreMesh(core_axis_name="core", subcore_axis_name="subcore",
                                  num_cores=SC_CORES, num_subcores=SC_SUBCORES)
    out = jax.ShapeDtypeStruct((p, part), x_words.dtype)

    @functools.partial(
        pl.kernel, out_type=[out] * DISPATCH_PARTS, mesh=mesh, scratch_types=[],
        compiler_params=pltpu.CompilerParams(use_tc_tiling_on_sc=True), name="moe_dispatch")
    def dispatch(x_hbm, slot_hbm, *o_hbm):
        for c in range(DISPATCH_PARTS):
            def body(x_vmem, slot_vmem, o_ref=o_hbm[c]):
                for k in range(TOP_K):
                    pltpu.sync_copy(x_vmem, o_ref.at[slot_vmem.at[k]])

            pltpu.emit_pipeline(
                body,
                grid=(n // DISPATCH_ROWS,),
                in_specs=[pl.BlockSpec((DISPATCH_ROWS, part), lambda i, c=c: (i, c)),
                          pl.BlockSpec((TOP_K, DISPATCH_ROWS), lambda i: (0, i))],
                out_specs=[],
                core_axis_name=("core", "subcore"),
                dimension_semantics=(pltpu.PARALLEL,),
            )(x_hbm, slot_hbm)

    return dispatch(x_words, slot_of.reshape(TOP_K, n))


def _combine_gather(y_parts, slot_of):
    nk = slot_of.shape[0]
    part = y_parts[0].shape[1]
    mesh = plsc.VectorSubcoreMesh(core_axis_name="core", subcore_axis_name="subcore",
                                  num_cores=SC_CORES, num_subcores=SC_SUBCORES)
    out = jax.ShapeDtypeStruct((nk, part), y_parts[0].dtype)

    @functools.partial(
        pl.kernel, out_type=[out] * DISPATCH_PARTS, mesh=mesh, scratch_types=[],
        compiler_params=pltpu.CompilerParams(use_tc_tiling_on_sc=True), name="moe_combine_gather")
    def gather(*refs):
        y_hbm, slot_hbm, o_hbm = refs[:DISPATCH_PARTS], refs[DISPATCH_PARTS], refs[DISPATCH_PARTS + 1:]
        for c in range(DISPATCH_PARTS):
            def body(slot_vmem, o_vmem, y_ref=y_hbm[c]):
                pltpu.sync_copy(y_ref.at[slot_vmem.at[0]], o_vmem)

            pltpu.emit_pipeline(
                body,
                grid=(nk // DISPATCH_ROWS,),
                in_specs=[pl.BlockSpec((1, DISPATCH_ROWS), lambda i: (0, i))],
                out_specs=[pl.BlockSpec((DISPATCH_ROWS, part), lambda i: (i, 0))],
                core_axis_name=("core", "subcore"),
                dimension_semantics=(pltpu.PARALLEL,),
            )(slot_hbm, o_hbm[c])

    return gather(*y_parts, slot_of.reshape(1, nk))


def _moe_ffn_kernel(block_e_ref, block_rows_ref, last_block_ref, xa_ref, xb_ref, wu_ref, bu_ref,
                    wd_f32_ref, bd_ref, oa_ref, ob_ref, wd_ref):
    step = pl.program_id(0)
    rows_used = block_rows_ref[step]
    part = oa_ref.shape[1]
    new_expert = jnp.logical_or(step == 0,
                                block_e_ref[step] != block_e_ref[jnp.maximum(step - 1, 0)])

    @pl.when(jnp.logical_and(new_expert, rows_used > 0))
    def _():
        wd_ref[...] = wd_f32_ref[0].astype(_BF16)

    def ffn_rows(*ranges):
        xs = [_unpack_bf16_pairs(jnp.concatenate([xa_ref[lo:hi, :], xb_ref[lo:hi, :]], axis=1))
              for lo, hi in ranges]
        hs = [jnp.dot(x, wu_ref[0], preferred_element_type=_F32) + bu_ref[0] for x in xs]
        acts = []
        for h in hs:
            h_glu = jnp.minimum(h[:, :D_FF], SWIGLU_LIMIT)
            h_lin = jnp.clip(h[:, D_FF:], -SWIGLU_LIMIT, SWIGLU_LIMIT)
            acts.append((h_glu * jax.nn.sigmoid(SWIGLU_ALPHA * h_glu) * (h_lin + 1.0)).astype(_BF16))
        ys = [jnp.dot(a, wd_ref[...], preferred_element_type=_F32) + bd_ref[0] for a in acts]
        for (lo, hi), y in zip(ranges, ys):
            words = _pack_bf16_pairs(y)
            oa_ref[lo:hi, :] = words[:, :part]
            ob_ref[lo:hi, :] = words[:, part:]

    def zero_rows(lo, hi):
        oa_ref[lo:hi, :] = jnp.zeros((hi - lo, part), oa_ref.dtype)
        ob_ref[lo:hi, :] = jnp.zeros((hi - lo, part), ob_ref.dtype)

    sub_blocks = [(lo, lo + MOE_ROWS) for lo in range(0, MOE_STEP_ROWS, MOE_ROWS)]
    whole_step = rows_used > MOE_STEP_ROWS - MOE_ROWS + MOE_TAIL_ROWS[-1]

    @pl.when(whole_step)
    def _():
        ffn_rows(*sub_blocks)

    @pl.when(jnp.logical_not(whole_step))
    def _():
        for lo, hi in sub_blocks:
            sizes = list(MOE_TAIL_ROWS) + [MOE_ROWS]
            for smaller, size in zip([0] + sizes, sizes):
                fits = rows_used > lo + smaller
                if size < MOE_ROWS:
                    fits = jnp.logical_and(fits, rows_used <= lo + size)

                @pl.when(fits)
                def _(size=size):
                    ffn_rows((lo, lo + size))
                    if size < MOE_ROWS:
                        zero_rows(lo + size, hi)

            @pl.when(rows_used <= lo)
            def _():
                zero_rows(lo, hi)


def _moe_ffn(xs, block_e, block_rows, last_block, wu, bu, wd_f32, bd):
    xa, xb = xs
    p, part = xa.shape
    ff, d = wd_f32.shape[1:]
    ff2 = wu.shape[2]
    row = lambda i, be, br, lb: (jnp.minimum(i, lb[0]), 0)
    exp3 = lambda i, be, br, lb: (be[i], 0, 0)
    rows_spec = pl.BlockSpec((MOE_STEP_ROWS, part), row)
    grid_spec = pltpu.PrefetchScalarGridSpec(
        num_scalar_prefetch=3,
        grid=(p // MOE_STEP_ROWS,),
        in_specs=[rows_spec, rows_spec,
                  pl.BlockSpec((1, d, ff2), exp3), pl.BlockSpec((1, 1, ff2), exp3),
                  pl.BlockSpec((1, ff, d), exp3), pl.BlockSpec((1, 1, d), exp3)],
        out_specs=[rows_spec, rows_spec],
        scratch_shapes=[pltpu.VMEM((ff, d), _BF16)],
    )
    out = jax.ShapeDtypeStruct((p, part), xa.dtype)
    return pl.pallas_call(
        _moe_ffn_kernel,
        grid_spec=grid_spec,
        out_shape=[out, out],
        compiler_params=_params("arbitrary"),
        name="moe_ffn",
    )(block_e, block_rows, last_block, xa, xb, wu, bu, wd_f32, bd)


def _combine_ln_kernel(ya_ref, yb_ref, gate_ref, res_ref, g_ref, b_ref, o_ref, *maybe_obf_ref):
    gate = gate_ref[...]
    moe = None
    for k in range(TOP_K):
        words = jnp.concatenate([ya_ref[k], yb_ref[k]], axis=1)
        term = _unpack_bf16_pairs(words).astype(_F32) * gate[:, k:k + 1]
        moe = term if moe is None else moe + term
    y = _layer_norm_rows(DEEPNORM_ALPHA * res_ref[...] + moe, g_ref[...], b_ref[...])
    o_ref[...] = y
    for obf_ref in maybe_obf_ref:
        obf_ref[...] = y.astype(_BF16)


def _combine_ln(y4_parts, gates, res, g, b, with_bf16):
    n, d = res.shape
    ya, yb = (t.reshape(TOP_K, n, t.shape[1]) for t in y4_parts)
    part = ya.shape[2]
    row = lambda i: (i, 0)
    y_spec = pl.BlockSpec((TOP_K, COMBINE_ROWS, part), lambda i: (0, i, 0))
    return pl.pallas_call(
        _combine_ln_kernel,
        grid=(n // COMBINE_ROWS,),
        in_specs=[y_spec, y_spec, pl.BlockSpec((COMBINE_ROWS, TOP_K), row),
                  pl.BlockSpec((COMBINE_ROWS, d), row), _const_spec((1, d)), _const_spec((1, d))],
        out_specs=[pl.BlockSpec((COMBINE_ROWS, d), row)] * (2 if with_bf16 else 1),
        out_shape=[jax.ShapeDtypeStruct((n, d), _F32)]
        + ([jax.ShapeDtypeStruct((n, d), _BF16)] if with_bf16 else []),
        compiler_params=_params("arbitrary"),
        name="combine_ln",
    )(ya, yb, gates, res, g.reshape(1, d), b.reshape(1, d))


def _moe_layer(x, x_words, router_w, router_b, w_up, b_up, w_down, b_down, ln_g, ln_b, with_bf16):
    n, d = x.shape
    nk = n * TOP_K
    idx_t, gate_t, rank_t, count = _router(x, router_w, router_b)
    counts = count[:, 0].astype(jnp.int32)
    padded = (counts + MOE_STEP_ROWS - 1) // MOE_STEP_ROWS * MOE_STEP_ROWS
    padded_ends = jnp.cumsum(padded)
    padded_starts = padded_ends - padded
    n_blocks = -(-(nk + N_EXPERTS * (MOE_STEP_ROWS - 1)) // MOE_STEP_ROWS)
    p = n_blocks * MOE_STEP_ROWS
    short = counts % MOE_STEP_ROWS
    experts = jnp.arange(N_EXPERTS, dtype=jnp.int32)
    chosen = idx_t[None] == experts[:, None, None]
    start_of = jnp.sum(jnp.where(chosen, padded_starts[:, None, None], 0), axis=0)
    short_of = jnp.sum(jnp.where(chosen, short[:, None, None], 0), axis=0)
    gap_of = jnp.where(jnp.logical_and(short_of > 0, rank_t >= short_of), MOE_STEP_ROWS - short_of, 0)
    slot_of = (start_of + rank_t + gap_of).reshape(nk)
    block_start = jnp.arange(n_blocks, dtype=jnp.int32) * MOE_STEP_ROWS
    block_e = jnp.minimum(jnp.sum(padded_ends[None, :] <= block_start[:, None], axis=1),
                          N_EXPERTS - 1).astype(jnp.int32)
    first_of_expert = block_start == padded_starts[block_e]
    block_rows = jnp.where(block_start >= padded_ends[-1], 0,
                           jnp.where(jnp.logical_and(first_of_expert, short[block_e] > 0),
                                     short[block_e], MOE_STEP_ROWS)).astype(jnp.int32)
    last_block = jnp.minimum(padded_ends[-1] // MOE_STEP_ROWS, n_blocks - 1).astype(jnp.int32)
    xs = _dispatch(x_words, slot_of, p)
    bu = b_up.reshape(N_EXPERTS, D_FF, 2).transpose(0, 2, 1).reshape(N_EXPERTS, 1, 2 * D_FF)
    ys = _moe_ffn(xs, block_e, block_rows, last_block.reshape(1), _prep_up(w_up), bu, w_down,
                  b_down.reshape(N_EXPERTS, 1, d))
    return _combine_ln(_combine_gather(ys, slot_of), gate_t.T, x, ln_g, ln_b, with_bf16)


def kernel(x, l0_w_in, l0_ret_gn_g, l0_conv_w, l0_conv_b, l0_conv_ln_g, l0_conv_ln_b, l0_w_out, l0_ln1_g, l0_ln1_b, l0_router_w, l0_router_b, l0_w_up, l0_b_up, l0_w_down, l0_b_down, l0_ln2_g, l0_ln2_b, l1_w_in, l1_w_out, l1_ln1_g, l1_ln1_b, l1_router_w, l1_router_b, l1_w_up, l1_b_up, l1_w_down, l1_b_down, l1_ln2_g, l1_ln2_b):
    batch, seq, d = x.shape
    n = batch * seq
    x0 = x.reshape(n, d)

    x1, x1_words = _l0_block(x0, batch, seq, l0_w_in, l0_ret_gn_g, l0_conv_w, l0_conv_b,
                             l0_conv_ln_g, l0_conv_ln_b, l0_w_out, l0_ln1_g, l0_ln1_b)
    x2, x2_bf16 = _moe_layer(x1, x1_words, l0_router_w, l0_router_b, l0_w_up, l0_b_up, l0_w_down,
                             l0_b_down, l0_ln2_g, l0_ln2_b, True)

    q_scale = jnp.where(jnp.arange(3 * d) < d, LOG2_E * SB_HEAD_DIM ** -0.5, 1.0).astype(_F32)
    qkv = _proj(x2_bf16, l1_w_in, q_scale)
    att = _sb_attention(qkv, batch, seq)
    x3, x3_words = _proj_res_ln(att, l1_w_out, x2, l1_ln1_g, l1_ln1_b)
    (x4,) = _moe_layer(x3, x3_words, l1_router_w, l1_router_b, l1_w_up, l1_b_up, l1_w_down,
                       l1_b_down, l1_ln2_g, l1_ln2_b, False)
    return x4.reshape(batch, seq, d)
```

```python
import functools

import jax
import jax.numpy as jnp
from jax import lax
from jax.experimental import pallas as pl
from jax.experimental.pallas import tpu as pltpu
from jax.experimental.pallas import tpu_sc as plsc

D_MODEL = 1024
CHUNK = 64
RET_HEADS = 4
RET_DK = 64
RET_DV = 128
RET_QK_WIDTH = RET_HEADS * RET_DK
RET_WIDTH = RET_HEADS * RET_DV
ROPE_BASE = 10000.0
CONV_CH = D_MODEL // 2
CONV_WIDTH = 31
L0_IN = 2 * RET_QK_WIDTH + 2 * RET_WIDTH + 2 * CONV_CH
SB_HEADS = 8
SB_HEAD_DIM = D_MODEL // SB_HEADS
N_EXPERTS = 32
TOP_K = 4
D_FF = D_MODEL
SWIGLU_LIMIT = 7.0
SWIGLU_ALPHA = 1.702
LN_EPS = 1e-5
DEPTH = 2
DEEPNORM_ALPHA = (2 * DEPTH) ** 0.25

VMEM_LIMIT_BYTES = 48 * 1024 * 1024
MOE_VMEM_LIMIT_BYTES = 56 * 1024 * 1024
F32_SUBLANES = 8
PROJ_ROWS = 1024
MIX_ROWS = 256
CONV_HALO = 32
ROUTER_ROWS = 1024
MOE_ROWS = 512
MOE_TAIL_ROWS = 256
MOE_STEP_ROWS = 1024
PREP_LANES = 128
COMBINE_ROWS = 512
DISPATCH_ROWS = 128
DISPATCH_PARTS = 2
SC_CORES = 2
SC_SUBCORES = 16
SB_BLOCK = 128
SB_STEP_BLOCKS = 2
SB_CUTOFF = 150.0
LOG2_E = 1.4426950408889634

_F32 = jnp.float32
_BF16 = jnp.bfloat16


def _params(*sem):
    return pltpu.CompilerParams(dimension_semantics=sem, vmem_limit_bytes=VMEM_LIMIT_BYTES)


def _const_spec(shape):
    nd = len(shape)
    return pl.BlockSpec(shape, lambda *_: (0,) * nd)


def _weight_spec(shape):
    nd = len(shape)
    return pl.BlockSpec(shape, lambda *_: (0,) * nd, pipeline_mode=pl.Buffered(1))


def _layer_norm_rows(y, g, b):
    mu = jnp.mean(y, axis=-1, keepdims=True)
    yc = y - mu
    var = jnp.mean(yc * yc, axis=-1, keepdims=True)
    return yc * lax.rsqrt(var + LN_EPS) * g + b


def _silu(x):
    return x * jax.nn.sigmoid(x)


def _pack_bf16_pairs(y):
    half = y.shape[1] // 2
    lo = lax.bitcast_convert_type(y[:, :half].astype(_BF16).astype(_F32), jnp.uint32)
    hi = lax.bitcast_convert_type(y[:, half:].astype(_BF16).astype(_F32), jnp.uint32)
    return (lo >> 16) | (hi & jnp.uint32(0xFFFF0000))


def _unpack_bf16_pairs(w):
    lo = lax.bitcast_convert_type(w << 16, _F32)
    hi = lax.bitcast_convert_type(w & jnp.uint32(0xFFFF0000), _F32)
    return jnp.concatenate([lo, hi], axis=1).astype(_BF16)


def _proj_kernel(x_ref, w_f32_ref, scale_ref, o_ref, w_ref):
    @pl.when(pl.program_id(0) == 0)
    def _():
        w_ref[...] = w_f32_ref[...].astype(_BF16)

    o_ref[...] = (jnp.dot(x_ref[...].astype(_BF16), w_ref[...], preferred_element_type=_F32)
                  * scale_ref[...]).astype(o_ref.dtype)


def _proj(x, w, col_scale):
    n, d = x.shape
    width = w.shape[1]
    return pl.pallas_call(
        _proj_kernel,
        grid=(n // PROJ_ROWS,),
        in_specs=[pl.BlockSpec((PROJ_ROWS, d), lambda i: (i, 0)), _weight_spec((d, width)),
                  _const_spec((1, width))],
        out_specs=pl.BlockSpec((PROJ_ROWS, width), lambda i: (i, 0)),
        out_shape=jax.ShapeDtypeStruct((n, width), _BF16),
        scratch_shapes=[pltpu.VMEM((d, width), _BF16)],
        compiler_params=_params("arbitrary"),
        name="proj",
    )(x, w, col_scale.reshape(1, width))


def _proj_res_ln_kernel(a_ref, w_f32_ref, res_ref, g_ref, b_ref, o_ref, words_ref, w_ref):
    @pl.when(pl.program_id(0) == 0)
    def _():
        w_ref[...] = w_f32_ref[...].astype(_BF16)

    m = jnp.dot(a_ref[...], w_ref[...], preferred_element_type=_F32)
    y = _layer_norm_rows(DEEPNORM_ALPHA * res_ref[...] + m, g_ref[...], b_ref[...])
    o_ref[...] = y
    words_ref[...] = _pack_bf16_pairs(y)


def _proj_res_ln(a_bf16, w, res, g, b):
    n, d = res.shape
    k = a_bf16.shape[1]
    row = lambda i: (i, 0)
    return pl.pallas_call(
        _proj_res_ln_kernel,
        grid=(n // PROJ_ROWS,),
        in_specs=[pl.BlockSpec((PROJ_ROWS, k), row), _weight_spec((k, d)),
                  pl.BlockSpec((PROJ_ROWS, d), row), _const_spec((1, d)), _const_spec((1, d))],
        out_specs=[pl.BlockSpec((PROJ_ROWS, d), row), pl.BlockSpec((PROJ_ROWS, d // 2), row)],
        out_shape=[jax.ShapeDtypeStruct((n, d), _F32), jax.ShapeDtypeStruct((n, d // 2), jnp.uint32)],
        scratch_shapes=[pltpu.VMEM((k, d), _BF16)],
        compiler_params=_params("arbitrary"),
        name="proj_res_ln",
    )(a_bf16, w, res, g.reshape(1, d), b.reshape(1, d))


def _l0_block_kernel(x_ref, w_in_f32_ref, cos_ref, sin_ref, xi_ref, zeta_ref, decay_ref, gstate_ref,
                     gn_g_ref, conv_w_ref, conv_b_ref, cln_g_ref, cln_b_ref, w_out_f32_ref, ln_g_ref,
                     ln_b_ref, o_ref, words_ref, state_ref, u_ref, shift_ref, h_ref, mix_ref,
                     w_in_ref, w_out_ref):
    rows = MIX_ROWS
    step = pl.program_id(1)
    v_off = 2 * RET_QK_WIDTH
    g_off = v_off + RET_WIDTH
    a_off = g_off + RET_WIDTH

    @pl.when(jnp.logical_and(pl.program_id(0) == 0, step == 0))
    def _():
        w_in_ref[...] = w_in_f32_ref[...].astype(_BF16)
        w_out_ref[...] = w_out_f32_ref[...].astype(_BF16)

    @pl.when(step == 0)
    def _():
        state_ref[...] = jnp.zeros_like(state_ref)
        u_ref[0:CONV_HALO, :] = jnp.zeros((CONV_HALO, CONV_CH), _F32)

    x_bf16 = x_ref[...].astype(_BF16)
    h_conv = jnp.dot(x_bf16, w_in_ref[:, a_off:a_off + 2 * CONV_CH], preferred_element_type=_F32)
    h_ref[...] = jnp.dot(x_bf16, w_in_ref[:, 0:a_off], preferred_element_type=_F32)

    u_ref[CONV_HALO:CONV_HALO + rows, :] = h_conv[:, :CONV_CH] * jax.nn.sigmoid(h_conv[:, CONV_CH:])
    acc = jnp.broadcast_to(conv_b_ref[...], (rows, CONV_CH))
    first = CONV_HALO - (CONV_WIDTH - 1)
    for phase in range(F32_SUBLANES):
        offsets = [first + j for j in range(CONV_WIDTH) if (first + j) % F32_SUBLANES == phase]
        span = max(offsets) - phase + rows
        if phase == 0:
            src = u_ref
        else:
            shift_ref[0:span, :] = u_ref[phase:phase + span, :]
            src = shift_ref
        for off in offsets:
            j = off - first
            acc = acc + conv_w_ref[j:j + 1, :] * src[off - phase:off - phase + rows, :]
    u_ref[0:CONV_HALO, :] = u_ref[rows:rows + CONV_HALO, :]
    c = _silu(_layer_norm_rows(acc, cln_g_ref[...], cln_b_ref[...]))
    mix_ref[:, RET_WIDTH:RET_WIDTH + CONV_CH] = c.astype(_BF16)

    lane = lax.broadcasted_iota(jnp.int32, (rows, RET_QK_WIDTH), 1)
    first_half = (lane % RET_DK) < (RET_DK // 2)
    cos = cos_ref[...]
    sin = sin_ref[...]

    def rotary(t):
        partner = jnp.where(first_half,
                            pltpu.roll(t, RET_QK_WIDTH - RET_DK // 2, axis=1),
                            pltpu.roll(t, RET_DK // 2, axis=1))
        return t * cos + partner * sin

    q = rotary(h_ref[:, 0:RET_QK_WIDTH])
    k = rotary(h_ref[:, RET_QK_WIDTH:2 * RET_QK_WIDTH]) * (RET_DK ** -0.5)
    q_in = (q * xi_ref[...]).astype(_BF16)
    k_out = (k * zeta_ref[...]).astype(_BF16)
    q = q.astype(_BF16)
    k = k.astype(_BF16)
    for hd in range(RET_HEADS):
        qk = slice(hd * RET_DK, (hd + 1) * RET_DK)
        vs = slice(hd * RET_DV, (hd + 1) * RET_DV)
        v = h_ref[:, v_off + hd * RET_DV:v_off + (hd + 1) * RET_DV].astype(_BF16)
        s = lax.dot_general(q[:, qk], k[:, qk], (((1,), (1,)), ((), ())),
                            preferred_element_type=_F32) * decay_ref[hd]
        r = jnp.dot(s.astype(_BF16), v, preferred_element_type=_F32)
        r += jnp.dot(q_in[:, qk], state_ref[hd].astype(_BF16), preferred_element_type=_F32)
        kv = lax.dot_general(k_out[:, qk], v, (((0,), (0,)), ((), ())),
                             preferred_element_type=_F32)
        state_ref[hd] = gstate_ref[hd] * state_ref[hd] + kv
        mu = jnp.mean(r, axis=-1, keepdims=True)
        rc = r - mu
        var = jnp.mean(rc * rc, axis=-1, keepdims=True)
        rn = rc * lax.rsqrt(var + LN_EPS) * gn_g_ref[:, vs]
        gate = h_ref[:, g_off + hd * RET_DV:g_off + (hd + 1) * RET_DV]
        mix_ref[:, vs] = (_silu(gate) * rn).astype(_BF16)

    m = jnp.dot(mix_ref[...], w_out_ref[...], preferred_element_type=_F32)
    y = _layer_norm_rows(DEEPNORM_ALPHA * x_ref[...] + m, ln_g_ref[...], ln_b_ref[...])
    o_ref[...] = y
    words_ref[...] = _pack_bf16_pairs(y)


def _l0_mixer_tables(seq):
    half = RET_DK // 2
    inv = ROPE_BASE ** (-jnp.arange(half, dtype=_F32) / half)
    ang = jnp.arange(seq).astype(_F32)[:, None] * inv[None, :]
    cos = jnp.tile(jnp.cos(ang), (1, 2 * RET_HEADS))
    sin = jnp.tile(jnp.concatenate([-jnp.sin(ang), jnp.sin(ang)], axis=1), (1, RET_HEADS))
    log_g = jnp.log(1.0 - 2.0 ** (-5.0 - jnp.arange(RET_HEADS, dtype=_F32)))
    n = jnp.arange(MIX_ROWS, dtype=_F32)
    diff = n[:, None] - n[None, :]
    chunk = jnp.arange(MIX_ROWS) // CHUNK
    expo = jnp.where(chunk[:, None] == chunk[None, :], jnp.abs(diff), diff)
    decay = jnp.where((chunk[:, None] >= chunk[None, :])[None],
                      jnp.exp(expo[None] * log_g[:, None, None]), 0.0)
    xi = jnp.repeat(jnp.exp((n[:, None] + 1.0) * log_g[None, :]), RET_DK, axis=1)
    zeta = jnp.repeat(jnp.exp((MIX_ROWS - 1.0 - n[:, None]) * log_g[None, :]), RET_DK, axis=1)
    g_state = jnp.exp(MIX_ROWS * log_g)
    return cos, sin, xi, zeta, decay, g_state


def _l0_block(x, batch, seq, w_in, gn_g, conv_w, conv_b, cln_g, cln_b, w_out, ln_g, ln_b):
    n, d = x.shape
    steps = seq // MIX_ROWS
    cos, sin, xi, zeta, decay, g_state = _l0_mixer_tables(seq)
    row = lambda b, i: (b * steps + i, 0)
    pos = lambda b, i: (i, 0)
    vec = lambda t: t.reshape(1, -1)
    return pl.pallas_call(
        _l0_block_kernel,
        grid=(batch, steps),
        in_specs=[pl.BlockSpec((MIX_ROWS, d), row), _weight_spec((d, L0_IN)),
                  pl.BlockSpec((MIX_ROWS, RET_QK_WIDTH), pos),
                  pl.BlockSpec((MIX_ROWS, RET_QK_WIDTH), pos),
                  _const_spec((MIX_ROWS, RET_QK_WIDTH)), _const_spec((MIX_ROWS, RET_QK_WIDTH)),
                  _const_spec((RET_HEADS, MIX_ROWS, MIX_ROWS)),
                  pl.BlockSpec(memory_space=pltpu.SMEM),
                  _const_spec((1, RET_WIDTH)), _const_spec((CONV_WIDTH, CONV_CH)),
                  _const_spec((1, CONV_CH)), _const_spec((1, CONV_CH)), _const_spec((1, CONV_CH)),
                  _weight_spec((RET_WIDTH + CONV_CH, d)), _const_spec((1, d)), _const_spec((1, d))],
        out_specs=[pl.BlockSpec((MIX_ROWS, d), row), pl.BlockSpec((MIX_ROWS, d // 2), row)],
        out_shape=[jax.ShapeDtypeStruct((n, d), _F32), jax.ShapeDtypeStruct((n, d // 2), jnp.uint32)],
        scratch_shapes=[pltpu.VMEM((RET_HEADS, RET_DK, RET_DV), _F32),
                        pltpu.VMEM((MIX_ROWS + CONV_HALO, CONV_CH), _F32),
                        pltpu.VMEM((MIX_ROWS + CONV_HALO, CONV_CH), _F32),
                        pltpu.VMEM((MIX_ROWS, 2 * RET_QK_WIDTH + 2 * RET_WIDTH), _F32),
                        pltpu.VMEM((MIX_ROWS, RET_WIDTH + CONV_CH), _BF16),
                        pltpu.VMEM((d, L0_IN), _BF16), pltpu.VMEM((RET_WIDTH + CONV_CH, d), _BF16)],
        compiler_params=_params("arbitrary", "arbitrary"),
        name="l0_block",
    )(x, w_in, cos, sin, xi, zeta, decay, g_state, vec(gn_g), conv_w, vec(conv_b),
      vec(cln_g), vec(cln_b), w_out, vec(ln_g), vec(ln_b))


def _sb_attention_kernel(q_ref, k_ref, v_ref, tri_ref, o_ref, acc_ref, carry_ref):
    blk = SB_BLOCK
    t_idx = lax.broadcasted_iota(jnp.int32, (blk, blk), 0)
    s_idx = lax.broadcasted_iota(jnp.int32, (blk, blk), 1)
    keep = s_idx < t_idx

    heads = [slice(hd * SB_HEAD_DIM, (hd + 1) * SB_HEAD_DIM) for hd in range(SB_HEADS)]

    def sweep(q_rows, blocks, carry):
        starts = [pl.multiple_of(j * blk, blk) for j, _ in blocks]
        zs = [[lax.dot_general(q_ref[q_rows, hs], k_ref[pl.ds(st, blk), hs], (((1,), (1,)), ((), ())),
                               preferred_element_type=_F32) for hs in heads] for st in starts]
        log_betas, addends = [], []
        for (_, mask), z_heads in zip(blocks, zs):
            for z in z_heads:
                softplus = jnp.log2(1.0 + jnp.exp2(-jnp.abs(z)))
                log_beta = jnp.minimum(z, 0.0) - softplus
                log_rest = log_beta - z
                if mask is not None:
                    log_rest = jnp.where(mask, log_rest, 0.0)
                log_betas.append(log_beta)
                addends.append(log_rest.astype(_BF16))
        sums = [jnp.dot(t, tri_ref[...], preferred_element_type=_F32) for t in addends]
        weights, total = [], None
        for b, (_, mask) in enumerate(blocks):
            block_sums = sums[b * SB_HEADS:(b + 1) * SB_HEADS]
            before = carry if total is None else (total if carry is None else carry + total)
            for hd, s in enumerate(block_sums):
                expo = log_betas[b * SB_HEADS + hd] + s[:, :blk]
                if before is not None:
                    expo = expo + before[:, heads[hd]]
                a = jnp.exp2(expo)
                if mask is not None:
                    a = jnp.where(mask, a, 0.0)
                weights.append(a.astype(_BF16))
            block_total = jnp.concatenate([s[:, blk:] for s in block_sums], axis=1)
            total = block_total if total is None else total + block_total
        outs = []
        for hd, hs in enumerate(heads):
            o = None
            for b, st in enumerate(starts):
                t = jnp.dot(weights[b * SB_HEADS + hd], v_ref[pl.ds(st, blk), hs],
                            preferred_element_type=_F32)
                o = t if o is None else o + t
            outs.append(o)
        return jnp.concatenate(outs, axis=1), total

    def query_block(sub):
        q_rows = slice(sub * blk, (sub + 1) * blk)
        qi = pl.program_id(1) * SB_STEP_BLOCKS + sub
        out, carry = sweep(q_rows, [(qi, keep)], None)
        acc_ref[...] = out
        carry_ref[...] = carry

        def cond(state):
            j, alive = state
            return jnp.logical_and(j >= 0, alive)

        def body(state):
            j, _ = state
            carry = carry_ref[...]
            out, total = sweep(q_rows, [(j, None), (jnp.maximum(j - 1, 0), j >= 1)], carry)
            carry = carry + total
            acc_ref[...] += out
            carry_ref[...] = carry
            return j - 2, jnp.max(carry) > -SB_CUTOFF

        lax.while_loop(cond, body, (qi - 1, jnp.max(carry) > -SB_CUTOFF))
        o_ref[q_rows, :] = acc_ref[...].astype(o_ref.dtype)

    for sub in range(SB_STEP_BLOCKS):
        query_block(sub)


def _sb_attention(qkv, batch, seq):
    n = qkv.shape[0]
    rows = SB_STEP_BLOCKS * SB_BLOCK
    nq = seq // rows
    j = jnp.arange(SB_BLOCK)[:, None]
    s = jnp.arange(SB_BLOCK)[None, :]
    tri = jnp.concatenate([(j > s).astype(_BF16), jnp.ones((SB_BLOCK, SB_BLOCK), _BF16)], axis=1)
    whole_seq = lambda col: pl.BlockSpec((seq, D_MODEL), lambda b, i: (b, col),
                                         pipeline_mode=pl.Buffered(1))
    return pl.pallas_call(
        _sb_attention_kernel,
        grid=(batch, nq),
        in_specs=[pl.BlockSpec((rows, D_MODEL), lambda b, i: (b * nq + i, 0)),
                  whole_seq(1), whole_seq(2), _const_spec((SB_BLOCK, 2 * SB_BLOCK))],
        out_specs=pl.BlockSpec((rows, D_MODEL), lambda b, i: (b * nq + i, 0)),
        out_shape=jax.ShapeDtypeStruct((n, D_MODEL), _BF16),
        scratch_shapes=[pltpu.VMEM((SB_BLOCK, D_MODEL), _F32), pltpu.VMEM((SB_BLOCK, D_MODEL), _F32)],
        compiler_params=_params("arbitrary", "arbitrary"),
        name="sb_attention",
    )(qkv, qkv, qkv, tri)


def _router_kernel(x_ref, w_ref, b_ref, tri_ref, idx_ref, gate_ref, rank_ref, count_ref, base_ref):
    @pl.when(pl.program_id(0) == 0)
    def _():
        base_ref[...] = jnp.zeros_like(base_ref)

    x = x_ref[...]
    w = w_ref[...]
    x_hi = x.astype(_BF16)
    x_lo = (x - x_hi.astype(_F32)).astype(_BF16)
    w_hi = w.astype(_BF16)
    w_lo = (w - w_hi.astype(_F32)).astype(_BF16)
    nt = (((1,), (1,)), ((), ()))
    logits = (lax.dot_general(w_hi, x_hi, nt, preferred_element_type=_F32)
              + lax.dot_general(w_lo, x_hi, nt, preferred_element_type=_F32)
              + lax.dot_general(w_hi, x_lo, nt, preferred_element_type=_F32)) + b_ref[...]
    expert = lax.broadcasted_iota(jnp.int32, logits.shape, 0)
    vals, idxs = [], []
    for _ in range(TOP_K):
        m = jnp.max(logits, axis=0, keepdims=True)
        sel = jnp.min(jnp.where(logits == m, expert, N_EXPERTS), axis=0, keepdims=True)
        vals.append(m)
        idxs.append(sel)
        logits = jnp.where(expert == sel, -jnp.inf, logits)
    vals = jnp.concatenate(vals, axis=0)
    e = jnp.exp(vals - vals[0:1])
    gate_ref[...] = e / jnp.sum(e, axis=0, keepdims=True)
    idx_ref[...] = jnp.concatenate(idxs, axis=0)

    member = jnp.zeros(logits.shape, _F32)
    for sel in idxs:
        member += (expert == sel).astype(_F32)
    before = jnp.dot(member.astype(_BF16), tri_ref[...], preferred_element_type=_F32) + base_ref[...]
    ranks = [jnp.sum(jnp.where(expert == sel, before, 0.0), axis=0, keepdims=True) for sel in idxs]
    rank_ref[...] = jnp.concatenate(ranks, axis=0).astype(jnp.int32)
    base_ref[...] += jnp.sum(member, axis=1, keepdims=True)
    count_ref[...] = base_ref[...]


def _router(x, router_w, router_b):
    n, d = x.shape
    t = jnp.arange(ROUTER_ROWS)
    tri = (t[:, None] < t[None, :]).astype(_BF16)
    tok = pl.BlockSpec((TOP_K, ROUTER_ROWS), lambda i: (0, i))
    return pl.pallas_call(
        _router_kernel,
        grid=(n // ROUTER_ROWS,),
        in_specs=[pl.BlockSpec((ROUTER_ROWS, d), lambda i: (i, 0)),
                  _const_spec((N_EXPERTS, d)), _const_spec((N_EXPERTS, 1)),
                  _const_spec((ROUTER_ROWS, ROUTER_ROWS))],
        out_specs=[tok, tok, tok, _const_spec((N_EXPERTS, 1))],
        out_shape=[jax.ShapeDtypeStruct((TOP_K, n), jnp.int32),
                   jax.ShapeDtypeStruct((TOP_K, n), _F32),
                   jax.ShapeDtypeStruct((TOP_K, n), jnp.int32),
                   jax.ShapeDtypeStruct((N_EXPERTS, 1), _F32)],
        scratch_shapes=[pltpu.VMEM((N_EXPERTS, 1), _F32)],
        compiler_params=_params("arbitrary"),
        name="router",
    )(x, router_w.T, router_b.reshape(N_EXPERTS, 1), tri)


def _dispatch(x_words, slot_of, p):
    n, words = x_words.shape
    part = words // DISPATCH_PARTS
    mesh = plsc.VectorSubcoreMesh(core_axis_name="core", subcore_axis_name="subcore",
                                  num_cores=SC_CORES, num_subcores=SC_SUBCORES)
    out = jax.ShapeDtypeStruct((p, part), x_words.dtype)

    @functools.partial(
        pl.kernel, out_type=[out] * DISPATCH_PARTS, mesh=mesh, scratch_types=[],
        compiler_params=pltpu.CompilerParams(use_tc_tiling_on_sc=True), name="moe_dispatch")
    def dispatch(x_hbm, slot_hbm, *o_hbm):
        for c in range(DISPATCH_PARTS):
            def body(x_vmem, slot_vmem, o_ref=o_hbm[c]):
                for k in range(TOP_K):
                    pltpu.sync_copy(x_vmem, o_ref.at[slot_vmem.at[k]])

            pltpu.emit_pipeline(
                body,
                grid=(n // DISPATCH_ROWS,),
                in_specs=[pl.BlockSpec((DISPATCH_ROWS, part), lambda i, c=c: (i, c)),
                          pl.BlockSpec((TOP_K, DISPATCH_ROWS), lambda i: (0, i))],
                out_specs=[],
                core_axis_name=("core", "subcore"),
                dimension_semantics=(pltpu.PARALLEL,),
            )(x_hbm, slot_hbm)

    return dispatch(x_words, slot_of.reshape(TOP_K, n))


def _combine_gather(y_parts, slot_of):
    nk = slot_of.shape[0]
    part = y_parts[0].shape[1]
    mesh = plsc.VectorSubcoreMesh(core_axis_name="core", subcore_axis_name="subcore",
                                  num_cores=SC_CORES, num_subcores=SC_SUBCORES)
    out = jax.ShapeDtypeStruct((nk, part), y_parts[0].dtype)

    @functools.partial(
        pl.kernel, out_type=[out] * DISPATCH_PARTS, mesh=mesh, scratch_types=[],
        compiler_params=pltpu.CompilerParams(use_tc_tiling_on_sc=True), name="moe_combine_gather")
    def gather(*refs):
        y_hbm, slot_hbm, o_hbm = refs[:DISPATCH_PARTS], refs[DISPATCH_PARTS], refs[DISPATCH_PARTS + 1:]
        for c in range(DISPATCH_PARTS):
            def body(slot_vmem, o_vmem, y_ref=y_hbm[c]):
                pltpu.sync_copy(y_ref.at[slot_vmem.at[0]], o_vmem)

            pltpu.emit_pipeline(
                body,
                grid=(nk // DISPATCH_ROWS,),
                in_specs=[pl.BlockSpec((1, DISPATCH_ROWS), lambda i: (0, i))],
                out_specs=[pl.BlockSpec((DISPATCH_ROWS, part), lambda i: (i, 0))],
                core_axis_name=("core", "subcore"),
                dimension_semantics=(pltpu.PARALLEL,),
            )(slot_hbm, o_hbm[c])

    return gather(*y_parts, slot_of.reshape(1, nk))


def _moe_ffn_kernel(block_e_ref, block_rows_ref, last_block_ref, xa_ref, xb_ref, wu_f32_ref, sel_ref,
                    bu_ref, wd_f32_ref, bd_ref, oa_ref, ob_ref, wu_ref, wd_ref):
    step = pl.program_id(0)
    rows_used = block_rows_ref[step]
    part = oa_ref.shape[1]
    new_expert = jnp.logical_or(step == 0,
                                block_e_ref[step] != block_e_ref[jnp.maximum(step - 1, 0)])

    @pl.when(jnp.logical_and(new_expert, rows_used > 0))
    def _():
        wd_ref[...] = wd_f32_ref[0].astype(_BF16)
        group = 2 * PREP_LANES
        for c in range(wu_f32_ref.shape[2] // group):
            t = jnp.dot(wu_f32_ref[0, :, c * group:(c + 1) * group].astype(_BF16), sel_ref[...],
                        preferred_element_type=_F32)
            wu_ref[:, c * PREP_LANES:(c + 1) * PREP_LANES] = t[:, :PREP_LANES].astype(_BF16)
            wu_ref[:, D_FF + c * PREP_LANES:D_FF + (c + 1) * PREP_LANES] = t[:, PREP_LANES:].astype(_BF16)

    def ffn_rows(*ranges):
        xs = [_unpack_bf16_pairs(jnp.concatenate([xa_ref[lo:hi, :], xb_ref[lo:hi, :]], axis=1))
              for lo, hi in ranges]
        hs = [jnp.dot(x, wu_ref[...], preferred_element_type=_F32) + bu_ref[0] for x in xs]
        acts = []
        for h in hs:
            h_glu = jnp.minimum(h[:, :D_FF], SWIGLU_LIMIT)
            h_lin = jnp.clip(h[:, D_FF:], -SWIGLU_LIMIT, SWIGLU_LIMIT)
            acts.append((h_glu * jax.nn.sigmoid(SWIGLU_ALPHA * h_glu) * (h_lin + 1.0)).astype(_BF16))
        ys = [jnp.dot(a, wd_ref[...], preferred_element_type=_F32) + bd_ref[0] for a in acts]
        for (lo, hi), y in zip(ranges, ys):
            words = _pack_bf16_pairs(y)
            oa_ref[lo:hi, :] = words[:, :part]
            ob_ref[lo:hi, :] = words[:, part:]

    def zero_rows(lo, hi):
        oa_ref[lo:hi, :] = jnp.zeros((hi - lo, part), oa_ref.dtype)
        ob_ref[lo:hi, :] = jnp.zeros((hi - lo, part), ob_ref.dtype)

    sub_blocks = [(lo, lo + MOE_ROWS) for lo in range(0, MOE_STEP_ROWS, MOE_ROWS)]
    whole_step = rows_used > MOE_STEP_ROWS - MOE_ROWS + MOE_TAIL_ROWS

    @pl.when(whole_step)
    def _():
        ffn_rows(*sub_blocks)

    @pl.when(jnp.logical_not(whole_step))
    def _():
        for lo, hi in sub_blocks:
            mid = lo + MOE_TAIL_ROWS

            @pl.when(rows_used > mid)
            def _():
                ffn_rows((lo, hi))

            @pl.when(jnp.logical_and(rows_used > lo, rows_used <= mid))
            def _():
                ffn_rows((lo, mid))
                zero_rows(mid, hi)

            @pl.when(rows_used <= lo)
            def _():
                zero_rows(lo, hi)


def _moe_ffn(xs, block_e, block_rows, last_block, wu_f32, bu, wd_f32, bd):
    xa, xb = xs
    p, part = xa.shape
    ff, d = wd_f32.shape[1:]
    ff2 = wu_f32.shape[2]
    lane = jnp.arange(2 * PREP_LANES)
    sel = (lane[:, None] == jnp.where(lane < PREP_LANES, 2 * lane, 2 * (lane - PREP_LANES) + 1)[None, :]
           ).astype(_BF16)
    row = lambda i, be, br, lb: (jnp.minimum(i, lb[0]), 0)
    exp3 = lambda i, be, br, lb: (be[i], 0, 0)
    rows_spec = pl.BlockSpec((MOE_STEP_ROWS, part), row)
    grid_spec = pltpu.PrefetchScalarGridSpec(
        num_scalar_prefetch=3,
        grid=(p // MOE_STEP_ROWS,),
        in_specs=[rows_spec, rows_spec,
                  pl.BlockSpec((1, d, ff2), exp3),
                  pl.BlockSpec((2 * PREP_LANES, 2 * PREP_LANES), lambda i, be, br, lb: (0, 0)),
                  pl.BlockSpec((1, 1, ff2), exp3),
                  pl.BlockSpec((1, ff, d), exp3), pl.BlockSpec((1, 1, d), exp3)],
        out_specs=[rows_spec, rows_spec],
        scratch_shapes=[pltpu.VMEM((d, ff2), _BF16), pltpu.VMEM((ff, d), _BF16)],
    )
    out = jax.ShapeDtypeStruct((p, part), xa.dtype)
    return pl.pallas_call(
        _moe_ffn_kernel,
        grid_spec=grid_spec,
        out_shape=[out, out],
        compiler_params=pltpu.CompilerParams(dimension_semantics=("arbitrary",),
                                             vmem_limit_bytes=MOE_VMEM_LIMIT_BYTES),
        name="moe_ffn",
    )(block_e, block_rows, last_block, xa, xb, wu_f32, sel, bu, wd_f32, bd)


def _combine_ln_kernel(ya_ref, yb_ref, gate_ref, res_ref, g_ref, b_ref, o_ref, *maybe_obf_ref):
    gate = gate_ref[...]
    moe = None
    for k in range(TOP_K):
        words = jnp.concatenate([ya_ref[k], yb_ref[k]], axis=1)
        term = _unpack_bf16_pairs(words).astype(_F32) * gate[:, k:k + 1]
        moe = term if moe is None else moe + term
    y = _layer_norm_rows(DEEPNORM_ALPHA * res_ref[...] + moe, g_ref[...], b_ref[...])
    o_ref[...] = y
    for obf_ref in maybe_obf_ref:
        obf_ref[...] = y.astype(_BF16)


def _combine_ln(y4_parts, gates, res, g, b, with_bf16):
    n, d = res.shape
    ya, yb = (t.reshape(TOP_K, n, t.shape[1]) for t in y4_parts)
    part = ya.shape[2]
    row = lambda i: (i, 0)
    y_spec = pl.BlockSpec((TOP_K, COMBINE_ROWS, part), lambda i: (0, i, 0))
    return pl.pallas_call(
        _combine_ln_kernel,
        grid=(n // COMBINE_ROWS,),
        in_specs=[y_spec, y_spec, pl.BlockSpec((COMBINE_ROWS, TOP_K), row),
                  pl.BlockSpec((COMBINE_ROWS, d), row), _const_spec((1, d)), _const_spec((1, d))],
        out_specs=[pl.BlockSpec((COMBINE_ROWS, d), row)] * (2 if with_bf16 else 1),
        out_shape=[jax.ShapeDtypeStruct((n, d), _F32)]
        + ([jax.ShapeDtypeStruct((n, d), _BF16)] if with_bf16 else []),
        compiler_params=_params("arbitrary"),
        name="combine_ln",
    )(ya, yb, gates, res, g.reshape(1, d), b.reshape(1, d))


def _moe_layer(x, x_words, router_w, router_b, w_up, b_up, w_down, b_down, ln_g, ln_b, with_bf16):
    n, d = x.shape
    nk = n * TOP_K
    idx_t, gate_t, rank_t, count = _router(x, router_w, router_b)
    counts = count[:, 0].astype(jnp.int32)
    padded = (counts + MOE_STEP_ROWS - 1) // MOE_STEP_ROWS * MOE_STEP_ROWS
    padded_ends = jnp.cumsum(padded)
    padded_starts = padded_ends - padded
    n_blocks = -(-(nk + N_EXPERTS * (MOE_STEP_ROWS - 1)) // MOE_STEP_ROWS)
    p = n_blocks * MOE_STEP_ROWS
    short = counts % MOE_STEP_ROWS
    experts = jnp.arange(N_EXPERTS, dtype=jnp.int32)
    chosen = idx_t[None] == experts[:, None, None]
    start_of = jnp.sum(jnp.where(chosen, padded_starts[:, None, None], 0), axis=0)
    short_of = jnp.sum(jnp.where(chosen, short[:, None, None], 0), axis=0)
    gap_of = jnp.where(jnp.logical_and(short_of > 0, rank_t >= short_of), MOE_STEP_ROWS - short_of, 0)
    slot_of = (start_of + rank_t + gap_of).reshape(nk)
    block_start = jnp.arange(n_blocks, dtype=jnp.int32) * MOE_STEP_ROWS
    block_e = jnp.minimum(jnp.sum(padded_ends[None, :] <= block_start[:, None], axis=1),
                          N_EXPERTS - 1).astype(jnp.int32)
    first_of_expert = block_start == padded_starts[block_e]
    block_rows = jnp.where(block_start >= padded_ends[-1], 0,
                           jnp.where(jnp.logical_and(first_of_expert, short[block_e] > 0),
                                     short[block_e], MOE_STEP_ROWS)).astype(jnp.int32)
    last_block = jnp.minimum(padded_ends[-1] // MOE_STEP_ROWS, n_blocks - 1).astype(jnp.int32)
    xs = _dispatch(x_words, slot_of, p)
    bu = b_up.reshape(N_EXPERTS, D_FF, 2).transpose(0, 2, 1).reshape(N_EXPERTS, 1, 2 * D_FF)
    ys = _moe_ffn(xs, block_e, block_rows, last_block.reshape(1), w_up, bu, w_down,
                  b_down.reshape(N_EXPERTS, 1, d))
    return _combine_ln(_combine_gather(ys, slot_of), gate_t.T, x, ln_g, ln_b, with_bf16)


def kernel(x, l0_w_in, l0_ret_gn_g, l0_conv_w, l0_conv_b, l0_conv_ln_g, l0_conv_ln_b, l0_w_out, l0_ln1_g, l0_ln1_b, l0_router_w, l0_router_b, l0_w_up, l0_b_up, l0_w_down, l0_b_down, l0_ln2_g, l0_ln2_b, l1_w_in, l1_w_out, l1_ln1_g, l1_ln1_b, l1_router_w, l1_router_b, l1_w_up, l1_b_up, l1_w_down, l1_b_down, l1_ln2_g, l1_ln2_b):
    batch, seq, d = x.shape
    n = batch * seq
    x0 = x.reshape(n, d)

    x1, x1_words = _l0_block(x0, batch, seq, l0_w_in, l0_ret_gn_g, l0_conv_w, l0_conv_b,
                             l0_conv_ln_g, l0_conv_ln_b, l0_w_out, l0_ln1_g, l0_ln1_b)
    x2, x2_bf16 = _moe_layer(x1, x1_words, l0_router_w, l0_router_b, l0_w_up, l0_b_up, l0_w_down,
                             l0_b_down, l0_ln2_g, l0_ln2_b, True)

    q_scale = jnp.where(jnp.arange(3 * d) < d, LOG2_E * SB_HEAD_DIM ** -0.5, 1.0).astype(_F32)
    qkv = _proj(x2_bf16, l1_w_in, q_scale)
    att = _sb_attention(qkv, batch, seq)
    x3, x3_words = _proj_res_ln(att, l1_w_out, x2, l1_ln1_g, l1_ln1_b)
    (x4,) = _moe_layer(x3, x3_words, l1_router_w, l1_router_b, l1_w_up, l1_b_up, l1_w_down,
                       l1_b_down, l1_ln2_g, l1_ln2_b, False)
    return x4.reshape(batch, seq, d)
```

```python
import functools

import jax
import jax.numpy as jnp
from jax import lax
from jax.experimental import pallas as pl
from jax.experimental.pallas import tpu as pltpu
from jax.experimental.pallas import tpu_sc as plsc

D_MODEL = 1024
CHUNK = 64
RET_HEADS = 4
RET_DK = 64
RET_DV = 128
RET_QK_WIDTH = RET_HEADS * RET_DK
RET_WIDTH = RET_HEADS * RET_DV
ROPE_BASE = 10000.0
CONV_CH = D_MODEL // 2
CONV_WIDTH = 31
L0_IN = 2 * RET_QK_WIDTH + 2 * RET_WIDTH + 2 * CONV_CH
SB_HEADS = 8
SB_HEAD_DIM = D_MODEL // SB_HEADS
N_EXPERTS = 32
TOP_K = 4
D_FF = D_MODEL
SWIGLU_LIMIT = 7.0
SWIGLU_ALPHA = 1.702
LN_EPS = 1e-5
DEPTH = 2
DEEPNORM_ALPHA = (2 * DEPTH) ** 0.25

VMEM_LIMIT_BYTES = 48 * 1024 * 1024
MOE_VMEM_LIMIT_BYTES = 56 * 1024 * 1024
F32_SUBLANES = 8
PROJ_ROWS = 1024
MIX_ROWS = 256
CONV_HALO = 32
ROUTER_ROWS = 1024
MOE_ROWS = 512
MOE_TAIL_ROWS = 256
MOE_STEP_ROWS = 1024
PREP_LANES = 128
COMBINE_ROWS = 512
DISPATCH_ROWS = 128
DISPATCH_PARTS = 2
SC_CORES = 2
SC_SUBCORES = 16
SB_BLOCK = 128
SB_STEP_BLOCKS = 2
SB_CUTOFF = 150.0
LOG2_E = 1.4426950408889634

_F32 = jnp.float32
_BF16 = jnp.bfloat16


def _params(*sem, vmem_limit=VMEM_LIMIT_BYTES):
    return pltpu.CompilerParams(dimension_semantics=sem, vmem_limit_bytes=vmem_limit)


def _const_spec(shape):
    nd = len(shape)
    return pl.BlockSpec(shape, lambda *_: (0,) * nd)


def _weight_spec(shape):
    nd = len(shape)
    return pl.BlockSpec(shape, lambda *_: (0,) * nd, pipeline_mode=pl.Buffered(1))


def _layer_norm_rows(y, g, b):
    mu = jnp.mean(y, axis=-1, keepdims=True)
    yc = y - mu
    var = jnp.mean(yc * yc, axis=-1, keepdims=True)
    return yc * lax.rsqrt(var + LN_EPS) * g + b


def _silu(x):
    return x * jax.nn.sigmoid(x)


def _pack_bf16_pairs(y):
    half = y.shape[1] // 2
    lo = lax.bitcast_convert_type(y[:, :half].astype(_BF16).astype(_F32), jnp.uint32)
    hi = lax.bitcast_convert_type(y[:, half:].astype(_BF16).astype(_F32), jnp.uint32)
    return (lo >> 16) | (hi & jnp.uint32(0xFFFF0000))


def _unpack_bf16_pairs(w):
    lo = lax.bitcast_convert_type(w << 16, _F32)
    hi = lax.bitcast_convert_type(w & jnp.uint32(0xFFFF0000), _F32)
    return jnp.concatenate([lo, hi], axis=1).astype(_BF16)


def _proj_kernel(x_ref, w_f32_ref, scale_ref, o_ref, w_ref):
    @pl.when(pl.program_id(0) == 0)
    def _():
        w_ref[...] = w_f32_ref[...].astype(_BF16)

    o_ref[...] = (jnp.dot(x_ref[...].astype(_BF16), w_ref[...], preferred_element_type=_F32)
                  * scale_ref[...]).astype(o_ref.dtype)


def _proj(x, w, col_scale):
    n, d = x.shape
    width = w.shape[1]
    return pl.pallas_call(
        _proj_kernel,
        grid=(n // PROJ_ROWS,),
        in_specs=[pl.BlockSpec((PROJ_ROWS, d), lambda i: (i, 0)), _weight_spec((d, width)),
                  _const_spec((1, width))],
        out_specs=pl.BlockSpec((PROJ_ROWS, width), lambda i: (i, 0)),
        out_shape=jax.ShapeDtypeStruct((n, width), _BF16),
        scratch_shapes=[pltpu.VMEM((d, width), _BF16)],
        compiler_params=_params("arbitrary"),
        name="proj",
    )(x, w, col_scale.reshape(1, width))


def _proj_res_ln_kernel(a_ref, w_f32_ref, res_ref, g_ref, b_ref, o_ref, words_ref, w_ref):
    @pl.when(pl.program_id(0) == 0)
    def _():
        w_ref[...] = w_f32_ref[...].astype(_BF16)

    m = jnp.dot(a_ref[...], w_ref[...], preferred_element_type=_F32)
    y = _layer_norm_rows(DEEPNORM_ALPHA * res_ref[...] + m, g_ref[...], b_ref[...])
    o_ref[...] = y
    words_ref[...] = _pack_bf16_pairs(y)


def _proj_res_ln(a_bf16, w, res, g, b):
    n, d = res.shape
    k = a_bf16.shape[1]
    row = lambda i: (i, 0)
    return pl.pallas_call(
        _proj_res_ln_kernel,
        grid=(n // PROJ_ROWS,),
        in_specs=[pl.BlockSpec((PROJ_ROWS, k), row), _weight_spec((k, d)),
                  pl.BlockSpec((PROJ_ROWS, d), row), _const_spec((1, d)), _const_spec((1, d))],
        out_specs=[pl.BlockSpec((PROJ_ROWS, d), row), pl.BlockSpec((PROJ_ROWS, d // 2), row)],
        out_shape=[jax.ShapeDtypeStruct((n, d), _F32), jax.ShapeDtypeStruct((n, d // 2), jnp.uint32)],
        scratch_shapes=[pltpu.VMEM((k, d), _BF16)],
        compiler_params=_params("arbitrary"),
        name="proj_res_ln",
    )(a_bf16, w, res, g.reshape(1, d), b.reshape(1, d))


def _l0_block_kernel(x_ref, w_in_f32_ref, cos_ref, sin_ref, xi_ref, zeta_ref, decay_ref, gstate_ref,
                     gn_g_ref, conv_w_ref, conv_b_ref, cln_g_ref, cln_b_ref, w_out_f32_ref, ln_g_ref,
                     ln_b_ref, o_ref, words_ref, state_ref, u_ref, shift_ref, h_ref, mix_ref,
                     w_in_ref, w_out_ref):
    rows = MIX_ROWS
    step = pl.program_id(1)
    v_off = 2 * RET_QK_WIDTH
    g_off = v_off + RET_WIDTH
    a_off = g_off + RET_WIDTH

    @pl.when(jnp.logical_and(pl.program_id(0) == 0, step == 0))
    def _():
        w_in_ref[...] = w_in_f32_ref[...].astype(_BF16)
        w_out_ref[...] = w_out_f32_ref[...].astype(_BF16)

    @pl.when(step == 0)
    def _():
        state_ref[...] = jnp.zeros_like(state_ref)
        u_ref[0:CONV_HALO, :] = jnp.zeros((CONV_HALO, CONV_CH), _F32)

    x_bf16 = x_ref[...].astype(_BF16)
    h_conv = jnp.dot(x_bf16, w_in_ref[:, a_off:a_off + 2 * CONV_CH], preferred_element_type=_F32)
    h_ref[...] = jnp.dot(x_bf16, w_in_ref[:, 0:a_off], preferred_element_type=_F32)

    u_ref[CONV_HALO:CONV_HALO + rows, :] = h_conv[:, :CONV_CH] * jax.nn.sigmoid(h_conv[:, CONV_CH:])
    acc = jnp.broadcast_to(conv_b_ref[...], (rows, CONV_CH))
    first = CONV_HALO - (CONV_WIDTH - 1)
    for phase in range(F32_SUBLANES):
        offsets = [first + j for j in range(CONV_WIDTH) if (first + j) % F32_SUBLANES == phase]
        span = max(offsets) - phase + rows
        if phase == 0:
            src = u_ref
        else:
            shift_ref[0:span, :] = u_ref[phase:phase + span, :]
            src = shift_ref
        for off in offsets:
            j = off - first
            acc = acc + conv_w_ref[j:j + 1, :] * src[off - phase:off - phase + rows, :]
    u_ref[0:CONV_HALO, :] = u_ref[rows:rows + CONV_HALO, :]
    c = _silu(_layer_norm_rows(acc, cln_g_ref[...], cln_b_ref[...]))
    mix_ref[:, RET_WIDTH:RET_WIDTH + CONV_CH] = c.astype(_BF16)

    lane = lax.broadcasted_iota(jnp.int32, (rows, RET_QK_WIDTH), 1)
    first_half = (lane % RET_DK) < (RET_DK // 2)
    cos = cos_ref[...]
    sin = sin_ref[...]

    def rotary(t):
        partner = jnp.where(first_half,
                            pltpu.roll(t, RET_QK_WIDTH - RET_DK // 2, axis=1),
                            pltpu.roll(t, RET_DK // 2, axis=1))
        return t * cos + partner * sin

    q = rotary(h_ref[:, 0:RET_QK_WIDTH])
    k = rotary(h_ref[:, RET_QK_WIDTH:2 * RET_QK_WIDTH]) * (RET_DK ** -0.5)
    q_in = (q * xi_ref[...]).astype(_BF16)
    k_out = (k * zeta_ref[...]).astype(_BF16)
    q = q.astype(_BF16)
    k = k.astype(_BF16)
    for hd in range(RET_HEADS):
        qk = slice(hd * RET_DK, (hd + 1) * RET_DK)
        vs = slice(hd * RET_DV, (hd + 1) * RET_DV)
        v = h_ref[:, v_off + hd * RET_DV:v_off + (hd + 1) * RET_DV].astype(_BF16)
        s = lax.dot_general(q[:, qk], k[:, qk], (((1,), (1,)), ((), ())),
                            preferred_element_type=_F32) * decay_ref[hd]
        r = jnp.dot(s.astype(_BF16), v, preferred_element_type=_F32)
        r += jnp.dot(q_in[:, qk], state_ref[hd].astype(_BF16), preferred_element_type=_F32)
        kv = lax.dot_general(k_out[:, qk], v, (((0,), (0,)), ((), ())),
                             preferred_element_type=_F32)
        state_ref[hd] = gstate_ref[hd] * state_ref[hd] + kv
        mu = jnp.mean(r, axis=-1, keepdims=True)
        rc = r - mu
        var = jnp.mean(rc * rc, axis=-1, keepdims=True)
        rn = rc * lax.rsqrt(var + LN_EPS) * gn_g_ref[:, vs]
        gate = h_ref[:, g_off + hd * RET_DV:g_off + (hd + 1) * RET_DV]
        mix_ref[:, vs] = (_silu(gate) * rn).astype(_BF16)

    m = jnp.dot(mix_ref[...], w_out_ref[...], preferred_element_type=_F32)
    y = _layer_norm_rows(DEEPNORM_ALPHA * x_ref[...] + m, ln_g_ref[...], ln_b_ref[...])
    o_ref[...] = y
    words_ref[...] = _pack_bf16_pairs(y)


def _l0_mixer_tables(seq):
    half = RET_DK // 2
    inv = ROPE_BASE ** (-jnp.arange(half, dtype=_F32) / half)
    ang = jnp.arange(seq).astype(_F32)[:, None] * inv[None, :]
    cos = jnp.tile(jnp.cos(ang), (1, 2 * RET_HEADS))
    sin = jnp.tile(jnp.concatenate([-jnp.sin(ang), jnp.sin(ang)], axis=1), (1, RET_HEADS))
    log_g = jnp.log(1.0 - 2.0 ** (-5.0 - jnp.arange(RET_HEADS, dtype=_F32)))
    n = jnp.arange(MIX_ROWS, dtype=_F32)
    diff = n[:, None] - n[None, :]
    chunk = jnp.arange(MIX_ROWS) // CHUNK
    expo = jnp.where(chunk[:, None] == chunk[None, :], jnp.abs(diff), diff)
    decay = jnp.where((chunk[:, None] >= chunk[None, :])[None],
                      jnp.exp(expo[None] * log_g[:, None, None]), 0.0)
    xi = jnp.repeat(jnp.exp((n[:, None] + 1.0) * log_g[None, :]), RET_DK, axis=1)
    zeta = jnp.repeat(jnp.exp((MIX_ROWS - 1.0 - n[:, None]) * log_g[None, :]), RET_DK, axis=1)
    g_state = jnp.exp(MIX_ROWS * log_g)
    return cos, sin, xi, zeta, decay, g_state


def _l0_block(x, batch, seq, w_in, gn_g, conv_w, conv_b, cln_g, cln_b, w_out, ln_g, ln_b):
    n, d = x.shape
    steps = seq // MIX_ROWS
    cos, sin, xi, zeta, decay, g_state = _l0_mixer_tables(seq)
    row = lambda b, i: (b * steps + i, 0)
    pos = lambda b, i: (i, 0)
    vec = lambda t: t.reshape(1, -1)
    return pl.pallas_call(
        _l0_block_kernel,
        grid=(batch, steps),
        in_specs=[pl.BlockSpec((MIX_ROWS, d), row), _weight_spec((d, L0_IN)),
                  pl.BlockSpec((MIX_ROWS, RET_QK_WIDTH), pos),
                  pl.BlockSpec((MIX_ROWS, RET_QK_WIDTH), pos),
                  _const_spec((MIX_ROWS, RET_QK_WIDTH)), _const_spec((MIX_ROWS, RET_QK_WIDTH)),
                  _const_spec((RET_HEADS, MIX_ROWS, MIX_ROWS)),
                  pl.BlockSpec(memory_space=pltpu.SMEM),
                  _const_spec((1, RET_WIDTH)), _const_spec((CONV_WIDTH, CONV_CH)),
                  _const_spec((1, CONV_CH)), _const_spec((1, CONV_CH)), _const_spec((1, CONV_CH)),
                  _weight_spec((RET_WIDTH + CONV_CH, d)), _const_spec((1, d)), _const_spec((1, d))],
        out_specs=[pl.BlockSpec((MIX_ROWS, d), row), pl.BlockSpec((MIX_ROWS, d // 2), row)],
        out_shape=[jax.ShapeDtypeStruct((n, d), _F32), jax.ShapeDtypeStruct((n, d // 2), jnp.uint32)],
        scratch_shapes=[pltpu.VMEM((RET_HEADS, RET_DK, RET_DV), _F32),
                        pltpu.VMEM((MIX_ROWS + CONV_HALO, CONV_CH), _F32),
                        pltpu.VMEM((MIX_ROWS + CONV_HALO, CONV_CH), _F32),
                        pltpu.VMEM((MIX_ROWS, 2 * RET_QK_WIDTH + 2 * RET_WIDTH), _F32),
                        pltpu.VMEM((MIX_ROWS, RET_WIDTH + CONV_CH), _BF16),
                        pltpu.VMEM((d, L0_IN), _BF16), pltpu.VMEM((RET_WIDTH + CONV_CH, d), _BF16)],
        compiler_params=_params("arbitrary", "arbitrary"),
        name="l0_block",
    )(x, w_in, cos, sin, xi, zeta, decay, g_state, vec(gn_g), conv_w, vec(conv_b),
      vec(cln_g), vec(cln_b), w_out, vec(ln_g), vec(ln_b))


def _sb_attention_kernel(q_ref, k_ref, v_ref, tri_ref, o_ref, acc_ref, carry_ref):
    blk = SB_BLOCK
    t_idx = lax.broadcasted_iota(jnp.int32, (blk, blk), 0)
    s_idx = lax.broadcasted_iota(jnp.int32, (blk, blk), 1)
    keep = s_idx < t_idx

    heads = [slice(hd * SB_HEAD_DIM, (hd + 1) * SB_HEAD_DIM) for hd in range(SB_HEADS)]

    def sweep(q_rows, blocks, carry):
        starts = [pl.multiple_of(j * blk, blk) for j, _ in blocks]
        zs = [[lax.dot_general(q_ref[q_rows, hs], k_ref[pl.ds(st, blk), hs], (((1,), (1,)), ((), ())),
                               preferred_element_type=_F32) for hs in heads] for st in starts]
        log_betas, addends = [], []
        for (_, mask), z_heads in zip(blocks, zs):
            for z in z_heads:
                softplus = jnp.log2(1.0 + jnp.exp2(-jnp.abs(z)))
                log_beta = jnp.minimum(z, 0.0) - softplus
                log_rest = log_beta - z
                if mask is not None:
                    log_rest = jnp.where(mask, log_rest, 0.0)
                log_betas.append(log_beta)
                addends.append(log_rest.astype(_BF16))
        sums = [jnp.dot(t, tri_ref[...], preferred_element_type=_F32) for t in addends]
        weights, total = [], None
        for b, (_, mask) in enumerate(blocks):
            block_sums = sums[b * SB_HEADS:(b + 1) * SB_HEADS]
            before = carry if total is None else (total if carry is None else carry + total)
            for hd, s in enumerate(block_sums):
                expo = log_betas[b * SB_HEADS + hd] + s[:, :blk]
                if before is not None:
                    expo = expo + before[:, heads[hd]]
                a = jnp.exp2(expo)
                if mask is not None:
                    a = jnp.where(mask, a, 0.0)
                weights.append(a.astype(_BF16))
            block_total = jnp.concatenate([s[:, blk:] for s in block_sums], axis=1)
            total = block_total if total is None else total + block_total
        outs = []
        for hd, hs in enumerate(heads):
            o = None
            for b, st in enumerate(starts):
                t = jnp.dot(weights[b * SB_HEADS + hd], v_ref[pl.ds(st, blk), hs],
                            preferred_element_type=_F32)
                o = t if o is None else o + t
            outs.append(o)
        return jnp.concatenate(outs, axis=1), total

    def query_block(sub):
        q_rows = slice(sub * blk, (sub + 1) * blk)
        qi = pl.program_id(1) * SB_STEP_BLOCKS + sub
        out, carry = sweep(q_rows, [(qi, keep)], None)
        acc_ref[...] = out
        carry_ref[...] = carry

        def cond(state):
            j, alive = state
            return jnp.logical_and(j >= 0, alive)

        def body(state):
            j, _ = state
            carry = carry_ref[...]
            out, total = sweep(q_rows, [(j, None), (jnp.maximum(j - 1, 0), j >= 1)], carry)
            carry = carry + total
            acc_ref[...] += out
            carry_ref[...] = carry
            return j - 2, jnp.max(carry) > -SB_CUTOFF

        lax.while_loop(cond, body, (qi - 1, jnp.max(carry) > -SB_CUTOFF))
        o_ref[q_rows, :] = acc_ref[...].astype(o_ref.dtype)

    for sub in range(SB_STEP_BLOCKS):
        query_block(sub)


def _sb_attention(qkv, batch, seq):
    n = qkv.shape[0]
    rows = SB_STEP_BLOCKS * SB_BLOCK
    nq = seq // rows
    j = jnp.arange(SB_BLOCK)[:, None]
    s = jnp.arange(SB_BLOCK)[None, :]
    tri = jnp.concatenate([(j > s).astype(_BF16), jnp.ones((SB_BLOCK, SB_BLOCK), _BF16)], axis=1)
    whole_seq = lambda col: pl.BlockSpec((seq, D_MODEL), lambda b, i: (b, col))
    return pl.pallas_call(
        _sb_attention_kernel,
        grid=(batch, nq),
        in_specs=[pl.BlockSpec((rows, D_MODEL), lambda b, i: (b * nq + i, 0)),
                  whole_seq(1), whole_seq(2), _const_spec((SB_BLOCK, 2 * SB_BLOCK))],
        out_specs=pl.BlockSpec((rows, D_MODEL), lambda b, i: (b * nq + i, 0)),
        out_shape=jax.ShapeDtypeStruct((n, D_MODEL), _BF16),
        scratch_shapes=[pltpu.VMEM((SB_BLOCK, D_MODEL), _F32), pltpu.VMEM((SB_BLOCK, D_MODEL), _F32)],
        compiler_params=_params("arbitrary", "arbitrary"),
        name="sb_attention",
    )(qkv, qkv, qkv, tri)


def _router_kernel(x_ref, w_ref, b_ref, tri_ref, idx_ref, gate_ref, rank_ref, count_ref, base_ref):
    @pl.when(pl.program_id(0) == 0)
    def _():
        base_ref[...] = jnp.zeros_like(base_ref)

    x = x_ref[...]
    w = w_ref[...]
    x_hi = x.astype(_BF16)
    x_lo = (x - x_hi.astype(_F32)).astype(_BF16)
    w_hi = w.astype(_BF16)
    w_lo = (w - w_hi.astype(_F32)).astype(_BF16)
    nt = (((1,), (1,)), ((), ()))
    logits = (lax.dot_general(w_hi, x_hi, nt, preferred_element_type=_F32)
              + lax.dot_general(w_lo, x_hi, nt, preferred_element_type=_F32)
              + lax.dot_general(w_hi, x_lo, nt, preferred_element_type=_F32)) + b_ref[...]
    expert = lax.broadcasted_iota(jnp.int32, logits.shape, 0)
    vals, idxs = [], []
    for _ in range(TOP_K):
        m = jnp.max(logits, axis=0, keepdims=True)
        sel = jnp.min(jnp.where(logits == m, expert, N_EXPERTS), axis=0, keepdims=True)
        vals.append(m)
        idxs.append(sel)
        logits = jnp.where(expert == sel, -jnp.inf, logits)
    vals = jnp.concatenate(vals, axis=0)
    e = jnp.exp(vals - vals[0:1])
    gate_ref[...] = e / jnp.sum(e, axis=0, keepdims=True)
    idx_ref[...] = jnp.concatenate(idxs, axis=0)

    member = jnp.zeros(logits.shape, _F32)
    for sel in idxs:
        member += (expert == sel).astype(_F32)
    before = jnp.dot(member.astype(_BF16), tri_ref[...], preferred_element_type=_F32) + base_ref[...]
    ranks = [jnp.sum(jnp.where(expert == sel, before, 0.0), axis=0, keepdims=True) for sel in idxs]
    rank_ref[...] = jnp.concatenate(ranks, axis=0).astype(jnp.int32)
    base_ref[...] += jnp.sum(member, axis=1, keepdims=True)
    count_ref[...] = base_ref[...]


def _router(x, router_w, router_b):
    n, d = x.shape
    t = jnp.arange(ROUTER_ROWS)
    tri = (t[:, None] < t[None, :]).astype(_BF16)
    tok = pl.BlockSpec((TOP_K, ROUTER_ROWS), lambda i: (0, i))
    return pl.pallas_call(
        _router_kernel,
        grid=(n // ROUTER_ROWS,),
        in_specs=[pl.BlockSpec((ROUTER_ROWS, d), lambda i: (i, 0)),
                  _const_spec((N_EXPERTS, d)), _const_spec((N_EXPERTS, 1)),
                  _const_spec((ROUTER_ROWS, ROUTER_ROWS))],
        out_specs=[tok, tok, tok, _const_spec((N_EXPERTS, 1))],
        out_shape=[jax.ShapeDtypeStruct((TOP_K, n), jnp.int32),
                   jax.ShapeDtypeStruct((TOP_K, n), _F32),
                   jax.ShapeDtypeStruct((TOP_K, n), jnp.int32),
                   jax.ShapeDtypeStruct((N_EXPERTS, 1), _F32)],
        scratch_shapes=[pltpu.VMEM((N_EXPERTS, 1), _F32)],
        compiler_params=_params("arbitrary"),
        name="router",
    )(x, router_w.T, router_b.reshape(N_EXPERTS, 1), tri)


def _dispatch(x_words, slot_of, p):
    n, words = x_words.shape
    part = words // DISPATCH_PARTS
    mesh = plsc.VectorSubcoreMesh(core_axis_name="core", subcore_axis_name="subcore",
                                  num_cores=SC_CORES, num_subcores=SC_SUBCORES)
    out = jax.ShapeDtypeStruct((p, part), x_words.dtype)

    @functools.partial(
        pl.kernel, out_type=[out] * DISPATCH_PARTS, mesh=mesh, scratch_types=[],
        compiler_params=pltpu.CompilerParams(use_tc_tiling_on_sc=True), name="moe_dispatch")
    def dispatch(x_hbm, slot_hbm, *o_hbm):
        for c in range(DISPATCH_PARTS):
            def body(x_vmem, slot_vmem, o_ref=o_hbm[c]):
                for k in range(TOP_K):
                    pltpu.sync_copy(x_vmem, o_ref.at[slot_vmem.at[k]])

            pltpu.emit_pipeline(
                body,
                grid=(n // DISPATCH_ROWS,),
                in_specs=[pl.BlockSpec((DISPATCH_ROWS, part), lambda i, c=c: (i, c)),
                          pl.BlockSpec((TOP_K, DISPATCH_ROWS), lambda i: (0, i))],
                out_specs=[],
                core_axis_name=("core", "subcore"),
                dimension_semantics=(pltpu.PARALLEL,),
            )(x_hbm, slot_hbm)

    return dispatch(x_words, slot_of.reshape(TOP_K, n))


def _combine_gather(y_parts, slot_of):
    nk = slot_of.shape[0]
    part = y_parts[0].shape[1]
    mesh = plsc.VectorSubcoreMesh(core_axis_name="core", subcore_axis_name="subcore",
                                  num_cores=SC_CORES, num_subcores=SC_SUBCORES)
    out = jax.ShapeDtypeStruct((nk, part), y_parts[0].dtype)

    @functools.partial(
        pl.kernel, out_type=[out] * DISPATCH_PARTS, mesh=mesh, scratch_types=[],
        compiler_params=pltpu.CompilerParams(use_tc_tiling_on_sc=True), name="moe_combine_gather")
    def gather(*refs):
        y_hbm, slot_hbm, o_hbm = refs[:DISPATCH_PARTS], refs[DISPATCH_PARTS], refs[DISPATCH_PARTS + 1:]
        for c in range(DISPATCH_PARTS):
            def body(slot_vmem, o_vmem, y_ref=y_hbm[c]):
                pltpu.sync_copy(y_ref.at[slot_vmem.at[0]], o_vmem)

            pltpu.emit_pipeline(
                body,
                grid=(nk // DISPATCH_ROWS,),
                in_specs=[pl.BlockSpec((1, DISPATCH_ROWS), lambda i: (0, i))],
                out_specs=[pl.BlockSpec((DISPATCH_ROWS, part), lambda i: (i, 0))],
                core_axis_name=("core", "subcore"),
                dimension_semantics=(pltpu.PARALLEL,),
            )(slot_hbm, o_hbm[c])

    return gather(*y_parts, slot_of.reshape(1, nk))


def _moe_ffn_kernel(block_e_ref, block_rows_ref, last_block_ref, xa_ref, xb_ref, wu_f32_ref, sel_ref,
                    bu_ref, wd_f32_ref, bd_ref, oa_ref, ob_ref, wu_ref, wd_ref):
    step = pl.program_id(0)
    rows_used = block_rows_ref[step]
    part = oa_ref.shape[1]
    new_expert = jnp.logical_or(step == 0,
                                block_e_ref[step] != block_e_ref[jnp.maximum(step - 1, 0)])

    @pl.when(jnp.logical_and(new_expert, rows_used > 0))
    def _():
        wd_ref[...] = wd_f32_ref[0].astype(_BF16)
        group = 2 * PREP_LANES
        for c in range(wu_f32_ref.shape[2] // group):
            t = jnp.dot(wu_f32_ref[0, :, c * group:(c + 1) * group].astype(_BF16), sel_ref[...],
                        preferred_element_type=_F32)
            wu_ref[:, c * PREP_LANES:(c + 1) * PREP_LANES] = t[:, :PREP_LANES].astype(_BF16)
            wu_ref[:, D_FF + c * PREP_LANES:D_FF + (c + 1) * PREP_LANES] = t[:, PREP_LANES:].astype(_BF16)

    def ffn_rows(*ranges):
        xs = [_unpack_bf16_pairs(jnp.concatenate([xa_ref[lo:hi, :], xb_ref[lo:hi, :]], axis=1))
              for lo, hi in ranges]
        hs = [jnp.dot(x, wu_ref[...], preferred_element_type=_F32) + bu_ref[0] for x in xs]
        acts = []
        for h in hs:
            h_glu = jnp.minimum(h[:, :D_FF], SWIGLU_LIMIT)
            h_lin = jnp.clip(h[:, D_FF:], -SWIGLU_LIMIT, SWIGLU_LIMIT)
            acts.append((h_glu * jax.nn.sigmoid(SWIGLU_ALPHA * h_glu) * (h_lin + 1.0)).astype(_BF16))
        ys = [jnp.dot(a, wd_ref[...], preferred_element_type=_F32) + bd_ref[0] for a in acts]
        for (lo, hi), y in zip(ranges, ys):
            words = _pack_bf16_pairs(y)
            oa_ref[lo:hi, :] = words[:, :part]
            ob_ref[lo:hi, :] = words[:, part:]

    def zero_rows(lo, hi):
        oa_ref[lo:hi, :] = jnp.zeros((hi - lo, part), oa_ref.dtype)
        ob_ref[lo:hi, :] = jnp.zeros((hi - lo, part), ob_ref.dtype)

    sub_blocks = [(lo, lo + MOE_ROWS) for lo in range(0, MOE_STEP_ROWS, MOE_ROWS)]
    whole_step = rows_used > MOE_STEP_ROWS - MOE_ROWS + MOE_TAIL_ROWS

    @pl.when(whole_step)
    def _():
        ffn_rows(*sub_blocks)

    @pl.when(jnp.logical_not(whole_step))
    def _():
        for lo, hi in sub_blocks:
            mid = lo + MOE_TAIL_ROWS

            @pl.when(rows_used > mid)
            def _():
                ffn_rows((lo, hi))

            @pl.when(jnp.logical_and(rows_used > lo, rows_used <= mid))
            def _():
                ffn_rows((lo, mid))
                zero_rows(mid, hi)

            @pl.when(rows_used <= lo)
            def _():
                zero_rows(lo, hi)


def _moe_ffn(xs, block_e, block_rows, last_block, wu_f32, bu, wd_f32, bd):
    xa, xb = xs
    p, part = xa.shape
    ff, d = wd_f32.shape[1:]
    ff2 = wu_f32.shape[2]
    lane = jnp.arange(2 * PREP_LANES)
    sel = (lane[:, None] == jnp.where(lane < PREP_LANES, 2 * lane, 2 * (lane - PREP_LANES) + 1)[None, :]
           ).astype(_BF16)
    row = lambda i, be, br, lb: (jnp.minimum(i, lb[0]), 0)
    exp3 = lambda i, be, br, lb: (be[i], 0, 0)
    rows_spec = pl.BlockSpec((MOE_STEP_ROWS, part), row)
    grid_spec = pltpu.PrefetchScalarGridSpec(
        num_scalar_prefetch=3,
        grid=(p // MOE_STEP_ROWS,),
        in_specs=[rows_spec, rows_spec,
                  pl.BlockSpec((1, d, ff2), exp3),
                  pl.BlockSpec((2 * PREP_LANES, 2 * PREP_LANES), lambda i, be, br, lb: (0, 0)),
                  pl.BlockSpec((1, 1, ff2), exp3),
                  pl.BlockSpec((1, ff, d), exp3), pl.BlockSpec((1, 1, d), exp3)],
        out_specs=[rows_spec, rows_spec],
        scratch_shapes=[pltpu.VMEM((d, ff2), _BF16), pltpu.VMEM((ff, d), _BF16)],
    )
    out = jax.ShapeDtypeStruct((p, part), xa.dtype)
    return pl.pallas_call(
        _moe_ffn_kernel,
        grid_spec=grid_spec,
        out_shape=[out, out],
        compiler_params=_params("arbitrary", vmem_limit=MOE_VMEM_LIMIT_BYTES),
        name="moe_ffn",
    )(block_e, block_rows, last_block, xa, xb, wu_f32, sel, bu, wd_f32, bd)


def _combine_ln_kernel(ya_ref, yb_ref, gate_ref, res_ref, g_ref, b_ref, o_ref, *maybe_obf_ref):
    gate = gate_ref[...]
    moe = None
    for k in range(TOP_K):
        words = jnp.concatenate([ya_ref[k], yb_ref[k]], axis=1)
        term = _unpack_bf16_pairs(words).astype(_F32) * gate[:, k:k + 1]
        moe = term if moe is None else moe + term
    y = _layer_norm_rows(DEEPNORM_ALPHA * res_ref[...] + moe, g_ref[...], b_ref[...])
    o_ref[...] = y
    for obf_ref in maybe_obf_ref:
        obf_ref[...] = y.astype(_BF16)


def _combine_ln(y4_parts, gates, res, g, b, with_bf16):
    n, d = res.shape
    ya, yb = (t.reshape(TOP_K, n, t.shape[1]) for t in y4_parts)
    part = ya.shape[2]
    row = lambda i: (i, 0)
    y_spec = pl.BlockSpec((TOP_K, COMBINE_ROWS, part), lambda i: (0, i, 0))
    return pl.pallas_call(
        _combine_ln_kernel,
        grid=(n // COMBINE_ROWS,),
        in_specs=[y_spec, y_spec, pl.BlockSpec((COMBINE_ROWS, TOP_K), row),
                  pl.BlockSpec((COMBINE_ROWS, d), row), _const_spec((1, d)), _const_spec((1, d))],
        out_specs=[pl.BlockSpec((COMBINE_ROWS, d), row)] * (2 if with_bf16 else 1),
        out_shape=[jax.ShapeDtypeStruct((n, d), _F32)]
        + ([jax.ShapeDtypeStruct((n, d), _BF16)] if with_bf16 else []),
        compiler_params=_params("arbitrary"),
        name="combine_ln",
    )(ya, yb, gates, res, g.reshape(1, d), b.reshape(1, d))


def _moe_layer(x, x_words, router_w, router_b, w_up, b_up, w_down, b_down, ln_g, ln_b, with_bf16):
    n, d = x.shape
    nk = n * TOP_K
    idx_t, gate_t, rank_t, count = _router(x, router_w, router_b)
    counts = count[:, 0].astype(jnp.int32)
    padded = (counts + MOE_STEP_ROWS - 1) // MOE_STEP_ROWS * MOE_STEP_ROWS
    padded_ends = jnp.cumsum(padded)
    padded_starts = padded_ends - padded
    n_blocks = -(-(nk + N_EXPERTS * (MOE_STEP_ROWS - 1)) // MOE_STEP_ROWS)
    p = n_blocks * MOE_STEP_ROWS
    short = counts % MOE_STEP_ROWS
    experts = jnp.arange(N_EXPERTS, dtype=jnp.int32)
    chosen = idx_t[None] == experts[:, None, None]
    start_of = jnp.sum(jnp.where(chosen, padded_starts[:, None, None], 0), axis=0)
    short_of = jnp.sum(jnp.where(chosen, short[:, None, None], 0), axis=0)
    gap_of = jnp.where(jnp.logical_and(short_of > 0, rank_t >= short_of), MOE_STEP_ROWS - short_of, 0)
    slot_of = (start_of + rank_t + gap_of).reshape(nk)
    block_start = jnp.arange(n_blocks, dtype=jnp.int32) * MOE_STEP_ROWS
    block_e = jnp.minimum(jnp.sum(padded_ends[None, :] <= block_start[:, None], axis=1),
                          N_EXPERTS - 1).astype(jnp.int32)
    first_of_expert = block_start == padded_starts[block_e]
    block_rows = jnp.where(block_start >= padded_ends[-1], 0,
                           jnp.where(jnp.logical_and(first_of_expert, short[block_e] > 0),
                                     short[block_e], MOE_STEP_ROWS)).astype(jnp.int32)
    last_block = jnp.minimum(padded_ends[-1] // MOE_STEP_ROWS, n_blocks - 1).astype(jnp.int32)
    xs = _dispatch(x_words, slot_of, p)
    bu = b_up.reshape(N_EXPERTS, D_FF, 2).transpose(0, 2, 1).reshape(N_EXPERTS, 1, 2 * D_FF)
    ys = _moe_ffn(xs, block_e, block_rows, last_block.reshape(1), w_up, bu, w_down,
                  b_down.reshape(N_EXPERTS, 1, d))
    return _combine_ln(_combine_gather(ys, slot_of), gate_t.T, x, ln_g, ln_b, with_bf16)


def kernel(x, l0_w_in, l0_ret_gn_g, l0_conv_w, l0_conv_b, l0_conv_ln_g, l0_conv_ln_b, l0_w_out, l0_ln1_g, l0_ln1_b, l0_router_w, l0_router_b, l0_w_up, l0_b_up, l0_w_down, l0_b_down, l0_ln2_g, l0_ln2_b, l1_w_in, l1_w_out, l1_ln1_g, l1_ln1_b, l1_router_w, l1_router_b, l1_w_up, l1_b_up, l1_w_down, l1_b_down, l1_ln2_g, l1_ln2_b):
    batch, seq, d = x.shape
    n = batch * seq
    x0 = x.reshape(n, d)

    x1, x1_words = _l0_block(x0, batch, seq, l0_w_in, l0_ret_gn_g, l0_conv_w, l0_conv_b,
                             l0_conv_ln_g, l0_conv_ln_b, l0_w_out, l0_ln1_g, l0_ln1_b)
    x2, x2_bf16 = _moe_layer(x1, x1_words, l0_router_w, l0_router_b, l0_w_up, l0_b_up, l0_w_down,
                             l0_b_down, l0_ln2_g, l0_ln2_b, True)

    q_scale = jnp.where(jnp.arange(3 * d) < d, LOG2_E * SB_HEAD_DIM ** -0.5, 1.0).astype(_F32)
    qkv = _proj(x2_bf16, l1_w_in, q_scale)
    att = _sb_attention(qkv, batch, seq)
    x3, x3_words = _proj_res_ln(att, l1_w_out, x2, l1_ln1_g, l1_ln1_b)
    (x4,) = _moe_layer(x3, x3_words, l1_router_w, l1_router_b, l1_w_up, l1_b_up, l1_w_down,
                       l1_b_down, l1_ln2_g, l1_ln2_b, False)
    return x4.reshape(batch, seq, d)
```

```python
import functools

import jax
import jax.numpy as jnp
from jax import lax
from jax.experimental import pallas as pl
from jax.experimental.pallas import tpu as pltpu
from jax.experimental.pallas import tpu_sc as plsc

D_MODEL = 1024
CHUNK = 64
RET_HEADS = 4
RET_DK = 64
RET_DV = 128
RET_QK_WIDTH = RET_HEADS * RET_DK
RET_WIDTH = RET_HEADS * RET_DV
ROPE_BASE = 10000.0
CONV_CH = D_MODEL // 2
CONV_WIDTH = 31
L0_IN = 2 * RET_QK_WIDTH + 2 * RET_WIDTH + 2 * CONV_CH
SB_HEADS = 8
SB_HEAD_DIM = D_MODEL // SB_HEADS
N_EXPERTS = 32
TOP_K = 4
D_FF = D_MODEL
SWIGLU_LIMIT = 7.0
SWIGLU_ALPHA = 1.702
LN_EPS = 1e-5
DEPTH = 2
DEEPNORM_ALPHA = (2 * DEPTH) ** 0.25

VMEM_LIMIT_BYTES = 48 * 1024 * 1024
MOE_VMEM_LIMIT_BYTES = 56 * 1024 * 1024
F32_SUBLANES = 8
PROJ_ROWS = 1024
MIX_ROWS = 256
CONV_HALO = 32
ROUTER_ROWS = 1024
MOE_ROWS = 512
MOE_TAIL_ROWS = 256
MOE_STEP_ROWS = 1024
PREP_LANES = 128
COMBINE_ROWS = 512
DISPATCH_ROWS = 128
DISPATCH_PARTS = 2
SC_CORES = 2
SC_SUBCORES = 16
SB_BLOCK = 128
SB_STEP_BLOCKS = 2
SB_CUTOFF = 150.0
LOG2_E = 1.4426950408889634

_F32 = jnp.float32
_BF16 = jnp.bfloat16


def _params(*sem, vmem_limit=VMEM_LIMIT_BYTES):
    return pltpu.CompilerParams(dimension_semantics=sem, vmem_limit_bytes=vmem_limit)


def _const_spec(shape):
    nd = len(shape)
    return pl.BlockSpec(shape, lambda *_: (0,) * nd)


def _weight_spec(shape):
    nd = len(shape)
    return pl.BlockSpec(shape, lambda *_: (0,) * nd, pipeline_mode=pl.Buffered(1))


def _layer_norm_rows(y, g, b):
    mu = jnp.mean(y, axis=-1, keepdims=True)
    yc = y - mu
    var = jnp.mean(yc * yc, axis=-1, keepdims=True)
    return yc * lax.rsqrt(var + LN_EPS) * g + b


def _silu(x):
    return x * jax.nn.sigmoid(x)


def _pack_bf16_pairs(y):
    half = y.shape[1] // 2
    lo = lax.bitcast_convert_type(y[:, :half].astype(_BF16).astype(_F32), jnp.uint32)
    hi = lax.bitcast_convert_type(y[:, half:].astype(_BF16).astype(_F32), jnp.uint32)
    return (lo >> 16) | (hi & jnp.uint32(0xFFFF0000))


def _unpack_bf16_pairs(w):
    lo = lax.bitcast_convert_type(w << 16, _F32)
    hi = lax.bitcast_convert_type(w & jnp.uint32(0xFFFF0000), _F32)
    return jnp.concatenate([lo, hi], axis=1).astype(_BF16)


def _proj_kernel(x_ref, w_f32_ref, scale_ref, o_ref, w_ref):
    @pl.when(pl.program_id(0) == 0)
    def _():
        w_ref[...] = w_f32_ref[...].astype(_BF16)

    o_ref[...] = (jnp.dot(x_ref[...].astype(_BF16), w_ref[...], preferred_element_type=_F32)
                  * scale_ref[...]).astype(o_ref.dtype)


def _proj(x, w, col_scale):
    n, d = x.shape
    width = w.shape[1]
    return pl.pallas_call(
        _proj_kernel,
        grid=(n // PROJ_ROWS,),
        in_specs=[pl.BlockSpec((PROJ_ROWS, d), lambda i: (i, 0)), _weight_spec((d, width)),
                  _const_spec((1, width))],
        out_specs=pl.BlockSpec((PROJ_ROWS, width), lambda i: (i, 0)),
        out_shape=jax.ShapeDtypeStruct((n, width), _BF16),
        scratch_shapes=[pltpu.VMEM((d, width), _BF16)],
        compiler_params=_params("arbitrary"),
        name="proj",
    )(x, w, col_scale.reshape(1, width))


def _proj_res_ln_kernel(a_ref, w_f32_ref, res_ref, g_ref, b_ref, o_ref, words_ref, w_ref):
    @pl.when(pl.program_id(0) == 0)
    def _():
        w_ref[...] = w_f32_ref[...].astype(_BF16)

    m = jnp.dot(a_ref[...], w_ref[...], preferred_element_type=_F32)
    y = _layer_norm_rows(DEEPNORM_ALPHA * res_ref[...] + m, g_ref[...], b_ref[...])
    o_ref[...] = y
    words_ref[...] = _pack_bf16_pairs(y)


def _proj_res_ln(a_bf16, w, res, g, b):
    n, d = res.shape
    k = a_bf16.shape[1]
    row = lambda i: (i, 0)
    return pl.pallas_call(
        _proj_res_ln_kernel,
        grid=(n // PROJ_ROWS,),
        in_specs=[pl.BlockSpec((PROJ_ROWS, k), row), _weight_spec((k, d)),
                  pl.BlockSpec((PROJ_ROWS, d), row), _const_spec((1, d)), _const_spec((1, d))],
        out_specs=[pl.BlockSpec((PROJ_ROWS, d), row), pl.BlockSpec((PROJ_ROWS, d // 2), row)],
        out_shape=[jax.ShapeDtypeStruct((n, d), _F32), jax.ShapeDtypeStruct((n, d // 2), jnp.uint32)],
        scratch_shapes=[pltpu.VMEM((k, d), _BF16)],
        compiler_params=_params("arbitrary"),
        name="proj_res_ln",
    )(a_bf16, w, res, g.reshape(1, d), b.reshape(1, d))


def _l0_block_kernel(x_ref, w_in_f32_ref, cos_ref, sin_ref, xi_ref, zeta_ref, decay_ref, gstate_ref,
                     gn_g_ref, conv_w_ref, conv_b_ref, cln_g_ref, cln_b_ref, w_out_f32_ref, ln_g_ref,
                     ln_b_ref, o_ref, words_ref, state_ref, u_ref, shift_ref, h_ref, mix_ref,
                     w_in_ref, w_out_ref):
    rows = MIX_ROWS
    step = pl.program_id(1)
    v_off = 2 * RET_QK_WIDTH
    g_off = v_off + RET_WIDTH
    a_off = g_off + RET_WIDTH

    @pl.when(jnp.logical_and(pl.program_id(0) == 0, step == 0))
    def _():
        w_in_ref[...] = w_in_f32_ref[...].astype(_BF16)
        w_out_ref[...] = w_out_f32_ref[...].astype(_BF16)

    @pl.when(step == 0)
    def _():
        state_ref[...] = jnp.zeros_like(state_ref)
        u_ref[0:CONV_HALO, :] = jnp.zeros((CONV_HALO, CONV_CH), _F32)

    x_bf16 = x_ref[...].astype(_BF16)
    h_conv = jnp.dot(x_bf16, w_in_ref[:, a_off:a_off + 2 * CONV_CH], preferred_element_type=_F32)
    h_ref[...] = jnp.dot(x_bf16, w_in_ref[:, 0:a_off], preferred_element_type=_F32)

    u_ref[CONV_HALO:CONV_HALO + rows, :] = h_conv[:, :CONV_CH] * jax.nn.sigmoid(h_conv[:, CONV_CH:])
    acc = jnp.broadcast_to(conv_b_ref[...], (rows, CONV_CH))
    first = CONV_HALO - (CONV_WIDTH - 1)
    for phase in range(F32_SUBLANES):
        offsets = [first + j for j in range(CONV_WIDTH) if (first + j) % F32_SUBLANES == phase]
        span = max(offsets) - phase + rows
        if phase == 0:
            src = u_ref
        else:
            shift_ref[0:span, :] = u_ref[phase:phase + span, :]
            src = shift_ref
        for off in offsets:
            j = off - first
            acc = acc + conv_w_ref[j:j + 1, :] * src[off - phase:off - phase + rows, :]
    u_ref[0:CONV_HALO, :] = u_ref[rows:rows + CONV_HALO, :]
    c = _silu(_layer_norm_rows(acc, cln_g_ref[...], cln_b_ref[...]))
    mix_ref[:, RET_WIDTH:RET_WIDTH + CONV_CH] = c.astype(_BF16)

    lane = lax.broadcasted_iota(jnp.int32, (rows, RET_QK_WIDTH), 1)
    first_half = (lane % RET_DK) < (RET_DK // 2)
    cos = cos_ref[...]
    sin = sin_ref[...]

    def rotary(t):
        partner = jnp.where(first_half,
                            pltpu.roll(t, RET_QK_WIDTH - RET_DK // 2, axis=1),
                            pltpu.roll(t, RET_DK // 2, axis=1))
        return t * cos + partner * sin

    q = rotary(h_ref[:, 0:RET_QK_WIDTH])
    k = rotary(h_ref[:, RET_QK_WIDTH:2 * RET_QK_WIDTH]) * (RET_DK ** -0.5)
    q_in = (q * xi_ref[...]).astype(_BF16)
    k_out = (k * zeta_ref[...]).astype(_BF16)
    q = q.astype(_BF16)
    k = k.astype(_BF16)
    for hd in range(RET_HEADS):
        qk = slice(hd * RET_DK, (hd + 1) * RET_DK)
        vs = slice(hd * RET_DV, (hd + 1) * RET_DV)
        v = h_ref[:, v_off + hd * RET_DV:v_off + (hd + 1) * RET_DV].astype(_BF16)
        s = lax.dot_general(q[:, qk], k[:, qk], (((1,), (1,)), ((), ())),
                            preferred_element_type=_F32) * decay_ref[hd]
        r = jnp.dot(s.astype(_BF16), v, preferred_element_type=_F32)
        r += jnp.dot(q_in[:, qk], state_ref[hd].astype(_BF16), preferred_element_type=_F32)
        kv = lax.dot_general(k_out[:, qk], v, (((0,), (0,)), ((), ())),
                             preferred_element_type=_F32)
        state_ref[hd] = gstate_ref[hd] * state_ref[hd] + kv
        mu = jnp.mean(r, axis=-1, keepdims=True)
        rc = r - mu
        var = jnp.mean(rc * rc, axis=-1, keepdims=True)
        rn = rc * lax.rsqrt(var + LN_EPS) * gn_g_ref[:, vs]
        gate = h_ref[:, g_off + hd * RET_DV:g_off + (hd + 1) * RET_DV]
        mix_ref[:, vs] = (_silu(gate) * rn).astype(_BF16)

    m = jnp.dot(mix_ref[...], w_out_ref[...], preferred_element_type=_F32)
    y = _layer_norm_rows(DEEPNORM_ALPHA * x_ref[...] + m, ln_g_ref[...], ln_b_ref[...])
    o_ref[...] = y
    words_ref[...] = _pack_bf16_pairs(y)


def _l0_mixer_tables(seq):
    half = RET_DK // 2
    inv = ROPE_BASE ** (-jnp.arange(half, dtype=_F32) / half)
    ang = jnp.arange(seq).astype(_F32)[:, None] * inv[None, :]
    cos = jnp.tile(jnp.cos(ang), (1, 2 * RET_HEADS))
    sin = jnp.tile(jnp.concatenate([-jnp.sin(ang), jnp.sin(ang)], axis=1), (1, RET_HEADS))
    log_g = jnp.log(1.0 - 2.0 ** (-5.0 - jnp.arange(RET_HEADS, dtype=_F32)))
    n = jnp.arange(MIX_ROWS, dtype=_F32)
    diff = n[:, None] - n[None, :]
    chunk = jnp.arange(MIX_ROWS) // CHUNK
    expo = jnp.where(chunk[:, None] == chunk[None, :], jnp.abs(diff), diff)
    decay = jnp.where((chunk[:, None] >= chunk[None, :])[None],
                      jnp.exp(expo[None] * log_g[:, None, None]), 0.0)
    xi = jnp.repeat(jnp.exp((n[:, None] + 1.0) * log_g[None, :]), RET_DK, axis=1)
    zeta = jnp.repeat(jnp.exp((MIX_ROWS - 1.0 - n[:, None]) * log_g[None, :]), RET_DK, axis=1)
    g_state = jnp.exp(MIX_ROWS * log_g)
    return cos, sin, xi, zeta, decay, g_state


def _l0_block(x, batch, seq, w_in, gn_g, conv_w, conv_b, cln_g, cln_b, w_out, ln_g, ln_b):
    n, d = x.shape
    steps = seq // MIX_ROWS
    cos, sin, xi, zeta, decay, g_state = _l0_mixer_tables(seq)
    row = lambda b, i: (b * steps + i, 0)
    pos = lambda b, i: (i, 0)
    vec = lambda t: t.reshape(1, -1)
    return pl.pallas_call(
        _l0_block_kernel,
        grid=(batch, steps),
        in_specs=[pl.BlockSpec((MIX_ROWS, d), row), _weight_spec((d, L0_IN)),
                  pl.BlockSpec((MIX_ROWS, RET_QK_WIDTH), pos),
                  pl.BlockSpec((MIX_ROWS, RET_QK_WIDTH), pos),
                  _const_spec((MIX_ROWS, RET_QK_WIDTH)), _const_spec((MIX_ROWS, RET_QK_WIDTH)),
                  _const_spec((RET_HEADS, MIX_ROWS, MIX_ROWS)),
                  pl.BlockSpec(memory_space=pltpu.SMEM),
                  _const_spec((1, RET_WIDTH)), _const_spec((CONV_WIDTH, CONV_CH)),
                  _const_spec((1, CONV_CH)), _const_spec((1, CONV_CH)), _const_spec((1, CONV_CH)),
                  _weight_spec((RET_WIDTH + CONV_CH, d)), _const_spec((1, d)), _const_spec((1, d))],
        out_specs=[pl.BlockSpec((MIX_ROWS, d), row), pl.BlockSpec((MIX_ROWS, d // 2), row)],
        out_shape=[jax.ShapeDtypeStruct((n, d), _F32), jax.ShapeDtypeStruct((n, d // 2), jnp.uint32)],
        scratch_shapes=[pltpu.VMEM((RET_HEADS, RET_DK, RET_DV), _F32),
                        pltpu.VMEM((MIX_ROWS + CONV_HALO, CONV_CH), _F32),
                        pltpu.VMEM((MIX_ROWS + CONV_HALO, CONV_CH), _F32),
                        pltpu.VMEM((MIX_ROWS, 2 * RET_QK_WIDTH + 2 * RET_WIDTH), _F32),
                        pltpu.VMEM((MIX_ROWS, RET_WIDTH + CONV_CH), _BF16),
                        pltpu.VMEM((d, L0_IN), _BF16), pltpu.VMEM((RET_WIDTH + CONV_CH, d), _BF16)],
        compiler_params=_params("arbitrary", "arbitrary"),
        name="l0_block",
    )(x, w_in, cos, sin, xi, zeta, decay, g_state, vec(gn_g), conv_w, vec(conv_b),
      vec(cln_g), vec(cln_b), w_out, vec(ln_g), vec(ln_b))


def _sb_attention_kernel(q_ref, k_ref, v_ref, tri_ref, o_ref, acc_ref, carry_ref):
    blk = SB_BLOCK
    t_idx = lax.broadcasted_iota(jnp.int32, (blk, blk), 0)
    s_idx = lax.broadcasted_iota(jnp.int32, (blk, blk), 1)
    keep = s_idx < t_idx

    heads = [slice(hd * SB_HEAD_DIM, (hd + 1) * SB_HEAD_DIM) for hd in range(SB_HEADS)]

    def sweep(q_rows, blocks, carry):
        starts = [pl.multiple_of(j * blk, blk) for j, _ in blocks]
        zs = [[lax.dot_general(q_ref[q_rows, hs], k_ref[pl.ds(st, blk), hs], (((1,), (1,)), ((), ())),
                               preferred_element_type=_F32) for hs in heads] for st in starts]
        log_betas, addends = [], []
        for (_, mask), z_heads in zip(blocks, zs):
            for z in z_heads:
                softplus = jnp.log2(1.0 + jnp.exp2(-jnp.abs(z)))
                log_beta = jnp.minimum(z, 0.0) - softplus
                log_rest = log_beta - z
                if mask is not None:
                    log_rest = jnp.where(mask, log_rest, 0.0)
                log_betas.append(log_beta)
                addends.append(log_rest.astype(_BF16))
        sums = [jnp.dot(t, tri_ref[...], preferred_element_type=_F32) for t in addends]
        weights, total = [], None
        for b, (_, mask) in enumerate(blocks):
            block_sums = sums[b * SB_HEADS:(b + 1) * SB_HEADS]
            before = carry if total is None else (total if carry is None else carry + total)
            for hd, s in enumerate(block_sums):
                expo = log_betas[b * SB_HEADS + hd] + s[:, :blk]
                if before is not None:
                    expo = expo + before[:, heads[hd]]
                a = jnp.exp2(expo)
                if mask is not None:
                    a = jnp.where(mask, a, 0.0)
                weights.append(a.astype(_BF16))
            block_total = jnp.concatenate([s[:, blk:] for s in block_sums], axis=1)
            total = block_total if total is None else total + block_total
        outs = []
        for hd, hs in enumerate(heads):
            o = None
            for b, st in enumerate(starts):
                t = jnp.dot(weights[b * SB_HEADS + hd], v_ref[pl.ds(st, blk), hs],
                            preferred_element_type=_F32)
                o = t if o is None else o + t
            outs.append(o)
        return jnp.concatenate(outs, axis=1), total

    def query_block(sub):
        q_rows = slice(sub * blk, (sub + 1) * blk)
        qi = pl.program_id(1) * SB_STEP_BLOCKS + sub
        out, carry = sweep(q_rows, [(qi, keep)], None)
        acc_ref[...] = out
        carry_ref[...] = carry

        def cond(state):
            j, alive = state
            return jnp.logical_and(j >= 0, alive)

        def body(state):
            j, _ = state
            carry = carry_ref[...]
            out, total = sweep(q_rows, [(j, None), (jnp.maximum(j - 1, 0), j >= 1)], carry)
            carry = carry + total
            acc_ref[...] += out
            carry_ref[...] = carry
            return j - 2, jnp.max(carry) > -SB_CUTOFF

        lax.while_loop(cond, body, (qi - 1, jnp.max(carry) > -SB_CUTOFF))
        o_ref[q_rows, :] = acc_ref[...].astype(o_ref.dtype)

    for sub in range(SB_STEP_BLOCKS):
        query_block(sub)


def _sb_attention(qkv, batch, seq):
    n = qkv.shape[0]
    rows = SB_STEP_BLOCKS * SB_BLOCK
    nq = seq // rows
    j = jnp.arange(SB_BLOCK)[:, None]
    s = jnp.arange(SB_BLOCK)[None, :]
    tri = jnp.concatenate([(j > s).astype(_BF16), jnp.ones((SB_BLOCK, SB_BLOCK), _BF16)], axis=1)
    whole_seq = lambda col: pl.BlockSpec((seq, D_MODEL), lambda b, i: (b, col))
    return pl.pallas_call(
        _sb_attention_kernel,
        grid=(batch, nq),
        in_specs=[pl.BlockSpec((rows, D_MODEL), lambda b, i: (b * nq + i, 0)),
                  whole_seq(1), whole_seq(2), _const_spec((SB_BLOCK, 2 * SB_BLOCK))],
        out_specs=pl.BlockSpec((rows, D_MODEL), lambda b, i: (b * nq + i, 0)),
        out_shape=jax.ShapeDtypeStruct((n, D_MODEL), _BF16),
        scratch_shapes=[pltpu.VMEM((SB_BLOCK, D_MODEL), _F32), pltpu.VMEM((SB_BLOCK, D_MODEL), _F32)],
        compiler_params=_params("arbitrary", "arbitrary"),
        name="sb_attention",
    )(qkv, qkv, qkv, tri)


def _router_kernel(x_ref, w_ref, b_ref, tri_ref, idx_ref, gate_ref, rank_ref, count_ref, base_ref):
    @pl.when(pl.program_id(0) == 0)
    def _():
        base_ref[...] = jnp.zeros_like(base_ref)

    x = x_ref[...]
    w = w_ref[...]
    x_hi = x.astype(_BF16)
    x_lo = (x - x_hi.astype(_F32)).astype(_BF16)
    w_hi = w.astype(_BF16)
    w_lo = (w - w_hi.astype(_F32)).astype(_BF16)
    nt = (((1,), (1,)), ((), ()))
    logits = (lax.dot_general(w_hi, x_hi, nt, preferred_element_type=_F32)
              + lax.dot_general(w_lo, x_hi, nt, preferred_element_type=_F32)
              + lax.dot_general(w_hi, x_lo, nt, preferred_element_type=_F32)) + b_ref[...]
    expert = lax.broadcasted_iota(jnp.int32, logits.shape, 0)
    vals, idxs = [], []
    for _ in range(TOP_K):
        m = jnp.max(logits, axis=0, keepdims=True)
        sel = jnp.min(jnp.where(logits == m, expert, N_EXPERTS), axis=0, keepdims=True)
        vals.append(m)
        idxs.append(sel)
        logits = jnp.where(expert == sel, -jnp.inf, logits)
    vals = jnp.concatenate(vals, axis=0)
    e = jnp.exp(vals - vals[0:1])
    gate_ref[...] = e / jnp.sum(e, axis=0, keepdims=True)
    idx_ref[...] = jnp.concatenate(idxs, axis=0)

    member = jnp.zeros(logits.shape, _F32)
    for sel in idxs:
        member += (expert == sel).astype(_F32)
    before = jnp.dot(member.astype(_BF16), tri_ref[...], preferred_element_type=_F32) + base_ref[...]
    ranks = [jnp.sum(jnp.where(expert == sel, before, 0.0), axis=0, keepdims=True) for sel in idxs]
    rank_ref[...] = jnp.concatenate(ranks, axis=0).astype(jnp.int32)
    base_ref[...] += jnp.sum(member, axis=1, keepdims=True)
    count_ref[...] = base_ref[...]


def _router(x, router_w, router_b):
    n, d = x.shape
    t = jnp.arange(ROUTER_ROWS)
    tri = (t[:, None] < t[None, :]).astype(_BF16)
    tok = pl.BlockSpec((TOP_K, ROUTER_ROWS), lambda i: (0, i))
    return pl.pallas_call(
        _router_kernel,
        grid=(n // ROUTER_ROWS,),
        in_specs=[pl.BlockSpec((ROUTER_ROWS, d), lambda i: (i, 0)),
                  _const_spec((N_EXPERTS, d)), _const_spec((N_EXPERTS, 1)),
                  _const_spec((ROUTER_ROWS, ROUTER_ROWS))],
        out_specs=[tok, tok, tok, _const_spec((N_EXPERTS, 1))],
        out_shape=[jax.ShapeDtypeStruct((TOP_K, n), jnp.int32),
                   jax.ShapeDtypeStruct((TOP_K, n), _F32),
                   jax.ShapeDtypeStruct((TOP_K, n), jnp.int32),
                   jax.ShapeDtypeStruct((N_EXPERTS, 1), _F32)],
        scratch_shapes=[pltpu.VMEM((N_EXPERTS, 1), _F32)],
        compiler_params=_params("arbitrary"),
        name="router",
    )(x, router_w.T, router_b.reshape(N_EXPERTS, 1), tri)


def _dispatch(x_words, slot_of, p):
    n, words = x_words.shape
    part = words // DISPATCH_PARTS
    mesh = plsc.VectorSubcoreMesh(core_axis_name="core", subcore_axis_name="subcore",
                                  num_cores=SC_CORES, num_subcores=SC_SUBCORES)
    out = jax.ShapeDtypeStruct((p, part), x_words.dtype)

    @functools.partial(
        pl.kernel, out_type=[out] * DISPATCH_PARTS, mesh=mesh, scratch_types=[],
        compiler_params=pltpu.CompilerParams(use_tc_tiling_on_sc=True), name="moe_dispatch")
    def dispatch(x_hbm, slot_hbm, *o_hbm):
        for c in range(DISPATCH_PARTS):
            def body(x_vmem, slot_vmem, o_ref=o_hbm[c]):
                for k in range(TOP_K):
                    pltpu.sync_copy(x_vmem, o_ref.at[slot_vmem.at[k]])

            pltpu.emit_pipeline(
                body,
                grid=(n // DISPATCH_ROWS,),
                in_specs=[pl.BlockSpec((DISPATCH_ROWS, part), lambda i, c=c: (i, c)),
                          pl.BlockSpec((TOP_K, DISPATCH_ROWS), lambda i: (0, i))],
                out_specs=[],
                core_axis_name=("core", "subcore"),
                dimension_semantics=(pltpu.PARALLEL,),
            )(x_hbm, slot_hbm)

    return dispatch(x_words, slot_of.reshape(TOP_K, n))


def _combine_gather(y_parts, slot_of):
    nk = slot_of.shape[0]
    part = y_parts[0].shape[1]
    mesh = plsc.VectorSubcoreMesh(core_axis_name="core", subcore_axis_name="subcore",
                                  num_cores=SC_CORES, num_subcores=SC_SUBCORES)
    out = jax.ShapeDtypeStruct((nk, part), y_parts[0].dtype)

    @functools.partial(
        pl.kernel, out_type=[out] * DISPATCH_PARTS, mesh=mesh, scratch_types=[],
        compiler_params=pltpu.CompilerParams(use_tc_tiling_on_sc=True), name="moe_combine_gather")
    def gather(*refs):
        y_hbm, slot_hbm, o_hbm = refs[:DISPATCH_PARTS], refs[DISPATCH_PARTS], refs[DISPATCH_PARTS + 1:]
        for c in range(DISPATCH_PARTS):
            def body(slot_vmem, o_vmem, y_ref=y_hbm[c]):
                pltpu.sync_copy(y_ref.at[slot_vmem.at[0]], o_vmem)

            pltpu.emit_pipeline(
                body,
                grid=(nk // DISPATCH_ROWS,),
                in_specs=[pl.BlockSpec((1, DISPATCH_ROWS), lambda i: (0, i))],
                out_specs=[pl.BlockSpec((DISPATCH_ROWS, part), lambda i: (i, 0))],
                core_axis_name=("core", "subcore"),
                dimension_semantics=(pltpu.PARALLEL,),
            )(slot_hbm, o_hbm[c])

    return gather(*y_parts, slot_of.reshape(1, nk))


def _moe_ffn_kernel(block_e_ref, block_rows_ref, last_block_ref, xa_ref, xb_ref, wu_f32_ref, sel_ref,
                    bu_ref, wd_f32_ref, bd_ref, oa_ref, ob_ref, wu_ref, wd_ref):
    step = pl.program_id(0)
    rows_used = block_rows_ref[step]
    part = oa_ref.shape[1]
    new_expert = jnp.logical_or(step == 0,
                                block_e_ref[step] != block_e_ref[jnp.maximum(step - 1, 0)])

    @pl.when(jnp.logical_and(new_expert, rows_used > 0))
    def _():
        wd_ref[...] = wd_f32_ref[0].astype(_BF16)
        group = 2 * PREP_LANES
        for c in range(wu_f32_ref.shape[2] // group):
            t = jnp.dot(wu_f32_ref[0, :, c * group:(c + 1) * group].astype(_BF16), sel_ref[...],
                        preferred_element_type=_F32)
            wu_ref[:, c * PREP_LANES:(c + 1) * PREP_LANES] = t[:, :PREP_LANES].astype(_BF16)
            wu_ref[:, D_FF + c * PREP_LANES:D_FF + (c + 1) * PREP_LANES] = t[:, PREP_LANES:].astype(_BF16)

    def ffn_rows(*ranges):
        xs = [_unpack_bf16_pairs(jnp.concatenate([xa_ref[lo:hi, :], xb_ref[lo:hi, :]], axis=1))
              for lo, hi in ranges]
        hs = [jnp.dot(x, wu_ref[...], preferred_element_type=_F32) + bu_ref[0] for x in xs]
        acts = []
        for h in hs:
            h_glu = jnp.minimum(h[:, :D_FF], SWIGLU_LIMIT)
            h_lin = jnp.clip(h[:, D_FF:], -SWIGLU_LIMIT, SWIGLU_LIMIT)
            acts.append((h_glu * jax.nn.sigmoid(SWIGLU_ALPHA * h_glu) * (h_lin + 1.0)).astype(_BF16))
        ys = [jnp.dot(a, wd_ref[...], preferred_element_type=_F32) + bd_ref[0] for a in acts]
        for (lo, hi), y in zip(ranges, ys):
            words = _pack_bf16_pairs(y)
            oa_ref[lo:hi, :] = words[:, :part]
            ob_ref[lo:hi, :] = words[:, part:]

    def zero_rows(lo, hi):
        oa_ref[lo:hi, :] = jnp.zeros((hi - lo, part), oa_ref.dtype)
        ob_ref[lo:hi, :] = jnp.zeros((hi - lo, part), ob_ref.dtype)

    sub_blocks = [(lo, lo + MOE_ROWS) for lo in range(0, MOE_STEP_ROWS, MOE_ROWS)]
    whole_step = rows_used > MOE_STEP_ROWS - MOE_ROWS + MOE_TAIL_ROWS

    @pl.when(whole_step)
    def _():
        ffn_rows(*sub_blocks)

    @pl.when(jnp.logical_not(whole_step))
    def _():
        for lo, hi in sub_blocks:
            mid = lo + MOE_TAIL_ROWS

            @pl.when(rows_used > mid)
            def _():
                ffn_rows((lo, hi))

            @pl.when(jnp.logical_and(rows_used > lo, rows_used <= mid))
            def _():
                ffn_rows((lo, mid))
                zero_rows(mid, hi)

            @pl.when(rows_used <= lo)
            def _():
                zero_rows(lo, hi)


def _moe_ffn(xs, block_e, block_rows, last_block, wu_f32, bu, wd_f32, bd):
    xa, xb = xs
    p, part = xa.shape
    ff, d = wd_f32.shape[1:]
    ff2 = wu_f32.shape[2]
    lane = jnp.arange(2 * PREP_LANES)
    sel = (lane[:, None] == jnp.where(lane < PREP_LANES, 2 * lane, 2 * (lane - PREP_LANES) + 1)[None, :]
           ).astype(_BF16)
    row = lambda i, be, br, lb: (jnp.minimum(i, lb[0]), 0)
    exp3 = lambda i, be, br, lb: (be[i], 0, 0)
    rows_spec = pl.BlockSpec((MOE_STEP_ROWS, part), row)
    grid_spec = pltpu.PrefetchScalarGridSpec(
        num_scalar_prefetch=3,
        grid=(p // MOE_STEP_ROWS,),
        in_specs=[rows_spec, rows_spec,
                  pl.BlockSpec((1, d, ff2), exp3),
                  pl.BlockSpec((2 * PREP_LANES, 2 * PREP_LANES), lambda i, be, br, lb: (0, 0)),
                  pl.BlockSpec((1, 1, ff2), exp3),
                  pl.BlockSpec((1, ff, d), exp3), pl.BlockSpec((1, 1, d), exp3)],
        out_specs=[rows_spec, rows_spec],
        scratch_shapes=[pltpu.VMEM((d, ff2), _BF16), pltpu.VMEM((ff, d), _BF16)],
    )
    out = jax.ShapeDtypeStruct((p, part), xa.dtype)
    return pl.pallas_call(
        _moe_ffn_kernel,
        grid_spec=grid_spec,
        out_shape=[out, out],
        compiler_params=_params("arbitrary", vmem_limit=MOE_VMEM_LIMIT_BYTES),
        name="moe_ffn",
    )(block_e, block_rows, last_block, xa, xb, wu_f32, sel, bu, wd_f32, bd)


def _combine_ln_kernel(ya_ref, yb_ref, gate_ref, res_ref, g_ref, b_ref, o_ref, *maybe_obf_ref):
    gate = gate_ref[...]
    gate = jnp.concatenate([gate, jnp.zeros((F32_SUBLANES - TOP_K, gate.shape[1]), _F32)], axis=0).T
    moe = None
    for k in range(TOP_K):
        words = jnp.concatenate([ya_ref[k], yb_ref[k]], axis=1)
        term = _unpack_bf16_pairs(words).astype(_F32) * gate[:, k:k + 1]
        moe = term if moe is None else moe + term
    y = _layer_norm_rows(DEEPNORM_ALPHA * res_ref[...] + moe, g_ref[...], b_ref[...])
    o_ref[...] = y
    for obf_ref in maybe_obf_ref:
        obf_ref[...] = y.astype(_BF16)


def _combine_ln(y4_parts, gates, res, g, b, with_bf16):
    n, d = res.shape
    ya, yb = (t.reshape(TOP_K, n, t.shape[1]) for t in y4_parts)
    part = ya.shape[2]
    row = lambda i: (i, 0)
    y_spec = pl.BlockSpec((TOP_K, COMBINE_ROWS, part), lambda i: (0, i, 0))
    return pl.pallas_call(
        _combine_ln_kernel,
        grid=(n // COMBINE_ROWS,),
        in_specs=[y_spec, y_spec, pl.BlockSpec((TOP_K, COMBINE_ROWS), lambda i: (0, i)),
                  pl.BlockSpec((COMBINE_ROWS, d), row), _const_spec((1, d)), _const_spec((1, d))],
        out_specs=[pl.BlockSpec((COMBINE_ROWS, d), row)] * (2 if with_bf16 else 1),
        out_shape=[jax.ShapeDtypeStruct((n, d), _F32)]
        + ([jax.ShapeDtypeStruct((n, d), _BF16)] if with_bf16 else []),
        compiler_params=_params("arbitrary"),
        name="combine_ln",
    )(ya, yb, gates, res, g.reshape(1, d), b.reshape(1, d))


def _moe_layer(x, x_words, router_w, router_b, w_up, b_up, w_down, b_down, ln_g, ln_b, with_bf16):
    n, d = x.shape
    nk = n * TOP_K
    idx_t, gate_t, rank_t, count = _router(x, router_w, router_b)
    counts = count[:, 0].astype(jnp.int32)
    padded = (counts + MOE_STEP_ROWS - 1) // MOE_STEP_ROWS * MOE_STEP_ROWS
    padded_ends = jnp.cumsum(padded)
    padded_starts = padded_ends - padded
    n_blocks = -(-(nk + N_EXPERTS * (MOE_STEP_ROWS - 1)) // MOE_STEP_ROWS)
    p = n_blocks * MOE_STEP_ROWS
    short = counts % MOE_STEP_ROWS
    experts = jnp.arange(N_EXPERTS, dtype=jnp.int32)
    chosen = idx_t[None] == experts[:, None, None]
    start_of = jnp.sum(jnp.where(chosen, padded_starts[:, None, None], 0), axis=0)
    short_of = jnp.sum(jnp.where(chosen, short[:, None, None], 0), axis=0)
    gap_of = jnp.where(jnp.logical_and(short_of > 0, rank_t >= short_of), MOE_STEP_ROWS - short_of, 0)
    slot_of = (start_of + rank_t + gap_of).reshape(nk)
    block_start = jnp.arange(n_blocks, dtype=jnp.int32) * MOE_STEP_ROWS
    block_e = jnp.minimum(jnp.sum(padded_ends[None, :] <= block_start[:, None], axis=1),
                          N_EXPERTS - 1).astype(jnp.int32)
    first_of_expert = block_start == padded_starts[block_e]
    block_rows = jnp.where(block_start >= padded_ends[-1], 0,
                           jnp.where(jnp.logical_and(first_of_expert, short[block_e] > 0),
                                     short[block_e], MOE_STEP_ROWS)).astype(jnp.int32)
    last_block = jnp.minimum(padded_ends[-1] // MOE_STEP_ROWS, n_blocks - 1).astype(jnp.int32)
    xs = _dispatch(x_words, slot_of, p)
    bu = b_up.reshape(N_EXPERTS, D_FF, 2).transpose(0, 2, 1).reshape(N_EXPERTS, 1, 2 * D_FF)
    ys = _moe_ffn(xs, block_e, block_rows, last_block.reshape(1), w_up, bu, w_down,
                  b_down.reshape(N_EXPERTS, 1, d))
    return _combine_ln(_combine_gather(ys, slot_of), gate_t, x, ln_g, ln_b, with_bf16)


def kernel(x, l0_w_in, l0_ret_gn_g, l0_conv_w, l0_conv_b, l0_conv_ln_g, l0_conv_ln_b, l0_w_out, l0_ln1_g, l0_ln1_b, l0_router_w, l0_router_b, l0_w_up, l0_b_up, l0_w_down, l0_b_down, l0_ln2_g, l0_ln2_b, l1_w_in, l1_w_out, l1_ln1_g, l1_ln1_b, l1_router_w, l1_router_b, l1_w_up, l1_b_up, l1_w_down, l1_b_down, l1_ln2_g, l1_ln2_b):
    batch, seq, d = x.shape
    n = batch * seq
    x0 = x.reshape(n, d)

    x1, x1_words = _l0_block(x0, batch, seq, l0_w_in, l0_ret_gn_g, l0_conv_w, l0_conv_b,
                             l0_conv_ln_g, l0_conv_ln_b, l0_w_out, l0_ln1_g, l0_ln1_b)
    x2, x2_bf16 = _moe_layer(x1, x1_words, l0_router_w, l0_router_b, l0_w_up, l0_b_up, l0_w_down,
                             l0_b_down, l0_ln2_g, l0_ln2_b, True)

    q_scale = jnp.where(jnp.arange(3 * d) < d, LOG2_E * SB_HEAD_DIM ** -0.5, 1.0).astype(_F32)
    qkv = _proj(x2_bf16, l1_w_in, q_scale)
    att = _sb_attention(qkv, batch, seq)
    x3, x3_words = _proj_res_ln(att, l1_w_out, x2, l1_ln1_g, l1_ln1_b)
    (x4,) = _moe_layer(x3, x3_words, l1_router_w, l1_router_b, l1_w_up, l1_b_up, l1_w_down,
                       l1_b_down, l1_ln2_g, l1_ln2_b, False)
    return x4.reshape(batch, seq, d)
```

```python
import functools

import jax
import jax.numpy as jnp
from jax import lax
from jax.experimental import pallas as pl
from jax.experimental.pallas import tpu as pltpu
from jax.experimental.pallas import tpu_sc as plsc

D_MODEL = 1024
CHUNK = 64
RET_HEADS = 4
RET_DK = 64
RET_DV = 128
RET_QK_WIDTH = RET_HEADS * RET_DK
RET_WIDTH = RET_HEADS * RET_DV
ROPE_BASE = 10000.0
CONV_CH = D_MODEL // 2
CONV_WIDTH = 31
L0_IN = 2 * RET_QK_WIDTH + 2 * RET_WIDTH + 2 * CONV_CH
SB_HEADS = 8
SB_HEAD_DIM = D_MODEL // SB_HEADS
N_EXPERTS = 32
TOP_K = 4
D_FF = D_MODEL
SWIGLU_LIMIT = 7.0
SWIGLU_ALPHA = 1.702
LN_EPS = 1e-5
DEPTH = 2
DEEPNORM_ALPHA = (2 * DEPTH) ** 0.25

VMEM_LIMIT_BYTES = 48 * 1024 * 1024
MOE_VMEM_LIMIT_BYTES = 56 * 1024 * 1024
F32_SUBLANES = 8
PROJ_ROWS = 1024
MIX_ROWS = 256
CONV_HALO = 32
ROUTER_ROWS = 1024
MOE_ROWS = 512
MOE_TAIL_ROWS = 256
MOE_STEP_ROWS = 1024
PREP_LANES = 128
COMBINE_ROWS = 512
DISPATCH_ROWS = 128
DISPATCH_PARTS = 2
SC_CORES = 2
SC_SUBCORES = 16
SB_BLOCK = 128
SB_STEP_BLOCKS = 2
SB_CUTOFF = 150.0
LOG2_E = 1.4426950408889634

_F32 = jnp.float32
_BF16 = jnp.bfloat16


def _params(*sem, vmem_limit=VMEM_LIMIT_BYTES):
    return pltpu.CompilerParams(dimension_semantics=sem, vmem_limit_bytes=vmem_limit)


def _const_spec(shape):
    nd = len(shape)
    return pl.BlockSpec(shape, lambda *_: (0,) * nd)


def _weight_spec(shape):
    nd = len(shape)
    return pl.BlockSpec(shape, lambda *_: (0,) * nd, pipeline_mode=pl.Buffered(1))


def _layer_norm_rows(y, g, b):
    mu = jnp.mean(y, axis=-1, keepdims=True)
    yc = y - mu
    var = jnp.mean(yc * yc, axis=-1, keepdims=True)
    return yc * lax.rsqrt(var + LN_EPS) * g + b


def _silu(x):
    return x * jax.nn.sigmoid(x)


def _pack_bf16_pairs(y):
    half = y.shape[1] // 2
    lo = lax.bitcast_convert_type(y[:, :half].astype(_BF16).astype(_F32), jnp.uint32)
    hi = lax.bitcast_convert_type(y[:, half:].astype(_BF16).astype(_F32), jnp.uint32)
    return (lo >> 16) | (hi & jnp.uint32(0xFFFF0000))


def _unpack_pairs_f32(w):
    lo = lax.bitcast_convert_type(w << 16, _F32)
    hi = lax.bitcast_convert_type(w & jnp.uint32(0xFFFF0000), _F32)
    return jnp.concatenate([lo, hi], axis=1)


def _unpack_bf16_pairs(w):
    return _unpack_pairs_f32(w).astype(_BF16)


def _proj_kernel(x_ref, w_f32_ref, scale_ref, o_ref, w_ref):
    @pl.when(pl.program_id(0) == 0)
    def _():
        w_ref[...] = w_f32_ref[...].astype(_BF16)

    o_ref[...] = (jnp.dot(x_ref[...].astype(_BF16), w_ref[...], preferred_element_type=_F32)
                  * scale_ref[...]).astype(o_ref.dtype)


def _proj(x, w, col_scale):
    n, d = x.shape
    width = w.shape[1]
    return pl.pallas_call(
        _proj_kernel,
        grid=(n // PROJ_ROWS,),
        in_specs=[pl.BlockSpec((PROJ_ROWS, d), lambda i: (i, 0)), _weight_spec((d, width)),
                  _const_spec((1, width))],
        out_specs=pl.BlockSpec((PROJ_ROWS, width), lambda i: (i, 0)),
        out_shape=jax.ShapeDtypeStruct((n, width), _BF16),
        scratch_shapes=[pltpu.VMEM((d, width), _BF16)],
        compiler_params=_params("arbitrary"),
        name="proj",
    )(x, w, col_scale.reshape(1, width))


def _proj_res_ln_kernel(a_ref, w_f32_ref, res_ref, g_ref, b_ref, o_ref, words_ref, w_ref):
    @pl.when(pl.program_id(0) == 0)
    def _():
        w_ref[...] = w_f32_ref[...].astype(_BF16)

    m = jnp.dot(a_ref[...], w_ref[...], preferred_element_type=_F32)
    y = _layer_norm_rows(DEEPNORM_ALPHA * res_ref[...] + m, g_ref[...], b_ref[...])
    o_ref[...] = y
    words_ref[...] = _pack_bf16_pairs(y)


def _proj_res_ln(a_bf16, w, res, g, b):
    n, d = res.shape
    k = a_bf16.shape[1]
    row = lambda i: (i, 0)
    return pl.pallas_call(
        _proj_res_ln_kernel,
        grid=(n // PROJ_ROWS,),
        in_specs=[pl.BlockSpec((PROJ_ROWS, k), row), _weight_spec((k, d)),
                  pl.BlockSpec((PROJ_ROWS, d), row), _const_spec((1, d)), _const_spec((1, d))],
        out_specs=[pl.BlockSpec((PROJ_ROWS, d), row), pl.BlockSpec((PROJ_ROWS, d // 2), row)],
        out_shape=[jax.ShapeDtypeStruct((n, d), _F32), jax.ShapeDtypeStruct((n, d // 2), jnp.uint32)],
        scratch_shapes=[pltpu.VMEM((k, d), _BF16)],
        compiler_params=_params("arbitrary"),
        name="proj_res_ln",
    )(a_bf16, w, res, g.reshape(1, d), b.reshape(1, d))


def _l0_block_kernel(x_ref, w_in_f32_ref, cos_ref, sin_ref, xi_ref, zeta_ref, decay_ref, gstate_ref,
                     gn_g_ref, conv_w_ref, conv_b_ref, cln_g_ref, cln_b_ref, w_out_f32_ref, ln_g_ref,
                     ln_b_ref, o_ref, words_ref, state_ref, u_ref, shift_ref, h_ref, mix_ref,
                     w_in_ref, w_out_ref):
    rows = MIX_ROWS
    step = pl.program_id(1)
    v_off = 2 * RET_QK_WIDTH
    g_off = v_off + RET_WIDTH
    a_off = g_off + RET_WIDTH

    @pl.when(jnp.logical_and(pl.program_id(0) == 0, step == 0))
    def _():
        w_in_ref[...] = w_in_f32_ref[...].astype(_BF16)
        w_out_ref[...] = w_out_f32_ref[...].astype(_BF16)

    @pl.when(step == 0)
    def _():
        state_ref[...] = jnp.zeros_like(state_ref)
        u_ref[0:CONV_HALO, :] = jnp.zeros((CONV_HALO, CONV_CH), _F32)

    x_bf16 = x_ref[...].astype(_BF16)
    h_conv = jnp.dot(x_bf16, w_in_ref[:, a_off:a_off + 2 * CONV_CH], preferred_element_type=_F32)
    h_ref[...] = jnp.dot(x_bf16, w_in_ref[:, 0:a_off], preferred_element_type=_F32)

    u_ref[CONV_HALO:CONV_HALO + rows, :] = h_conv[:, :CONV_CH] * jax.nn.sigmoid(h_conv[:, CONV_CH:])
    acc = jnp.broadcast_to(conv_b_ref[...], (rows, CONV_CH))
    first = CONV_HALO - (CONV_WIDTH - 1)
    for phase in range(F32_SUBLANES):
        offsets = [first + j for j in range(CONV_WIDTH) if (first + j) % F32_SUBLANES == phase]
        span = max(offsets) - phase + rows
        if phase == 0:
            src = u_ref
        else:
            shift_ref[0:span, :] = u_ref[phase:phase + span, :]
            src = shift_ref
        for off in offsets:
            j = off - first
            acc = acc + conv_w_ref[j:j + 1, :] * src[off - phase:off - phase + rows, :]
    u_ref[0:CONV_HALO, :] = u_ref[rows:rows + CONV_HALO, :]
    c = _silu(_layer_norm_rows(acc, cln_g_ref[...], cln_b_ref[...]))
    mix_ref[:, RET_WIDTH:RET_WIDTH + CONV_CH] = c.astype(_BF16)

    lane = lax.broadcasted_iota(jnp.int32, (rows, RET_QK_WIDTH), 1)
    first_half = (lane % RET_DK) < (RET_DK // 2)
    cos = cos_ref[...]
    sin = sin_ref[...]

    def rotary(t):
        partner = jnp.where(first_half,
                            pltpu.roll(t, RET_QK_WIDTH - RET_DK // 2, axis=1),
                            pltpu.roll(t, RET_DK // 2, axis=1))
        return t * cos + partner * sin

    q = rotary(h_ref[:, 0:RET_QK_WIDTH])
    k = rotary(h_ref[:, RET_QK_WIDTH:2 * RET_QK_WIDTH]) * (RET_DK ** -0.5)
    q_in = (q * xi_ref[...]).astype(_BF16)
    k_out = (k * zeta_ref[...]).astype(_BF16)
    q = q.astype(_BF16)
    k = k.astype(_BF16)
    for hd in range(RET_HEADS):
        qk = slice(hd * RET_DK, (hd + 1) * RET_DK)
        vs = slice(hd * RET_DV, (hd + 1) * RET_DV)
        v = h_ref[:, v_off + hd * RET_DV:v_off + (hd + 1) * RET_DV].astype(_BF16)
        s = lax.dot_general(q[:, qk], k[:, qk], (((1,), (1,)), ((), ())),
                            preferred_element_type=_F32) * decay_ref[hd]
        r = jnp.dot(s.astype(_BF16), v, preferred_element_type=_F32)
        r += jnp.dot(q_in[:, qk], state_ref[hd].astype(_BF16), preferred_element_type=_F32)
        kv = lax.dot_general(k_out[:, qk], v, (((0,), (0,)), ((), ())),
                             preferred_element_type=_F32)
        state_ref[hd] = gstate_ref[hd] * state_ref[hd] + kv
        mu = jnp.mean(r, axis=-1, keepdims=True)
        rc = r - mu
        var = jnp.mean(rc * rc, axis=-1, keepdims=True)
        rn = rc * lax.rsqrt(var + LN_EPS) * gn_g_ref[:, vs]
        gate = h_ref[:, g_off + hd * RET_DV:g_off + (hd + 1) * RET_DV]
        mix_ref[:, vs] = (_silu(gate) * rn).astype(_BF16)

    m = jnp.dot(mix_ref[...], w_out_ref[...], preferred_element_type=_F32)
    y = _layer_norm_rows(DEEPNORM_ALPHA * x_ref[...] + m, ln_g_ref[...], ln_b_ref[...])
    o_ref[...] = y
    words_ref[...] = _pack_bf16_pairs(y)


def _l0_mixer_tables(seq):
    half = RET_DK // 2
    inv = ROPE_BASE ** (-jnp.arange(half, dtype=_F32) / half)
    ang = jnp.arange(seq).astype(_F32)[:, None] * inv[None, :]
    cos = jnp.tile(jnp.cos(ang), (1, 2 * RET_HEADS))
    sin = jnp.tile(jnp.concatenate([-jnp.sin(ang), jnp.sin(ang)], axis=1), (1, RET_HEADS))
    log_g = jnp.log(1.0 - 2.0 ** (-5.0 - jnp.arange(RET_HEADS, dtype=_F32)))
    n = jnp.arange(MIX_ROWS, dtype=_F32)
    diff = n[:, None] - n[None, :]
    chunk = jnp.arange(MIX_ROWS) // CHUNK
    expo = jnp.where(chunk[:, None] == chunk[None, :], jnp.abs(diff), diff)
    decay = jnp.where((chunk[:, None] >= chunk[None, :])[None],
                      jnp.exp(expo[None] * log_g[:, None, None]), 0.0)
    xi = jnp.repeat(jnp.exp((n[:, None] + 1.0) * log_g[None, :]), RET_DK, axis=1)
    zeta = jnp.repeat(jnp.exp((MIX_ROWS - 1.0 - n[:, None]) * log_g[None, :]), RET_DK, axis=1)
    g_state = jnp.exp(MIX_ROWS * log_g)
    return cos, sin, xi, zeta, decay, g_state


def _l0_block(x, batch, seq, w_in, gn_g, conv_w, conv_b, cln_g, cln_b, w_out, ln_g, ln_b):
    n, d = x.shape
    steps = seq // MIX_ROWS
    cos, sin, xi, zeta, decay, g_state = _l0_mixer_tables(seq)
    row = lambda b, i: (b * steps + i, 0)
    pos = lambda b, i: (i, 0)
    vec = lambda t: t.reshape(1, -1)
    return pl.pallas_call(
        _l0_block_kernel,
        grid=(batch, steps),
        in_specs=[pl.BlockSpec((MIX_ROWS, d), row), _weight_spec((d, L0_IN)),
                  pl.BlockSpec((MIX_ROWS, RET_QK_WIDTH), pos),
                  pl.BlockSpec((MIX_ROWS, RET_QK_WIDTH), pos),
                  _const_spec((MIX_ROWS, RET_QK_WIDTH)), _const_spec((MIX_ROWS, RET_QK_WIDTH)),
                  _const_spec((RET_HEADS, MIX_ROWS, MIX_ROWS)),
                  pl.BlockSpec(memory_space=pltpu.SMEM),
                  _const_spec((1, RET_WIDTH)), _const_spec((CONV_WIDTH, CONV_CH)),
                  _const_spec((1, CONV_CH)), _const_spec((1, CONV_CH)), _const_spec((1, CONV_CH)),
                  _weight_spec((RET_WIDTH + CONV_CH, d)), _const_spec((1, d)), _const_spec((1, d))],
        out_specs=[pl.BlockSpec((MIX_ROWS, d), row), pl.BlockSpec((MIX_ROWS, d // 2), row)],
        out_shape=[jax.ShapeDtypeStruct((n, d), _F32), jax.ShapeDtypeStruct((n, d // 2), jnp.uint32)],
        scratch_shapes=[pltpu.VMEM((RET_HEADS, RET_DK, RET_DV), _F32),
                        pltpu.VMEM((MIX_ROWS + CONV_HALO, CONV_CH), _F32),
                        pltpu.VMEM((MIX_ROWS + CONV_HALO, CONV_CH), _F32),
                        pltpu.VMEM((MIX_ROWS, 2 * RET_QK_WIDTH + 2 * RET_WIDTH), _F32),
                        pltpu.VMEM((MIX_ROWS, RET_WIDTH + CONV_CH), _BF16),
                        pltpu.VMEM((d, L0_IN), _BF16), pltpu.VMEM((RET_WIDTH + CONV_CH, d), _BF16)],
        compiler_params=_params("arbitrary", "arbitrary"),
        name="l0_block",
    )(x, w_in, cos, sin, xi, zeta, decay, g_state, vec(gn_g), conv_w, vec(conv_b),
      vec(cln_g), vec(cln_b), w_out, vec(ln_g), vec(ln_b))


def _sb_attention_kernel(q_ref, k_ref, v_ref, tri_ref, o_ref, acc_ref, carry_ref):
    blk = SB_BLOCK
    t_idx = lax.broadcasted_iota(jnp.int32, (blk, blk), 0)
    s_idx = lax.broadcasted_iota(jnp.int32, (blk, blk), 1)
    keep = s_idx < t_idx

    heads = [slice(hd * SB_HEAD_DIM, (hd + 1) * SB_HEAD_DIM) for hd in range(SB_HEADS)]

    def sweep(q_rows, blocks, carry):
        starts = [pl.multiple_of(j * blk, blk) for j, _ in blocks]
        zs = [[lax.dot_general(q_ref[q_rows, hs], k_ref[pl.ds(st, blk), hs], (((1,), (1,)), ((), ())),
                               preferred_element_type=_F32) for hs in heads] for st in starts]
        log_betas, addends = [], []
        for (_, mask), z_heads in zip(blocks, zs):
            for z in z_heads:
                softplus = jnp.log2(1.0 + jnp.exp2(-jnp.abs(z)))
                log_beta = jnp.minimum(z, 0.0) - softplus
                log_rest = log_beta - z
                if mask is not None:
                    log_rest = jnp.where(mask, log_rest, 0.0)
                log_betas.append(log_beta)
                addends.append(log_rest.astype(_BF16))
        sums = [jnp.dot(t, tri_ref[...], preferred_element_type=_F32) for t in addends]
        weights, total = [], None
        for b, (_, mask) in enumerate(blocks):
            block_sums = sums[b * SB_HEADS:(b + 1) * SB_HEADS]
            before = carry if total is None else (total if carry is None else carry + total)
            for hd, s in enumerate(block_sums):
                expo = log_betas[b * SB_HEADS + hd] + s[:, :blk]
                if before is not None:
                    expo = expo + before[:, heads[hd]]
                a = jnp.exp2(expo)
                if mask is not None:
                    a = jnp.where(mask, a, 0.0)
                weights.append(a.astype(_BF16))
            block_total = jnp.concatenate([s[:, blk:] for s in block_sums], axis=1)
            total = block_total if total is None else total + block_total
        outs = []
        for hd, hs in enumerate(heads):
            o = None
            for b, st in enumerate(starts):
                t = jnp.dot(weights[b * SB_HEADS + hd], v_ref[pl.ds(st, blk), hs],
                            preferred_element_type=_F32)
                o = t if o is None else o + t
            outs.append(o)
        return jnp.concatenate(outs, axis=1), total

    def query_block(sub):
        q_rows = slice(sub * blk, (sub + 1) * blk)
        qi = pl.program_id(1) * SB_STEP_BLOCKS + sub
        out, carry = sweep(q_rows, [(qi, keep)], None)
        acc_ref[...] = out
        carry_ref[...] = carry

        def cond(state):
            j, alive = state
            return jnp.logical_and(j >= 0, alive)

        def body(state):
            j, _ = state
            carry = carry_ref[...]
            out, total = sweep(q_rows, [(j, None), (jnp.maximum(j - 1, 0), j >= 1)], carry)
            carry = carry + total
            acc_ref[...] += out
            carry_ref[...] = carry
            return j - 2, jnp.max(carry) > -SB_CUTOFF

        lax.while_loop(cond, body, (qi - 1, jnp.max(carry) > -SB_CUTOFF))
        o_ref[q_rows, :] = acc_ref[...].astype(o_ref.dtype)

    for sub in range(SB_STEP_BLOCKS):
        query_block(sub)


def _sb_attention(qkv, batch, seq):
    n = qkv.shape[0]
    rows = SB_STEP_BLOCKS * SB_BLOCK
    nq = seq // rows
    j = jnp.arange(SB_BLOCK)[:, None]
    s = jnp.arange(SB_BLOCK)[None, :]
    tri = jnp.concatenate([(j > s).astype(_BF16), jnp.ones((SB_BLOCK, SB_BLOCK), _BF16)], axis=1)
    whole_seq = lambda col: pl.BlockSpec((seq, D_MODEL), lambda b, i: (b, col))
    return pl.pallas_call(
        _sb_attention_kernel,
        grid=(batch, nq),
        in_specs=[pl.BlockSpec((rows, D_MODEL), lambda b, i: (b * nq + i, 0)),
                  whole_seq(1), whole_seq(2), _const_spec((SB_BLOCK, 2 * SB_BLOCK))],
        out_specs=pl.BlockSpec((rows, D_MODEL), lambda b, i: (b * nq + i, 0)),
        out_shape=jax.ShapeDtypeStruct((n, D_MODEL), _BF16),
        scratch_shapes=[pltpu.VMEM((SB_BLOCK, D_MODEL), _F32), pltpu.VMEM((SB_BLOCK, D_MODEL), _F32)],
        compiler_params=_params("arbitrary", "arbitrary"),
        name="sb_attention",
    )(qkv, qkv, qkv, tri)


def _router_kernel(x_ref, w_ref, b_ref, tri_ref, idx_ref, gate_ref, rank_ref, count_ref, base_ref):
    @pl.when(pl.program_id(0) == 0)
    def _():
        base_ref[...] = jnp.zeros_like(base_ref)

    x = x_ref[...]
    w = w_ref[...]
    x_hi = x.astype(_BF16)
    x_lo = (x - x_hi.astype(_F32)).astype(_BF16)
    w_hi = w.astype(_BF16)
    w_lo = (w - w_hi.astype(_F32)).astype(_BF16)
    nt = (((1,), (1,)), ((), ()))
    logits = (lax.dot_general(w_hi, x_hi, nt, preferred_element_type=_F32)
              + lax.dot_general(w_lo, x_hi, nt, preferred_element_type=_F32)
              + lax.dot_general(w_hi, x_lo, nt, preferred_element_type=_F32)) + b_ref[...]
    expert = lax.broadcasted_iota(jnp.int32, logits.shape, 0)
    vals, idxs = [], []
    for _ in range(TOP_K):
        m = jnp.max(logits, axis=0, keepdims=True)
        sel = jnp.min(jnp.where(logits == m, expert, N_EXPERTS), axis=0, keepdims=True)
        vals.append(m)
        idxs.append(sel)
        logits = jnp.where(expert == sel, -jnp.inf, logits)
    vals = jnp.concatenate(vals, axis=0)
    e = jnp.exp(vals - vals[0:1])
    gate_ref[...] = e / jnp.sum(e, axis=0, keepdims=True)
    idx_ref[...] = jnp.concatenate(idxs, axis=0)

    member = jnp.zeros(logits.shape, _F32)
    for sel in idxs:
        member += (expert == sel).astype(_F32)
    before = jnp.dot(member.astype(_BF16), tri_ref[...], preferred_element_type=_F32) + base_ref[...]
    ranks = [jnp.sum(jnp.where(expert == sel, before, 0.0), axis=0, keepdims=True) for sel in idxs]
    rank_ref[...] = jnp.concatenate(ranks, axis=0).astype(jnp.int32)
    base_ref[...] += jnp.sum(member, axis=1, keepdims=True)
    count_ref[...] = base_ref[...]


def _router(x, router_w, router_b):
    n, d = x.shape
    t = jnp.arange(ROUTER_ROWS)
    tri = (t[:, None] < t[None, :]).astype(_BF16)
    tok = pl.BlockSpec((TOP_K, ROUTER_ROWS), lambda i: (0, i))
    return pl.pallas_call(
        _router_kernel,
        grid=(n // ROUTER_ROWS,),
        in_specs=[pl.BlockSpec((ROUTER_ROWS, d), lambda i: (i, 0)),
                  _const_spec((N_EXPERTS, d)), _const_spec((N_EXPERTS, 1)),
                  _const_spec((ROUTER_ROWS, ROUTER_ROWS))],
        out_specs=[tok, tok, tok, _const_spec((N_EXPERTS, 1))],
        out_shape=[jax.ShapeDtypeStruct((TOP_K, n), jnp.int32),
                   jax.ShapeDtypeStruct((TOP_K, n), _F32),
                   jax.ShapeDtypeStruct((TOP_K, n), jnp.int32),
                   jax.ShapeDtypeStruct((N_EXPERTS, 1), _F32)],
        scratch_shapes=[pltpu.VMEM((N_EXPERTS, 1), _F32)],
        compiler_params=_params("arbitrary"),
        name="router",
    )(x, router_w.T, router_b.reshape(N_EXPERTS, 1), tri)


def _dispatch(x_words, slot_of, p):
    n, words = x_words.shape
    part = words // DISPATCH_PARTS
    mesh = plsc.VectorSubcoreMesh(core_axis_name="core", subcore_axis_name="subcore",
                                  num_cores=SC_CORES, num_subcores=SC_SUBCORES)
    out = jax.ShapeDtypeStruct((p, part), x_words.dtype)

    @functools.partial(
        pl.kernel, out_type=[out] * DISPATCH_PARTS, mesh=mesh, scratch_types=[],
        compiler_params=pltpu.CompilerParams(use_tc_tiling_on_sc=True), name="moe_dispatch")
    def dispatch(x_hbm, slot_hbm, *o_hbm):
        for c in range(DISPATCH_PARTS):
            def body(x_vmem, slot_vmem, o_ref=o_hbm[c]):
                for k in range(TOP_K):
                    pltpu.sync_copy(x_vmem, o_ref.at[slot_vmem.at[k]])

            pltpu.emit_pipeline(
                body,
                grid=(n // DISPATCH_ROWS,),
                in_specs=[pl.BlockSpec((DISPATCH_ROWS, part), lambda i, c=c: (i, c)),
                          pl.BlockSpec((TOP_K, DISPATCH_ROWS), lambda i: (0, i))],
                out_specs=[],
                core_axis_name=("core", "subcore"),
                dimension_semantics=(pltpu.PARALLEL,),
            )(x_hbm, slot_hbm)

    return dispatch(x_words, slot_of.reshape(TOP_K, n))


def _combine_gather(y_parts, slot_of):
    nk = slot_of.shape[0]
    part = y_parts[0].shape[1]
    mesh = plsc.VectorSubcoreMesh(core_axis_name="core", subcore_axis_name="subcore",
                                  num_cores=SC_CORES, num_subcores=SC_SUBCORES)
    out = jax.ShapeDtypeStruct((nk, part), y_parts[0].dtype)

    @functools.partial(
        pl.kernel, out_type=[out] * DISPATCH_PARTS, mesh=mesh, scratch_types=[],
        compiler_params=pltpu.CompilerParams(use_tc_tiling_on_sc=True), name="moe_combine_gather")
    def gather(*refs):
        y_hbm, slot_hbm, o_hbm = refs[:DISPATCH_PARTS], refs[DISPATCH_PARTS], refs[DISPATCH_PARTS + 1:]
        for c in range(DISPATCH_PARTS):
            def body(slot_vmem, o_vmem, y_ref=y_hbm[c]):
                pltpu.sync_copy(y_ref.at[slot_vmem.at[0]], o_vmem)

            pltpu.emit_pipeline(
                body,
                grid=(nk // DISPATCH_ROWS,),
                in_specs=[pl.BlockSpec((1, DISPATCH_ROWS), lambda i: (0, i))],
                out_specs=[pl.BlockSpec((DISPATCH_ROWS, part), lambda i: (i, 0))],
                core_axis_name=("core", "subcore"),
                dimension_semantics=(pltpu.PARALLEL,),
            )(slot_hbm, o_hbm[c])

    return gather(*y_parts, slot_of.reshape(1, nk))


def _moe_ffn_kernel(block_e_ref, block_rows_ref, last_block_ref, xa_ref, xb_ref, wu_f32_ref, sel_ref,
                    bu_ref, wd_f32_ref, bd_ref, oa_ref, ob_ref, wu_ref, wd_ref):
    step = pl.program_id(0)
    rows_used = block_rows_ref[step]
    part = oa_ref.shape[1]
    new_expert = jnp.logical_or(step == 0,
                                block_e_ref[step] != block_e_ref[jnp.maximum(step - 1, 0)])

    @pl.when(jnp.logical_and(new_expert, rows_used > 0))
    def _():
        wd_ref[...] = wd_f32_ref[0].astype(_BF16)
        group = 2 * PREP_LANES
        for c in range(wu_f32_ref.shape[2] // group):
            t = jnp.dot(wu_f32_ref[0, :, c * group:(c + 1) * group].astype(_BF16), sel_ref[...],
                        preferred_element_type=_F32)
            wu_ref[:, c * PREP_LANES:(c + 1) * PREP_LANES] = t[:, :PREP_LANES].astype(_BF16)
            wu_ref[:, D_FF + c * PREP_LANES:D_FF + (c + 1) * PREP_LANES] = t[:, PREP_LANES:].astype(_BF16)

    def ffn_rows(*ranges):
        xs = [_unpack_bf16_pairs(jnp.concatenate([xa_ref[lo:hi, :], xb_ref[lo:hi, :]], axis=1))
              for lo, hi in ranges]
        hs = [jnp.dot(x, wu_ref[...], preferred_element_type=_F32) + bu_ref[0] for x in xs]
        acts = []
        for h in hs:
            h_glu = jnp.minimum(h[:, :D_FF], SWIGLU_LIMIT)
            h_lin = jnp.clip(h[:, D_FF:], -SWIGLU_LIMIT, SWIGLU_LIMIT)
            acts.append((h_glu * jax.nn.sigmoid(SWIGLU_ALPHA * h_glu) * (h_lin + 1.0)).astype(_BF16))
        ys = [jnp.dot(a, wd_ref[...], preferred_element_type=_F32) + bd_ref[0] for a in acts]
        for (lo, hi), y in zip(ranges, ys):
            words = _pack_bf16_pairs(y)
            oa_ref[lo:hi, :] = words[:, :part]
            ob_ref[lo:hi, :] = words[:, part:]

    def zero_rows(lo, hi):
        oa_ref[lo:hi, :] = jnp.zeros((hi - lo, part), oa_ref.dtype)
        ob_ref[lo:hi, :] = jnp.zeros((hi - lo, part), ob_ref.dtype)

    sub_blocks = [(lo, lo + MOE_ROWS) for lo in range(0, MOE_STEP_ROWS, MOE_ROWS)]
    whole_step = rows_used > MOE_STEP_ROWS - MOE_ROWS + MOE_TAIL_ROWS

    @pl.when(whole_step)
    def _():
        ffn_rows(*sub_blocks)

    @pl.when(jnp.logical_not(whole_step))
    def _():
        for lo, hi in sub_blocks:
            mid = lo + MOE_TAIL_ROWS

            @pl.when(rows_used > mid)
            def _():
                ffn_rows((lo, hi))

            @pl.when(jnp.logical_and(rows_used > lo, rows_used <= mid))
            def _():
                ffn_rows((lo, mid))
                zero_rows(mid, hi)

            @pl.when(rows_used <= lo)
            def _():
                zero_rows(lo, hi)


def _moe_ffn(xs, block_e, block_rows, last_block, wu_f32, bu, wd_f32, bd):
    xa, xb = xs
    p, part = xa.shape
    ff, d = wd_f32.shape[1:]
    ff2 = wu_f32.shape[2]
    lane = jnp.arange(2 * PREP_LANES)
    sel = (lane[:, None] == jnp.where(lane < PREP_LANES, 2 * lane, 2 * (lane - PREP_LANES) + 1)[None, :]
           ).astype(_BF16)
    row = lambda i, be, br, lb: (jnp.minimum(i, lb[0]), 0)
    exp3 = lambda i, be, br, lb: (be[i], 0, 0)
    rows_spec = pl.BlockSpec((MOE_STEP_ROWS, part), row)
    grid_spec = pltpu.PrefetchScalarGridSpec(
        num_scalar_prefetch=3,
        grid=(p // MOE_STEP_ROWS,),
        in_specs=[rows_spec, rows_spec,
                  pl.BlockSpec((1, d, ff2), exp3),
                  pl.BlockSpec((2 * PREP_LANES, 2 * PREP_LANES), lambda i, be, br, lb: (0, 0)),
                  pl.BlockSpec((1, 1, ff2), exp3),
                  pl.BlockSpec((1, ff, d), exp3), pl.BlockSpec((1, 1, d), exp3)],
        out_specs=[rows_spec, rows_spec],
        scratch_shapes=[pltpu.VMEM((d, ff2), _BF16), pltpu.VMEM((ff, d), _BF16)],
    )
    out = jax.ShapeDtypeStruct((p, part), xa.dtype)
    return pl.pallas_call(
        _moe_ffn_kernel,
        grid_spec=grid_spec,
        out_shape=[out, out],
        compiler_params=_params("arbitrary", vmem_limit=MOE_VMEM_LIMIT_BYTES),
        name="moe_ffn",
    )(block_e, block_rows, last_block, xa, xb, wu_f32, sel, bu, wd_f32, bd)


def _combine_ln_kernel(ya_ref, yb_ref, gate_ref, res_ref, g_ref, b_ref, o_ref, *maybe_obf_ref):
    gate = gate_ref[...]
    gate = jnp.concatenate([gate, jnp.zeros((F32_SUBLANES - TOP_K, gate.shape[1]), _F32)], axis=0).T
    moe = None
    for k in range(TOP_K):
        words = jnp.concatenate([ya_ref[k], yb_ref[k]], axis=1)
        term = _unpack_pairs_f32(words) * gate[:, k:k + 1]
        moe = term if moe is None else moe + term
    y = _layer_norm_rows(DEEPNORM_ALPHA * res_ref[...] + moe, g_ref[...], b_ref[...])
    o_ref[...] = y
    for obf_ref in maybe_obf_ref:
        obf_ref[...] = y.astype(_BF16)


def _combine_ln(y4_parts, gates, res, g, b, with_bf16):
    n, d = res.shape
    ya, yb = (t.reshape(TOP_K, n, t.shape[1]) for t in y4_parts)
    part = ya.shape[2]
    row = lambda i: (i, 0)
    y_spec = pl.BlockSpec((TOP_K, COMBINE_ROWS, part), lambda i: (0, i, 0))
    return pl.pallas_call(
        _combine_ln_kernel,
        grid=(n // COMBINE_ROWS,),
        in_specs=[y_spec, y_spec, pl.BlockSpec((TOP_K, COMBINE_ROWS), lambda i: (0, i)),
                  pl.BlockSpec((COMBINE_ROWS, d), row), _const_spec((1, d)), _const_spec((1, d))],
        out_specs=[pl.BlockSpec((COMBINE_ROWS, d), row)] * (2 if with_bf16 else 1),
        out_shape=[jax.ShapeDtypeStruct((n, d), _F32)]
        + ([jax.ShapeDtypeStruct((n, d), _BF16)] if with_bf16 else []),
        compiler_params=_params("arbitrary"),
        name="combine_ln",
    )(ya, yb, gates, res, g.reshape(1, d), b.reshape(1, d))


def _moe_layer(x, x_words, router_w, router_b, w_up, b_up, w_down, b_down, ln_g, ln_b, with_bf16):
    n, d = x.shape
    nk = n * TOP_K
    idx_t, gate_t, rank_t, count = _router(x, router_w, router_b)
    counts = count[:, 0].astype(jnp.int32)
    padded = (counts + MOE_STEP_ROWS - 1) // MOE_STEP_ROWS * MOE_STEP_ROWS
    padded_ends = jnp.cumsum(padded)
    padded_starts = padded_ends - padded
    n_blocks = -(-(nk + N_EXPERTS * (MOE_STEP_ROWS - 1)) // MOE_STEP_ROWS)
    p = n_blocks * MOE_STEP_ROWS
    short = counts % MOE_STEP_ROWS
    experts = jnp.arange(N_EXPERTS, dtype=jnp.int32)
    chosen = idx_t[None] == experts[:, None, None]
    start_of = jnp.sum(jnp.where(chosen, padded_starts[:, None, None], 0), axis=0)
    short_of = jnp.sum(jnp.where(chosen, short[:, None, None], 0), axis=0)
    gap_of = jnp.where(jnp.logical_and(short_of > 0, rank_t >= short_of), MOE_STEP_ROWS - short_of, 0)
    slot_of = (start_of + rank_t + gap_of).reshape(nk)
    block_start = jnp.arange(n_blocks, dtype=jnp.int32) * MOE_STEP_ROWS
    block_e = jnp.minimum(jnp.sum(padded_ends[None, :] <= block_start[:, None], axis=1),
                          N_EXPERTS - 1).astype(jnp.int32)
    first_of_expert = block_start == padded_starts[block_e]
    block_rows = jnp.where(block_start >= padded_ends[-1], 0,
                           jnp.where(jnp.logical_and(first_of_expert, short[block_e] > 0),
                                     short[block_e], MOE_STEP_ROWS)).astype(jnp.int32)
    last_block = jnp.minimum(padded_ends[-1] // MOE_STEP_ROWS, n_blocks - 1).astype(jnp.int32)
    xs = _dispatch(x_words, slot_of, p)
    bu = b_up.reshape(N_EXPERTS, D_FF, 2).transpose(0, 2, 1).reshape(N_EXPERTS, 1, 2 * D_FF)
    ys = _moe_ffn(xs, block_e, block_rows, last_block.reshape(1), w_up, bu, w_down,
                  b_down.reshape(N_EXPERTS, 1, d))
    return _combine_ln(_combine_gather(ys, slot_of), gate_t, x, ln_g, ln_b, with_bf16)


def kernel(x, l0_w_in, l0_ret_gn_g, l0_conv_w, l0_conv_b, l0_conv_ln_g, l0_conv_ln_b, l0_w_out, l0_ln1_g, l0_ln1_b, l0_router_w, l0_router_b, l0_w_up, l0_b_up, l0_w_down, l0_b_down, l0_ln2_g, l0_ln2_b, l1_w_in, l1_w_out, l1_ln1_g, l1_ln1_b, l1_router_w, l1_router_b, l1_w_up, l1_b_up, l1_w_down, l1_b_down, l1_ln2_g, l1_ln2_b):
    batch, seq, d = x.shape
    n = batch * seq
    x0 = x.reshape(n, d)

    x1, x1_words = _l0_block(x0, batch, seq, l0_w_in, l0_ret_gn_g, l0_conv_w, l0_conv_b,
                             l0_conv_ln_g, l0_conv_ln_b, l0_w_out, l0_ln1_g, l0_ln1_b)
    x2, x2_bf16 = _moe_layer(x1, x1_words, l0_router_w, l0_router_b, l0_w_up, l0_b_up, l0_w_down,
                             l0_b_down, l0_ln2_g, l0_ln2_b, True)

    q_scale = jnp.where(jnp.arange(3 * d) < d, LOG2_E * SB_HEAD_DIM ** -0.5, 1.0).astype(_F32)
    qkv = _proj(x2_bf16, l1_w_in, q_scale)
    att = _sb_attention(qkv, batch, seq)
    x3, x3_words = _proj_res_ln(att, l1_w_out, x2, l1_ln1_g, l1_ln1_b)
    (x4,) = _moe_layer(x3, x3_words, l1_router_w, l1_router_b, l1_w_up, l1_b_up, l1_w_down,
                       l1_b_down, l1_ln2_g, l1_ln2_b, False)
    return x4.reshape(batch, seq, d)
```

```python
import functools

import jax
import jax.numpy as jnp
from jax import lax
from jax.experimental import pallas as pl
from jax.experimental.pallas import tpu as pltpu
from jax.experimental.pallas import tpu_sc as plsc

D_MODEL = 1024
CHUNK = 64
RET_HEADS = 4
RET_DK = 64
RET_DV = 128
RET_QK_WIDTH = RET_HEADS * RET_DK
RET_WIDTH = RET_HEADS * RET_DV
ROPE_BASE = 10000.0
CONV_CH = D_MODEL // 2
CONV_WIDTH = 31
L0_IN = 2 * RET_QK_WIDTH + 2 * RET_WIDTH + 2 * CONV_CH
SB_HEADS = 8
SB_HEAD_DIM = D_MODEL // SB_HEADS
N_EXPERTS = 32
TOP_K = 4
D_FF = D_MODEL
SWIGLU_LIMIT = 7.0
SWIGLU_ALPHA = 1.702
LN_EPS = 1e-5
DEPTH = 2
DEEPNORM_ALPHA = (2 * DEPTH) ** 0.25

VMEM_LIMIT_BYTES = 48 * 1024 * 1024
MOE_VMEM_LIMIT_BYTES = 56 * 1024 * 1024
F32_SUBLANES = 8
PROJ_ROWS = 1024
MIX_ROWS = 256
CONV_HALO = 32
ROUTER_ROWS = 2048
MOE_ROWS = 512
MOE_TAIL_ROWS = 256
MOE_STEP_ROWS = 1024
PREP_LANES = 128
COMBINE_ROWS = 1024
DISPATCH_ROWS = 128
DISPATCH_PARTS = 2
SC_CORES = 2
SC_SUBCORES = 16
SB_BLOCK = 128
SB_STEP_BLOCKS = 2
SB_CUTOFF = 150.0
LOG2_E = 1.4426950408889634

_F32 = jnp.float32
_BF16 = jnp.bfloat16


def _params(*sem, vmem_limit=VMEM_LIMIT_BYTES):
    return pltpu.CompilerParams(dimension_semantics=sem, vmem_limit_bytes=vmem_limit)


def _const_spec(shape):
    nd = len(shape)
    return pl.BlockSpec(shape, lambda *_: (0,) * nd)


def _weight_spec(shape):
    nd = len(shape)
    return pl.BlockSpec(shape, lambda *_: (0,) * nd, pipeline_mode=pl.Buffered(1))


def _layer_norm_rows(y, g, b):
    mu = jnp.mean(y, axis=-1, keepdims=True)
    yc = y - mu
    var = jnp.mean(yc * yc, axis=-1, keepdims=True)
    return yc * lax.rsqrt(var + LN_EPS) * g + b


def _silu(x):
    return x * jax.nn.sigmoid(x)


def _pack_bf16_pairs(y):
    half = y.shape[1] // 2
    lo = lax.bitcast_convert_type(y[:, :half].astype(_BF16).astype(_F32), jnp.uint32)
    hi = lax.bitcast_convert_type(y[:, half:].astype(_BF16).astype(_F32), jnp.uint32)
    return (lo >> 16) | (hi & jnp.uint32(0xFFFF0000))


def _unpack_bf16_pairs(w):
    lo = lax.bitcast_convert_type(w << 16, _F32)
    hi = lax.bitcast_convert_type(w & jnp.uint32(0xFFFF0000), _F32)
    return jnp.concatenate([lo, hi], axis=1).astype(_BF16)


def _proj_kernel(x_ref, w_f32_ref, scale_ref, o_ref, w_ref):
    @pl.when(pl.program_id(0) == 0)
    def _():
        w_ref[...] = w_f32_ref[...].astype(_BF16)

    o_ref[...] = (jnp.dot(x_ref[...].astype(_BF16), w_ref[...], preferred_element_type=_F32)
                  * scale_ref[...]).astype(o_ref.dtype)


def _proj(x, w, col_scale):
    n, d = x.shape
    width = w.shape[1]
    return pl.pallas_call(
        _proj_kernel,
        grid=(n // PROJ_ROWS,),
        in_specs=[pl.BlockSpec((PROJ_ROWS, d), lambda i: (i, 0)), _weight_spec((d, width)),
                  _const_spec((1, width))],
        out_specs=pl.BlockSpec((PROJ_ROWS, width), lambda i: (i, 0)),
        out_shape=jax.ShapeDtypeStruct((n, width), _BF16),
        scratch_shapes=[pltpu.VMEM((d, width), _BF16)],
        compiler_params=_params("arbitrary"),
        name="proj",
    )(x, w, col_scale.reshape(1, width))


def _proj_res_ln_kernel(a_ref, w_f32_ref, res_ref, g_ref, b_ref, o_ref, words_ref, w_ref):
    @pl.when(pl.program_id(0) == 0)
    def _():
        w_ref[...] = w_f32_ref[...].astype(_BF16)

    m = jnp.dot(a_ref[...], w_ref[...], preferred_element_type=_F32)
    y = _layer_norm_rows(DEEPNORM_ALPHA * res_ref[...] + m, g_ref[...], b_ref[...])
    o_ref[...] = y
    words_ref[...] = _pack_bf16_pairs(y)


def _proj_res_ln(a_bf16, w, res, g, b):
    n, d = res.shape
    k = a_bf16.shape[1]
    row = lambda i: (i, 0)
    return pl.pallas_call(
        _proj_res_ln_kernel,
        grid=(n // PROJ_ROWS,),
        in_specs=[pl.BlockSpec((PROJ_ROWS, k), row), _weight_spec((k, d)),
                  pl.BlockSpec((PROJ_ROWS, d), row), _const_spec((1, d)), _const_spec((1, d))],
        out_specs=[pl.BlockSpec((PROJ_ROWS, d), row), pl.BlockSpec((PROJ_ROWS, d // 2), row)],
        out_shape=[jax.ShapeDtypeStruct((n, d), _F32), jax.ShapeDtypeStruct((n, d // 2), jnp.uint32)],
        scratch_shapes=[pltpu.VMEM((k, d), _BF16)],
        compiler_params=_params("arbitrary"),
        name="proj_res_ln",
    )(a_bf16, w, res, g.reshape(1, d), b.reshape(1, d))


def _l0_block_kernel(x_ref, w_in_f32_ref, cos_ref, sin_ref, xi_ref, zeta_ref, decay_ref, gstate_ref,
                     gn_g_ref, conv_w_ref, conv_b_ref, cln_g_ref, cln_b_ref, w_out_f32_ref, ln_g_ref,
                     ln_b_ref, o_ref, words_ref, state_ref, u_ref, shift_ref, h_ref, mix_ref,
                     w_in_ref, w_out_ref):
    rows = MIX_ROWS
    step = pl.program_id(1)
    v_off = 2 * RET_QK_WIDTH
    g_off = v_off + RET_WIDTH
    a_off = g_off + RET_WIDTH

    @pl.when(jnp.logical_and(pl.program_id(0) == 0, step == 0))
    def _():
        w_in_ref[...] = w_in_f32_ref[...].astype(_BF16)
        w_out_ref[...] = w_out_f32_ref[...].astype(_BF16)

    @pl.when(step == 0)
    def _():
        state_ref[...] = jnp.zeros_like(state_ref)
        u_ref[0:CONV_HALO, :] = jnp.zeros((CONV_HALO, CONV_CH), _F32)

    x_bf16 = x_ref[...].astype(_BF16)
    h_conv = jnp.dot(x_bf16, w_in_ref[:, a_off:a_off + 2 * CONV_CH], preferred_element_type=_F32)
    h_ref[...] = jnp.dot(x_bf16, w_in_ref[:, 0:a_off], preferred_element_type=_F32)

    u_ref[CONV_HALO:CONV_HALO + rows, :] = h_conv[:, :CONV_CH] * jax.nn.sigmoid(h_conv[:, CONV_CH:])
    acc = jnp.broadcast_to(conv_b_ref[...], (rows, CONV_CH))
    first = CONV_HALO - (CONV_WIDTH - 1)
    for phase in range(F32_SUBLANES):
        offsets = [first + j for j in range(CONV_WIDTH) if (first + j) % F32_SUBLANES == phase]
        span = max(offsets) - phase + rows
        if phase == 0:
            src = u_ref
        else:
            shift_ref[0:span, :] = u_ref[phase:phase + span, :]
            src = shift_ref
        for off in offsets:
            j = off - first
            acc = acc + conv_w_ref[j:j + 1, :] * src[off - phase:off - phase + rows, :]
    u_ref[0:CONV_HALO, :] = u_ref[rows:rows + CONV_HALO, :]
    c = _silu(_layer_norm_rows(acc, cln_g_ref[...], cln_b_ref[...]))
    mix_ref[:, RET_WIDTH:RET_WIDTH + CONV_CH] = c.astype(_BF16)

    lane = lax.broadcasted_iota(jnp.int32, (rows, RET_QK_WIDTH), 1)
    first_half = (lane % RET_DK) < (RET_DK // 2)
    cos = cos_ref[...]
    sin = sin_ref[...]

    def rotary(t):
        partner = jnp.where(first_half,
                            pltpu.roll(t, RET_QK_WIDTH - RET_DK // 2, axis=1),
                            pltpu.roll(t, RET_DK // 2, axis=1))
        return t * cos + partner * sin

    q = rotary(h_ref[:, 0:RET_QK_WIDTH])
    k = rotary(h_ref[:, RET_QK_WIDTH:2 * RET_QK_WIDTH]) * (RET_DK ** -0.5)
    q_in = (q * xi_ref[...]).astype(_BF16)
    k_out = (k * zeta_ref[...]).astype(_BF16)
    q = q.astype(_BF16)
    k = k.astype(_BF16)
    for hd in range(RET_HEADS):
        qk = slice(hd * RET_DK, (hd + 1) * RET_DK)
        vs = slice(hd * RET_DV, (hd + 1) * RET_DV)
        v = h_ref[:, v_off + hd * RET_DV:v_off + (hd + 1) * RET_DV].astype(_BF16)
        s = lax.dot_general(q[:, qk], k[:, qk], (((1,), (1,)), ((), ())),
                            preferred_element_type=_F32) * decay_ref[hd]
        r = jnp.dot(s.astype(_BF16), v, preferred_element_type=_F32)
        r += jnp.dot(q_in[:, qk], state_ref[hd].astype(_BF16), preferred_element_type=_F32)
        kv = lax.dot_general(k_out[:, qk], v, (((0,), (0,)), ((), ())),
                             preferred_element_type=_F32)
        state_ref[hd] = gstate_ref[hd] * state_ref[hd] + kv
        mu = jnp.mean(r, axis=-1, keepdims=True)
        rc = r - mu
        var = jnp.mean(rc * rc, axis=-1, keepdims=True)
        rn = rc * lax.rsqrt(var + LN_EPS) * gn_g_ref[:, vs]
        gate = h_ref[:, g_off + hd * RET_DV:g_off + (hd + 1) * RET_DV]
        mix_ref[:, vs] = (_silu(gate) * rn).astype(_BF16)

    m = jnp.dot(mix_ref[...], w_out_ref[...], preferred_element_type=_F32)
    y = _layer_norm_rows(DEEPNORM_ALPHA * x_ref[...] + m, ln_g_ref[...], ln_b_ref[...])
    o_ref[...] = y
    words_ref[...] = _pack_bf16_pairs(y)


def _l0_mixer_tables(seq):
    half = RET_DK // 2
    inv = ROPE_BASE ** (-jnp.arange(half, dtype=_F32) / half)
    ang = jnp.arange(seq).astype(_F32)[:, None] * inv[None, :]
    cos = jnp.tile(jnp.cos(ang), (1, 2 * RET_HEADS))
    sin = jnp.tile(jnp.concatenate([-jnp.sin(ang), jnp.sin(ang)], axis=1), (1, RET_HEADS))
    log_g = jnp.log(1.0 - 2.0 ** (-5.0 - jnp.arange(RET_HEADS, dtype=_F32)))
    n = jnp.arange(MIX_ROWS, dtype=_F32)
    diff = n[:, None] - n[None, :]
    chunk = jnp.arange(MIX_ROWS) // CHUNK
    expo = jnp.where(chunk[:, None] == chunk[None, :], jnp.abs(diff), diff)
    decay = jnp.where((chunk[:, None] >= chunk[None, :])[None],
                      jnp.exp(expo[None] * log_g[:, None, None]), 0.0)
    xi = jnp.repeat(jnp.exp((n[:, None] + 1.0) * log_g[None, :]), RET_DK, axis=1)
    zeta = jnp.repeat(jnp.exp((MIX_ROWS - 1.0 - n[:, None]) * log_g[None, :]), RET_DK, axis=1)
    g_state = jnp.exp(MIX_ROWS * log_g)
    return cos, sin, xi, zeta, decay, g_state


def _l0_block(x, batch, seq, w_in, gn_g, conv_w, conv_b, cln_g, cln_b, w_out, ln_g, ln_b):
    n, d = x.shape
    steps = seq // MIX_ROWS
    cos, sin, xi, zeta, decay, g_state = _l0_mixer_tables(seq)
    row = lambda b, i: (b * steps + i, 0)
    pos = lambda b, i: (i, 0)
    vec = lambda t: t.reshape(1, -1)
    return pl.pallas_call(
        _l0_block_kernel,
        grid=(batch, steps),
        in_specs=[pl.BlockSpec((MIX_ROWS, d), row), _weight_spec((d, L0_IN)),
                  pl.BlockSpec((MIX_ROWS, RET_QK_WIDTH), pos),
                  pl.BlockSpec((MIX_ROWS, RET_QK_WIDTH), pos),
                  _const_spec((MIX_ROWS, RET_QK_WIDTH)), _const_spec((MIX_ROWS, RET_QK_WIDTH)),
                  _const_spec((RET_HEADS, MIX_ROWS, MIX_ROWS)),
                  pl.BlockSpec(memory_space=pltpu.SMEM),
                  _const_spec((1, RET_WIDTH)), _const_spec((CONV_WIDTH, CONV_CH)),
                  _const_spec((1, CONV_CH)), _const_spec((1, CONV_CH)), _const_spec((1, CONV_CH)),
                  _weight_spec((RET_WIDTH + CONV_CH, d)), _const_spec((1, d)), _const_spec((1, d))],
        out_specs=[pl.BlockSpec((MIX_ROWS, d), row), pl.BlockSpec((MIX_ROWS, d // 2), row)],
        out_shape=[jax.ShapeDtypeStruct((n, d), _F32), jax.ShapeDtypeStruct((n, d // 2), jnp.uint32)],
        scratch_shapes=[pltpu.VMEM((RET_HEADS, RET_DK, RET_DV), _F32),
                        pltpu.VMEM((MIX_ROWS + CONV_HALO, CONV_CH), _F32),
                        pltpu.VMEM((MIX_ROWS + CONV_HALO, CONV_CH), _F32),
                        pltpu.VMEM((MIX_ROWS, 2 * RET_QK_WIDTH + 2 * RET_WIDTH), _F32),
                        pltpu.VMEM((MIX_ROWS, RET_WIDTH + CONV_CH), _BF16),
                        pltpu.VMEM((d, L0_IN), _BF16), pltpu.VMEM((RET_WIDTH + CONV_CH, d), _BF16)],
        compiler_params=_params("arbitrary", "arbitrary"),
        name="l0_block",
    )(x, w_in, cos, sin, xi, zeta, decay, g_state, vec(gn_g), conv_w, vec(conv_b),
      vec(cln_g), vec(cln_b), w_out, vec(ln_g), vec(ln_b))


def _sb_attention_kernel(q_ref, k_ref, v_ref, tri_ref, o_ref, acc_ref, carry_ref):
    blk = SB_BLOCK
    t_idx = lax.broadcasted_iota(jnp.int32, (blk, blk), 0)
    s_idx = lax.broadcasted_iota(jnp.int32, (blk, blk), 1)
    keep = s_idx < t_idx

    heads = [slice(hd * SB_HEAD_DIM, (hd + 1) * SB_HEAD_DIM) for hd in range(SB_HEADS)]

    def sweep(q_rows, blocks, carry):
        starts = [pl.multiple_of(j * blk, blk) for j, _ in blocks]
        zs = [[lax.dot_general(q_ref[q_rows, hs], k_ref[pl.ds(st, blk), hs], (((1,), (1,)), ((), ())),
                               preferred_element_type=_F32) for hs in heads] for st in starts]
        log_betas, addends = [], []
        for (_, mask), z_heads in zip(blocks, zs):
            for z in z_heads:
                softplus = jnp.log2(1.0 + jnp.exp2(-jnp.abs(z)))
                log_beta = jnp.minimum(z, 0.0) - softplus
                log_rest = log_beta - z
                if mask is not None:
                    log_rest = jnp.where(mask, log_rest, 0.0)
                log_betas.append(log_beta)
                addends.append(log_rest.astype(_BF16))
        sums = [jnp.dot(t, tri_ref[...], preferred_element_type=_F32) for t in addends]
        weights, total = [], None
        for b, (_, mask) in enumerate(blocks):
            block_sums = sums[b * SB_HEADS:(b + 1) * SB_HEADS]
            before = carry if total is None else (total if carry is None else carry + total)
            for hd, s in enumerate(block_sums):
                expo = log_betas[b * SB_HEADS + hd] + s[:, :blk]
                if before is not None:
                    expo = expo + before[:, heads[hd]]
                a = jnp.exp2(expo)
                if mask is not None:
                    a = jnp.where(mask, a, 0.0)
                weights.append(a.astype(_BF16))
            block_total = jnp.concatenate([s[:, blk:] for s in block_sums], axis=1)
            total = block_total if total is None else total + block_total
        outs = []
        for hd, hs in enumerate(heads):
            o = None
            for b, st in enumerate(starts):
                t = jnp.dot(weights[b * SB_HEADS + hd], v_ref[pl.ds(st, blk), hs],
                            preferred_element_type=_F32)
                o = t if o is None else o + t
            outs.append(o)
        return jnp.concatenate(outs, axis=1), total

    def query_block(sub):
        q_rows = slice(sub * blk, (sub + 1) * blk)
        qi = pl.program_id(1) * SB_STEP_BLOCKS + sub
        out, carry = sweep(q_rows, [(qi, keep)], None)
        acc_ref[...] = out
        carry_ref[...] = carry

        def cond(state):
            j, alive = state
            return jnp.logical_and(j >= 0, alive)

        def body(state):
            j, _ = state
            carry = carry_ref[...]
            out, total = sweep(q_rows, [(j, None), (jnp.maximum(j - 1, 0), j >= 1)], carry)
            carry = carry + total
            acc_ref[...] += out
            carry_ref[...] = carry
            return j - 2, jnp.max(carry) > -SB_CUTOFF

        lax.while_loop(cond, body, (qi - 1, jnp.max(carry) > -SB_CUTOFF))
        o_ref[q_rows, :] = acc_ref[...].astype(o_ref.dtype)

    for sub in range(SB_STEP_BLOCKS):
        query_block(sub)


def _sb_attention(qkv, batch, seq):
    n = qkv.shape[0]
    rows = SB_STEP_BLOCKS * SB_BLOCK
    nq = seq // rows
    j = jnp.arange(SB_BLOCK)[:, None]
    s = jnp.arange(SB_BLOCK)[None, :]
    tri = jnp.concatenate([(j > s).astype(_BF16), jnp.ones((SB_BLOCK, SB_BLOCK), _BF16)], axis=1)
    whole_seq = lambda col: pl.BlockSpec((seq, D_MODEL), lambda b, i: (b, col))
    return pl.pallas_call(
        _sb_attention_kernel,
        grid=(batch, nq),
        in_specs=[pl.BlockSpec((rows, D_MODEL), lambda b, i: (b * nq + i, 0)),
                  whole_seq(1), whole_seq(2), _const_spec((SB_BLOCK, 2 * SB_BLOCK))],
        out_specs=pl.BlockSpec((rows, D_MODEL), lambda b, i: (b * nq + i, 0)),
        out_shape=jax.ShapeDtypeStruct((n, D_MODEL), _BF16),
        scratch_shapes=[pltpu.VMEM((SB_BLOCK, D_MODEL), _F32), pltpu.VMEM((SB_BLOCK, D_MODEL), _F32)],
        compiler_params=_params("arbitrary", "arbitrary"),
        name="sb_attention",
    )(qkv, qkv, qkv, tri)


def _router_kernel(x_ref, w_ref, b_ref, tri_ref, idx_ref, gate_ref, rank_ref, count_ref, base_ref):
    @pl.when(pl.program_id(0) == 0)
    def _():
        base_ref[...] = jnp.zeros_like(base_ref)

    x = x_ref[...]
    w = w_ref[...]
    x_hi = x.astype(_BF16)
    x_lo = (x - x_hi.astype(_F32)).astype(_BF16)
    w_hi = w.astype(_BF16)
    w_lo = (w - w_hi.astype(_F32)).astype(_BF16)
    nt = (((1,), (1,)), ((), ()))
    logits = (lax.dot_general(w_hi, x_hi, nt, preferred_element_type=_F32)
              + lax.dot_general(w_lo, x_hi, nt, preferred_element_type=_F32)
              + lax.dot_general(w_hi, x_lo, nt, preferred_element_type=_F32)) + b_ref[...]
    expert = lax.broadcasted_iota(jnp.int32, logits.shape, 0)
    vals, idxs = [], []
    for _ in range(TOP_K):
        m = jnp.max(logits, axis=0, keepdims=True)
        sel = jnp.min(jnp.where(logits == m, expert, N_EXPERTS), axis=0, keepdims=True)
        vals.append(m)
        idxs.append(sel)
        logits = jnp.where(expert == sel, -jnp.inf, logits)
    vals = jnp.concatenate(vals, axis=0)
    e = jnp.exp(vals - vals[0:1])
    gate_ref[...] = e / jnp.sum(e, axis=0, keepdims=True)
    idx_ref[...] = jnp.concatenate(idxs, axis=0)

    member = jnp.zeros(logits.shape, _F32)
    for sel in idxs:
        member += (expert == sel).astype(_F32)
    before = jnp.dot(member.astype(_BF16), tri_ref[...], preferred_element_type=_F32) + base_ref[...]
    ranks = [jnp.sum(jnp.where(expert == sel, before, 0.0), axis=0, keepdims=True) for sel in idxs]
    rank_ref[...] = jnp.concatenate(ranks, axis=0).astype(jnp.int32)
    base_ref[...] += jnp.sum(member, axis=1, keepdims=True)
    count_ref[...] = base_ref[...]


def _router(x, router_w, router_b):
    n, d = x.shape
    t = jnp.arange(ROUTER_ROWS)
    tri = (t[:, None] < t[None, :]).astype(_BF16)
    tok = pl.BlockSpec((TOP_K, ROUTER_ROWS), lambda i: (0, i))
    return pl.pallas_call(
        _router_kernel,
        grid=(n // ROUTER_ROWS,),
        in_specs=[pl.BlockSpec((ROUTER_ROWS, d), lambda i: (i, 0)),
                  _const_spec((N_EXPERTS, d)), _const_spec((N_EXPERTS, 1)),
                  _const_spec((ROUTER_ROWS, ROUTER_ROWS))],
        out_specs=[tok, tok, tok, _const_spec((N_EXPERTS, 1))],
        out_shape=[jax.ShapeDtypeStruct((TOP_K, n), jnp.int32),
                   jax.ShapeDtypeStruct((TOP_K, n), _F32),
                   jax.ShapeDtypeStruct((TOP_K, n), jnp.int32),
                   jax.ShapeDtypeStruct((N_EXPERTS, 1), _F32)],
        scratch_shapes=[pltpu.VMEM((N_EXPERTS, 1), _F32)],
        compiler_params=_params("arbitrary"),
        name="router",
    )(x, router_w.T, router_b.reshape(N_EXPERTS, 1), tri)


def _dispatch(x_words, slot_of, p):
    n, words = x_words.shape
    part = words // DISPATCH_PARTS
    mesh = plsc.VectorSubcoreMesh(core_axis_name="core", subcore_axis_name="subcore",
                                  num_cores=SC_CORES, num_subcores=SC_SUBCORES)
    out = jax.ShapeDtypeStruct((p, part), x_words.dtype)

    @functools.partial(
        pl.kernel, out_type=[out] * DISPATCH_PARTS, mesh=mesh, scratch_types=[],
        compiler_params=pltpu.CompilerParams(use_tc_tiling_on_sc=True), name="moe_dispatch")
    def dispatch(x_hbm, slot_hbm, *o_hbm):
        for c in range(DISPATCH_PARTS):
            def body(x_vmem, slot_vmem, o_ref=o_hbm[c]):
                for k in range(TOP_K):
                    pltpu.sync_copy(x_vmem, o_ref.at[slot_vmem.at[k]])

            pltpu.emit_pipeline(
                body,
                grid=(n // DISPATCH_ROWS,),
                in_specs=[pl.BlockSpec((DISPATCH_ROWS, part), lambda i, c=c: (i, c)),
                          pl.BlockSpec((TOP_K, DISPATCH_ROWS), lambda i: (0, i))],
                out_specs=[],
                core_axis_name=("core", "subcore"),
                dimension_semantics=(pltpu.PARALLEL,),
            )(x_hbm, slot_hbm)

    return dispatch(x_words, slot_of.reshape(TOP_K, n))


def _combine_gather(y_parts, slot_of):
    nk = slot_of.shape[0]
    part = y_parts[0].shape[1]
    mesh = plsc.VectorSubcoreMesh(core_axis_name="core", subcore_axis_name="subcore",
                                  num_cores=SC_CORES, num_subcores=SC_SUBCORES)
    out = jax.ShapeDtypeStruct((nk, part), y_parts[0].dtype)

    @functools.partial(
        pl.kernel, out_type=[out] * DISPATCH_PARTS, mesh=mesh, scratch_types=[],
        compiler_params=pltpu.CompilerParams(use_tc_tiling_on_sc=True), name="moe_combine_gather")
    def gather(*refs):
        y_hbm, slot_hbm, o_hbm = refs[:DISPATCH_PARTS], refs[DISPATCH_PARTS], refs[DISPATCH_PARTS + 1:]
        for c in range(DISPATCH_PARTS):
            def body(slot_vmem, o_vmem, y_ref=y_hbm[c]):
                pltpu.sync_copy(y_ref.at[slot_vmem.at[0]], o_vmem)

            pltpu.emit_pipeline(
                body,
                grid=(nk // DISPATCH_ROWS,),
                in_specs=[pl.BlockSpec((1, DISPATCH_ROWS), lambda i: (0, i))],
                out_specs=[pl.BlockSpec((DISPATCH_ROWS, part), lambda i: (i, 0))],
                core_axis_name=("core", "subcore"),
                dimension_semantics=(pltpu.PARALLEL,),
            )(slot_hbm, o_hbm[c])

    return gather(*y_parts, slot_of.reshape(1, nk))


def _moe_ffn_kernel(block_e_ref, block_rows_ref, last_block_ref, xa_ref, xb_ref, wu_f32_ref, sel_ref,
                    bu_ref, wd_f32_ref, bd_ref, oa_ref, ob_ref, wu_ref, wd_ref):
    step = pl.program_id(0)
    rows_used = block_rows_ref[step]
    part = oa_ref.shape[1]
    new_expert = jnp.logical_or(step == 0,
                                block_e_ref[step] != block_e_ref[jnp.maximum(step - 1, 0)])

    @pl.when(jnp.logical_and(new_expert, rows_used > 0))
    def _():
        wd_ref[...] = wd_f32_ref[0].astype(_BF16)
        group = 2 * PREP_LANES
        for c in range(wu_f32_ref.shape[2] // group):
            t = jnp.dot(wu_f32_ref[0, :, c * group:(c + 1) * group].astype(_BF16), sel_ref[...],
                        preferred_element_type=_F32)
            wu_ref[:, c * PREP_LANES:(c + 1) * PREP_LANES] = t[:, :PREP_LANES].astype(_BF16)
            wu_ref[:, D_FF + c * PREP_LANES:D_FF + (c + 1) * PREP_LANES] = t[:, PREP_LANES:].astype(_BF16)

    def ffn_rows(*ranges):
        xs = [_unpack_bf16_pairs(jnp.concatenate([xa_ref[lo:hi, :], xb_ref[lo:hi, :]], axis=1))
              for lo, hi in ranges]
        hs = [jnp.dot(x, wu_ref[...], preferred_element_type=_F32) + bu_ref[0] for x in xs]
        acts = []
        for h in hs:
            h_glu = jnp.minimum(h[:, :D_FF], SWIGLU_LIMIT)
            h_lin = jnp.clip(h[:, D_FF:], -SWIGLU_LIMIT, SWIGLU_LIMIT)
            acts.append((h_glu * jax.nn.sigmoid(SWIGLU_ALPHA * h_glu) * (h_lin + 1.0)).astype(_BF16))
        ys = [jnp.dot(a, wd_ref[...], preferred_element_type=_F32) + bd_ref[0] for a in acts]
        for (lo, hi), y in zip(ranges, ys):
            words = _pack_bf16_pairs(y)
            oa_ref[lo:hi, :] = words[:, :part]
            ob_ref[lo:hi, :] = words[:, part:]

    def zero_rows(lo, hi):
        oa_ref[lo:hi, :] = jnp.zeros((hi - lo, part), oa_ref.dtype)
        ob_ref[lo:hi, :] = jnp.zeros((hi - lo, part), ob_ref.dtype)

    sub_blocks = [(lo, lo + MOE_ROWS) for lo in range(0, MOE_STEP_ROWS, MOE_ROWS)]
    whole_step = rows_used > MOE_STEP_ROWS - MOE_ROWS + MOE_TAIL_ROWS

    @pl.when(whole_step)
    def _():
        ffn_rows(*sub_blocks)

    @pl.when(jnp.logical_not(whole_step))
    def _():
        for lo, hi in sub_blocks:
            mid = lo + MOE_TAIL_ROWS

            @pl.when(rows_used > mid)
            def _():
                ffn_rows((lo, hi))

            @pl.when(jnp.logical_and(rows_used > lo, rows_used <= mid))
            def _():
                ffn_rows((lo, mid))
                zero_rows(mid, hi)

            @pl.when(rows_used <= lo)
            def _():
                zero_rows(lo, hi)


def _moe_ffn(xs, block_e, block_rows, last_block, wu_f32, bu, wd_f32, bd):
    xa, xb = xs
    p, part = xa.shape
    ff, d = wd_f32.shape[1:]
    ff2 = wu_f32.shape[2]
    lane = jnp.arange(2 * PREP_LANES)
    sel = (lane[:, None] == jnp.where(lane < PREP_LANES, 2 * lane, 2 * (lane - PREP_LANES) + 1)[None, :]
           ).astype(_BF16)
    row = lambda i, be, br, lb: (jnp.minimum(i, lb[0]), 0)
    exp3 = lambda i, be, br, lb: (be[i], 0, 0)
    rows_spec = pl.BlockSpec((MOE_STEP_ROWS, part), row)
    grid_spec = pltpu.PrefetchScalarGridSpec(
        num_scalar_prefetch=3,
        grid=(p // MOE_STEP_ROWS,),
        in_specs=[rows_spec, rows_spec,
                  pl.BlockSpec((1, d, ff2), exp3),
                  pl.BlockSpec((2 * PREP_LANES, 2 * PREP_LANES), lambda i, be, br, lb: (0, 0)),
                  pl.BlockSpec((1, 1, ff2), exp3),
                  pl.BlockSpec((1, ff, d), exp3), pl.BlockSpec((1, 1, d), exp3)],
        out_specs=[rows_spec, rows_spec],
        scratch_shapes=[pltpu.VMEM((d, ff2), _BF16), pltpu.VMEM((ff, d), _BF16)],
    )
    out = jax.ShapeDtypeStruct((p, part), xa.dtype)
    return pl.pallas_call(
        _moe_ffn_kernel,
        grid_spec=grid_spec,
        out_shape=[out, out],
        compiler_params=_params("arbitrary", vmem_limit=MOE_VMEM_LIMIT_BYTES),
        name="moe_ffn",
    )(block_e, block_rows, last_block, xa, xb, wu_f32, sel, bu, wd_f32, bd)


def _combine_ln_kernel(ya_ref, yb_ref, gate_ref, res_ref, g_ref, b_ref, o_ref, *maybe_obf_ref):
    gate = gate_ref[...]
    gate = jnp.concatenate([gate, jnp.zeros((F32_SUBLANES - TOP_K, gate.shape[1]), _F32)], axis=0).T
    moe = None
    for k in range(TOP_K):
        words = jnp.concatenate([ya_ref[k], yb_ref[k]], axis=1)
        term = _unpack_bf16_pairs(words).astype(_F32) * gate[:, k:k + 1]
        moe = term if moe is None else moe + term
    y = _layer_norm_rows(DEEPNORM_ALPHA * res_ref[...] + moe, g_ref[...], b_ref[...])
    o_ref[...] = y
    for obf_ref in maybe_obf_ref:
        obf_ref[...] = y.astype(_BF16)


def _combine_ln(y4_parts, gates, res, g, b, with_bf16):
    n, d = res.shape
    ya, yb = (t.reshape(TOP_K, n, t.shape[1]) for t in y4_parts)
    part = ya.shape[2]
    row = lambda i: (i, 0)
    y_spec = pl.BlockSpec((TOP_K, COMBINE_ROWS, part), lambda i: (0, i, 0))
    return pl.pallas_call(
        _combine_ln_kernel,
        grid=(n // COMBINE_ROWS,),
        in_specs=[y_spec, y_spec, pl.BlockSpec((TOP_K, COMBINE_ROWS), lambda i: (0, i)),
                  pl.BlockSpec((COMBINE_ROWS, d), row), _const_spec((1, d)), _const_spec((1, d))],
        out_specs=[pl.BlockSpec((COMBINE_ROWS, d), row)] * (2 if with_bf16 else 1),
        out_shape=[jax.ShapeDtypeStruct((n, d), _F32)]
        + ([jax.ShapeDtypeStruct((n, d), _BF16)] if with_bf16 else []),
        compiler_params=_params("arbitrary"),
        name="combine_ln",
    )(ya, yb, gates, res, g.reshape(1, d), b.reshape(1, d))


def _moe_layer(x, x_words, router_w, router_b, w_up, b_up, w_down, b_down, ln_g, ln_b, with_bf16):
    n, d = x.shape
    nk = n * TOP_K
    idx_t, gate_t, rank_t, count = _router(x, router_w, router_b)
    counts = count[:, 0].astype(jnp.int32)
    padded = (counts + MOE_STEP_ROWS - 1) // MOE_STEP_ROWS * MOE_STEP_ROWS
    padded_ends = jnp.cumsum(padded)
    padded_starts = padded_ends - padded
    n_blocks = -(-(nk + N_EXPERTS * (MOE_STEP_ROWS - 1)) // MOE_STEP_ROWS)
    p = n_blocks * MOE_STEP_ROWS
    short = counts % MOE_STEP_ROWS
    experts = jnp.arange(N_EXPERTS, dtype=jnp.int32)
    chosen = idx_t[None] == experts[:, None, None]
    start_of = jnp.sum(jnp.where(chosen, padded_starts[:, None, None], 0), axis=0)
    short_of = jnp.sum(jnp.where(chosen, short[:, None, None], 0), axis=0)
    gap_of = jnp.where(jnp.logical_and(short_of > 0, rank_t >= short_of), MOE_STEP_ROWS - short_of, 0)
    slot_of = (start_of + rank_t + gap_of).reshape(nk)
    block_start = jnp.arange(n_blocks, dtype=jnp.int32) * MOE_STEP_ROWS
    block_e = jnp.minimum(jnp.sum(padded_ends[None, :] <= block_start[:, None], axis=1),
                          N_EXPERTS - 1).astype(jnp.int32)
    first_of_expert = block_start == padded_starts[block_e]
    block_rows = jnp.where(block_start >= padded_ends[-1], 0,
                           jnp.where(jnp.logical_and(first_of_expert, short[block_e] > 0),
                                     short[block_e], MOE_STEP_ROWS)).astype(jnp.int32)
    last_block = jnp.minimum(padded_ends[-1] // MOE_STEP_ROWS, n_blocks - 1).astype(jnp.int32)
    xs = _dispatch(x_words, slot_of, p)
    bu = b_up.reshape(N_EXPERTS, D_FF, 2).transpose(0, 2, 1).reshape(N_EXPERTS, 1, 2 * D_FF)
    ys = _moe_ffn(xs, block_e, block_rows, last_block.reshape(1), w_up, bu, w_down,
                  b_down.reshape(N_EXPERTS, 1, d))
    return _combine_ln(_combine_gather(ys, slot_of), gate_t, x, ln_g, ln_b, with_bf16)


def kernel(x, l0_w_in, l0_ret_gn_g, l0_conv_w, l0_conv_b, l0_conv_ln_g, l0_conv_ln_b, l0_w_out, l0_ln1_g, l0_ln1_b, l0_router_w, l0_router_b, l0_w_up, l0_b_up, l0_w_down, l0_b_down, l0_ln2_g, l0_ln2_b, l1_w_in, l1_w_out, l1_ln1_g, l1_ln1_b, l1_router_w, l1_router_b, l1_w_up, l1_b_up, l1_w_down, l1_b_down, l1_ln2_g, l1_ln2_b):
    batch, seq, d = x.shape
    n = batch * seq
    x0 = x.reshape(n, d)

    x1, x1_words = _l0_block(x0, batch, seq, l0_w_in, l0_ret_gn_g, l0_conv_w, l0_conv_b,
                             l0_conv_ln_g, l0_conv_ln_b, l0_w_out, l0_ln1_g, l0_ln1_b)
    x2, x2_bf16 = _moe_layer(x1, x1_words, l0_router_w, l0_router_b, l0_w_up, l0_b_up, l0_w_down,
                             l0_b_down, l0_ln2_g, l0_ln2_b, True)

    q_scale = jnp.where(jnp.arange(3 * d) < d, LOG2_E * SB_HEAD_DIM ** -0.5, 1.0).astype(_F32)
    qkv = _proj(x2_bf16, l1_w_in, q_scale)
    att = _sb_attention(qkv, batch, seq)
    x3, x3_words = _proj_res_ln(att, l1_w_out, x2, l1_ln1_g, l1_ln1_b)
    (x4,) = _moe_layer(x3, x3_words, l1_router_w, l1_router_b, l1_w_up, l1_b_up, l1_w_down,
                       l1_b_down, l1_ln2_g, l1_ln2_b, False)
    return x4.reshape(batch, seq, d)
```
